```python
import math
import jax, jax.numpy as jnp
from jax import lax
import numpy as np

D_MODEL = 1024
BATCH = 8
SEQ = 2048
DEPTH = 2
DEC_BATCH = 128
DEC_SEQ = 1
PAST_LEN = 16384
PAGE_SIZE = 128

N_META = 16
D_MIX = 2 * D_MODEL
CONV_W = 4
EPS = 1e-6
NEG = -1e30
SSD_WIDTH = D_MIX // 2
SSD_HEAD_DIM = 64
SSD_HEADS = SSD_WIDTH // SSD_HEAD_DIM
SSD_GROUPS = 2
SSD_HPG = SSD_HEADS // SSD_GROUPS
SSD_STATE = 128
SSD_CONV_CH = SSD_WIDTH + 2 * SSD_GROUPS * SSD_STATE
SSD_CHUNK = 128
ML_WIDTH = D_MIX // 2
ML_HEADS = 4
ML_HEAD_DIM = ML_WIDTH // ML_HEADS
ML_QKV_BLOCK = 4
ML_CHUNK = 128
LRU_WIDTH = D_MIX
LRU_BLOCKS = 16
LRU_BLOCK = LRU_WIDTH // LRU_BLOCKS
LRU_C = 8.0
IN_MIX = SSD_WIDTH + SSD_CONV_CH + SSD_HEADS + 2 * ML_WIDTH
MIX_SPLITS = [SSD_WIDTH, SSD_WIDTH + SSD_CONV_CH, SSD_WIDTH + SSD_CONV_CH + SSD_HEADS, SSD_WIDTH + SSD_CONV_CH + SSD_HEADS + ML_WIDTH]
N_EVEN = (DEPTH + 1) // 2
N_ODD = DEPTH // 2

kernel_name = 'hybrid_ssd_mlstm_rglru_step'


def rmsnorm(x, w):
    xf = x.astype(jnp.float32)
    xf = xf * lax.rsqrt(jnp.mean(xf * xf, axis=-1, keepdims=True) + EPS)
    return (xf * w.astype(jnp.float32)).astype(x.dtype)


def causal_conv(x, buf, w, b):
    L = x.shape[1]
    xp = jnp.concatenate([buf.astype(x.dtype), x], axis=1)
    y = b + w[0] * xp[:, 0:L]
    for tap in range(1, CONV_W):
        y = y + w[tap] * xp[:, tap:tap + L]
    return y, xp[:, L:]


def blockdiag(x, w):
    nb, bi, bo = w.shape
    xb = x.reshape(x.shape[:-1] + (nb, bi))
    return jnp.einsum('...ki,kio->...ko', xb, w).reshape(x.shape[:-1] + (nb * bo,))


def pad_front(t, n, value=0.0):
    widths = [(0, 0)] * t.ndim
    widths[1] = (n, 0)
    return jnp.pad(t, widths, constant_values=value)


def time_major(t):
    return jnp.moveaxis(t, 1, 0)


def ssd_chunked(xdt, log_a, bm, cm):
    b, L = xdt.shape[:2]
    pad = (-L) % SSD_CHUNK
    xdt, log_a, bm, cm = [pad_front(t, pad) for t in (xdt, log_a, bm, cm)]
    Q = SSD_CHUNK
    nc = (L + pad) // Q
    X = xdt.reshape(b, nc, Q, SSD_GROUPS, SSD_HPG, SSD_HEAD_DIM)
    Bc = bm.reshape(b, nc, Q, SSD_GROUPS, SSD_STATE)
    Cc = cm.reshape(b, nc, Q, SSD_GROUPS, SSD_STATE)
    a_cs = jnp.cumsum(log_a.reshape(b, nc, Q, SSD_GROUPS, SSD_HPG), axis=2)
    causal = jnp.tril(jnp.ones((Q, Q), bool))[None, None, :, :, None, None]
    seg = jnp.exp(jnp.where(causal, a_cs[:, :, :, None] - a_cs[:, :, None, :], -jnp.inf))
    cb = jnp.einsum('bcign,bcjgn->bcijg', Cc, Bc)
    y_diag = jnp.einsum('bcijge,bcjgep->bcigep', cb[..., None] * seg, X)
    decay_end = jnp.exp(a_cs[:, :, -1:] - a_cs)
    chunk_states = jnp.einsum('bcjgn,bcjgep->bcgepn', Bc, X * decay_end[..., None])
    chunk_decay = jnp.exp(a_cs[:, :, -1])

    def step(s, inp):
        dec, st = inp
        return dec[..., None, None] * s + st, s

    s0 = jnp.zeros((b, SSD_GROUPS, SSD_HPG, SSD_HEAD_DIM, SSD_STATE), jnp.float32)
    s_final, s_in = lax.scan(step, s0, (time_major(chunk_decay), time_major(chunk_states).astype(jnp.float32)))
    y_off = jnp.einsum('bcign,cbgepn->bcigep', Cc, s_in) * jnp.exp(a_cs)[..., None]
    y = (y_diag + y_off).reshape(b, nc * Q, SSD_GROUPS, SSD_HPG, SSD_HEAD_DIM)[:, pad:]
    return y, s_final


def ssd_recurrent(xdt, log_a, bm, cm, s0):
    def step(s, inp):
        x_t, a_t, b_t, c_t = inp
        s = jnp.exp(a_t)[..., None, None] * s + jnp.einsum('bgep,bgn->bgepn', x_t, b_t)
        return s, jnp.einsum('bgepn,bgn->bgep', s, c_t)

    s_final, y = lax.scan(step, s0, (time_major(xdt), time_major(log_a), time_major(bm), time_major(cm)))
    return time_major(y), s_final


def ssd_mixer(z, xbc, dt_raw, conv_buf, s0, conv_w, conv_b, dt_bias, a_log, d_skip, norm_w, chunked):
    b, L, _ = z.shape
    xbc, new_buf = causal_conv(xbc, conv_buf, conv_w, conv_b)
    xbc = jax.nn.silu(xbc)
    x = xbc[..., :SSD_WIDTH].reshape(b, L, SSD_GROUPS, SSD_HPG, SSD_HEAD_DIM)
    bm = xbc[..., SSD_WIDTH:SSD_WIDTH + SSD_GROUPS * SSD_STATE].reshape(b, L, SSD_GROUPS, SSD_STATE)
    cm = xbc[..., SSD_WIDTH + SSD_GROUPS * SSD_STATE:].reshape(b, L, SSD_GROUPS, SSD_STATE)
    dt = jax.nn.softplus(dt_raw.astype(jnp.float32) + dt_bias.astype(jnp.float32)).reshape(b, L, SSD_GROUPS, SSD_HPG)
    log_a = -dt * jnp.exp(a_log.astype(jnp.float32)).reshape(SSD_GROUPS, SSD_HPG)
    xdt = x * dt[..., None]
    if chunked:
        y, s_new = ssd_chunked(xdt, log_a, bm, cm)
    else:
        s_init = s0.astype(jnp.float32).reshape(b, SSD_GROUPS, SSD_HPG, SSD_HEAD_DIM, SSD_STATE)
        y, s_new = ssd_recurrent(xdt, log_a, bm, cm, s_init)
    y = y + d_skip.reshape(SSD_GROUPS, SSD_HPG)[..., None] * x
    y = y.reshape(b, L, SSD_WIDTH) * jax.nn.silu(z.astype(jnp.float32))
    yg = y.reshape(b, L, SSD_GROUPS, SSD_WIDTH // SSD_GROUPS)
    yg = yg * lax.rsqrt(jnp.mean(yg * yg, axis=-1, keepdims=True) + EPS)
    y = yg.reshape(b, L, SSD_WIDTH) * norm_w
    return y, new_buf, s_new.reshape(b, SSD_HEADS, SSD_HEAD_DIM, SSD_STATE)


def mlstm_chunked(q, k, v, ig, logf):
    b, L, H, D = q.shape
    pad = (-L) % ML_CHUNK
    q, k, v, logf = [pad_front(t, pad) for t in (q, k, v, logf)]
    ig = pad_front(ig, pad, NEG)
    Q = ML_CHUNK
    nc = (L + pad) // Q

    def chunk(t):
        return jnp.moveaxis(t.reshape((b, nc, Q) + t.shape[2:]), 3, 1)

    qc, kc, vc, igc, lfc = [chunk(t) for t in (q, k, v, ig, logf)]
    bcum = jnp.cumsum(lfc, axis=-1)
    ftot = bcum[..., -1]
    w_end = ftot[..., None] - bcum + igc

    def step(carry, inp):
        c, n, m = carry
        f_c, w_c, k_c, v_c = inp
        m_new = jnp.maximum(f_c + m, jnp.max(w_c, axis=-1))
        sc = jnp.exp(f_c + m - m_new)
        wexp = jnp.exp(w_c - m_new[..., None])
        c_new = sc[..., None, None] * c + jnp.einsum('bhjd,bhje->bhde', v_c * wexp[..., None], k_c)
        n_new = sc[..., None] * n + jnp.einsum('bhj,bhje->bhe', wexp, k_c)
        return (c_new, n_new, m_new), (c, n, m)

    init = (jnp.zeros((b, H, D, D), jnp.float32), jnp.zeros((b, H, D), jnp.float32), jnp.zeros((b, H), jnp.float32))
    cmaj = lambda t: jnp.moveaxis(t, 2, 0)
    (cf, nf, mf), (cs, ns, ms) = lax.scan(step, init, (cmaj(ftot), cmaj(w_end), cmaj(kc), cmaj(vc)))
    inter = bcum + jnp.moveaxis(ms, 0, 2)[..., None]
    causal = jnp.tril(jnp.ones((Q, Q), bool))
    dmat = jnp.where(causal, bcum[..., :, None] - bcum[..., None, :] + igc[..., None, :], -jnp.inf)
    m_t = jnp.maximum(inter, jnp.max(dmat, axis=-1))
    dexp = jnp.exp(dmat - m_t[..., None])
    inter_sc = jnp.exp(inter - m_t)
    s = jnp.einsum('bhcid,bhcjd->bhcij', qc, kc) * dexp
    num = jnp.einsum('bhcij,bhcjd->bhcid', s, vc) + inter_sc[..., None] * jnp.einsum('cbhde,bhcie->bhcid', cs, qc)
    den = jnp.sum(s, axis=-1) + inter_sc * jnp.einsum('cbhe,bhcie->bhci', ns, qc)
    h = num / jnp.maximum(jnp.abs(den), jnp.exp(-m_t))[..., None]
    h = jnp.moveaxis(h, 1, 3).reshape(b, nc * Q, H, D)[:, pad:]
    return h, cf, nf, mf


def mlstm_recurrent(q, k, v, ig, logf, c0, n0, m0):
    def step(carry, inp):
        c, n, m = carry
        q_t, k_t, v_t, i_t, lf_t = inp
        m_new = jnp.maximum(lf_t + m, i_t)
        fs = jnp.exp(lf_t + m - m_new)
        is_ = jnp.exp(i_t - m_new)
        c = fs[..., None, None] * c + is_[..., None, None] * jnp.einsum('bhd,bhe->bhde', v_t, k_t)
        n = fs[..., None] * n + is_[..., None] * k_t
        num = jnp.einsum('bhde,bhe->bhd', c, q_t)
        den = jnp.einsum('bhe,bhe->bh', n, q_t)
        h = num / jnp.maximum(jnp.abs(den), jnp.exp(-m_new))[..., None]
        return (c, n, m_new), h

    (cf, nf, mf), h = lax.scan(step, (c0, n0, m0), tuple(time_major(t) for t in (q, k, v, ig, logf)))
    return time_major(h), cf, nf, mf


def mlstm_mixer(z, xm, conv_buf, c0, n0, m0, conv_w, conv_b, wq, wk, wv, w_gate, b_gate, skip, norm_w, chunked):
    b, L, _ = xm.shape
    xc, new_buf = causal_conv(xm, conv_buf, conv_w, conv_b)
    xc = jax.nn.silu(xc)
    q = blockdiag(xc, wq)
    k = blockdiag(xc, wk)
    v = blockdiag(xm, wv)
    gates = (jnp.concatenate([q, k, v], axis=-1) @ w_gate + b_gate).astype(jnp.float32)
    ig = gates[..., :ML_HEADS]
    logf = jax.nn.log_sigmoid(gates[..., ML_HEADS:])
    heads = lambda t: t.reshape(b, L, ML_HEADS, ML_HEAD_DIM)
    q, k, v = heads(q), heads(k) * (ML_HEAD_DIM ** -0.5), heads(v)
    if chunked:
        h, c, n, m = mlstm_chunked(q, k, v, ig, logf)
    else:
        h, c, n, m = mlstm_recurrent(q, k, v, ig, logf, c0.astype(jnp.float32), n0.astype(jnp.float32), m0.astype(jnp.float32))
    h = h.astype(jnp.float32)
    mu = jnp.mean(h, axis=-1, keepdims=True)
    var = jnp.mean(jnp.square(h - mu), axis=-1, keepdims=True)
    h = ((h - mu) * lax.rsqrt(var + EPS)).reshape(b, L, ML_WIDTH) * norm_w
    y = (h + skip * xc) * jax.nn.silu(z.astype(jnp.float32))
    return y, new_buf, c, n, m


def _lin_combine(e1, e2):
    a1, b1 = e1
    a2, b2 = e2
    return a1 * a2, a2 * b1 + b2


def rglru_mixer(gate, xr, conv_buf, h0, conv_w, conv_b, wa, ba, wx, bx, lam):
    xc, new_buf = causal_conv(xr, conv_buf, conv_w, conv_b)
    r = jax.nn.sigmoid((blockdiag(xc, wa) + ba).astype(jnp.float32))
    i = jax.nn.sigmoid((blockdiag(xc, wx) + bx).astype(jnp.float32))
    log_a = -LRU_C * r * jax.nn.softplus(-lam.astype(jnp.float32))
    a = jnp.exp(log_a)
    u = jnp.sqrt(-jnp.expm1(2.0 * log_a)) * (i * xc)
    u = u.at[:, 0].add(a[:, 0] * h0.astype(jnp.float32))
    _, h = lax.associative_scan(_lin_combine, (a, u), axis=1)
    y = h * jax.nn.silu(gate.astype(jnp.float32))
    return y, new_buf, h[:, -1]


def trunk(h, states, p, chunked):
    b = h.shape[0]
    dtype = h.dtype
    o_sc, o_s, o_mc, o_c, o_n, o_m, o_lc, o_lh = [], [], [], [], [], [], [], []
    for layer in range(DEPTH):
        hn = rmsnorm(h, p['norm_w'][layer])
        if layer % 2 == 0:
            e = layer // 2
            z_s, xbc, dt_raw, z_m, xm = jnp.split(hn @ p['w_in_mix'][e], MIX_SPLITS, axis=-1)
            if chunked:
                buf_s = jnp.zeros((b, CONV_W - 1, SSD_CONV_CH), dtype)
                buf_m = jnp.zeros((b, CONV_W - 1, ML_WIDTH), dtype)
                s0 = c0 = n0 = m0 = None
            else:
                buf_s, s0, buf_m = states[0][e], states[1][e], states[2][e]
                c0, n0, m0 = states[3][e], states[4][e], states[5][e]
            ys, nb_s, s_new = ssd_mixer(z_s, xbc, dt_raw, buf_s, s0, p['ssd_conv_w'][e], p['ssd_conv_b'][e],
                                        p['ssd_dt_bias'][e], p['ssd_a_log'][e], p['ssd_d'][e], p['ssd_norm_w'][e], chunked)
            ym, nb_m, c, n, m = mlstm_mixer(z_m, xm, buf_m, c0, n0, m0, p['ml_conv_w'][e], p['ml_conv_b'][e],
                                            p['ml_wq'][e], p['ml_wk'][e], p['ml_wv'][e], p['ml_w_gate'][e],
                                            p['ml_b_gate'][e], p['ml_skip'][e], p['ml_norm_w'][e], chunked)
            mix = jnp.concatenate([ys, ym], axis=-1).astype(dtype)
            h = h + (mix @ p['w_out_mix'][e]).astype(dtype)
            o_sc.append(nb_s); o_s.append(s_new); o_mc.append(nb_m)
            o_c.append(c); o_n.append(n); o_m.append(m)
        else:
            o = layer // 2
            gate, xr = jnp.split(hn @ p['lru_w_in'][o], 2, axis=-1)
            if chunked:
                buf_l = jnp.zeros((b, CONV_W - 1, LRU_WIDTH), dtype)
                h0 = jnp.zeros((b, LRU_WIDTH), jnp.float32)
            else:
                buf_l, h0 = states[6][o], states[7][o]
            yl, nb_l, h_last = rglru_mixer(gate, xr, buf_l, h0, p['lru_conv_w'][o], p['lru_conv_b'][o],
                                           p['lru_wa'][o], p['lru_ba'][o], p['lru_wx'][o], p['lru_bx'][o], p['lru_lambda'][o])
            h = h + (yl.astype(dtype) @ p['lru_w_out'][o]).astype(dtype)
            o_lc.append(nb_l); o_lh.append(h_last)
    y = rmsnorm(h, p['final_norm_w'])
    new_states = (jnp.stack(o_sc), jnp.stack(o_s), jnp.stack(o_mc), jnp.stack(o_c),
                  jnp.stack(o_n), jnp.stack(o_m), jnp.stack(o_lc), jnp.stack(o_lh))
    return y, new_states


def setup_inputs(seed: int = 0) -> dict:
    key = jax.random.key(seed)
    ks = iter(jax.random.split(key, 64))
    f32 = jnp.float32

    def nrm(shape, scale):
        return scale * jax.random.normal(next(ks), shape, f32)

    def uni(shape, lo, hi):
        return jax.random.uniform(next(ks), shape, f32, minval=lo, maxval=hi)

    E, O, Bd = N_EVEN, N_ODD, DEC_BATCH
    x_prompt = nrm((BATCH, SEQ, D_MODEL), 1.0)
    x_sample = nrm((DEC_BATCH, DEC_SEQ, D_MODEL), 1.0)
    state_ssd_conv = nrm((E, Bd, CONV_W - 1, SSD_CONV_CH), 1.0)
    state_ssd = nrm((E, Bd, SSD_HEADS, SSD_HEAD_DIM, SSD_STATE), 0.1)
    state_mlstm_conv = nrm((E, Bd, CONV_W - 1, ML_WIDTH), 1.0)
    state_mlstm_C = nrm((E, Bd, ML_HEADS, ML_HEAD_DIM, ML_HEAD_DIM), 0.05)
    state_mlstm_n = nrm((E, Bd, ML_HEADS, ML_HEAD_DIM), 0.5)
    state_mlstm_m = nrm((E, Bd, ML_HEADS), 0.5)
    state_lru_conv = nrm((O, Bd, CONV_W - 1, LRU_WIDTH), 1.0)
    state_lru_h = nrm((O, Bd, LRU_WIDTH), 0.5)
    meta_tokens = nrm((N_META, D_MODEL), 1.0)
    norm_w = 1.0 + nrm((DEPTH, D_MODEL), 0.02)
    final_norm_w = 1.0 + nrm((D_MODEL,), 0.02)
    w_in_mix = nrm((E, D_MODEL, IN_MIX), D_MODEL ** -0.5)
    w_out_mix = nrm((E, D_MIX, D_MODEL), D_MIX ** -0.5)
    ssd_conv_w = nrm((E, CONV_W, SSD_CONV_CH), CONV_W ** -0.5)
    ssd_conv_b = nrm((E, SSD_CONV_CH), 0.02)
    dt0 = jnp.exp(uni((E, SSD_HEADS), math.log(1e-3), math.log(1e-1)))
    ssd_dt_bias = dt0 + jnp.log(-jnp.expm1(-dt0))
    ssd_a_log = jnp.log(uni((E, SSD_HEADS), 1.0, 16.0))
    ssd_d = 1.0 + nrm((E, SSD_HEADS), 0.02)
    ssd_norm_w = 1.0 + nrm((E, SSD_WIDTH), 0.02)
    ml_conv_w = nrm((E, CONV_W, ML_WIDTH), CONV_W ** -0.5)
    ml_conv_b = nrm((E, ML_WIDTH), 0.02)
    nb = ML_WIDTH // ML_QKV_BLOCK
    ml_wq = nrm((E, nb, ML_QKV_BLOCK, ML_QKV_BLOCK), ML_QKV_BLOCK ** -0.5)
    ml_wk = nrm((E, nb, ML_QKV_BLOCK, ML_QKV_BLOCK), ML_QKV_BLOCK ** -0.5)
    ml_wv = nrm((E, nb, ML_QKV_BLOCK, ML_QKV_BLOCK), ML_QKV_BLOCK ** -0.5)
    ml_w_gate = nrm((E, 3 * ML_WIDTH, 2 * ML_HEADS), (3 * ML_WIDTH) ** -0.5)
    f_bias = jnp.broadcast_to(jnp.linspace(3.0, 6.0, ML_HEADS, dtype=f32), (E, ML_HEADS))
    ml_b_gate = jnp.concatenate([nrm((E, ML_HEADS), 0.1), f_bias + nrm((E, ML_HEADS), 0.01)], axis=-1)
    ml_skip = 1.0 + nrm((E, ML_WIDTH), 0.02)
    ml_norm_w = 1.0 + nrm((E, ML_WIDTH), 0.02)
    lru_w_in = nrm((O, D_MODEL, 2 * LRU_WIDTH), D_MODEL ** -0.5)
    lru_w_out = nrm((O, LRU_WIDTH, D_MODEL), LRU_WIDTH ** -0.5)
    lru_conv_w = nrm((O, CONV_W, LRU_WIDTH), CONV_W ** -0.5)
    lru_conv_b = nrm((O, LRU_WIDTH), 0.02)
    lru_wa = nrm((O, LRU_BLOCKS, LRU_BLOCK, LRU_BLOCK), LRU_BLOCK ** -0.5)
    lru_ba = nrm((O, LRU_WIDTH), 0.02)
    lru_wx = nrm((O, LRU_BLOCKS, LRU_BLOCK, LRU_BLOCK), LRU_BLOCK ** -0.5)
    lru_bx = nrm((O, LRU_WIDTH), 0.02)
    s = uni((O, LRU_WIDTH), 0.9, 0.999) ** (1.0 / LRU_C)
    lru_lambda = jnp.log(s) - jnp.log1p(-s)
    return {'x_prompt': x_prompt, 'x_sample': x_sample,
            'state_ssd_conv': state_ssd_conv, 'state_ssd': state_ssd,
            'state_mlstm_conv': state_mlstm_conv, 'state_mlstm_C': state_mlstm_C,
            'state_mlstm_n': state_mlstm_n, 'state_mlstm_m': state_mlstm_m,
            'state_lru_conv': state_lru_conv, 'state_lru_h': state_lru_h,
            'meta_tokens': meta_tokens, 'norm_w': norm_w, 'final_norm_w': final_norm_w,
            'w_in_mix': w_in_mix, 'w_out_mix': w_out_mix,
            'ssd_conv_w': ssd_conv_w, 'ssd_conv_b': ssd_conv_b, 'ssd_dt_bias': ssd_dt_bias,
            'ssd_a_log': ssd_a_log, 'ssd_d': ssd_d, 'ssd_norm_w': ssd_norm_w,
            'ml_conv_w': ml_conv_w, 'ml_conv_b': ml_conv_b, 'ml_wq': ml_wq, 'ml_wk': ml_wk, 'ml_wv': ml_wv,
            'ml_w_gate': ml_w_gate, 'ml_b_gate': ml_b_gate, 'ml_skip': ml_skip, 'ml_norm_w': ml_norm_w,
            'lru_w_in': lru_w_in, 'lru_w_out': lru_w_out, 'lru_conv_w': lru_conv_w, 'lru_conv_b': lru_conv_b,
            'lru_wa': lru_wa, 'lru_ba': lru_ba, 'lru_wx': lru_wx, 'lru_bx': lru_bx, 'lru_lambda': lru_lambda}


def reference(x_prompt, x_sample, state_ssd_conv, state_ssd, state_mlstm_conv, state_mlstm_C, state_mlstm_n,
              state_mlstm_m, state_lru_conv, state_lru_h, meta_tokens, norm_w, final_norm_w, w_in_mix, w_out_mix,
              ssd_conv_w, ssd_conv_b, ssd_dt_bias, ssd_a_log, ssd_d, ssd_norm_w, ml_conv_w, ml_conv_b, ml_wq, ml_wk,
              ml_wv, ml_w_gate, ml_b_gate, ml_skip, ml_norm_w, lru_w_in, lru_w_out, lru_conv_w, lru_conv_b, lru_wa,
              lru_ba, lru_wx, lru_bx, lru_lambda):
    p = dict(norm_w=norm_w, final_norm_w=final_norm_w, w_in_mix=w_in_mix, w_out_mix=w_out_mix,
             ssd_conv_w=ssd_conv_w, ssd_conv_b=ssd_conv_b, ssd_dt_bias=ssd_dt_bias, ssd_a_log=ssd_a_log,
             ssd_d=ssd_d, ssd_norm_w=ssd_norm_w, ml_conv_w=ml_conv_w, ml_conv_b=ml_conv_b, ml_wq=ml_wq,
             ml_wk=ml_wk, ml_wv=ml_wv, ml_w_gate=ml_w_gate, ml_b_gate=ml_b_gate, ml_skip=ml_skip,
             ml_norm_w=ml_norm_w, lru_w_in=lru_w_in, lru_w_out=lru_w_out, lru_conv_w=lru_conv_w,
             lru_conv_b=lru_conv_b, lru_wa=lru_wa, lru_ba=lru_ba, lru_wx=lru_wx, lru_bx=lru_bx,
             lru_lambda=lru_lambda)
    b = x_prompt.shape[0]
    meta = jnp.broadcast_to(meta_tokens.astype(x_prompt.dtype)[None], (b, N_META, D_MODEL))
    h_prompt = jnp.concatenate([meta, x_prompt], axis=1)
    y_full, (p_sc, p_s, p_mc, p_c, p_n, p_m, p_lc, p_lh) = trunk(h_prompt, None, p, True)
    y_prompt = y_full[:, N_META:]
    states = (state_ssd_conv, state_ssd, state_mlstm_conv, state_mlstm_C, state_mlstm_n, state_mlstm_m,
              state_lru_conv, state_lru_h)
    y_sample, (s_sc, s_s, s_mc, s_c, s_n, s_m, s_lc, s_lh) = trunk(x_sample, states, p, False)
    return (y_prompt, y_sample, p_sc, p_s, p_mc, p_c, p_n, p_m, p_lc, p_lh,
            s_sc, s_s, s_mc, s_c, s_n, s_m, s_lc, s_lh)
```

```python
import functools

import jax
import jax.numpy as jnp
from jax import lax
from jax.experimental import pallas as pl
from jax.experimental.pallas import tpu as pltpu

F32 = jnp.float32
BF16 = jnp.bfloat16

D_MODEL = 1024
N_META = 16
CONV_W = 4
EPS = 1e-6
NEG = -1e30
SSD_WIDTH = 1024
SSD_HEAD_DIM = 64
SSD_HEADS = 16
SSD_GROUPS = 2
SSD_HPG = 8
SSD_STATE = 128
SSD_CONV_CH = 1536
ML_WIDTH = 1024
ML_HEADS = 4
ML_HEAD_DIM = 256
ML_QKV_BLOCK = 4
LRU_WIDTH = 2048
LRU_BLOCKS = 16
LRU_BLOCK = 128
LRU_C = 8.0

LANE = 128
SUBLANE = 8
CHUNK = 128
TAIL = CONV_W - 1

OFF_ZS = 0
OFF_XBC = OFF_ZS + SSD_WIDTH
OFF_DT = OFF_XBC + SSD_CONV_CH
OFF_ZM = OFF_DT + LANE
OFF_XM = OFF_ZM + ML_WIDTH
IN_MIX_PAD = OFF_XM + ML_WIDTH

VMEM_LIMIT = 56 * 1024 * 1024


def _sigmoid(x):
    return 1.0 / (1.0 + jnp.exp(-x))


def _silu(x):
    return x * _sigmoid(x)


def _softplus(x):
    return jnp.maximum(x, 0.0) + jnp.log1p(jnp.exp(-jnp.abs(x)))


def _rms(x, w):
    return x * lax.rsqrt(jnp.mean(x * x, axis=-1, keepdims=True) + EPS) * w


def _bdot(a, b):
    return jnp.dot(a.astype(BF16), b.astype(BF16), preferred_element_type=F32)


def _split3(x):
    hi = x.astype(BF16)
    r = x - hi.astype(F32)
    mid = r.astype(BF16)
    lo = (r - mid.astype(F32)).astype(BF16)
    return hi, mid, lo


def _cumsum_rows(x, tril):
    hi, mid, lo = _split3(x)
    d = functools.partial(jnp.dot, preferred_element_type=F32)
    return d(tril, hi) + d(tril, mid) + d(tril, lo)


def _expand_heads(x, expand):
    hi, mid, lo = _split3(x)
    d = functools.partial(jnp.dot, preferred_element_type=F32)
    return d(hi, expand) + d(mid, expand) + d(lo, expand)


def _tril(n):
    r = lax.broadcasted_iota(jnp.int32, (n, n), 0)
    c = lax.broadcasted_iota(jnp.int32, (n, n), 1)
    return r >= c


def _expand_matrix():
    r = lax.broadcasted_iota(jnp.int32, (LANE, SSD_WIDTH), 0)
    c = lax.broadcasted_iota(jnp.int32, (LANE, SSD_WIDTH), 1)
    return jnp.where(lax.shift_right_logical(c, 6) == r, 1.0, 0.0).astype(BF16)


def _conv_chunk(buf, x, w_ref, b_ref):
    n = x.shape[0]
    buf[SUBLANE:SUBLANE + n, :] = x
    y = b_ref[...] + w_ref[3:4, :] * x
    for tap in range(TAIL):
        y = y + w_ref[tap:tap + 1, :] * buf[SUBLANE - TAIL + tap:SUBLANE - TAIL + tap + n, :]
    buf[SUBLANE - TAIL:SUBLANE, :] = buf[SUBLANE + n - TAIL:SUBLANE + n, :]
    return y


def _blockdiag_tiles(x, w_ref):
    k = w_ref.shape[0]
    return jnp.concatenate(
        [_bdot(x[:, t * LANE:(t + 1) * LANE], w_ref[t]) for t in range(k)], axis=-1)


def _group_rmsnorm(y, w):
    half = SSD_WIDTH // SSD_GROUPS
    parts = []
    for g in range(SSD_GROUPS):
        yg = y[:, g * half:(g + 1) * half]
        parts.append(yg * lax.rsqrt(jnp.mean(yg * yg, axis=-1, keepdims=True) + EPS))
    return jnp.concatenate(parts, axis=-1) * w


def _head_layernorm(h):
    parts = []
    for k in range(ML_HEADS):
        hk = h[:, k * ML_HEAD_DIM:(k + 1) * ML_HEAD_DIM]
        mu = jnp.mean(hk, axis=-1, keepdims=True)
        d = hk - mu
        var = jnp.mean(d * d, axis=-1, keepdims=True)
        parts.append(d * lax.rsqrt(var + EPS))
    return jnp.concatenate(parts, axis=-1)


def _mlstm_qkv_gates(xm, xc, wq_ref, wk_ref, wv_ref, wg_ref, bg_ref):
    q = _blockdiag_tiles(xc, wq_ref)
    k = _blockdiag_tiles(xc, wk_ref)
    v = _blockdiag_tiles(xm, wv_ref)
    gates = _bdot(jnp.concatenate([q, k, v], axis=-1), wg_ref[...]) + bg_ref[...]
    ig = gates[:, :LANE]
    logf = -_softplus(-gates[:, LANE:])
    return q, k * (ML_HEAD_DIM ** -0.5), v, ig, logf


def _l0_prompt_kernel(x_ref, nw_ref, win_ref, wout_ref,
                      scw_ref, scb_ref, dtb_ref, alog_ref, dsk_ref, snw_ref,
                      mcw_ref, mcb_ref, wq_ref, wk_ref, wv_ref, wg_ref, bg_ref, msk_ref, mnw_ref,
                      isc_ref, iss_ref, imc_ref, ict_ref, inn_ref, imm_ref,
                      h1_ref, osc_ref, oss_ref, omc_ref, oct_ref, onn_ref, omm_ref,
                      sbuf, mbuf, s_st, ct_st, n_st, m_st, *, front_pad):
    c = pl.program_id(1)
    q_len = x_ref.shape[1]

    @pl.when(c == 0)
    def _():
        sbuf[SUBLANE - TAIL:SUBLANE, :] = isc_ref[0]
        mbuf[SUBLANE - TAIL:SUBLANE, :] = imc_ref[0]
        s_st[...] = iss_ref[0]
        ct_st[...] = ict_ref[0]
        n_st[...] = inn_ref[0]
        m_st[...] = imm_ref[0]

    x = x_ref[0]
    hn = _rms(x, nw_ref[...])
    proj = _bdot(hn, win_ref[...])
    z_s = proj[:, OFF_ZS:OFF_ZS + SSD_WIDTH]
    xbc = proj[:, OFF_XBC:OFF_XBC + SSD_CONV_CH]
    dt_raw = proj[:, OFF_DT:OFF_DT + LANE]
    z_m = proj[:, OFF_ZM:OFF_ZM + ML_WIDTH]
    xm = proj[:, OFF_XM:OFF_XM + ML_WIDTH]

    causal = _tril(q_len)
    tril = jnp.where(causal, 1.0, 0.0).astype(BF16)
    if front_pad:
        valid = lax.broadcasted_iota(jnp.int32, (q_len, 1), 0) >= front_pad

    xbc = _silu(_conv_chunk(sbuf, xbc, scw_ref, scb_ref))
    xs = xbc[:, :SSD_WIDTH]
    bm = xbc[:, SSD_WIDTH:SSD_WIDTH + SSD_GROUPS * SSD_STATE]
    cm = xbc[:, SSD_WIDTH + SSD_GROUPS * SSD_STATE:]
    dt = _softplus(dt_raw + dtb_ref[...])
    if front_pad:
        dt = jnp.where(valid, dt, 0.0)
    log_a = -dt * jnp.exp(alog_ref[...])
    a_cs = _cumsum_rows(log_a, tril)
    a_last = a_cs[q_len - 1:q_len, :]
    expand = _expand_matrix()
    w_state = _expand_heads(dt * jnp.exp(a_last - a_cs), expand)
    e_acs = _expand_heads(jnp.exp(a_cs), expand)
    a_cs_t = a_cs.T
    dt_t = dt.T
    pair_lo = lax.broadcasted_iota(jnp.int32, (q_len, LANE), 1) < SSD_HEAD_DIM
    half = SSD_WIDTH // SSD_GROUPS
    y_groups = []
    for g in range(SSD_GROUPS):
        bg = bm[:, g * SSD_STATE:(g + 1) * SSD_STATE]
        cg = cm[:, g * SSD_STATE:(g + 1) * SSD_STATE]
        bg_t = bg.T
        xg = xs[:, g * half:(g + 1) * half]
        eg = e_acs[:, g * half:(g + 1) * half]
        s_old = s_st[g]
        cb = _bdot(cg, bg_t)
        y_off = _bdot(cg, s_old) * eg
        s_st[g] = eg[q_len - 1:q_len, :] * s_old + _bdot(bg_t, xg * w_state[:, g * half:(g + 1) * half])
        y_pairs = []
        for pr in range(SSD_HPG // 2):
            ms = []
            for e in (2 * pr, 2 * pr + 1):
                hd = g * SSD_HPG + e
                seg = jnp.exp(jnp.where(causal, a_cs[:, hd:hd + 1] - a_cs_t[hd:hd + 1, :], -jnp.inf))
                ms.append(cb * seg * dt_t[hd:hd + 1, :])
            xp = xg[:, pr * LANE:(pr + 1) * LANE]
            rhs = jnp.concatenate([jnp.where(pair_lo, xp, 0.0), jnp.where(pair_lo, 0.0, xp)], axis=0)
            y_pairs.append(_bdot(jnp.concatenate(ms, axis=-1), rhs))
        y_groups.append(jnp.concatenate(y_pairs, axis=-1) + y_off)
    y_s = jnp.concatenate(y_groups, axis=-1) + dsk_ref[...] * xs
    y_s = _group_rmsnorm(y_s * _silu(z_s), snw_ref[...])

    xc = _silu(_conv_chunk(mbuf, xm, mcw_ref, mcb_ref))
    q, k, v, ig, logf = _mlstm_qkv_gates(xm, xc, wq_ref, wk_ref, wv_ref, wg_ref, bg_ref)
    if front_pad:
        ig = jnp.where(valid, ig, NEG)
        logf = jnp.where(valid, logf, 0.0)
    bcum = _cumsum_rows(logf, tril)
    ftot = bcum[q_len - 1:q_len, :]
    m_prev = m_st[...]
    w_end = ftot - bcum + ig
    m_new = jnp.maximum(ftot + m_prev, jnp.max(w_end, axis=0, keepdims=True))
    sc = jnp.exp(ftot + m_prev - m_new)
    wexp = jnp.exp(w_end - m_new)
    inter = bcum + m_prev
    bcum_t = bcum.T
    ig_t = ig.T
    h_heads = []
    for hd in range(ML_HEADS):
        sl = slice(hd * ML_HEAD_DIM, (hd + 1) * ML_HEAD_DIM)
        q_h, k_h, v_h = q[:, sl], k[:, sl], v[:, sl]
        k_t = k_h.T
        dmat = jnp.where(causal, bcum[:, hd:hd + 1] - bcum_t[hd:hd + 1, :] + ig_t[hd:hd + 1, :], -jnp.inf)
        inter_h = inter[:, hd:hd + 1]
        m_t = jnp.maximum(inter_h, jnp.max(dmat, axis=-1, keepdims=True))
        dexp = jnp.exp(dmat - m_t)
        inter_sc = jnp.exp(inter_h - m_t)
        s = _bdot(q_h, k_t) * dexp
        ct_old = ct_st[hd]
        n_old = n_st[hd:hd + 1, :]
        num = _bdot(s, v_h) + inter_sc * _bdot(q_h, ct_old)
        den = jnp.sum(s, axis=-1, keepdims=True) + inter_sc * jnp.sum(q_h * n_old, axis=-1, keepdims=True)
        h_heads.append(num / jnp.maximum(jnp.abs(den), jnp.exp(-m_t)))
        w_col = wexp[:, hd:hd + 1]
        sc_h = sc[:, hd:hd + 1]
        ct_st[hd] = sc_h * ct_old + _bdot(k_t, v_h * w_col)
        n_st[hd:hd + 1, :] = sc_h * n_old + jnp.sum(k_h * w_col, axis=0, keepdims=True)
    m_st[...] = m_new
    h_m = _head_layernorm(jnp.concatenate(h_heads, axis=-1)) * mnw_ref[...]
    y_m = (h_m + msk_ref[...] * xc) * _silu(z_m)

    h1 = x + _bdot(jnp.concatenate([y_s, y_m], axis=-1), wout_ref[...])
    if front_pad:
        h1 = jnp.where(valid, h1, 0.0)
    h1_ref[0] = h1

    @pl.when(c == pl.num_programs(1) - 1)
    def _():
        osc_ref[0] = sbuf[SUBLANE - TAIL:SUBLANE, :]
        omc_ref[0] = mbuf[SUBLANE - TAIL:SUBLANE, :]
        oss_ref[0] = s_st[...]
        oct_ref[0] = ct_st[...]
        onn_ref[0] = n_st[...]
        omm_ref[0] = m_st[...]


def _const_spec(shape):
    nd = len(shape)
    return pl.BlockSpec(shape, lambda b, c: (0,) * nd)


def _state_spec(shape, batched):
    nd = len(shape)
    if batched:
        return pl.BlockSpec((1,) + shape, lambda b, c: (b,) + (0,) * nd)
    return pl.BlockSpec((1,) + shape, lambda b, c: (0,) * (nd + 1))


L0_STATE_SHAPES = ((TAIL, SSD_CONV_CH), (SSD_GROUPS, SSD_STATE, SSD_WIDTH // SSD_GROUPS), (TAIL, ML_WIDTH),
                   (ML_HEADS, ML_HEAD_DIM, ML_HEAD_DIM), (ML_HEADS, ML_HEAD_DIM), (1, LANE))


def _l0_prompt(x, weights, init, front_pad):
    bsz, length, _ = x.shape
    q_len = min(CHUNK, length)
    assert length % q_len == 0
    grid = (bsz, length // q_len)
    x_spec = pl.BlockSpec((1, q_len, D_MODEL), lambda b, c: (b, c, 0))
    in_specs = ([x_spec] + [_const_spec(w.shape) for w in weights]
                + [_state_spec(s, False) for s in L0_STATE_SHAPES])
    out_shape = ([jax.ShapeDtypeStruct((bsz, length, D_MODEL), F32)]
                 + [jax.ShapeDtypeStruct((bsz,) + s, F32) for s in L0_STATE_SHAPES])
    out_specs = [x_spec] + [_state_spec(s, True) for s in L0_STATE_SHAPES]
    scratch = [pltpu.VMEM((q_len + SUBLANE, SSD_CONV_CH), F32), pltpu.VMEM((q_len + SUBLANE, ML_WIDTH), F32),
               pltpu.VMEM(L0_STATE_SHAPES[1], F32), pltpu.VMEM(L0_STATE_SHAPES[3], F32),
               pltpu.VMEM(L0_STATE_SHAPES[4], F32), pltpu.VMEM(L0_STATE_SHAPES[5], F32)]
    return pl.pallas_call(
        functools.partial(_l0_prompt_kernel, front_pad=front_pad),
        grid=grid, in_specs=in_specs, out_specs=out_specs, out_shape=out_shape, scratch_shapes=scratch,
        compiler_params=pltpu.CompilerParams(dimension_semantics=("arbitrary", "arbitrary"),
                                             vmem_limit_bytes=VMEM_LIMIT),
        name="l0_prompt",
    )(x, *weights, *init)


def _rglru_gates(xc, wa_ref, ba_ref, wx_ref, bx_ref, lam_ref):
    r = _sigmoid(_blockdiag_tiles(xc, wa_ref) + ba_ref[...])
    i = _sigmoid(_blockdiag_tiles(xc, wx_ref) + bx_ref[...])
    log_a = -LRU_C * r * _softplus(-lam_ref[...])
    a = jnp.exp(log_a)
    u = jnp.sqrt(1.0 - a * a) * (i * xc)
    return a, u


def _l1_prompt_kernel(h_ref, nw_ref, fnw_ref, win_ref, wout_ref, cw_ref, cb_ref,
                      wa_ref, ba_ref, wx_ref, bx_ref, lam_ref, ilc_ref, ilh_ref,
                      y_ref, olc_ref, olh_ref, lbuf, h_st, *, front_pad):
    c = pl.program_id(1)
    q_len = h_ref.shape[1]

    @pl.when(c == 0)
    def _():
        lbuf[SUBLANE - TAIL:SUBLANE, :] = ilc_ref[0]
        h_st[...] = ilh_ref[0]

    h_in = h_ref[0]
    hn = _rms(h_in, nw_ref[...])
    proj = _bdot(hn, win_ref[...])
    gate = proj[:, :LRU_WIDTH]
    xr = proj[:, LRU_WIDTH:]
    xc = _conv_chunk(lbuf, xr, cw_ref, cb_ref)
    a, u = _rglru_gates(xc, wa_ref, ba_ref, wx_ref, bx_ref, lam_ref)
    row = lax.broadcasted_iota(jnp.int32, (q_len, 1), 0)
    if front_pad:
        valid = row >= front_pad
        a = jnp.where(valid, a, 1.0)
        u = jnp.where(valid, u, 0.0)
    shift = 1
    while shift < q_len:
        keep = row >= shift
        a_sh = jnp.where(keep, pltpu.roll(a, shift, 0), 1.0)
        u_sh = jnp.where(keep, pltpu.roll(u, shift, 0), 0.0)
        u = a * u_sh + u
        a = a * a_sh
        shift *= 2
    h = a * h_st[...] + u
    h_st[...] = h[q_len - 1:q_len, :]
    y = h * _silu(gate)
    h2 = h_in + _bdot(y, wout_ref[...])
    y_ref[0] = _rms(h2, fnw_ref[...])

    @pl.when(c == pl.num_programs(1) - 1)
    def _():
        olc_ref[0] = lbuf[SUBLANE - TAIL:SUBLANE, :]
        olh_ref[0] = h_st[...]


L1_STATE_SHAPES = ((TAIL, LRU_WIDTH), (1, LRU_WIDTH))


def _l1_prompt(h1, weights, init, front_pad):
    bsz, length, _ = h1.shape
    q_len = min(CHUNK, length)
    assert length % q_len == 0
    grid = (bsz, length // q_len)
    x_spec = pl.BlockSpec((1, q_len, D_MODEL), lambda b, c: (b, c, 0))
    in_specs = ([x_spec] + [_const_spec(w.shape) for w in weights]
                + [_state_spec(s, False) for s in L1_STATE_SHAPES])
    out_shape = ([jax.ShapeDtypeStruct((bsz, length, D_MODEL), F32)]
                 + [jax.ShapeDtypeStruct((bsz,) + s, F32) for s in L1_STATE_SHAPES])
    out_specs = [x_spec] + [_state_spec(s, True) for s in L1_STATE_SHAPES]
    scratch = [pltpu.VMEM((q_len + SUBLANE, LRU_WIDTH), F32), pltpu.VMEM((1, LRU_WIDTH), F32)]
    return pl.pallas_call(
        functools.partial(_l1_prompt_kernel, front_pad=front_pad),
        grid=grid, in_specs=in_specs, out_specs=out_specs, out_shape=out_shape, scratch_shapes=scratch,
        compiler_params=pltpu.CompilerParams(dimension_semantics=("arbitrary", "arbitrary"),
                                             vmem_limit_bytes=VMEM_LIMIT),
        name="l1_prompt",
    )(h1, *weights, *init)


def _conv_step(buf_ref, x, w_ref, b_ref, newbuf_ref):
    y = b_ref[...] + w_ref[3:4, :] * x
    for tap in range(TAIL):
        y = y + w_ref[tap:tap + 1, :] * buf_ref[tap]
    for tap in range(TAIL - 1):
        newbuf_ref[tap] = buf_ref[tap + 1]
    newbuf_ref[TAIL - 1] = x
    return y


def _l0_sample_pre_kernel(x_ref, nw_ref, win_ref, scw_ref, scb_ref, dtb_ref, alog_ref,
                          mcw_ref, mcb_ref, wq_ref, wk_ref, wv_ref, wg_ref, bg_ref,
                          sbuf_ref, mbuf_ref, m0_ref, n0_ref,
                          nsb_ref, nmb_ref, zs_ref, xs_ref, bm_ref, cm_ref, xdt_t_ref, dec_t_ref,
                          zm_ref, xc_ref, q_ref, isv_t_ref, fs_t_ref, k_ref, mnew_ref, nnew_ref, den_ref):
    x = x_ref[...]
    hn = _rms(x, nw_ref[...])
    proj = _bdot(hn, win_ref[...])
    zs_ref[...] = proj[:, OFF_ZS:OFF_ZS + SSD_WIDTH]
    zm_ref[...] = proj[:, OFF_ZM:OFF_ZM + ML_WIDTH]
    xbc = proj[:, OFF_XBC:OFF_XBC + SSD_CONV_CH]
    dt_raw = proj[:, OFF_DT:OFF_DT + LANE]
    xm = proj[:, OFF_XM:OFF_XM + ML_WIDTH]
    expand = _expand_matrix()

    xbc = _silu(_conv_step(sbuf_ref, xbc, scw_ref, scb_ref, nsb_ref))
    xs = xbc[:, :SSD_WIDTH]
    xs_ref[...] = xs
    bm_ref[...] = xbc[:, SSD_WIDTH:SSD_WIDTH + SSD_GROUPS * SSD_STATE]
    cm_ref[...] = xbc[:, SSD_WIDTH + SSD_GROUPS * SSD_STATE:]
    dt = _softplus(dt_raw + dtb_ref[...])
    log_a = -dt * jnp.exp(alog_ref[...])
    xdt_t_ref[...] = xs * _expand_heads(dt, expand)
    dec_t_ref[...] = _expand_heads(jnp.exp(log_a), expand)

    xc = _silu(_conv_step(mbuf_ref, xm, mcw_ref, mcb_ref, nmb_ref))
    xc_ref[...] = xc
    q, k, v, ig, logf = _mlstm_qkv_gates(xm, xc, wq_ref, wk_ref, wv_ref, wg_ref, bg_ref)
    m0 = m0_ref[...]
    m_new = jnp.maximum(logf + m0, ig)
    fs = jnp.exp(logf + m0 - m_new)
    is_ = jnp.exp(ig - m_new)
    mnew_ref[...] = m_new
    r = lax.broadcasted_iota(jnp.int32, (LANE, ML_WIDTH), 0)
    cidx = lax.broadcasted_iota(jnp.int32, (LANE, ML_WIDTH), 1)
    expand_m = jnp.where(lax.shift_right_logical(cidx, 8) == r, 1.0, 0.0).astype(BF16)
    fs_e = _expand_heads(fs, expand_m)
    is_e = _expand_heads(is_, expand_m)
    n_new = fs_e * n0_ref[...] + is_e * k
    nnew_ref[...] = n_new
    q_ref[...] = q
    k_ref[...] = k
    isv_t_ref[...] = is_e * v
    fs_t_ref[...] = fs_e
    nq = n_new * q
    floor = jnp.exp(-m_new)
    for hd in range(ML_HEADS):
        den = jnp.sum(nq[:, hd * ML_HEAD_DIM:(hd + 1) * ML_HEAD_DIM], axis=-1, keepdims=True)
        den_ref[:, hd:hd + 1] = jnp.maximum(jnp.abs(den), floor[:, hd:hd + 1])


BT_S = 8
BT_C = 4


def _ssd_state_kernel(s_ref, xdt_ref, dec_ref, bm_ref, cm_ref, snew_ref, y_ref):
    for i in range(BT_S):
        x_col = xdt_ref[0, :, i:i + 1].reshape(SSD_HEADS, SSD_HEAD_DIM, 1)
        d_col = dec_ref[0, :, i:i + 1].reshape(SSD_HEADS, SSD_HEAD_DIM, 1)
        ys = []
        for g in range(SSD_GROUPS):
            hs = slice(g * SSD_HPG, (g + 1) * SSD_HPG)
            b_row = bm_ref[i:i + 1, g * SSD_STATE:(g + 1) * SSD_STATE].reshape(1, 1, SSD_STATE)
            c_row = cm_ref[i:i + 1, g * SSD_STATE:(g + 1) * SSD_STATE].reshape(1, 1, SSD_STATE)
            s_new = d_col[hs] * s_ref[i, hs] + x_col[hs] * b_row
            snew_ref[i, hs] = s_new
            ys.append(jnp.sum(s_new * c_row, axis=-1, keepdims=True))
        y_ref[0, :, i:i + 1] = jnp.concatenate(ys, axis=0).reshape(SSD_WIDTH, 1)


def _mlstm_state_kernel(c_ref, isv_ref, fs_ref, k_ref, q_ref, cnew_ref, num_ref):
    for i in range(BT_C):
        for hd in range(ML_HEADS):
            sl = slice(hd * ML_HEAD_DIM, (hd + 1) * ML_HEAD_DIM)
            v_col = isv_ref[0, sl, i:i + 1]
            f_col = fs_ref[0, sl, i:i + 1]
            c_new = f_col * c_ref[i, hd] + v_col * k_ref[0, i:i + 1, sl]
            cnew_ref[i, hd] = c_new
            num_ref[0, sl, i:i + 1] = jnp.sum(c_new * q_ref[0, i:i + 1, sl], axis=-1, keepdims=True)


def _sample_post_kernel(x_ref, ys_t_ref, num_t_ref, den_ref, zs_ref, xs_ref, zm_ref, xc_ref,
                        dsk_ref, snw_ref, msk_ref, mnw_ref, wout_ref,
                        nw1_ref, fnw_ref, win1_ref, wout1_ref, cw_ref, cb_ref,
                        wa_ref, ba_ref, wx_ref, bx_ref, lam_ref, lbuf_ref, h0_ref,
                        y_ref, nlb_ref, hnew_ref):
    xs = xs_ref[...]
    y_s = ys_t_ref[...] + dsk_ref[...] * xs
    y_s = _group_rmsnorm(y_s * _silu(zs_ref[...]), snw_ref[...])
    num = num_t_ref[...]
    den = den_ref[...]
    h_m = jnp.concatenate(
        [num[:, hd * ML_HEAD_DIM:(hd + 1) * ML_HEAD_DIM] / den[:, hd:hd + 1] for hd in range(ML_HEADS)], axis=-1)
    h_m = _head_layernorm(h_m) * mnw_ref[...]
    y_m = (h_m + msk_ref[...] * xc_ref[...]) * _silu(zm_ref[...])
    h1 = x_ref[...] + _bdot(jnp.concatenate([y_s, y_m], axis=-1), wout_ref[...])

    hn = _rms(h1, nw1_ref[...])
    proj = _bdot(hn, win1_ref[...])
    gate = proj[:, :LRU_WIDTH]
    xr = proj[:, LRU_WIDTH:]
    xc = _conv_step(lbuf_ref, xr, cw_ref, cb_ref, nlb_ref)
    a, u = _rglru_gates(xc, wa_ref, ba_ref, wx_ref, bx_ref, lam_ref)
    h = a * h0_ref[...] + u
    hnew_ref[...] = h
    h2 = h1 + _bdot(h * _silu(gate), wout1_ref[...])
    y_ref[...] = _rms(h2, fnw_ref[...])


def _full_call(kernel_fn, out_shapes, args, name):
    return pl.pallas_call(
        kernel_fn,
        out_shape=[jax.ShapeDtypeStruct(s, F32) for s in out_shapes],
        compiler_params=pltpu.CompilerParams(vmem_limit_bytes=VMEM_LIMIT),
        name=name,
    )(*args)


def _to_cols(a, bt):
    rows, ch = a.shape
    return a.reshape(rows // bt, bt, ch).transpose(0, 2, 1)


def _from_cols(a):
    tiles, ch, bt = a.shape
    return a.transpose(0, 2, 1).reshape(tiles * bt, ch)


def _row(v, width=None):
    v = v.reshape(1, -1).astype(F32)
    if width is not None and v.shape[1] < width:
        v = jnp.pad(v, ((0, 0), (0, width - v.shape[1])))
    return v


def _dense_block_tiles(w):
    nb, bi, bo = w.shape
    per = LANE // bi
    w = w.reshape(nb // per, per, bi, bo)
    eye = jnp.eye(per, dtype=w.dtype)
    dense = jnp.einsum('tpio,pq->tpiqo', w, eye).reshape(nb // per, per * bi, per * bo)
    return dense.astype(BF16)


def kernel(x_prompt, x_sample, state_ssd_conv, state_ssd, state_mlstm_conv, state_mlstm_C, state_mlstm_n,
           state_mlstm_m, state_lru_conv, state_lru_h, meta_tokens, norm_w, final_norm_w, w_in_mix, w_out_mix,
           ssd_conv_w, ssd_conv_b, ssd_dt_bias, ssd_a_log, ssd_d, ssd_norm_w, ml_conv_w, ml_conv_b, ml_wq, ml_wk,
           ml_wv, ml_w_gate, ml_b_gate, ml_skip, ml_norm_w, lru_w_in, lru_w_out, lru_conv_w, lru_conv_b, lru_wa,
           lru_ba, lru_wx, lru_bx, lru_lambda):
    bsz = x_prompt.shape[0]
    dec = x_sample.shape[0]

    w_in = w_in_mix[0]
    o1 = SSD_WIDTH
    o2 = o1 + SSD_CONV_CH
    o3 = o2 + SSD_HEADS
    win = jnp.concatenate([w_in[:, :o2], jnp.pad(w_in[:, o2:o3], ((0, 0), (0, LANE - SSD_HEADS))), w_in[:, o3:]],
                          axis=1).astype(BF16)
    wout = w_out_mix[0].astype(BF16)
    nw0 = _row(norm_w[0])
    nw1 = _row(norm_w[1])
    fnw = _row(final_norm_w)
    scw = ssd_conv_w[0]
    scb = _row(ssd_conv_b[0])
    dtb = _row(ssd_dt_bias[0], LANE)
    alog = _row(ssd_a_log[0], LANE)
    dsk = _row(jnp.repeat(ssd_d[0], SSD_HEAD_DIM))
    snw = _row(ssd_norm_w[0])
    mcw = ml_conv_w[0]
    mcb = _row(ml_conv_b[0])
    wq = _dense_block_tiles(ml_wq[0])
    wk = _dense_block_tiles(ml_wk[0])
    wv = _dense_block_tiles(ml_wv[0])
    wg_raw = ml_w_gate[0]
    wg = jnp.concatenate([jnp.pad(wg_raw[:, :ML_HEADS], ((0, 0), (0, LANE - ML_HEADS))),
                          jnp.pad(wg_raw[:, ML_HEADS:], ((0, 0), (0, LANE - ML_HEADS)))], axis=1).astype(BF16)
    bg = jnp.concatenate([_row(ml_b_gate[0, :ML_HEADS], LANE), _row(ml_b_gate[0, ML_HEADS:], LANE)], axis=1)
    msk = _row(ml_skip[0])
    mnw = _row(ml_norm_w[0])
    win1 = lru_w_in[0].astype(BF16)
    wout1 = lru_w_out[0].astype(BF16)
    lcw = lru_conv_w[0]
    lcb = _row(lru_conv_b[0])
    wa = lru_wa[0].astype(BF16)
    wx = lru_wx[0].astype(BF16)
    ba = _row(lru_ba[0])
    bx = _row(lru_bx[0])
    lam = _row(lru_lambda[0])

    l0_w = (nw0, win, wout, scw, scb, dtb, alog, dsk, snw, mcw, mcb, wq, wk, wv, wg, bg, msk, mnw)
    l1_w = (nw1, fnw, win1, wout1, lcw, lcb, wa, ba, wx, bx, lam)

    zero0 = tuple(jnp.zeros((1,) + s, F32) for s in L0_STATE_SHAPES)
    zero1 = tuple(jnp.zeros((1,) + s, F32) for s in L1_STATE_SHAPES)
    meta = jnp.pad(meta_tokens.astype(F32), ((CHUNK - N_META, 0), (0, 0)))[None]
    meta_out = _l0_prompt(meta, l0_w, zero0, CHUNK - N_META)
    meta1_out = _l1_prompt(meta_out[0], l1_w, zero1, CHUNK - N_META)
    l0_out = _l0_prompt(x_prompt, l0_w, tuple(meta_out[1:]), 0)
    h1_p, p_sc, p_s_t, p_mc, p_ct, p_n, p_m = l0_out
    y_prompt, p_lc, p_lh = _l1_prompt(h1_p, l1_w, tuple(meta1_out[1:]), 0)

    p_s = p_s_t.reshape(bsz, SSD_GROUPS, SSD_STATE, SSD_HPG, SSD_HEAD_DIM).transpose(0, 1, 3, 4, 2)
    p_s = p_s.reshape(bsz, SSD_HEADS, SSD_HEAD_DIM, SSD_STATE)
    p_c = p_ct.transpose(0, 1, 3, 2)
    p_m = p_m[:, 0, :ML_HEADS]
    p_lh = p_lh[:, 0]

    xs2 = x_sample[:, 0]
    sbuf = jnp.moveaxis(state_ssd_conv[0], 1, 0)
    mbuf = jnp.moveaxis(state_mlstm_conv[0], 1, 0)
    lbuf = jnp.moveaxis(state_lru_conv[0], 1, 0)
    m0 = jnp.pad(state_mlstm_m[0], ((0, 0), (0, LANE - ML_HEADS)))
    n0 = state_mlstm_n[0].reshape(dec, ML_WIDTH)
    pre_shapes = ((TAIL, dec, SSD_CONV_CH), (TAIL, dec, ML_WIDTH), (dec, SSD_WIDTH), (dec, SSD_WIDTH),
                  (dec, SSD_GROUPS * SSD_STATE), (dec, SSD_GROUPS * SSD_STATE), (dec, SSD_WIDTH), (dec, SSD_WIDTH),
                  (dec, ML_WIDTH), (dec, ML_WIDTH), (dec, ML_WIDTH), (dec, ML_WIDTH), (dec, ML_WIDTH),
                  (dec, ML_WIDTH), (dec, LANE), (dec, ML_WIDTH), (dec, ML_HEADS))
    (nsb, nmb, zs, xs_c, bm, cm, xdt_t, dec_t, zm, xc_m, q, isv_t, fs_t, k, m_new, n_new, den) = _full_call(
        _l0_sample_pre_kernel, pre_shapes,
        (xs2, nw0, win, scw, scb, dtb, alog, mcw, mcb, wq, wk, wv, wg, bg, sbuf, mbuf, m0, n0), "l0_sample_pre")

    xdt_c = _to_cols(xdt_t, BT_S)
    dec_c = _to_cols(dec_t, BT_S)
    s_new, ys_c = pl.pallas_call(
        _ssd_state_kernel,
        grid=(dec // BT_S,),
        in_specs=[pl.BlockSpec((BT_S, SSD_HEADS, SSD_HEAD_DIM, SSD_STATE), lambda i: (i, 0, 0, 0)),
                  pl.BlockSpec((1, SSD_WIDTH, BT_S), lambda i: (i, 0, 0)),
                  pl.BlockSpec((1, SSD_WIDTH, BT_S), lambda i: (i, 0, 0)),
                  pl.BlockSpec((BT_S, SSD_GROUPS * SSD_STATE), lambda i: (i, 0)),
                  pl.BlockSpec((BT_S, SSD_GROUPS * SSD_STATE), lambda i: (i, 0))],
        out_specs=[pl.BlockSpec((BT_S, SSD_HEADS, SSD_HEAD_DIM, SSD_STATE), lambda i: (i, 0, 0, 0)),
                   pl.BlockSpec((1, SSD_WIDTH, BT_S), lambda i: (i, 0, 0))],
        out_shape=[jax.ShapeDtypeStruct((dec, SSD_HEADS, SSD_HEAD_DIM, SSD_STATE), F32),
                   jax.ShapeDtypeStruct((dec // BT_S, SSD_WIDTH, BT_S), F32)],
        compiler_params=pltpu.CompilerParams(dimension_semantics=("arbitrary",), vmem_limit_bytes=VMEM_LIMIT),
        name="ssd_state",
    )(state_ssd[0], xdt_c, dec_c, bm, cm)

    isv_c = _to_cols(isv_t, BT_C)
    fs_c = _to_cols(fs_t, BT_C)
    c_new, num_c = pl.pallas_call(
        _mlstm_state_kernel,
        grid=(dec // BT_C,),
        in_specs=[pl.BlockSpec((BT_C, ML_HEADS, ML_HEAD_DIM, ML_HEAD_DIM), lambda i: (i, 0, 0, 0)),
                  pl.BlockSpec((1, ML_WIDTH, BT_C), lambda i: (i, 0, 0)),
                  pl.BlockSpec((1, ML_WIDTH, BT_C), lambda i: (i, 0, 0)),
                  pl.BlockSpec((1, BT_C, ML_WIDTH), lambda i: (i, 0, 0)),
                  pl.BlockSpec((1, BT_C, ML_WIDTH), lambda i: (i, 0, 0))],
        out_specs=[pl.BlockSpec((BT_C, ML_HEADS, ML_HEAD_DIM, ML_HEAD_DIM), lambda i: (i, 0, 0, 0)),
                   pl.BlockSpec((1, ML_WIDTH, BT_C), lambda i: (i, 0, 0))],
        out_shape=[jax.ShapeDtypeStruct((dec, ML_HEADS, ML_HEAD_DIM, ML_HEAD_DIM), F32),
                   jax.ShapeDtypeStruct((dec // BT_C, ML_WIDTH, BT_C), F32)],
        compiler_params=pltpu.CompilerParams(dimension_semantics=("arbitrary",), vmem_limit_bytes=VMEM_LIMIT),
        name="mlstm_state",
    )(state_mlstm_C[0], isv_c, fs_c, k.reshape(dec // BT_C, BT_C, ML_WIDTH), q.reshape(dec // BT_C, BT_C, ML_WIDTH))

    post_shapes = ((dec, D_MODEL), (TAIL, dec, LRU_WIDTH), (dec, LRU_WIDTH))
    y_s2, nlb, h_new = _full_call(
        _sample_post_kernel, post_shapes,
        (xs2, _from_cols(ys_c), _from_cols(num_c), den, zs, xs_c, zm, xc_m, dsk, snw, msk, mnw, wout,
         nw1, fnw, win1, wout1, lcw, lcb, wa, ba, wx, bx, lam, lbuf, state_lru_h[0]), "sample_post")

    s_sc = jnp.moveaxis(nsb, 0, 1)[None]
    s_mc = jnp.moveaxis(nmb, 0, 1)[None]
    s_lc = jnp.moveaxis(nlb, 0, 1)[None]
    return (y_prompt, y_s2[:, None, :],
            p_sc[None], p_s[None], p_mc[None], p_c[None], p_n[None], p_m[None], p_lc[None], p_lh[None],
            s_sc, s_new[None], s_mc, c_new[None], n_new.reshape(dec, ML_HEADS, ML_HEAD_DIM)[None],
            m_new[:, :ML_HEADS][None], s_lc, h_new[None])
```

```python
import functools

import jax
import jax.numpy as jnp
from jax import lax
from jax.experimental import pallas as pl
from jax.experimental.pallas import tpu as pltpu

F32 = jnp.float32
BF16 = jnp.bfloat16

D_MODEL = 1024
N_META = 16
CONV_W = 4
EPS = 1e-6
NEG = -1e30
SSD_WIDTH = 1024
SSD_HEAD_DIM = 64
SSD_HEADS = 16
SSD_GROUPS = 2
SSD_HPG = 8
SSD_STATE = 128
SSD_CONV_CH = 1536
ML_WIDTH = 1024
ML_HEADS = 4
ML_HEAD_DIM = 256
ML_QKV_BLOCK = 4
LRU_WIDTH = 2048
LRU_BLOCKS = 16
LRU_BLOCK = 128
LRU_C = 8.0

LANE = 128
SUBLANE = 8
CHUNK = 128
L0_ROWS = 1
L1_ROWS = 2
L1_GROUPS = 4
TAIL = CONV_W - 1

OFF_ZS = 0
OFF_XBC = OFF_ZS + SSD_WIDTH
OFF_DT = OFF_XBC + SSD_CONV_CH
OFF_ZM = OFF_DT + LANE
OFF_XM = OFF_ZM + ML_WIDTH
IN_MIX_PAD = OFF_XM + ML_WIDTH

VMEM_LIMIT = 56 * 1024 * 1024


def _sigmoid(x):
    return 1.0 / (1.0 + jnp.exp(-x))


def _silu(x):
    return x * _sigmoid(x)


def _softplus(x):
    return jnp.maximum(x, 0.0) + jnp.log1p(jnp.exp(-jnp.abs(x)))


def _rms(x, w):
    return x * lax.rsqrt(jnp.mean(x * x, axis=-1, keepdims=True) + EPS) * w


def _bdot(a, b):
    return jnp.dot(a.astype(BF16), b.astype(BF16), preferred_element_type=F32)


def _wload(w):
    return pltpu.bitcast(w, BF16)


def _split3(x):
    hi = x.astype(BF16)
    r = x - hi.astype(F32)
    mid = r.astype(BF16)
    lo = (r - mid.astype(F32)).astype(BF16)
    return hi, mid, lo


def _cumsum_rows(x, tril):
    hi, mid, lo = _split3(x)
    d = functools.partial(jnp.dot, preferred_element_type=F32)
    return d(tril, hi) + d(tril, mid) + d(tril, lo)


def _expand_heads(x, expand):
    hi, mid, _ = _split3(x)
    d = functools.partial(jnp.dot, preferred_element_type=F32)
    return d(hi, expand) + d(mid, expand)


def _tril(n):
    r = lax.broadcasted_iota(jnp.int32, (n, n), 0)
    c = lax.broadcasted_iota(jnp.int32, (n, n), 1)
    return r >= c


def _expand_matrix():
    r = lax.broadcasted_iota(jnp.int32, (LANE, SSD_WIDTH), 0)
    c = lax.broadcasted_iota(jnp.int32, (LANE, SSD_WIDTH), 1)
    return jnp.where(lax.shift_right_logical(c, 6) == r, 1.0, 0.0).astype(BF16)


def _conv_chunk(tail_ref, x, w_ref, b_ref):
    n, ch = x.shape
    nb = n // SUBLANE
    x3 = x.reshape(nb, SUBLANE, ch)
    prev8 = tail_ref[...].reshape(1, SUBLANE, ch)
    tail_ref[...] = x[n - SUBLANE:, :]
    blocks = jnp.concatenate([prev8, x3], axis=0)
    sub = lax.broadcasted_iota(jnp.int32, (1, SUBLANE, ch), 1)
    y = b_ref[...].reshape(1, 1, ch) + w_ref[TAIL:TAIL + 1, :].reshape(1, 1, ch) * x3
    for back in range(1, CONV_W):
        rot = pltpu.roll(blocks, back, 1)
        shifted = jnp.where(sub < back, rot[:nb], rot[1:])
        y = y + w_ref[TAIL - back:TAIL - back + 1, :].reshape(1, 1, ch) * shifted
    return y.reshape(n, ch)


def _blockdiag_tiles(x, w_ref):
    k = w_ref.shape[0]
    m = w_ref.shape[2] // LANE
    prods = [_bdot(x[:, t * LANE:(t + 1) * LANE], _wload(w_ref[t])) for t in range(k)]
    return [jnp.concatenate([p[:, j * LANE:(j + 1) * LANE] for p in prods], axis=-1) for j in range(m)]


def _group_rmsnorm(y, w):
    half = SSD_WIDTH // SSD_GROUPS
    parts = []
    for g in range(SSD_GROUPS):
        yg = y[:, g * half:(g + 1) * half]
        parts.append(yg * lax.rsqrt(jnp.mean(yg * yg, axis=-1, keepdims=True) + EPS))
    return jnp.concatenate(parts, axis=-1) * w


def _head_layernorm(h):
    parts = []
    for k in range(ML_HEADS):
        hk = h[:, k * ML_HEAD_DIM:(k + 1) * ML_HEAD_DIM]
        mu = jnp.mean(hk, axis=-1, keepdims=True)
        d = hk - mu
        var = jnp.mean(d * d, axis=-1, keepdims=True)
        parts.append(d * lax.rsqrt(var + EPS))
    return jnp.concatenate(parts, axis=-1)


def _mlstm_qkv_gates(xm, xc, wqk_ref, wv_ref, wg_ref, bg_ref):
    q, k = _blockdiag_tiles(xc, wqk_ref)
    v, = _blockdiag_tiles(xm, wv_ref)
    gates = _bdot(jnp.concatenate([q, k, v], axis=-1), _wload(wg_ref[...])) + bg_ref[...]
    ig = gates[:, :LANE]
    logf = -_softplus(-gates[:, LANE:])
    return q, k * (ML_HEAD_DIM ** -0.5), v, ig, logf


N_L0_W = 18
N_L0_S = 6


def _l0_prompt_kernel(x_ref, *refs, front_pad, rows):
    w_refs = refs[:N_L0_W]
    init_refs = refs[N_L0_W:N_L0_W + N_L0_S]
    out_refs = refs[N_L0_W + N_L0_S:N_L0_W + 2 * N_L0_S + 1]
    scratch = refs[N_L0_W + 2 * N_L0_S + 1:]
    c = pl.program_id(1)

    def each_row(phase):
        for r in range(rows):
            _l0_prompt_row(x_ref.at[r], *w_refs, *init_refs, *(o.at[r] for o in out_refs),
                           *(s.at[r] for s in scratch), front_pad=front_pad, phase=phase)

    pl.when(c == 0)(functools.partial(each_row, "init"))
    each_row("body")
    pl.when(c == pl.num_programs(1) - 1)(functools.partial(each_row, "final"))


def _l0_prompt_row(x_ref, nw_ref, win_ref, wout_ref,
                   scw_ref, scb_ref, dtb_ref, alog_ref, dsk_ref, snw_ref,
                   mcw_ref, mcb_ref, wqk_ref, wv_ref, wg_ref, bg_ref, msk_ref, mnw_ref, expand_ref,
                   isc_ref, iss_ref, imc_ref, ict_ref, inn_ref, imm_ref,
                   h1_ref, osc_ref, oss_ref, omc_ref, oct_ref, onn_ref, omm_ref,
                   sbuf, mbuf, s_st, ct_st, n_st, m_st, *, front_pad, phase):
    q_len = x_ref.shape[0]

    if phase == "init":
        sbuf[...] = jnp.zeros(sbuf.shape, F32)
        mbuf[...] = jnp.zeros(mbuf.shape, F32)
        sbuf[SUBLANE - TAIL:SUBLANE, :] = isc_ref[0]
        mbuf[SUBLANE - TAIL:SUBLANE, :] = imc_ref[0]
        s_st[...] = iss_ref[0]
        ct_st[...] = ict_ref[0]
        n_st[...] = inn_ref[0]
        m_st[...] = imm_ref[0]
        return
    if phase == "final":
        osc_ref[...] = sbuf[SUBLANE - TAIL:SUBLANE, :]
        omc_ref[...] = mbuf[SUBLANE - TAIL:SUBLANE, :]
        oss_ref[...] = s_st[...]
        oct_ref[...] = ct_st[...]
        onn_ref[...] = n_st[...]
        omm_ref[...] = m_st[...]
        return

    x = x_ref[...]
    hn = _rms(x, nw_ref[...])
    proj = _bdot(hn, _wload(win_ref[...]))
    z_s = proj[:, OFF_ZS:OFF_ZS + SSD_WIDTH]
    xbc = proj[:, OFF_XBC:OFF_XBC + SSD_CONV_CH]
    dt_raw = proj[:, OFF_DT:OFF_DT + LANE]
    z_m = proj[:, OFF_ZM:OFF_ZM + ML_WIDTH]
    xm = proj[:, OFF_XM:OFF_XM + ML_WIDTH]

    causal = _tril(q_len)
    tril = jnp.where(causal, 1.0, 0.0).astype(BF16)
    if front_pad:
        valid = lax.broadcasted_iota(jnp.int32, (q_len, 1), 0) >= front_pad

    xbc = _silu(_conv_chunk(sbuf, xbc, scw_ref, scb_ref))
    xs = xbc[:, :SSD_WIDTH]
    bm = xbc[:, SSD_WIDTH:SSD_WIDTH + SSD_GROUPS * SSD_STATE]
    cm = xbc[:, SSD_WIDTH + SSD_GROUPS * SSD_STATE:]
    dt = _softplus(dt_raw + dtb_ref[...])
    if front_pad:
        dt = jnp.where(valid, dt, 0.0)
    log_a = -dt * jnp.exp(alog_ref[...])
    a_cs = _cumsum_rows(log_a, tril)
    a_last = a_cs[q_len - 1:q_len, :]
    expand = _wload(expand_ref[...])
    w_state = _expand_heads(dt * jnp.exp(a_last - a_cs), expand)
    e_acs = _expand_heads(jnp.exp(a_cs), expand)
    a_cs_t = a_cs.T
    dt_t = dt.T
    pair_lo = lax.broadcasted_iota(jnp.int32, (q_len, LANE), 1) < SSD_HEAD_DIM
    half = SSD_WIDTH // SSD_GROUPS
    y_groups = []
    for g in range(SSD_GROUPS):
        bg = bm[:, g * SSD_STATE:(g + 1) * SSD_STATE]
        cg = cm[:, g * SSD_STATE:(g + 1) * SSD_STATE]
        bg_t = bg.T
        xg = xs[:, g * half:(g + 1) * half]
        eg = e_acs[:, g * half:(g + 1) * half]
        s_old = s_st[g]
        cb = _bdot(cg, bg_t)
        y_off = _bdot(cg, s_old) * eg
        s_st[g] = eg[q_len - 1:q_len, :] * s_old + _bdot(bg_t, xg * w_state[:, g * half:(g + 1) * half])
        y_pairs = []
        for pr in range(SSD_HPG // 2):
            ms = []
            for e in (2 * pr, 2 * pr + 1):
                hd = g * SSD_HPG + e
                seg = jnp.exp(jnp.where(causal, a_cs[:, hd:hd + 1] - a_cs_t[hd:hd + 1, :], -jnp.inf))
                ms.append(cb * seg * dt_t[hd:hd + 1, :])
            xp = xg[:, pr * LANE:(pr + 1) * LANE]
            rhs = jnp.concatenate([jnp.where(pair_lo, xp, 0.0), jnp.where(pair_lo, 0.0, xp)], axis=0)
            y_pairs.append(_bdot(jnp.concatenate(ms, axis=-1), rhs))
        y_groups.append(jnp.concatenate(y_pairs, axis=-1) + y_off)
    y_s = jnp.concatenate(y_groups, axis=-1) + dsk_ref[...] * xs
    y_s = _group_rmsnorm(y_s * _silu(z_s), snw_ref[...])

    xc = _silu(_conv_chunk(mbuf, xm, mcw_ref, mcb_ref))
    q, k, v, ig, logf = _mlstm_qkv_gates(xm, xc, wqk_ref, wv_ref, wg_ref, bg_ref)
    if front_pad:
        ig = jnp.where(valid, ig, NEG)
        logf = jnp.where(valid, logf, 0.0)
    bcum = _cumsum_rows(logf, tril)
    ftot = bcum[q_len - 1:q_len, :]
    m_prev = m_st[...]
    w_end = ftot - bcum + ig
    m_new = jnp.maximum(ftot + m_prev, jnp.max(w_end, axis=0, keepdims=True))
    sc = jnp.exp(ftot + m_prev - m_new)
    wexp = jnp.exp(w_end - m_new)
    inter = bcum + m_prev
    bcum_t = bcum.T
    ig_t = ig.T
    h_heads = []
    for hd in range(ML_HEADS):
        sl = slice(hd * ML_HEAD_DIM, (hd + 1) * ML_HEAD_DIM)
        q_h, k_h, v_h = q[:, sl], k[:, sl], v[:, sl]
        k_t = k_h.T
        dmat = jnp.where(causal, bcum[:, hd:hd + 1] - bcum_t[hd:hd + 1, :] + ig_t[hd:hd + 1, :], -jnp.inf)
        inter_h = inter[:, hd:hd + 1]
        m_t = jnp.maximum(inter_h, jnp.max(dmat, axis=-1, keepdims=True))
        dexp = jnp.exp(dmat - m_t)
        inter_sc = jnp.exp(inter_h - m_t)
        s = _bdot(q_h, k_t) * dexp
        ct_old = ct_st[hd]
        n_old = n_st[hd:hd + 1, :]
        num = _bdot(s, v_h) + inter_sc * _bdot(q_h, ct_old)
        den = jnp.sum(s, axis=-1, keepdims=True) + inter_sc * jnp.sum(q_h * n_old, axis=-1, keepdims=True)
        h_heads.append(num / jnp.maximum(jnp.abs(den), jnp.exp(-m_t)))
        w_col = wexp[:, hd:hd + 1]
        sc_h = sc[:, hd:hd + 1]
        ct_st[hd] = sc_h * ct_old + _bdot(k_t, v_h * w_col)
        n_st[hd:hd + 1, :] = sc_h * n_old + jnp.sum(k_h * w_col, axis=0, keepdims=True)
    m_st[...] = m_new
    h_m = _head_layernorm(jnp.concatenate(h_heads, axis=-1)) * mnw_ref[...]
    y_m = (h_m + msk_ref[...] * xc) * _silu(z_m)

    h1 = x + _bdot(jnp.concatenate([y_s, y_m], axis=-1), _wload(wout_ref[...]))
    if front_pad:
        h1 = jnp.where(valid, h1, 0.0)
    h1_ref[...] = h1


def _const_spec(shape):
    nd = len(shape)
    return pl.BlockSpec(shape, lambda b, c: (0,) * nd)


def _state_spec(shape, rows):
    nd = len(shape)
    if rows:
        return pl.BlockSpec((rows,) + shape, lambda b, c: (b,) + (0,) * nd)
    return pl.BlockSpec((1,) + shape, lambda b, c: (0,) * (nd + 1))


def _rows_per_step(bsz, want):
    return want if bsz % want == 0 else 1


L0_STATE_SHAPES = ((TAIL, SSD_CONV_CH), (SSD_GROUPS, SSD_STATE, SSD_WIDTH // SSD_GROUPS), (TAIL, ML_WIDTH),
                   (ML_HEADS, ML_HEAD_DIM, ML_HEAD_DIM), (ML_HEADS, ML_HEAD_DIM), (1, LANE))


def _l0_prompt(x, weights, init, front_pad):
    bsz, length, _ = x.shape
    q_len = min(CHUNK, length)
    assert length % q_len == 0
    rows = _rows_per_step(bsz, L0_ROWS)
    assert len(weights) == N_L0_W and len(init) == N_L0_S
    grid = (bsz // rows, length // q_len)
    x_spec = pl.BlockSpec((rows, q_len, D_MODEL), lambda b, c: (b, c, 0))
    in_specs = ([x_spec] + [_const_spec(w.shape) for w in weights]
                + [_state_spec(s, 0) for s in L0_STATE_SHAPES])
    out_shape = ([jax.ShapeDtypeStruct((bsz, length, D_MODEL), F32)]
                 + [jax.ShapeDtypeStruct((bsz,) + s, F32) for s in L0_STATE_SHAPES])
    out_specs = [x_spec] + [_state_spec(s, rows) for s in L0_STATE_SHAPES]
    scratch = [pltpu.VMEM((rows,) + s, F32) for s in
               ((SUBLANE, SSD_CONV_CH), (SUBLANE, ML_WIDTH), L0_STATE_SHAPES[1], L0_STATE_SHAPES[3],
                L0_STATE_SHAPES[4], L0_STATE_SHAPES[5])]
    return pl.pallas_call(
        functools.partial(_l0_prompt_kernel, front_pad=front_pad, rows=rows),
        grid=grid, in_specs=in_specs, out_specs=out_specs, out_shape=out_shape, scratch_shapes=scratch,
        compiler_params=pltpu.CompilerParams(dimension_semantics=("arbitrary", "arbitrary"),
                                             vmem_limit_bytes=VMEM_LIMIT),
        name="l0_prompt",
    )(x, *weights, *init)


def _rglru_gates(xc, ra, ix, ba_ref, bx_ref, lam_ref):
    r = _sigmoid(ra + ba_ref[...])
    i = _sigmoid(ix + bx_ref[...])
    log_a = r * (-LRU_C * _softplus(-lam_ref[...]))
    a = jnp.exp(log_a)
    u = jnp.sqrt(1.0 - a * a) * (i * xc)
    return a, u


def _perm_time(n):
    p = lax.broadcasted_iota(jnp.int32, (n, 1), 0)
    return (n // SUBLANE) * (p & (SUBLANE - 1)) + lax.shift_right_logical(p, 3)


def _perm_matrices(n):
    nb = n // SUBLANE
    r = lax.broadcasted_iota(jnp.int32, (n, n), 0)
    c = lax.broadcasted_iota(jnp.int32, (n, n), 1)
    to_perm = jnp.where(c == nb * (r & (SUBLANE - 1)) + lax.shift_right_logical(r, 3), 1.0, 0.0)
    to_time = jnp.where(r == nb * (c & (SUBLANE - 1)) + lax.shift_right_logical(c, 3), 1.0, 0.0)
    return to_perm.astype(BF16), to_time.astype(BF16)


def _move_rows(sel, x_bf16):
    return jnp.dot(sel, x_bf16, preferred_element_type=F32).astype(BF16)


def _conv_perm(tail_ref, x, w_ref, b_ref):
    n, ch = x.shape
    nb = n // SUBLANE
    x3 = x.reshape(nb, SUBLANE, ch)
    tail8 = tail_ref[...]
    sub = lax.broadcasted_iota(jnp.int32, (SUBLANE, ch), 0)
    y = b_ref[...].reshape(1, 1, ch) + w_ref[TAIL:TAIL + 1, :].reshape(1, 1, ch) * x3
    wrapped = [jnp.where(sub >= 1, pltpu.roll(x3[nb - d], 1, 0), tail8[SUBLANE - d:SUBLANE - d + 1, :])
               for d in range(1, CONV_W)]
    for back in range(1, CONV_W):
        head = jnp.stack([wrapped[back - j - 1] for j in range(back)], axis=0)
        shifted = jnp.concatenate([head, x3[:nb - back]], axis=0)
        y = y + w_ref[TAIL - back:TAIL - back + 1, :].reshape(1, 1, ch) * shifted
    for d in range(1, CONV_W):
        tail_ref[SUBLANE - d:SUBLANE - d + 1, :] = x3[nb - d][SUBLANE - 1:SUBLANE, :]
    return y.reshape(n, ch)


def _scan_perm(a, u, h_prev):
    n, ch = a.shape
    nb = n // SUBLANE
    a3 = a.reshape(nb, SUBLANE, ch)
    u3 = u.reshape(nb, SUBLANE, ch)
    local = [u3[0]]
    decay = [a3[0]]
    for j in range(1, nb):
        local.append(a3[j] * local[-1] + u3[j])
        decay.append(a3[j] * decay[-1])
    seg_u, seg_a = local[-1], decay[-1]
    sub = lax.broadcasted_iota(jnp.int32, (SUBLANE, ch), 0)
    shift = 1
    while shift < SUBLANE:
        keep = sub >= shift
        seg_u = seg_u + seg_a * jnp.where(keep, pltpu.roll(seg_u, shift, 0), 0.0)
        seg_a = seg_a * jnp.where(keep, pltpu.roll(seg_a, shift, 0), 1.0)
        shift *= 2
    seg_end = seg_a * h_prev + seg_u
    carry = jnp.where(sub >= 1, pltpu.roll(seg_end, 1, 0), h_prev)
    h3 = jnp.stack([local[j] + decay[j] * carry for j in range(nb)], axis=0)
    return h3.reshape(n, ch), seg_end[SUBLANE - 1:SUBLANE, :]


N_L1_W = 10
N_L1_S = 2


def _l1_prompt_kernel(h_ref, hnext_ref, *refs, front_pad, rows):
    w_refs = refs[:N_L1_W]
    init_refs = refs[N_L1_W:N_L1_W + N_L1_S]
    out_refs = refs[N_L1_W + N_L1_S:N_L1_W + 2 * N_L1_S + 1]
    scratch = refs[N_L1_W + 2 * N_L1_S + 1:]
    c = pl.program_id(1)

    def each_row(phase):
        return [_l1_prompt_row(h_ref.at[r], hnext_ref.at[r], *w_refs, *init_refs, *(o.at[r] for o in out_refs),
                               *(s.at[r] for s in scratch), front_pad=front_pad, phase=phase)
                for r in range(rows)]

    @pl.when(c == 0)
    def _():
        each_row("init")

    bodies = each_row("body")
    _run_staggered([gen for gens, _ in bodies for gen in gens])
    for _, finish in bodies:
        finish()

    @pl.when(c == pl.num_programs(1) - 1)
    def _():
        each_row("final")


def _l1_in_proj(h_val, nw_ref, win_ref, to_perm):
    hn = _move_rows(to_perm, _rms(h_val, nw_ref[...]).astype(BF16))
    return _bdot(hn, _wload(win_ref[...]))


def _l1_prompt_row(h_ref, hnext_ref, nw_ref, fnw_ref, win_ref, wout_ref, cw_ref, cb_ref,
                   wax_ref, ba_ref, bx_ref, lam_ref, ilc_ref, ilh_ref,
                   y_ref, olc_ref, olh_ref, lbuf, h_st, proj_s, *, front_pad, phase):
    if phase == "init":
        lbuf[...] = jnp.zeros(lbuf.shape, F32)
        lbuf[SUBLANE - TAIL:SUBLANE, :] = ilc_ref[0]
        h_st[...] = ilh_ref[0]
        proj_s[...] = _l1_in_proj(h_ref[...], nw_ref, win_ref, _perm_matrices(h_ref.shape[0])[0])
        return None
    if phase == "final":
        olc_ref[...] = lbuf[SUBLANE - TAIL:SUBLANE, :]
        olh_ref[...] = h_st[...]
        return None
    return _l1_row_body(h_ref, hnext_ref, nw_ref, fnw_ref, win_ref, wout_ref, cw_ref, cb_ref,
                        wax_ref, ba_ref, bx_ref, lam_ref, y_ref, lbuf, h_st, proj_s, front_pad)


def _l1_row_body(h_ref, hnext_ref, nw_ref, fnw_ref, win_ref, wout_ref, cw_ref, cb_ref,
                 wax_ref, ba_ref, bx_ref, lam_ref, y_ref, lbuf, h_st, proj_s, front_pad):
    q_len = h_ref.shape[0]
    h_in = h_ref[...]
    to_perm, to_time = _perm_matrices(q_len)
    hn_next = _move_rows(to_perm, _rms(hnext_ref[...], nw_ref[...]).astype(BF16))
    if front_pad:
        valid = _perm_time(q_len) >= front_pad
    gw = LRU_WIDTH // L1_GROUPS
    tiles = gw // LANE
    partial_out = []

    def group(g):
        cg = slice(g * gw, (g + 1) * gw)
        cx = slice(LRU_WIDTH + g * gw, LRU_WIDTH + (g + 1) * gw)
        gate = proj_s[:, cg]
        xr = proj_s[:, cx]
        xc = _conv_perm(lbuf.at[:, cg], xr, cw_ref.at[:, cg], cb_ref.at[:, cg])
        ra, ix = _blockdiag_tiles(xc, wax_ref.at[g * tiles:(g + 1) * tiles])
        yield
        proj_s[:, cg] = _bdot(hn_next, _wload(win_ref[:, cg]))
        proj_s[:, cx] = _bdot(hn_next, _wload(win_ref[:, cx]))
        a, u = _rglru_gates(xc, ra, ix, ba_ref.at[:, cg], bx_ref.at[:, cg], lam_ref.at[:, cg])
        if front_pad:
            a = jnp.where(valid, a, 1.0)
            u = jnp.where(valid, u, 0.0)
        yield
        h, h_last = _scan_perm(a, u, h_st[:, cg])
        h_st[:, cg] = h_last
        yield
        y = _move_rows(to_time, (h * _silu(gate)).astype(BF16))
        partial_out.append(_bdot(y, _wload(wout_ref[g * gw // 2:(g + 1) * gw // 2, :])))

    def finish():
        h2 = h_in
        for part in partial_out:
            h2 = h2 + part
        y_ref[...] = _rms(h2, fnw_ref[...])

    return [group(g) for g in range(L1_GROUPS)], finish


def _run_staggered(gens):
    live = []
    pending = list(gens)
    while pending or live:
        if pending:
            live.append(pending.pop(0))
        for gen in list(live):
            if next(gen, "done") == "done":
                live.remove(gen)


L1_STATE_SHAPES = ((TAIL, LRU_WIDTH), (1, LRU_WIDTH))


def _l1_prompt(h1, weights, init, front_pad):
    bsz, length, _ = h1.shape
    q_len = min(CHUNK, length)
    assert length % q_len == 0
    rows = _rows_per_step(bsz, L1_ROWS)
    assert len(weights) == N_L1_W and len(init) == N_L1_S
    grid = (bsz // rows, length // q_len)
    last = length // q_len - 1
    x_spec = pl.BlockSpec((rows, q_len, D_MODEL), lambda b, c: (b, c, 0))
    next_spec = pl.BlockSpec((rows, q_len, D_MODEL), lambda b, c: (b, jnp.minimum(c + 1, last), 0))
    in_specs = ([x_spec, next_spec] + [_const_spec(w.shape) for w in weights]
                + [_state_spec(s, 0) for s in L1_STATE_SHAPES])
    out_shape = ([jax.ShapeDtypeStruct((bsz, length, D_MODEL), F32)]
                 + [jax.ShapeDtypeStruct((bsz,) + s, F32) for s in L1_STATE_SHAPES])
    out_specs = [x_spec] + [_state_spec(s, rows) for s in L1_STATE_SHAPES]
    scratch = [pltpu.VMEM((rows, SUBLANE, LRU_WIDTH), F32), pltpu.VMEM((rows, 1, LRU_WIDTH), F32),
               pltpu.VMEM((rows, q_len, 2 * LRU_WIDTH), F32)]
    return pl.pallas_call(
        functools.partial(_l1_prompt_kernel, front_pad=front_pad, rows=rows),
        grid=grid, in_specs=in_specs, out_specs=out_specs, out_shape=out_shape, scratch_shapes=scratch,
        compiler_params=pltpu.CompilerParams(dimension_semantics=("arbitrary", "arbitrary"),
                                             vmem_limit_bytes=VMEM_LIMIT),
        name="l1_prompt",
    )(h1, h1, *weights, *init)


def _conv_step(buf_ref, x, w_ref, b_ref, newbuf_ref):
    y = b_ref[...] + w_ref[3:4, :] * x
    for tap in range(TAIL):
        y = y + w_ref[tap:tap + 1, :] * buf_ref[tap]
    for tap in range(TAIL - 1):
        newbuf_ref[tap] = buf_ref[tap + 1]
    newbuf_ref[TAIL - 1] = x
    return y


def _l0_sample_pre_kernel(x_ref, nw_ref, win_ref, scw_ref, scb_ref, dtb_ref, alog_ref,
                          mcw_ref, mcb_ref, wqk_ref, wv_ref, wg_ref, bg_ref,
                          sbuf_ref, mbuf_ref, m0_ref, n0_ref,
                          nsb_ref, nmb_ref, zs_ref, xs_ref, bm_ref, cm_ref, xdt_t_ref, dec_t_ref,
                          zm_ref, xc_ref, q_ref, isv_t_ref, fs_t_ref, k_ref, mnew_ref, nnew_ref, den_ref):
    x = x_ref[...]
    hn = _rms(x, nw_ref[...])
    proj = _bdot(hn, _wload(win_ref[...]))
    zs_ref[...] = proj[:, OFF_ZS:OFF_ZS + SSD_WIDTH]
    zm_ref[...] = proj[:, OFF_ZM:OFF_ZM + ML_WIDTH]
    xbc = proj[:, OFF_XBC:OFF_XBC + SSD_CONV_CH]
    dt_raw = proj[:, OFF_DT:OFF_DT + LANE]
    xm = proj[:, OFF_XM:OFF_XM + ML_WIDTH]
    expand = _expand_matrix()

    xbc = _silu(_conv_step(sbuf_ref, xbc, scw_ref, scb_ref, nsb_ref))
    xs = xbc[:, :SSD_WIDTH]
    xs_ref[...] = xs
    bm_ref[...] = xbc[:, SSD_WIDTH:SSD_WIDTH + SSD_GROUPS * SSD_STATE]
    cm_ref[...] = xbc[:, SSD_WIDTH + SSD_GROUPS * SSD_STATE:]
    dt = _softplus(dt_raw + dtb_ref[...])
    log_a = -dt * jnp.exp(alog_ref[...])
    xdt_t_ref[...] = xs * _expand_heads(dt, expand)
    dec_t_ref[...] = _expand_heads(jnp.exp(log_a), expand)

    xc = _silu(_conv_step(mbuf_ref, xm, mcw_ref, mcb_ref, nmb_ref))
    xc_ref[...] = xc
    q, k, v, ig, logf = _mlstm_qkv_gates(xm, xc, wqk_ref, wv_ref, wg_ref, bg_ref)
    m0 = m0_ref[...]
    m_new = jnp.maximum(logf + m0, ig)
    fs = jnp.exp(logf + m0 - m_new)
    is_ = jnp.exp(ig - m_new)
    mnew_ref[...] = m_new
    r = lax.broadcasted_iota(jnp.int32, (LANE, ML_WIDTH), 0)
    cidx = lax.broadcasted_iota(jnp.int32, (LANE, ML_WIDTH), 1)
    expand_m = jnp.where(lax.shift_right_logical(cidx, 8) == r, 1.0, 0.0).astype(BF16)
    fs_e = _expand_heads(fs, expand_m)
    is_e = _expand_heads(is_, expand_m)
    n_new = fs_e * n0_ref[...] + is_e * k
    nnew_ref[...] = n_new
    q_ref[...] = q
    k_ref[...] = k
    isv_t_ref[...] = is_e * v
    fs_t_ref[...] = fs_e
    nq = n_new * q
    floor = jnp.exp(-m_new)
    for hd in range(ML_HEADS):
        den = jnp.sum(nq[:, hd * ML_HEAD_DIM:(hd + 1) * ML_HEAD_DIM], axis=-1, keepdims=True)
        den_ref[:, hd:hd + 1] = jnp.maximum(jnp.abs(den), floor[:, hd:hd + 1])


BT_S = 8
BT_C = 4


def _ssd_state_kernel(s_ref, xdt_ref, dec_ref, bm_ref, cm_ref, snew_ref, y_ref):
    for i in range(BT_S):
        x_col = xdt_ref[0, :, i:i + 1].reshape(SSD_HEADS, SSD_HEAD_DIM, 1)
        d_col = dec_ref[0, :, i:i + 1].reshape(SSD_HEADS, SSD_HEAD_DIM, 1)
        ys = []
        for g in range(SSD_GROUPS):
            hs = slice(g * SSD_HPG, (g + 1) * SSD_HPG)
            b_row = bm_ref[i:i + 1, g * SSD_STATE:(g + 1) * SSD_STATE].reshape(1, 1, SSD_STATE)
            c_row = cm_ref[i:i + 1, g * SSD_STATE:(g + 1) * SSD_STATE].reshape(1, 1, SSD_STATE)
            s_new = d_col[hs] * s_ref[i, hs] + x_col[hs] * b_row
            snew_ref[i, hs] = s_new
            ys.append(jnp.sum(s_new * c_row, axis=-1, keepdims=True))
        y_ref[0, :, i:i + 1] = jnp.concatenate(ys, axis=0).reshape(SSD_WIDTH, 1)


def _mlstm_state_kernel(c_ref, isv_ref, fs_ref, k_ref, q_ref, cnew_ref, num_ref):
    for i in range(BT_C):
        for hd in range(ML_HEADS):
            sl = slice(hd * ML_HEAD_DIM, (hd + 1) * ML_HEAD_DIM)
            v_col = isv_ref[0, sl, i:i + 1]
            f_col = fs_ref[0, sl, i:i + 1]
            c_new = f_col * c_ref[i, hd] + v_col * k_ref[0, i:i + 1, sl]
            cnew_ref[i, hd] = c_new
            num_ref[0, sl, i:i + 1] = jnp.sum(c_new * q_ref[0, i:i + 1, sl], axis=-1, keepdims=True)


def _sample_post_kernel(x_ref, ys_t_ref, num_t_ref, den_ref, zs_ref, xs_ref, zm_ref, xc_ref,
                        dsk_ref, snw_ref, msk_ref, mnw_ref, wout_ref,
                        nw1_ref, fnw_ref, win1_ref, wout1_ref, cw_ref, cb_ref,
                        wax_ref, ba_ref, bx_ref, lam_ref, lbuf_ref, h0_ref,
                        y_ref, nlb_ref, hnew_ref):
    xs = xs_ref[...]
    y_s = ys_t_ref[...] + dsk_ref[...] * xs
    y_s = _group_rmsnorm(y_s * _silu(zs_ref[...]), snw_ref[...])
    num = num_t_ref[...]
    den = den_ref[...]
    h_m = jnp.concatenate(
        [num[:, hd * ML_HEAD_DIM:(hd + 1) * ML_HEAD_DIM] / den[:, hd:hd + 1] for hd in range(ML_HEADS)], axis=-1)
    h_m = _head_layernorm(h_m) * mnw_ref[...]
    y_m = (h_m + msk_ref[...] * xc_ref[...]) * _silu(zm_ref[...])
    h1 = x_ref[...] + _bdot(jnp.concatenate([y_s, y_m], axis=-1), _wload(wout_ref[...]))

    hn = _rms(h1, nw1_ref[...])
    proj = _bdot(hn, _wload(win1_ref[...]))
    gate = proj[:, :LRU_WIDTH]
    xr = proj[:, LRU_WIDTH:]
    xc = _conv_step(lbuf_ref, xr, cw_ref, cb_ref, nlb_ref)
    ra, ix = _blockdiag_tiles(xc, wax_ref)
    a, u = _rglru_gates(xc, ra, ix, ba_ref, bx_ref, lam_ref)
    h = a * h0_ref[...] + u
    hnew_ref[...] = h
    h2 = h1 + _bdot(h * _silu(gate), _wload(wout1_ref[...]))
    y_ref[...] = _rms(h2, fnw_ref[...])


def _full_call(kernel_fn, out_shapes, args, name):
    return pl.pallas_call(
        kernel_fn,
        out_shape=[jax.ShapeDtypeStruct(s, F32) for s in out_shapes],
        compiler_params=pltpu.CompilerParams(vmem_limit_bytes=VMEM_LIMIT),
        name=name,
    )(*args)


def _to_cols(a, bt):
    rows, ch = a.shape
    return a.reshape(rows // bt, bt, ch).transpose(0, 2, 1)


def _from_cols(a):
    tiles, ch, bt = a.shape
    return a.transpose(0, 2, 1).reshape(tiles * bt, ch)


def _row(v, width=None):
    v = v.reshape(1, -1).astype(F32)
    if width is not None and v.shape[1] < width:
        v = jnp.pad(v, ((0, 0), (0, width - v.shape[1])))
    return v


def _pack(w):
    w = w.astype(BF16)
    k, n = w.shape[-2:]
    w = w.reshape(w.shape[:-2] + (k // 2, 2, n))
    return lax.bitcast_convert_type(jnp.swapaxes(w, -1, -2), jnp.uint32)


def _dense_block_tiles(w):
    nb, bi, bo = w.shape
    per = LANE // bi
    w = w.reshape(nb // per, per, bi, bo)
    eye = jnp.eye(per, dtype=w.dtype)
    dense = jnp.einsum('tpio,pq->tpiqo', w, eye).reshape(nb // per, per * bi, per * bo)
    return dense.astype(BF16)


def kernel(x_prompt, x_sample, state_ssd_conv, state_ssd, state_mlstm_conv, state_mlstm_C, state_mlstm_n,
           state_mlstm_m, state_lru_conv, state_lru_h, meta_tokens, norm_w, final_norm_w, w_in_mix, w_out_mix,
           ssd_conv_w, ssd_conv_b, ssd_dt_bias, ssd_a_log, ssd_d, ssd_norm_w, ml_conv_w, ml_conv_b, ml_wq, ml_wk,
           ml_wv, ml_w_gate, ml_b_gate, ml_skip, ml_norm_w, lru_w_in, lru_w_out, lru_conv_w, lru_conv_b, lru_wa,
           lru_ba, lru_wx, lru_bx, lru_lambda):
    bsz = x_prompt.shape[0]
    dec = x_sample.shape[0]

    w_in = w_in_mix[0]
    o1 = SSD_WIDTH
    o2 = o1 + SSD_CONV_CH
    o3 = o2 + SSD_HEADS
    win = jnp.concatenate([w_in[:, :o2], jnp.pad(w_in[:, o2:o3], ((0, 0), (0, LANE - SSD_HEADS))), w_in[:, o3:]],
                          axis=1).astype(BF16)
    wout = w_out_mix[0].astype(BF16)
    nw0 = _row(norm_w[0])
    nw1 = _row(norm_w[1])
    fnw = _row(final_norm_w)
    scw = ssd_conv_w[0]
    scb = _row(ssd_conv_b[0])
    dtb = _row(ssd_dt_bias[0], LANE)
    alog = _row(ssd_a_log[0], LANE)
    dsk = _row(jnp.repeat(ssd_d[0], SSD_HEAD_DIM))
    snw = _row(ssd_norm_w[0])
    mcw = ml_conv_w[0]
    mcb = _row(ml_conv_b[0])
    wqk = jnp.concatenate([_dense_block_tiles(ml_wq[0]), _dense_block_tiles(ml_wk[0])], axis=2)
    wv = _dense_block_tiles(ml_wv[0])
    wg_raw = ml_w_gate[0]
    wg = jnp.concatenate([jnp.pad(wg_raw[:, :ML_HEADS], ((0, 0), (0, LANE - ML_HEADS))),
                          jnp.pad(wg_raw[:, ML_HEADS:], ((0, 0), (0, LANE - ML_HEADS)))], axis=1).astype(BF16)
    bg = jnp.concatenate([_row(ml_b_gate[0, :ML_HEADS], LANE), _row(ml_b_gate[0, ML_HEADS:], LANE)], axis=1)
    msk = _row(ml_skip[0])
    mnw = _row(ml_norm_w[0])
    win1 = lru_w_in[0].astype(BF16)
    wout1 = lru_w_out[0].astype(BF16)
    lcw = lru_conv_w[0]
    lcb = _row(lru_conv_b[0])
    wax = jnp.concatenate([lru_wa[0], lru_wx[0]], axis=2).astype(BF16)
    r_idx = lax.broadcasted_iota(jnp.int32, (LANE, SSD_WIDTH), 0)
    c_idx = lax.broadcasted_iota(jnp.int32, (LANE, SSD_WIDTH), 1)
    expand = (c_idx // SSD_HEAD_DIM == r_idx).astype(BF16)
    ba = _row(lru_ba[0])
    bx = _row(lru_bx[0])
    lam = _row(lru_lambda[0])

    win, wout, wqk, wv, wg, expand, win1, wout1, wax = (
        _pack(w) for w in (win, wout, wqk, wv, wg, expand, win1, wout1, wax))
    l0_w = (nw0, win, wout, scw, scb, dtb, alog, dsk, snw, mcw, mcb, wqk, wv, wg, bg, msk, mnw, expand)
    l1_w = (nw1, fnw, win1, wout1, lcw, lcb, wax, ba, bx, lam)

    zero0 = tuple(jnp.zeros((1,) + s, F32) for s in L0_STATE_SHAPES)
    zero1 = tuple(jnp.zeros((1,) + s, F32) for s in L1_STATE_SHAPES)
    meta = jnp.pad(meta_tokens.astype(F32), ((CHUNK - N_META, 0), (0, 0)))[None]
    meta_out = _l0_prompt(meta, l0_w, zero0, CHUNK - N_META)
    meta1_out = _l1_prompt(meta_out[0], l1_w, zero1, CHUNK - N_META)
    l0_out = _l0_prompt(x_prompt, l0_w, tuple(meta_out[1:]), 0)
    h1_p, p_sc, p_s_t, p_mc, p_ct, p_n, p_m = l0_out
    y_prompt, p_lc, p_lh = _l1_prompt(h1_p, l1_w, tuple(meta1_out[1:]), 0)

    p_s = p_s_t.reshape(bsz, SSD_GROUPS, SSD_STATE, SSD_HPG, SSD_HEAD_DIM).transpose(0, 1, 3, 4, 2)
    p_s = p_s.reshape(bsz, SSD_HEADS, SSD_HEAD_DIM, SSD_STATE)
    p_c = p_ct.transpose(0, 1, 3, 2)
    p_m = p_m[:, 0, :ML_HEADS]
    p_lh = p_lh[:, 0]

    xs2 = x_sample[:, 0]
    sbuf = jnp.moveaxis(state_ssd_conv[0], 1, 0)
    mbuf = jnp.moveaxis(state_mlstm_conv[0], 1, 0)
    lbuf = jnp.moveaxis(state_lru_conv[0], 1, 0)
    m0 = jnp.pad(state_mlstm_m[0], ((0, 0), (0, LANE - ML_HEADS)))
    n0 = state_mlstm_n[0].reshape(dec, ML_WIDTH)
    pre_shapes = ((TAIL, dec, SSD_CONV_CH), (TAIL, dec, ML_WIDTH), (dec, SSD_WIDTH), (dec, SSD_WIDTH),
                  (dec, SSD_GROUPS * SSD_STATE), (dec, SSD_GROUPS * SSD_STATE), (dec, SSD_WIDTH), (dec, SSD_WIDTH),
                  (dec, ML_WIDTH), (dec, ML_WIDTH), (dec, ML_WIDTH), (dec, ML_WIDTH), (dec, ML_WIDTH),
                  (dec, ML_WIDTH), (dec, LANE), (dec, ML_WIDTH), (dec, ML_HEADS))
    (nsb, nmb, zs, xs_c, bm, cm, xdt_t, dec_t, zm, xc_m, q, isv_t, fs_t, k, m_new, n_new, den) = _full_call(
        _l0_sample_pre_kernel, pre_shapes,
        (xs2, nw0, win, scw, scb, dtb, alog, mcw, mcb, wqk, wv, wg, bg, sbuf, mbuf, m0, n0), "l0_sample_pre")

    xdt_c = _to_cols(xdt_t, BT_S)
    dec_c = _to_cols(dec_t, BT_S)
    s_new, ys_c = pl.pallas_call(
        _ssd_state_kernel,
        grid=(dec // BT_S,),
        in_specs=[pl.BlockSpec((BT_S, SSD_HEADS, SSD_HEAD_DIM, SSD_STATE), lambda i: (i, 0, 0, 0)),
                  pl.BlockSpec((1, SSD_WIDTH, BT_S), lambda i: (i, 0, 0)),
                  pl.BlockSpec((1, SSD_WIDTH, BT_S), lambda i: (i, 0, 0)),
                  pl.BlockSpec((BT_S, SSD_GROUPS * SSD_STATE), lambda i: (i, 0)),
                  pl.BlockSpec((BT_S, SSD_GROUPS * SSD_STATE), lambda i: (i, 0))],
        out_specs=[pl.BlockSpec((BT_S, SSD_HEADS, SSD_HEAD_DIM, SSD_STATE), lambda i: (i, 0, 0, 0)),
                   pl.BlockSpec((1, SSD_WIDTH, BT_S), lambda i: (i, 0, 0))],
        out_shape=[jax.ShapeDtypeStruct((dec, SSD_HEADS, SSD_HEAD_DIM, SSD_STATE), F32),
                   jax.ShapeDtypeStruct((dec // BT_S, SSD_WIDTH, BT_S), F32)],
        compiler_params=pltpu.CompilerParams(dimension_semantics=("arbitrary",), vmem_limit_bytes=VMEM_LIMIT),
        name="ssd_state",
    )(state_ssd[0], xdt_c, dec_c, bm, cm)

    isv_c = _to_cols(isv_t, BT_C)
    fs_c = _to_cols(fs_t, BT_C)
    c_new, num_c = pl.pallas_call(
        _mlstm_state_kernel,
        grid=(dec // BT_C,),
        in_specs=[pl.BlockSpec((BT_C, ML_HEADS, ML_HEAD_DIM, ML_HEAD_DIM), lambda i: (i, 0, 0, 0)),
                  pl.BlockSpec((1, ML_WIDTH, BT_C), lambda i: (i, 0, 0)),
                  pl.BlockSpec((1, ML_WIDTH, BT_C), lambda i: (i, 0, 0)),
                  pl.BlockSpec((1, BT_C, ML_WIDTH), lambda i: (i, 0, 0)),
                  pl.BlockSpec((1, BT_C, ML_WIDTH), lambda i: (i, 0, 0))],
        out_specs=[pl.BlockSpec((BT_C, ML_HEADS, ML_HEAD_DIM, ML_HEAD_DIM), lambda i: (i, 0, 0, 0)),
                   pl.BlockSpec((1, ML_WIDTH, BT_C), lambda i: (i, 0, 0))],
        out_shape=[jax.ShapeDtypeStruct((dec, ML_HEADS, ML_HEAD_DIM, ML_HEAD_DIM), F32),
                   jax.ShapeDtypeStruct((dec // BT_C, ML_WIDTH, BT_C), F32)],
        compiler_params=pltpu.CompilerParams(dimension_semantics=("arbitrary",), vmem_limit_bytes=VMEM_LIMIT),
        name="mlstm_state",
    )(state_mlstm_C[0], isv_c, fs_c, k.reshape(dec // BT_C, BT_C, ML_WIDTH), q.reshape(dec // BT_C, BT_C, ML_WIDTH))

    post_shapes = ((dec, D_MODEL), (TAIL, dec, LRU_WIDTH), (dec, LRU_WIDTH))
    y_s2, nlb, h_new = _full_call(
        _sample_post_kernel, post_shapes,
        (xs2, _from_cols(ys_c), _from_cols(num_c), den, zs, xs_c, zm, xc_m, dsk, snw, msk, mnw, wout,
         nw1, fnw, win1, wout1, lcw, lcb, wax, ba, bx, lam, lbuf, state_lru_h[0]), "sample_post")

    s_sc = jnp.moveaxis(nsb, 0, 1)[None]
    s_mc = jnp.moveaxis(nmb, 0, 1)[None]
    s_lc = jnp.moveaxis(nlb, 0, 1)[None]
    return (y_prompt, y_s2[:, None, :],
            p_sc[None], p_s[None], p_mc[None], p_c[None], p_n[None], p_m[None], p_lc[None], p_lh[None],
            s_sc, s_new[None], s_mc, c_new[None], n_new.reshape(dec, ML_HEADS, ML_HEAD_DIM)[None],
            m_new[:, :ML_HEADS][None], s_lc, h_new[None])
```

```python
import functools

import jax
import jax.numpy as jnp
from jax import lax
from jax.experimental import pallas as pl
from jax.experimental.pallas import tpu as pltpu

F32 = jnp.float32
BF16 = jnp.bfloat16

D_MODEL = 1024
N_META = 16
CONV_W = 4
EPS = 1e-6
NEG = -1e30
SSD_WIDTH = 1024
SSD_HEAD_DIM = 64
SSD_HEADS = 16
SSD_GROUPS = 2
SSD_HPG = 8
SSD_STATE = 128
SSD_CONV_CH = 1536
ML_WIDTH = 1024
ML_HEADS = 4
ML_HEAD_DIM = 256
ML_QKV_BLOCK = 4
LRU_WIDTH = 2048
LRU_BLOCKS = 16
LRU_BLOCK = 128
LRU_C = 8.0

LANE = 128
SUBLANE = 8
CHUNK = 128
L0_ROWS = 2
L1_ROWS = 2
L1_GROUPS = 4
TAIL = CONV_W - 1

OFF_ZS = 0
OFF_XBC = OFF_ZS + SSD_WIDTH
OFF_DT = OFF_XBC + SSD_CONV_CH
OFF_ZM = OFF_DT + LANE
OFF_XM = OFF_ZM + ML_WIDTH
IN_MIX_PAD = OFF_XM + ML_WIDTH

VMEM_LIMIT = 56 * 1024 * 1024


def _sigmoid(x):
    return 1.0 / (1.0 + jnp.exp(-x))


def _silu(x):
    return x * _sigmoid(x)


def _softplus(x):
    return jnp.maximum(x, 0.0) + jnp.log1p(jnp.exp(-jnp.abs(x)))


def _rms(x, w):
    return x * lax.rsqrt(jnp.mean(x * x, axis=-1, keepdims=True) + EPS) * w


def _bdot(a, b):
    return jnp.dot(a.astype(BF16), b.astype(BF16), preferred_element_type=F32)


def _wload(w):
    return pltpu.bitcast(w, BF16)


def _split3(x):
    hi = x.astype(BF16)
    r = x - hi.astype(F32)
    mid = r.astype(BF16)
    lo = (r - mid.astype(F32)).astype(BF16)
    return hi, mid, lo


def _cumsum_rows(x, tril):
    hi, mid, lo = _split3(x)
    d = functools.partial(jnp.dot, preferred_element_type=F32)
    return d(tril, hi) + d(tril, mid) + d(tril, lo)


def _expand_heads(x, expand):
    hi, mid, _ = _split3(x)
    d = functools.partial(jnp.dot, preferred_element_type=F32)
    return d(hi, expand) + d(mid, expand)


def _expand_matrix():
    r = lax.broadcasted_iota(jnp.int32, (LANE, SSD_WIDTH), 0)
    c = lax.broadcasted_iota(jnp.int32, (LANE, SSD_WIDTH), 1)
    return jnp.where(lax.shift_right_logical(c, 6) == r, 1.0, 0.0).astype(BF16)


def _blockdiag_tiles(x, w_ref):
    k = w_ref.shape[0]
    m = w_ref.shape[2] // LANE
    prods = [_bdot(x[:, t * LANE:(t + 1) * LANE], _wload(w_ref[t])) for t in range(k)]
    return [jnp.concatenate([p[:, j * LANE:(j + 1) * LANE] for p in prods], axis=-1) for j in range(m)]


def _group_rmsnorm(y, w):
    half = SSD_WIDTH // SSD_GROUPS
    parts = []
    for g in range(SSD_GROUPS):
        yg = y[:, g * half:(g + 1) * half]
        parts.append(yg * lax.rsqrt(jnp.mean(yg * yg, axis=-1, keepdims=True) + EPS))
    return jnp.concatenate(parts, axis=-1) * w


def _head_layernorm(h):
    parts = []
    for k in range(ML_HEADS):
        hk = h[:, k * ML_HEAD_DIM:(k + 1) * ML_HEAD_DIM]
        mu = jnp.mean(hk, axis=-1, keepdims=True)
        d = hk - mu
        var = jnp.mean(d * d, axis=-1, keepdims=True)
        parts.append(d * lax.rsqrt(var + EPS))
    return jnp.concatenate(parts, axis=-1)


def _mlstm_qkv_gates(xm, xc, wqk_ref, wv_ref, wg_ref, bg_ref):
    q, k = _blockdiag_tiles(xc, wqk_ref)
    v, = _blockdiag_tiles(xm, wv_ref)
    gates = _bdot(jnp.concatenate([q, k, v], axis=-1), _wload(wg_ref[...])) + bg_ref[...]
    ig = gates[:, :LANE]
    logf = -_softplus(-gates[:, LANE:])
    return q, k * (ML_HEAD_DIM ** -0.5), v, ig, logf


N_L0_W = 18
N_L0_S = 6


def _l0_prompt_kernel(x_ref, xnext_ref, *refs, front_pad, rows):
    w_refs = refs[:N_L0_W]
    init_refs = refs[N_L0_W:N_L0_W + N_L0_S]
    out_refs = refs[N_L0_W + N_L0_S:N_L0_W + 2 * N_L0_S + 1]
    scratch = refs[N_L0_W + 2 * N_L0_S + 1:]
    c = pl.program_id(1)

    def each_row(phase):
        return [_l0_prompt_row(x_ref.at[r], xnext_ref.at[r], *w_refs, *init_refs, *(o.at[r] for o in out_refs),
                               *(s.at[r] for s in scratch), front_pad=front_pad, phase=phase)
                for r in range(rows)]

    @pl.when(c == 0)
    def _():
        each_row("init")

    bodies = each_row("body")
    _run_round_robin([gen for gens, _ in bodies for gen in gens])
    for _, finish in bodies:
        finish()

    @pl.when(c == pl.num_programs(1) - 1)
    def _():
        each_row("final")


def _run_round_robin(gens):
    live = list(gens)
    while live:
        for gen in list(live):
            if next(gen, "done") == "done":
                live.remove(gen)


def _l0_prompt_row(x_ref, xnext_ref, nw_ref, win_ref, wout_ref,
                   scw_ref, scb_ref, dtb_ref, alog_ref, dsk_ref, snw_ref,
                   mcw_ref, mcb_ref, wqk_ref, wv_ref, wg_ref, bg_ref, msk_ref, mnw_ref, expand_ref,
                   isc_ref, iss_ref, imc_ref, ict_ref, inn_ref, imm_ref,
                   h1_ref, osc_ref, oss_ref, omc_ref, oct_ref, onn_ref, omm_ref,
                   sbuf, mbuf, s_st, ct_st, n_st, m_st, proj_s, *, front_pad, phase):
    q_len = x_ref.shape[0]

    if phase == "init":
        sbuf[...] = jnp.zeros(sbuf.shape, F32)
        mbuf[...] = jnp.zeros(mbuf.shape, F32)
        sbuf[SUBLANE - TAIL:SUBLANE, :] = isc_ref[0]
        mbuf[SUBLANE - TAIL:SUBLANE, :] = imc_ref[0]
        s_st[...] = iss_ref[0]
        ct_st[...] = ict_ref[0]
        n_st[...] = inn_ref[0]
        m_st[...] = imm_ref[0]
        hn0 = _move_rows(_perm_matrices(q_len)[0], _rms(x_ref[...], nw_ref[...]).astype(BF16))
        proj_s[...] = _bdot(hn0, _wload(win_ref[...]))
        return None
    if phase == "final":
        osc_ref[...] = sbuf[SUBLANE - TAIL:SUBLANE, :]
        omc_ref[...] = mbuf[SUBLANE - TAIL:SUBLANE, :]
        oss_ref[...] = s_st[...]
        oct_ref[...] = ct_st[...]
        onn_ref[...] = n_st[...]
        omm_ref[...] = m_st[...]
        return None

    x = x_ref[...]
    to_perm, to_time = _perm_matrices(q_len)
    t_col = _perm_time(q_len)
    t_row = _perm_time(q_len, row=True)
    causal = t_col >= t_row
    tril = jnp.where(causal, 1.0, 0.0).astype(BF16)
    valid = (t_col >= front_pad) if front_pad else None
    hn_next = _move_rows(to_perm, _rms(xnext_ref[...], nw_ref[...]).astype(BF16))
    partial_out = []

    def in_proj_next(cols):
        proj_s[:, cols] = _bdot(hn_next, _wload(win_ref[:, cols]))

    def out_piece(y_piece, k0):
        width = y_piece.shape[1]
        y_t = _move_rows(to_time, y_piece.astype(BF16))
        partial_out.append(_bdot(y_t, _wload(wout_ref[k0 // 2:(k0 + width) // 2, :])))

    def ssd():
        xbc = proj_s[:, OFF_XBC:OFF_XBC + SSD_CONV_CH]
        dt_raw = proj_s[:, OFF_DT:OFF_DT + LANE]
        xbc = _silu(_conv_perm(sbuf, xbc, scw_ref, scb_ref))
        in_proj_next(slice(OFF_XBC, OFF_DT + LANE))
        yield
        xs = xbc[:, :SSD_WIDTH]
        bm = xbc[:, SSD_WIDTH:SSD_WIDTH + SSD_GROUPS * SSD_STATE]
        cm = xbc[:, SSD_WIDTH + SSD_GROUPS * SSD_STATE:]
        dt = _softplus(dt_raw + dtb_ref[...])
        if front_pad:
            dt = jnp.where(valid, dt, 0.0)
        log_a = -dt * jnp.exp(alog_ref[...])
        a_cs = _cumsum_rows(log_a, tril)
        yield
        a_last = a_cs[q_len - 1:q_len, :]
        expand = _wload(expand_ref[...])
        w_state = _expand_heads(dt * jnp.exp(a_last - a_cs), expand)
        e_acs = _expand_heads(jnp.exp(a_cs), expand)
        a_cs_t = a_cs.T
        dt_t = dt.T
        yield
        pair_lo = lax.broadcasted_iota(jnp.int32, (q_len, LANE), 1) < SSD_HEAD_DIM
        half = SSD_WIDTH // SSD_GROUPS
        y_groups = []
        for g in range(SSD_GROUPS):
            bg = bm[:, g * SSD_STATE:(g + 1) * SSD_STATE]
            cg = cm[:, g * SSD_STATE:(g + 1) * SSD_STATE]
            bg_t = bg.T
            xg = xs[:, g * half:(g + 1) * half]
            eg = e_acs[:, g * half:(g + 1) * half]
            s_old = s_st[g]
            cb = _bdot(cg, bg_t)
            y_off = _bdot(cg, s_old) * eg
            s_st[g] = eg[q_len - 1:q_len, :] * s_old + _bdot(bg_t, xg * w_state[:, g * half:(g + 1) * half])
            yield
            y_pairs = []
            for pr in range(SSD_HPG // 2):
                ms = []
                for e in (2 * pr, 2 * pr + 1):
                    hd = g * SSD_HPG + e
                    seg = jnp.exp(jnp.where(causal, a_cs[:, hd:hd + 1] - a_cs_t[hd:hd + 1, :], -jnp.inf))
                    ms.append(cb * seg * dt_t[hd:hd + 1, :])
                xp = xg[:, pr * LANE:(pr + 1) * LANE]
                rhs = jnp.concatenate([jnp.where(pair_lo, xp, 0.0), jnp.where(pair_lo, 0.0, xp)], axis=0)
                y_pairs.append(_bdot(jnp.concatenate(ms, axis=-1), rhs))
                yield
            y_groups.append(jnp.concatenate(y_pairs, axis=-1) + y_off)
        z_s = proj_s[:, OFF_ZS:OFF_ZS + SSD_WIDTH]
        in_proj_next(slice(OFF_ZS, OFF_ZS + SSD_WIDTH))
        y_s = jnp.concatenate(y_groups, axis=-1) + dsk_ref[...] * xs
        y_s = _group_rmsnorm(y_s * _silu(z_s), snw_ref[...])
        yield
        out_piece(y_s, 0)

    def mlstm():
        xm = proj_s[:, OFF_XM:OFF_XM + ML_WIDTH]
        xc = _silu(_conv_perm(mbuf, xm, mcw_ref, mcb_ref))
        in_proj_next(slice(OFF_XM, OFF_XM + ML_WIDTH))
        yield
        q, k = _blockdiag_tiles(xc, wqk_ref)
        v, = _blockdiag_tiles(xm, wv_ref)
        yield
        gates = _bdot(jnp.concatenate([q, k, v], axis=-1), _wload(wg_ref[...])) + bg_ref[...]
        k = k * (ML_HEAD_DIM ** -0.5)
        yield
        ig = gates[:, :LANE]
        logf = -_softplus(-gates[:, LANE:])
        if front_pad:
            ig = jnp.where(valid, ig, NEG)
            logf = jnp.where(valid, logf, 0.0)
        bcum = _cumsum_rows(logf, tril)
        yield
        ftot = bcum[q_len - 1:q_len, :]
        m_prev = m_st[...]
        w_end = ftot - bcum + ig
        m_new = jnp.maximum(ftot + m_prev, jnp.max(w_end, axis=0, keepdims=True))
        sc = jnp.exp(ftot + m_prev - m_new)
        wexp = jnp.exp(w_end - m_new)
        inter = bcum + m_prev
        bcum_t = bcum.T
        ig_t = ig.T
        m_st[...] = m_new
        z_m = proj_s[:, OFF_ZM:OFF_ZM + ML_WIDTH]
        in_proj_next(slice(OFF_ZM, OFF_ZM + ML_WIDTH))
        yield
        for hd in range(ML_HEADS):
            sl = slice(hd * ML_HEAD_DIM, (hd + 1) * ML_HEAD_DIM)
            q_h, k_h, v_h = q[:, sl], k[:, sl], v[:, sl]
            k_t = k_h.T
            dmat = jnp.where(causal, bcum[:, hd:hd + 1] - bcum_t[hd:hd + 1, :] + ig_t[hd:hd + 1, :], -jnp.inf)
            inter_h = inter[:, hd:hd + 1]
            m_t = jnp.maximum(inter_h, jnp.max(dmat, axis=-1, keepdims=True))
            dexp = jnp.exp(dmat - m_t)
            inter_sc = jnp.exp(inter_h - m_t)
            s = _bdot(q_h, k_t) * dexp
            yield
            ct_old = ct_st[hd]
            n_old = n_st[hd:hd + 1, :]
            num = _bdot(s, v_h) + inter_sc * _bdot(q_h, ct_old)
            den = jnp.sum(s, axis=-1, keepdims=True) + inter_sc * jnp.sum(q_h * n_old, axis=-1, keepdims=True)
            h_h = num / jnp.maximum(jnp.abs(den), jnp.exp(-m_t))
            w_col = wexp[:, hd:hd + 1]
            sc_h = sc[:, hd:hd + 1]
            ct_st[hd] = sc_h * ct_old + _bdot(k_t, v_h * w_col)
            n_st[hd:hd + 1, :] = sc_h * n_old + jnp.sum(k_h * w_col, axis=0, keepdims=True)
            yield
            mu = jnp.mean(h_h, axis=-1, keepdims=True)
            dev = h_h - mu
            var = jnp.mean(dev * dev, axis=-1, keepdims=True)
            h_h = dev * lax.rsqrt(var + EPS) * mnw_ref[:, sl]
            out_piece((h_h + msk_ref[:, sl] * xc[:, sl]) * _silu(z_m[:, sl]), SSD_WIDTH + hd * ML_HEAD_DIM)
            yield

    def finish():
        h1 = x
        for part in partial_out:
            h1 = h1 + part
        if front_pad:
            h1 = jnp.where(lax.broadcasted_iota(jnp.int32, (q_len, 1), 0) >= front_pad, h1, 0.0)
        h1_ref[...] = h1

    return [ssd(), mlstm()], finish


def _const_spec(shape):
    nd = len(shape)
    return pl.BlockSpec(shape, lambda b, c: (0,) * nd)


def _state_spec(shape, rows):
    nd = len(shape)
    if rows:
        return pl.BlockSpec((rows,) + shape, lambda b, c: (b,) + (0,) * nd)
    return pl.BlockSpec((1,) + shape, lambda b, c: (0,) * (nd + 1))


def _rows_per_step(bsz, want):
    return want if bsz % want == 0 else 1


L0_STATE_SHAPES = ((TAIL, SSD_CONV_CH), (SSD_GROUPS, SSD_STATE, SSD_WIDTH // SSD_GROUPS), (TAIL, ML_WIDTH),
                   (ML_HEADS, ML_HEAD_DIM, ML_HEAD_DIM), (ML_HEADS, ML_HEAD_DIM), (1, LANE))


def _l0_prompt(x, weights, init, front_pad):
    bsz, length, _ = x.shape
    q_len = min(CHUNK, length)
    assert length % q_len == 0
    rows = _rows_per_step(bsz, L0_ROWS)
    assert len(weights) == N_L0_W and len(init) == N_L0_S
    grid = (bsz // rows, length // q_len)
    last = length // q_len - 1
    x_spec = pl.BlockSpec((rows, q_len, D_MODEL), lambda b, c: (b, c, 0))
    next_spec = pl.BlockSpec((rows, q_len, D_MODEL), lambda b, c: (b, jnp.minimum(c + 1, last), 0))
    in_specs = ([x_spec, next_spec] + [_const_spec(w.shape) for w in weights]
                + [_state_spec(s, 0) for s in L0_STATE_SHAPES])
    out_shape = ([jax.ShapeDtypeStruct((bsz, length, D_MODEL), F32)]
                 + [jax.ShapeDtypeStruct((bsz,) + s, F32) for s in L0_STATE_SHAPES])
    out_specs = [x_spec] + [_state_spec(s, rows) for s in L0_STATE_SHAPES]
    scratch = [pltpu.VMEM((rows,) + s, F32) for s in
               ((SUBLANE, SSD_CONV_CH), (SUBLANE, ML_WIDTH), L0_STATE_SHAPES[1], L0_STATE_SHAPES[3],
                L0_STATE_SHAPES[4], L0_STATE_SHAPES[5], (q_len, IN_MIX_PAD))]
    return pl.pallas_call(
        functools.partial(_l0_prompt_kernel, front_pad=front_pad, rows=rows),
        grid=grid, in_specs=in_specs, out_specs=out_specs, out_shape=out_shape, scratch_shapes=scratch,
        compiler_params=pltpu.CompilerParams(dimension_semantics=("arbitrary", "arbitrary"),
                                             vmem_limit_bytes=VMEM_LIMIT),
        name="l0_prompt",
    )(x, x, *weights, *init)


def _rglru_gates(xc, ra, ix, ba_ref, bx_ref, lam_ref):
    r = _sigmoid(ra + ba_ref[...])
    i = _sigmoid(ix + bx_ref[...])
    log_a = r * (-LRU_C * _softplus(-lam_ref[...]))
    a = jnp.exp(log_a)
    u = jnp.sqrt(1.0 - a * a) * (i * xc)
    return a, u


def _perm_time(n, row=False):
    p = lax.broadcasted_iota(jnp.int32, (1, n) if row else (n, 1), 1 if row else 0)
    return (n // SUBLANE) * (p & (SUBLANE - 1)) + lax.shift_right_logical(p, 3)


def _perm_matrices(n):
    nb = n // SUBLANE
    r = lax.broadcasted_iota(jnp.int32, (n, n), 0)
    c = lax.broadcasted_iota(jnp.int32, (n, n), 1)
    to_perm = jnp.where(c == nb * (r & (SUBLANE - 1)) + lax.shift_right_logical(r, 3), 1.0, 0.0)
    to_time = jnp.where(r == nb * (c & (SUBLANE - 1)) + lax.shift_right_logical(c, 3), 1.0, 0.0)
    return to_perm.astype(BF16), to_time.astype(BF16)


def _move_rows(sel, x_bf16):
    return jnp.dot(sel, x_bf16, preferred_element_type=F32).astype(BF16)


def _conv_perm(tail_ref, x, w_ref, b_ref):
    n, ch = x.shape
    nb = n // SUBLANE
    x3 = x.reshape(nb, SUBLANE, ch)
    tail8 = tail_ref[...]
    sub = lax.broadcasted_iota(jnp.int32, (SUBLANE, ch), 0)
    y = b_ref[...].reshape(1, 1, ch) + w_ref[TAIL:TAIL + 1, :].reshape(1, 1, ch) * x3
    wrapped = [jnp.where(sub >= 1, pltpu.roll(x3[nb - d], 1, 0), tail8[SUBLANE - d:SUBLANE - d + 1, :])
               for d in range(1, CONV_W)]
    for back in range(1, CONV_W):
        head = jnp.stack([wrapped[back - j - 1] for j in range(back)], axis=0)
        shifted = jnp.concatenate([head, x3[:nb - back]], axis=0)
        y = y + w_ref[TAIL - back:TAIL - back + 1, :].reshape(1, 1, ch) * shifted
    for d in range(1, CONV_W):
        tail_ref[SUBLANE - d:SUBLANE - d + 1, :] = x3[nb - d][SUBLANE - 1:SUBLANE, :]
    return y.reshape(n, ch)


def _scan_perm(a, u, h_prev):
    n, ch = a.shape
    nb = n // SUBLANE
    a3 = a.reshape(nb, SUBLANE, ch)
    u3 = u.reshape(nb, SUBLANE, ch)
    local = [u3[0]]
    decay = [a3[0]]
    for j in range(1, nb):
        local.append(a3[j] * local[-1] + u3[j])
        decay.append(a3[j] * decay[-1])
    seg_u, seg_a = local[-1], decay[-1]
    sub = lax.broadcasted_iota(jnp.int32, (SUBLANE, ch), 0)
    shift = 1
    while shift < SUBLANE:
        keep = sub >= shift
        seg_u = seg_u + seg_a * jnp.where(keep, pltpu.roll(seg_u, shift, 0), 0.0)
        seg_a = seg_a * jnp.where(keep, pltpu.roll(seg_a, shift, 0), 1.0)
        shift *= 2
    seg_end = seg_a * h_prev + seg_u
    carry = jnp.where(sub >= 1, pltpu.roll(seg_end, 1, 0), h_prev)
    h3 = jnp.stack([local[j] + decay[j] * carry for j in range(nb)], axis=0)
    return h3.reshape(n, ch), seg_end[SUBLANE - 1:SUBLANE, :]


N_L1_W = 10
N_L1_S = 2


def _l1_prompt_kernel(h_ref, hnext_ref, *refs, front_pad, rows):
    w_refs = refs[:N_L1_W]
    init_refs = refs[N_L1_W:N_L1_W + N_L1_S]
    out_refs = refs[N_L1_W + N_L1_S:N_L1_W + 2 * N_L1_S + 1]
    scratch = refs[N_L1_W + 2 * N_L1_S + 1:]
    c = pl.program_id(1)

    def each_row(phase):
        return [_l1_prompt_row(h_ref.at[r], hnext_ref.at[r], *w_refs, *init_refs, *(o.at[r] for o in out_refs),
                               *(s.at[r] for s in scratch), front_pad=front_pad, phase=phase)
                for r in range(rows)]

    @pl.when(c == 0)
    def _():
        each_row("init")

    bodies = each_row("body")
    _run_staggered([gen for gens, _ in bodies for gen in gens])
    for _, finish in bodies:
        finish()

    @pl.when(c == pl.num_programs(1) - 1)
    def _():
        each_row("final")


def _l1_in_proj(h_val, nw_ref, win_ref, to_perm):
    hn = _move_rows(to_perm, _rms(h_val, nw_ref[...]).astype(BF16))
    return _bdot(hn, _wload(win_ref[...]))


def _l1_prompt_row(h_ref, hnext_ref, nw_ref, fnw_ref, win_ref, wout_ref, cw_ref, cb_ref,
                   wax_ref, ba_ref, bx_ref, lam_ref, ilc_ref, ilh_ref,
                   y_ref, olc_ref, olh_ref, lbuf, h_st, proj_s, *, front_pad, phase):
    if phase == "init":
        lbuf[...] = jnp.zeros(lbuf.shape, F32)
        lbuf[SUBLANE - TAIL:SUBLANE, :] = ilc_ref[0]
        h_st[...] = ilh_ref[0]
        proj_s[...] = _l1_in_proj(h_ref[...], nw_ref, win_ref, _perm_matrices(h_ref.shape[0])[0])
        return None
    if phase == "final":
        olc_ref[...] = lbuf[SUBLANE - TAIL:SUBLANE, :]
        olh_ref[...] = h_st[...]
        return None
    return _l1_row_body(h_ref, hnext_ref, nw_ref, fnw_ref, win_ref, wout_ref, cw_ref, cb_ref,
                        wax_ref, ba_ref, bx_ref, lam_ref, y_ref, lbuf, h_st, proj_s, front_pad)


def _l1_row_body(h_ref, hnext_ref, nw_ref, fnw_ref, win_ref, wout_ref, cw_ref, cb_ref,
                 wax_ref, ba_ref, bx_ref, lam_ref, y_ref, lbuf, h_st, proj_s, front_pad):
    q_len = h_ref.shape[0]
    h_in = h_ref[...]
    to_perm, to_time = _perm_matrices(q_len)
    hn_next = _move_rows(to_perm, _rms(hnext_ref[...], nw_ref[...]).astype(BF16))
    if front_pad:
        valid = _perm_time(q_len) >= front_pad
    gw = LRU_WIDTH // L1_GROUPS
    tiles = gw // LANE
    partial_out = []

    def group(g):
        cg = slice(g * gw, (g + 1) * gw)
        cx = slice(LRU_WIDTH + g * gw, LRU_WIDTH + (g + 1) * gw)
        gate = proj_s[:, cg]
        xr = proj_s[:, cx]
        xc = _conv_perm(lbuf.at[:, cg], xr, cw_ref.at[:, cg], cb_ref.at[:, cg])
        ra, ix = _blockdiag_tiles(xc, wax_ref.at[g * tiles:(g + 1) * tiles])
        yield
        proj_s[:, cg] = _bdot(hn_next, _wload(win_ref[:, cg]))
        proj_s[:, cx] = _bdot(hn_next, _wload(win_ref[:, cx]))
        a, u = _rglru_gates(xc, ra, ix, ba_ref.at[:, cg], bx_ref.at[:, cg], lam_ref.at[:, cg])
        if front_pad:
            a = jnp.where(valid, a, 1.0)
            u = jnp.where(valid, u, 0.0)
        yield
        h, h_last = _scan_perm(a, u, h_st[:, cg])
        h_st[:, cg] = h_last
        yield
        y = _move_rows(to_time, (h * _silu(gate)).astype(BF16))
        partial_out.append(_bdot(y, _wload(wout_ref[g * gw // 2:(g + 1) * gw // 2, :])))

    def finish():
        h2 = h_in
        for part in partial_out:
            h2 = h2 + part
        y_ref[...] = _rms(h2, fnw_ref[...])

    return [group(g) for g in range(L1_GROUPS)], finish


def _run_staggered(gens):
    live = []
    pending = list(gens)
    while pending or live:
        if pending:
            live.append(pending.pop(0))
        for gen in list(live):
            if next(gen, "done") == "done":
                live.remove(gen)


L1_STATE_SHAPES = ((TAIL, LRU_WIDTH), (1, LRU_WIDTH))


def _l1_prompt(h1, weights, init, front_pad):
    bsz, length, _ = h1.shape
    q_len = min(CHUNK, length)
    assert length % q_len == 0
    rows = _rows_per_step(bsz, L1_ROWS)
    assert len(weights) == N_L1_W and len(init) == N_L1_S
    grid = (bsz // rows, length // q_len)
    last = length // q_len - 1
    x_spec = pl.BlockSpec((rows, q_len, D_MODEL), lambda b, c: (b, c, 0))
    next_spec = pl.BlockSpec((rows, q_len, D_MODEL), lambda b, c: (b, jnp.minimum(c + 1, last), 0))
    in_specs = ([x_spec, next_spec] + [_const_spec(w.shape) for w in weights]
                + [_state_spec(s, 0) for s in L1_STATE_SHAPES])
    out_shape = ([jax.ShapeDtypeStruct((bsz, length, D_MODEL), F32)]
                 + [jax.ShapeDtypeStruct((bsz,) + s, F32) for s in L1_STATE_SHAPES])
    out_specs = [x_spec] + [_state_spec(s, rows) for s in L1_STATE_SHAPES]
    scratch = [pltpu.VMEM((rows, SUBLANE, LRU_WIDTH), F32), pltpu.VMEM((rows, 1, LRU_WIDTH), F32),
               pltpu.VMEM((rows, q_len, 2 * LRU_WIDTH), F32)]
    return pl.pallas_call(
        functools.partial(_l1_prompt_kernel, front_pad=front_pad, rows=rows),
        grid=grid, in_specs=in_specs, out_specs=out_specs, out_shape=out_shape, scratch_shapes=scratch,
        compiler_params=pltpu.CompilerParams(dimension_semantics=("arbitrary", "arbitrary"),
                                             vmem_limit_bytes=VMEM_LIMIT),
        name="l1_prompt",
    )(h1, h1, *weights, *init)


def _conv_step(buf_ref, x, w_ref, b_ref, newbuf_ref):
    y = b_ref[...] + w_ref[3:4, :] * x
    for tap in range(TAIL):
        y = y + w_ref[tap:tap + 1, :] * buf_ref[tap]
    for tap in range(TAIL - 1):
        newbuf_ref[tap] = buf_ref[tap + 1]
    newbuf_ref[TAIL - 1] = x
    return y


def _l0_sample_pre_kernel(x_ref, nw_ref, win_ref, scw_ref, scb_ref, dtb_ref, alog_ref,
                          mcw_ref, mcb_ref, wqk_ref, wv_ref, wg_ref, bg_ref,
                          sbuf_ref, mbuf_ref, m0_ref, n0_ref,
                          nsb_ref, nmb_ref, zs_ref, xs_ref, bm_ref, cm_ref, xdt_t_ref, dec_t_ref,
                          zm_ref, xc_ref, q_ref, isv_t_ref, fs_t_ref, k_ref, mnew_ref, nnew_ref, den_ref):
    x = x_ref[...]
    hn = _rms(x, nw_ref[...])
    proj = _bdot(hn, _wload(win_ref[...]))
    zs_ref[...] = proj[:, OFF_ZS:OFF_ZS + SSD_WIDTH]
    zm_ref[...] = proj[:, OFF_ZM:OFF_ZM + ML_WIDTH]
    xbc = proj[:, OFF_XBC:OFF_XBC + SSD_CONV_CH]
    dt_raw = proj[:, OFF_DT:OFF_DT + LANE]
    xm = proj[:, OFF_XM:OFF_XM + ML_WIDTH]
    expand = _expand_matrix()

    xbc = _silu(_conv_step(sbuf_ref, xbc, scw_ref, scb_ref, nsb_ref))
    xs = xbc[:, :SSD_WIDTH]
    xs_ref[...] = xs
    bm_ref[...] = xbc[:, SSD_WIDTH:SSD_WIDTH + SSD_GROUPS * SSD_STATE]
    cm_ref[...] = xbc[:, SSD_WIDTH + SSD_GROUPS * SSD_STATE:]
    dt = _softplus(dt_raw + dtb_ref[...])
    log_a = -dt * jnp.exp(alog_ref[...])
    xdt_t_ref[...] = xs * _expand_heads(dt, expand)
    dec_t_ref[...] = _expand_heads(jnp.exp(log_a), expand)

    xc = _silu(_conv_step(mbuf_ref, xm, mcw_ref, mcb_ref, nmb_ref))
    xc_ref[...] = xc
    q, k, v, ig, logf = _mlstm_qkv_gates(xm, xc, wqk_ref, wv_ref, wg_ref, bg_ref)
    m0 = m0_ref[...]
    m_new = jnp.maximum(logf + m0, ig)
    fs = jnp.exp(logf + m0 - m_new)
    is_ = jnp.exp(ig - m_new)
    mnew_ref[...] = m_new
    r = lax.broadcasted_iota(jnp.int32, (LANE, ML_WIDTH), 0)
    cidx = lax.broadcasted_iota(jnp.int32, (LANE, ML_WIDTH), 1)
    expand_m = jnp.where(lax.shift_right_logical(cidx, 8) == r, 1.0, 0.0).astype(BF16)
    fs_e = _expand_heads(fs, expand_m)
    is_e = _expand_heads(is_, expand_m)
    n_new = fs_e * n0_ref[...] + is_e * k
    nnew_ref[...] = n_new
    q_ref[...] = q
    k_ref[...] = k
    isv_t_ref[...] = is_e * v
    fs_t_ref[...] = fs_e
    nq = n_new * q
    floor = jnp.exp(-m_new)
    for hd in range(ML_HEADS):
        den = jnp.sum(nq[:, hd * ML_HEAD_DIM:(hd + 1) * ML_HEAD_DIM], axis=-1, keepdims=True)
        den_ref[:, hd:hd + 1] = jnp.maximum(jnp.abs(den), floor[:, hd:hd + 1])


BT_S = 8
BT_C = 4


def _ssd_state_kernel(s_ref, xdt_ref, dec_ref, bm_ref, cm_ref, snew_ref, y_ref):
    for i in range(BT_S):
        x_col = xdt_ref[0, :, i:i + 1].reshape(SSD_HEADS, SSD_HEAD_DIM, 1)
        d_col = dec_ref[0, :, i:i + 1].reshape(SSD_HEADS, SSD_HEAD_DIM, 1)
        ys = []
        for g in range(SSD_GROUPS):
            hs = slice(g * SSD_HPG, (g + 1) * SSD_HPG)
            b_row = bm_ref[i:i + 1, g * SSD_STATE:(g + 1) * SSD_STATE].reshape(1, 1, SSD_STATE)
            c_row = cm_ref[i:i + 1, g * SSD_STATE:(g + 1) * SSD_STATE].reshape(1, 1, SSD_STATE)
            s_new = d_col[hs] * s_ref[i, hs] + x_col[hs] * b_row
            snew_ref[i, hs] = s_new
            ys.append(jnp.sum(s_new * c_row, axis=-1, keepdims=True))
        y_ref[0, :, i:i + 1] = jnp.concatenate(ys, axis=0).reshape(SSD_WIDTH, 1)


def _mlstm_state_kernel(c_ref, isv_ref, fs_ref, k_ref, q_ref, cnew_ref, num_ref):
    for i in range(BT_C):
        for hd in range(ML_HEADS):
            sl = slice(hd * ML_HEAD_DIM, (hd + 1) * ML_HEAD_DIM)
            v_col = isv_ref[0, sl, i:i + 1]
            f_col = fs_ref[0, sl, i:i + 1]
            c_new = f_col * c_ref[i, hd] + v_col * k_ref[0, i:i + 1, sl]
            cnew_ref[i, hd] = c_new
            num_ref[0, sl, i:i + 1] = jnp.sum(c_new * q_ref[0, i:i + 1, sl], axis=-1, keepdims=True)


def _sample_post_kernel(x_ref, ys_t_ref, num_t_ref, den_ref, zs_ref, xs_ref, zm_ref, xc_ref,
                        dsk_ref, snw_ref, msk_ref, mnw_ref, wout_ref,
                        nw1_ref, fnw_ref, win1_ref, wout1_ref, cw_ref, cb_ref,
                        wax_ref, ba_ref, bx_ref, lam_ref, lbuf_ref, h0_ref,
                        y_ref, nlb_ref, hnew_ref):
    xs = xs_ref[...]
    y_s = ys_t_ref[...] + dsk_ref[...] * xs
    y_s = _group_rmsnorm(y_s * _silu(zs_ref[...]), snw_ref[...])
    num = num_t_ref[...]
    den = den_ref[...]
    h_m = jnp.concatenate(
        [num[:, hd * ML_HEAD_DIM:(hd + 1) * ML_HEAD_DIM] / den[:, hd:hd + 1] for hd in range(ML_HEADS)], axis=-1)
    h_m = _head_layernorm(h_m) * mnw_ref[...]
    y_m = (h_m + msk_ref[...] * xc_ref[...]) * _silu(zm_ref[...])
    h1 = x_ref[...] + _bdot(jnp.concatenate([y_s, y_m], axis=-1), _wload(wout_ref[...]))

    hn = _rms(h1, nw1_ref[...])
    proj = _bdot(hn, _wload(win1_ref[...]))
    gate = proj[:, :LRU_WIDTH]
    xr = proj[:, LRU_WIDTH:]
    xc = _conv_step(lbuf_ref, xr, cw_ref, cb_ref, nlb_ref)
    ra, ix = _blockdiag_tiles(xc, wax_ref)
    a, u = _rglru_gates(xc, ra, ix, ba_ref, bx_ref, lam_ref)
    h = a * h0_ref[...] + u
    hnew_ref[...] = h
    h2 = h1 + _bdot(h * _silu(gate), _wload(wout1_ref[...]))
    y_ref[...] = _rms(h2, fnw_ref[...])


def _full_call(kernel_fn, out_shapes, args, name):
    return pl.pallas_call(
        kernel_fn,
        out_shape=[jax.ShapeDtypeStruct(s, F32) for s in out_shapes],
        compiler_params=pltpu.CompilerParams(vmem_limit_bytes=VMEM_LIMIT),
        name=name,
    )(*args)


def _to_cols(a, bt):
    rows, ch = a.shape
    return a.reshape(rows // bt, bt, ch).transpose(0, 2, 1)


def _from_cols(a):
    tiles, ch, bt = a.shape
    return a.transpose(0, 2, 1).reshape(tiles * bt, ch)


def _row(v, width=None):
    v = v.reshape(1, -1).astype(F32)
    if width is not None and v.shape[1] < width:
        v = jnp.pad(v, ((0, 0), (0, width - v.shape[1])))
    return v


def _pack(w):
    w = w.astype(BF16)
    k, n = w.shape[-2:]
    flat = w.reshape(-1, n)
    rows = flat.shape[0]
    block = PACK_ROWS if rows % PACK_ROWS == 0 else rows
    packed = pl.pallas_call(
        _pack_kernel,
        grid=(rows // block,),
        in_specs=[pl.BlockSpec((block, n), lambda i: (i, 0))],
        out_specs=pl.BlockSpec((block // 2, n), lambda i: (i, 0)),
        out_shape=jax.ShapeDtypeStruct((rows // 2, n), jnp.uint32),
        name="pack_weight",
    )(flat)
    return packed.reshape(w.shape[:-2] + (k // 2, n))


PACK_ROWS = 256


def _pack_kernel(w_ref, o_ref):
    o_ref[...] = pltpu.bitcast(w_ref[...], jnp.uint32)


def _dense_block_tiles(w):
    nb, bi, bo = w.shape
    per = LANE // bi
    w = w.reshape(nb // per, per, bi, bo)
    eye = jnp.eye(per, dtype=w.dtype)
    dense = jnp.einsum('tpio,pq->tpiqo', w, eye).reshape(nb // per, per * bi, per * bo)
    return dense.astype(BF16)


def kernel(x_prompt, x_sample, state_ssd_conv, state_ssd, state_mlstm_conv, state_mlstm_C, state_mlstm_n,
           state_mlstm_m, state_lru_conv, state_lru_h, meta_tokens, norm_w, final_norm_w, w_in_mix, w_out_mix,
           ssd_conv_w, ssd_conv_b, ssd_dt_bias, ssd_a_log, ssd_d, ssd_norm_w, ml_conv_w, ml_conv_b, ml_wq, ml_wk,
           ml_wv, ml_w_gate, ml_b_gate, ml_skip, ml_norm_w, lru_w_in, lru_w_out, lru_conv_w, lru_conv_b, lru_wa,
           lru_ba, lru_wx, lru_bx, lru_lambda):
    bsz = x_prompt.shape[0]
    dec = x_sample.shape[0]

    w_in = w_in_mix[0]
    o1 = SSD_WIDTH
    o2 = o1 + SSD_CONV_CH
    o3 = o2 + SSD_HEADS
    win = jnp.concatenate([w_in[:, :o2], jnp.pad(w_in[:, o2:o3], ((0, 0), (0, LANE - SSD_HEADS))), w_in[:, o3:]],
                          axis=1).astype(BF16)
    wout = w_out_mix[0].astype(BF16)
    nw0 = _row(norm_w[0])
    nw1 = _row(norm_w[1])
    fnw = _row(final_norm_w)
    scw = ssd_conv_w[0]
    scb = _row(ssd_conv_b[0])
    dtb = _row(ssd_dt_bias[0], LANE)
    alog = _row(ssd_a_log[0], LANE)
    dsk = _row(jnp.repeat(ssd_d[0], SSD_HEAD_DIM))
    snw = _row(ssd_norm_w[0])
    mcw = ml_conv_w[0]
    mcb = _row(ml_conv_b[0])
    wqk = jnp.concatenate([_dense_block_tiles(ml_wq[0]), _dense_block_tiles(ml_wk[0])], axis=2)
    wv = _dense_block_tiles(ml_wv[0])
    wg_raw = ml_w_gate[0]
    wg = jnp.concatenate([jnp.pad(wg_raw[:, :ML_HEADS], ((0, 0), (0, LANE - ML_HEADS))),
                          jnp.pad(wg_raw[:, ML_HEADS:], ((0, 0), (0, LANE - ML_HEADS)))], axis=1).astype(BF16)
    bg = jnp.concatenate([_row(ml_b_gate[0, :ML_HEADS], LANE), _row(ml_b_gate[0, ML_HEADS:], LANE)], axis=1)
    msk = _row(ml_skip[0])
    mnw = _row(ml_norm_w[0])
    win1 = lru_w_in[0].astype(BF16)
    wout1 = lru_w_out[0].astype(BF16)
    lcw = lru_conv_w[0]
    lcb = _row(lru_conv_b[0])
    wax = jnp.concatenate([lru_wa[0], lru_wx[0]], axis=2).astype(BF16)
    r_idx = lax.broadcasted_iota(jnp.int32, (LANE, SSD_WIDTH), 0)
    c_idx = lax.broadcasted_iota(jnp.int32, (LANE, SSD_WIDTH), 1)
    expand = (c_idx // SSD_HEAD_DIM == r_idx).astype(BF16)
    ba = _row(lru_ba[0])
    bx = _row(lru_bx[0])
    lam = _row(lru_lambda[0])

    win, wout, wqk, wv, wg, expand, win1, wout1, wax = (
        _pack(w) for w in (win, wout, wqk, wv, wg, expand, win1, wout1, wax))
    l0_w = (nw0, win, wout, scw, scb, dtb, alog, dsk, snw, mcw, mcb, wqk, wv, wg, bg, msk, mnw, expand)
    l1_w = (nw1, fnw, win1, wout1, lcw, lcb, wax, ba, bx, lam)

    zero0 = tuple(jnp.zeros((1,) + s, F32) for s in L0_STATE_SHAPES)
    zero1 = tuple(jnp.zeros((1,) + s, F32) for s in L1_STATE_SHAPES)
    meta = jnp.pad(meta_tokens.astype(F32), ((CHUNK - N_META, 0), (0, 0)))[None]
    meta_out = _l0_prompt(meta, l0_w, zero0, CHUNK - N_META)
    meta1_out = _l1_prompt(meta_out[0], l1_w, zero1, CHUNK - N_META)
    l0_out = _l0_prompt(x_prompt, l0_w, tuple(meta_out[1:]), 0)
    h1_p, p_sc, p_s_t, p_mc, p_ct, p_n, p_m = l0_out
    y_prompt, p_lc, p_lh = _l1_prompt(h1_p, l1_w, tuple(meta1_out[1:]), 0)

    p_s = p_s_t.reshape(bsz, SSD_GROUPS, SSD_STATE, SSD_HPG, SSD_HEAD_DIM).transpose(0, 1, 3, 4, 2)
    p_s = p_s.reshape(bsz, SSD_HEADS, SSD_HEAD_DIM, SSD_STATE)
    p_c = p_ct.transpose(0, 1, 3, 2)
    p_m = p_m[:, 0, :ML_HEADS]
    p_lh = p_lh[:, 0]

    xs2 = x_sample[:, 0]
    sbuf = jnp.moveaxis(state_ssd_conv[0], 1, 0)
    mbuf = jnp.moveaxis(state_mlstm_conv[0], 1, 0)
    lbuf = jnp.moveaxis(state_lru_conv[0], 1, 0)
    m0 = jnp.pad(state_mlstm_m[0], ((0, 0), (0, LANE - ML_HEADS)))
    n0 = state_mlstm_n[0].reshape(dec, ML_WIDTH)
    pre_shapes = ((TAIL, dec, SSD_CONV_CH), (TAIL, dec, ML_WIDTH), (dec, SSD_WIDTH), (dec, SSD_WIDTH),
                  (dec, SSD_GROUPS * SSD_STATE), (dec, SSD_GROUPS * SSD_STATE), (dec, SSD_WIDTH), (dec, SSD_WIDTH),
                  (dec, ML_WIDTH), (dec, ML_WIDTH), (dec, ML_WIDTH), (dec, ML_WIDTH), (dec, ML_WIDTH),
                  (dec, ML_WIDTH), (dec, LANE), (dec, ML_WIDTH), (dec, ML_HEADS))
    (nsb, nmb, zs, xs_c, bm, cm, xdt_t, dec_t, zm, xc_m, q, isv_t, fs_t, k, m_new, n_new, den) = _full_call(
        _l0_sample_pre_kernel, pre_shapes,
        (xs2, nw0, win, scw, scb, dtb, alog, mcw, mcb, wqk, wv, wg, bg, sbuf, mbuf, m0, n0), "l0_sample_pre")

    xdt_c = _to_cols(xdt_t, BT_S)
    dec_c = _to_cols(dec_t, BT_S)
    s_new, ys_c = pl.pallas_call(
        _ssd_state_kernel,
        grid=(dec // BT_S,),
        in_specs=[pl.BlockSpec((BT_S, SSD_HEADS, SSD_HEAD_DIM, SSD_STATE), lambda i: (i, 0, 0, 0)),
                  pl.BlockSpec((1, SSD_WIDTH, BT_S), lambda i: (i, 0, 0)),
                  pl.BlockSpec((1, SSD_WIDTH, BT_S), lambda i: (i, 0, 0)),
                  pl.BlockSpec((BT_S, SSD_GROUPS * SSD_STATE), lambda i: (i, 0)),
                  pl.BlockSpec((BT_S, SSD_GROUPS * SSD_STATE), lambda i: (i, 0))],
        out_specs=[pl.BlockSpec((BT_S, SSD_HEADS, SSD_HEAD_DIM, SSD_STATE), lambda i: (i, 0, 0, 0)),
                   pl.BlockSpec((1, SSD_WIDTH, BT_S), lambda i: (i, 0, 0))],
        out_shape=[jax.ShapeDtypeStruct((dec, SSD_HEADS, SSD_HEAD_DIM, SSD_STATE), F32),
                   jax.ShapeDtypeStruct((dec // BT_S, SSD_WIDTH, BT_S), F32)],
        compiler_params=pltpu.CompilerParams(dimension_semantics=("arbitrary",), vmem_limit_bytes=VMEM_LIMIT),
        name="ssd_state",
    )(state_ssd[0], xdt_c, dec_c, bm, cm)

    isv_c = _to_cols(isv_t, BT_C)
    fs_c = _to_cols(fs_t, BT_C)
    c_new, num_c = pl.pallas_call(
        _mlstm_state_kernel,
        grid=(dec // BT_C,),
        in_specs=[pl.BlockSpec((BT_C, ML_HEADS, ML_HEAD_DIM, ML_HEAD_DIM), lambda i: (i, 0, 0, 0)),
                  pl.BlockSpec((1, ML_WIDTH, BT_C), lambda i: (i, 0, 0)),
                  pl.BlockSpec((1, ML_WIDTH, BT_C), lambda i: (i, 0, 0)),
                  pl.BlockSpec((1, BT_C, ML_WIDTH), lambda i: (i, 0, 0)),
                  pl.BlockSpec((1, BT_C, ML_WIDTH), lambda i: (i, 0, 0))],
        out_specs=[pl.BlockSpec((BT_C, ML_HEADS, ML_HEAD_DIM, ML_HEAD_DIM), lambda i: (i, 0, 0, 0)),
                   pl.BlockSpec((1, ML_WIDTH, BT_C), lambda i: (i, 0, 0))],
        out_shape=[jax.ShapeDtypeStruct((dec, ML_HEADS, ML_HEAD_DIM, ML_HEAD_DIM), F32),
                   jax.ShapeDtypeStruct((dec // BT_C, ML_WIDTH, BT_C), F32)],
        compiler_params=pltpu.CompilerParams(dimension_semantics=("arbitrary",), vmem_limit_bytes=VMEM_LIMIT),
        name="mlstm_state",
    )(state_mlstm_C[0], isv_c, fs_c, k.reshape(dec // BT_C, BT_C, ML_WIDTH), q.reshape(dec // BT_C, BT_C, ML_WIDTH))

    post_shapes = ((dec, D_MODEL), (TAIL, dec, LRU_WIDTH), (dec, LRU_WIDTH))
    y_s2, nlb, h_new = _full_call(
        _sample_post_kernel, post_shapes,
        (xs2, _from_cols(ys_c), _from_cols(num_c), den, zs, xs_c, zm, xc_m, dsk, snw, msk, mnw, wout,
         nw1, fnw, win1, wout1, lcw, lcb, wax, ba, bx, lam, lbuf, state_lru_h[0]), "sample_post")

    s_sc = jnp.moveaxis(nsb, 0, 1)[None]
    s_mc = jnp.moveaxis(nmb, 0, 1)[None]
    s_lc = jnp.moveaxis(nlb, 0, 1)[None]
    return (y_prompt, y_s2[:, None, :],
            p_sc[None], p_s[None], p_mc[None], p_c[None], p_n[None], p_m[None], p_lc[None], p_lh[None],
            s_sc, s_new[None], s_mc, c_new[None], n_new.reshape(dec, ML_HEADS, ML_HEAD_DIM)[None],
            m_new[:, :ML_HEADS][None], s_lc, h_new[None])
```

```python
import functools

import jax
import jax.numpy as jnp
from jax import lax
from jax.experimental import pallas as pl
from jax.experimental.pallas import tpu as pltpu

F32 = jnp.float32
BF16 = jnp.bfloat16

D_MODEL = 1024
N_META = 16
CONV_W = 4
EPS = 1e-6
NEG = -1e30
SSD_WIDTH = 1024
SSD_HEAD_DIM = 64
SSD_HEADS = 16
SSD_GROUPS = 2
SSD_HPG = 8
SSD_STATE = 128
SSD_CONV_CH = 1536
ML_WIDTH = 1024
ML_HEADS = 4
ML_HEAD_DIM = 256
ML_QKV_BLOCK = 4
LRU_WIDTH = 2048
LRU_BLOCKS = 16
LRU_BLOCK = 128
LRU_C = 8.0

LANE = 128
SUBLANE = 8
CHUNK = 128
L0_ROWS = 2
L1_ROWS = 2
L1_GROUPS = 4
TAIL = CONV_W - 1

OFF_ZS = 0
OFF_XBC = OFF_ZS + SSD_WIDTH
OFF_DT = OFF_XBC + SSD_CONV_CH
OFF_ZM = OFF_DT + LANE
OFF_XM = OFF_ZM + ML_WIDTH
IN_MIX_PAD = OFF_XM + ML_WIDTH

VMEM_LIMIT = 56 * 1024 * 1024


def _sigmoid(x):
    return 1.0 / (1.0 + jnp.exp(-x))


def _silu(x):
    return x * _sigmoid(x)


def _softplus(x):
    return jnp.maximum(x, 0.0) + jnp.log1p(jnp.exp(-jnp.abs(x)))


def _rms(x, w):
    return x * lax.rsqrt(jnp.mean(x * x, axis=-1, keepdims=True) + EPS) * w


def _bdot(a, b):
    return jnp.dot(a.astype(BF16), b.astype(BF16), preferred_element_type=F32)


def _wload(w):
    return pltpu.bitcast(w, BF16)


def _split3(x):
    hi = x.astype(BF16)
    r = x - hi.astype(F32)
    mid = r.astype(BF16)
    lo = (r - mid.astype(F32)).astype(BF16)
    return hi, mid, lo


def _cumsum_rows(x, tril):
    hi, mid, lo = _split3(x)
    d = functools.partial(jnp.dot, preferred_element_type=F32)
    return d(tril, hi) + d(tril, mid) + d(tril, lo)


def _expand_heads(x, expand):
    hi, mid, _ = _split3(x)
    d = functools.partial(jnp.dot, preferred_element_type=F32)
    return d(hi, expand) + d(mid, expand)


def _expand_matrix():
    r = lax.broadcasted_iota(jnp.int32, (LANE, SSD_WIDTH), 0)
    c = lax.broadcasted_iota(jnp.int32, (LANE, SSD_WIDTH), 1)
    return jnp.where(lax.shift_right_logical(c, 6) == r, 1.0, 0.0).astype(BF16)


def _blockdiag_tiles(x, w_ref):
    k = w_ref.shape[0]
    m = w_ref.shape[2] // LANE
    prods = [_bdot(x[:, t * LANE:(t + 1) * LANE], _wload(w_ref[t])) for t in range(k)]
    return [jnp.concatenate([p[:, j * LANE:(j + 1) * LANE] for p in prods], axis=-1) for j in range(m)]


def _group_rmsnorm(y, w):
    half = SSD_WIDTH // SSD_GROUPS
    parts = []
    for g in range(SSD_GROUPS):
        yg = y[:, g * half:(g + 1) * half]
        parts.append(yg * lax.rsqrt(jnp.mean(yg * yg, axis=-1, keepdims=True) + EPS))
    return jnp.concatenate(parts, axis=-1) * w


def _head_layernorm(h):
    parts = []
    for k in range(ML_HEADS):
        hk = h[:, k * ML_HEAD_DIM:(k + 1) * ML_HEAD_DIM]
        mu = jnp.mean(hk, axis=-1, keepdims=True)
        d = hk - mu
        var = jnp.mean(d * d, axis=-1, keepdims=True)
        parts.append(d * lax.rsqrt(var + EPS))
    return jnp.concatenate(parts, axis=-1)


def _mlstm_qkv_gates(xm, xc, wqk_ref, wv_ref, wg_ref, bg_ref):
    q, k = _blockdiag_tiles(xc, wqk_ref)
    v, = _blockdiag_tiles(xm, wv_ref)
    gates = _bdot(jnp.concatenate([q, k, v], axis=-1), _wload(wg_ref[...])) + bg_ref[...]
    ig = gates[:, :LANE]
    logf = -_softplus(-gates[:, LANE:])
    return q, k * (ML_HEAD_DIM ** -0.5), v, ig, logf


N_L0_W = 18
N_L0_S = 6


def _l0_prompt_kernel(x_ref, xnext_ref, *refs, front_pad, rows):
    w_refs = refs[:N_L0_W]
    init_refs = refs[N_L0_W:N_L0_W + N_L0_S]
    out_refs = refs[N_L0_W + N_L0_S:N_L0_W + 2 * N_L0_S + 1]
    scratch = refs[N_L0_W + 2 * N_L0_S + 1:]
    c = pl.program_id(1)

    def each_row(phase):
        return [_l0_prompt_row(x_ref.at[r], xnext_ref.at[r], *w_refs, *init_refs, *(o.at[r] for o in out_refs),
                               *(s.at[r] for s in scratch), front_pad=front_pad, phase=phase)
                for r in range(rows)]

    @pl.when(c == 0)
    def _():
        each_row("init")

    bodies = each_row("body")
    _run_round_robin([gen for gens, _ in bodies for gen in gens])
    for _, finish in bodies:
        finish()

    @pl.when(c == pl.num_programs(1) - 1)
    def _():
        each_row("final")


def _run_round_robin(gens):
    live = list(gens)
    while live:
        for gen in list(live):
            if next(gen, "done") == "done":
                live.remove(gen)


def _l0_prompt_row(x_ref, xnext_ref, nw_ref, win_ref, wout_ref,
                   scw_ref, scb_ref, dtb_ref, alog_ref, dsk_ref, snw_ref,
                   mcw_ref, mcb_ref, wqk_ref, wv_ref, wg_ref, bg_ref, msk_ref, mnw_ref, expand_ref,
                   isc_ref, iss_ref, imc_ref, ict_ref, inn_ref, imm_ref,
                   h1_ref, osc_ref, oss_ref, omc_ref, oct_ref, onn_ref, omm_ref,
                   sbuf, mbuf, s_st, ct_st, n_st, m_st, proj_s, *, front_pad, phase):
    q_len = x_ref.shape[0]

    if phase == "init":
        sbuf[...] = jnp.zeros(sbuf.shape, F32)
        mbuf[...] = jnp.zeros(mbuf.shape, F32)
        sbuf[SUBLANE - TAIL:SUBLANE, :] = isc_ref[0]
        mbuf[SUBLANE - TAIL:SUBLANE, :] = imc_ref[0]
        for g in range(SSD_GROUPS):
            heads = iss_ref[0, g * SSD_HPG:(g + 1) * SSD_HPG]
            s_st[g] = heads.reshape(SSD_HPG * SSD_HEAD_DIM, SSD_STATE).T
        for hd in range(ML_HEADS):
            ct_st[hd] = ict_ref[0, hd].T
        n_st[...] = inn_ref[0]
        m_st[...] = imm_ref[0]
        hn0 = _move_rows(_perm_matrices(q_len)[0], _rms(x_ref[...], nw_ref[...]).astype(BF16))
        proj_s[...] = _bdot(hn0, _wload(win_ref[...]))
        return None
    if phase == "final":
        osc_ref[...] = sbuf[SUBLANE - TAIL:SUBLANE, :]
        omc_ref[...] = mbuf[SUBLANE - TAIL:SUBLANE, :]
        for g in range(SSD_GROUPS):
            oss_ref[g * SSD_HPG:(g + 1) * SSD_HPG] = s_st[g].T.reshape(SSD_HPG, SSD_HEAD_DIM, SSD_STATE)
        for hd in range(ML_HEADS):
            oct_ref[hd] = ct_st[hd].T
        onn_ref[...] = n_st[...]
        omm_ref[...] = m_st[...]
        return None

    x = x_ref[...]
    to_perm, to_time = _perm_matrices(q_len)
    t_col = _perm_time(q_len)
    t_row = _perm_time(q_len, row=True)
    causal = t_col >= t_row
    tril = jnp.where(causal, 1.0, 0.0).astype(BF16)
    valid = (t_col >= front_pad) if front_pad else None
    hn_next = _move_rows(to_perm, _rms(xnext_ref[...], nw_ref[...]).astype(BF16))
    partial_out = []

    def in_proj_next(cols):
        proj_s[:, cols] = _bdot(hn_next, _wload(win_ref[:, cols]))

    def out_piece(y_piece, k0):
        width = y_piece.shape[1]
        y_t = _move_rows(to_time, y_piece.astype(BF16))
        partial_out.append(_bdot(y_t, _wload(wout_ref[k0 // 2:(k0 + width) // 2, :])))

    def ssd():
        xbc = proj_s[:, OFF_XBC:OFF_XBC + SSD_CONV_CH]
        dt_raw = proj_s[:, OFF_DT:OFF_DT + LANE]
        xbc = _silu(_conv_perm(sbuf, xbc, scw_ref, scb_ref))
        in_proj_next(slice(OFF_XBC, OFF_DT + LANE))
        yield
        xs = xbc[:, :SSD_WIDTH]
        bm = xbc[:, SSD_WIDTH:SSD_WIDTH + SSD_GROUPS * SSD_STATE]
        cm = xbc[:, SSD_WIDTH + SSD_GROUPS * SSD_STATE:]
        dt = _softplus(dt_raw + dtb_ref[...])
        if front_pad:
            dt = jnp.where(valid, dt, 0.0)
        log_a = -dt * jnp.exp(alog_ref[...])
        a_cs = _cumsum_rows(log_a, tril)
        yield
        a_last = a_cs[q_len - 1:q_len, :]
        expand = _wload(expand_ref[...])
        w_state = _expand_heads(dt * jnp.exp(a_last - a_cs), expand)
        e_acs = _expand_heads(jnp.exp(a_cs), expand)
        a_cs_t = a_cs.T
        dt_t = dt.T
        yield
        pair_lo = lax.broadcasted_iota(jnp.int32, (q_len, LANE), 1) < SSD_HEAD_DIM
        half = SSD_WIDTH // SSD_GROUPS
        y_groups = []
        for g in range(SSD_GROUPS):
            bg = bm[:, g * SSD_STATE:(g + 1) * SSD_STATE]
            cg = cm[:, g * SSD_STATE:(g + 1) * SSD_STATE]
            bg_t = bg.T
            xg = xs[:, g * half:(g + 1) * half]
            eg = e_acs[:, g * half:(g + 1) * half]
            s_old = s_st[g]
            cb = _bdot(cg, bg_t)
            y_off = _bdot(cg, s_old) * eg
            s_st[g] = eg[q_len - 1:q_len, :] * s_old + _bdot(bg_t, xg * w_state[:, g * half:(g + 1) * half])
            yield
            y_pairs = []
            for pr in range(SSD_HPG // 2):
                ms = []
                for e in (2 * pr, 2 * pr + 1):
                    hd = g * SSD_HPG + e
                    seg = jnp.exp(jnp.where(causal, a_cs[:, hd:hd + 1] - a_cs_t[hd:hd + 1, :], -jnp.inf))
                    ms.append(cb * seg * dt_t[hd:hd + 1, :])
                xp = xg[:, pr * LANE:(pr + 1) * LANE]
                rhs = jnp.concatenate([jnp.where(pair_lo, xp, 0.0), jnp.where(pair_lo, 0.0, xp)], axis=0)
                y_pairs.append(_bdot(jnp.concatenate(ms, axis=-1), rhs))
                yield
            y_groups.append(jnp.concatenate(y_pairs, axis=-1) + y_off)
        z_s = proj_s[:, OFF_ZS:OFF_ZS + SSD_WIDTH]
        in_proj_next(slice(OFF_ZS, OFF_ZS + SSD_WIDTH))
        y_s = jnp.concatenate(y_groups, axis=-1) + dsk_ref[...] * xs
        y_s = _group_rmsnorm(y_s * _silu(z_s), snw_ref[...])
        yield
        out_piece(y_s, 0)

    def mlstm():
        xm = proj_s[:, OFF_XM:OFF_XM + ML_WIDTH]
        xc = _silu(_conv_perm(mbuf, xm, mcw_ref, mcb_ref))
        in_proj_next(slice(OFF_XM, OFF_XM + ML_WIDTH))
        yield
        q, k = _blockdiag_tiles(xc, wqk_ref)
        v, = _blockdiag_tiles(xm, wv_ref)
        yield
        gates = _bdot(jnp.concatenate([q, k, v], axis=-1), _wload(wg_ref[...])) + bg_ref[...]
        k = k * (ML_HEAD_DIM ** -0.5)
        yield
        ig = gates[:, :LANE]
        logf = -_softplus(-gates[:, LANE:])
        if front_pad:
            ig = jnp.where(valid, ig, NEG)
            logf = jnp.where(valid, logf, 0.0)
        bcum = _cumsum_rows(logf, tril)
        yield
        ftot = bcum[q_len - 1:q_len, :]
        m_prev = m_st[...]
        w_end = ftot - bcum + ig
        m_new = jnp.maximum(ftot + m_prev, jnp.max(w_end, axis=0, keepdims=True))
        sc = jnp.exp(ftot + m_prev - m_new)
        wexp = jnp.exp(w_end - m_new)
        inter = bcum + m_prev
        bcum_t = bcum.T
        ig_t = ig.T
        m_st[...] = m_new
        z_m = proj_s[:, OFF_ZM:OFF_ZM + ML_WIDTH]
        in_proj_next(slice(OFF_ZM, OFF_ZM + ML_WIDTH))
        yield
        for hd in range(ML_HEADS):
            sl = slice(hd * ML_HEAD_DIM, (hd + 1) * ML_HEAD_DIM)
            q_h, k_h, v_h = q[:, sl], k[:, sl], v[:, sl]
            k_t = k_h.T
            dmat = jnp.where(causal, bcum[:, hd:hd + 1] - bcum_t[hd:hd + 1, :] + ig_t[hd:hd + 1, :], -jnp.inf)
            inter_h = inter[:, hd:hd + 1]
            m_t = jnp.maximum(inter_h, jnp.max(dmat, axis=-1, keepdims=True))
            dexp = jnp.exp(dmat - m_t)
            inter_sc = jnp.exp(inter_h - m_t)
            s = _bdot(q_h, k_t) * dexp
            yield
            ct_old = ct_st[hd]
            n_old = n_st[hd:hd + 1, :]
            num = _bdot(s, v_h) + inter_sc * _bdot(q_h, ct_old)
            den = jnp.sum(s, axis=-1, keepdims=True) + inter_sc * jnp.sum(q_h * n_old, axis=-1, keepdims=True)
            h_h = num / jnp.maximum(jnp.abs(den), jnp.exp(-m_t))
            w_col = wexp[:, hd:hd + 1]
            sc_h = sc[:, hd:hd + 1]
            ct_st[hd] = sc_h * ct_old + _bdot(k_t, v_h * w_col)
            n_st[hd:hd + 1, :] = sc_h * n_old + jnp.sum(k_h * w_col, axis=0, keepdims=True)
            yield
            mu = jnp.mean(h_h, axis=-1, keepdims=True)
            dev = h_h - mu
            var = jnp.mean(dev * dev, axis=-1, keepdims=True)
            h_h = dev * lax.rsqrt(var + EPS) * mnw_ref[:, sl]
            out_piece((h_h + msk_ref[:, sl] * xc[:, sl]) * _silu(z_m[:, sl]), SSD_WIDTH + hd * ML_HEAD_DIM)
            yield

    def finish():
        h1 = x
        for part in partial_out:
            h1 = h1 + part
        if front_pad:
            h1 = jnp.where(lax.broadcasted_iota(jnp.int32, (q_len, 1), 0) >= front_pad, h1, 0.0)
        h1_ref[...] = h1

    return [ssd(), mlstm()], finish


def _const_spec(shape):
    nd = len(shape)
    return pl.BlockSpec(shape, lambda b, c: (0,) * nd)


def _state_spec(shape, rows):
    nd = len(shape)
    if rows:
        return pl.BlockSpec((rows,) + shape, lambda b, c: (b,) + (0,) * nd)
    return pl.BlockSpec((1,) + shape, lambda b, c: (0,) * (nd + 1))


def _rows_per_step(bsz, want):
    return want if bsz % want == 0 else 1


L0_STATE_SHAPES = ((TAIL, SSD_CONV_CH), (SSD_HEADS, SSD_HEAD_DIM, SSD_STATE), (TAIL, ML_WIDTH),
                   (ML_HEADS, ML_HEAD_DIM, ML_HEAD_DIM), (ML_HEADS, ML_HEAD_DIM), (1, LANE))
L0_CARRY_SHAPES = ((SUBLANE, SSD_CONV_CH), (SUBLANE, ML_WIDTH), (SSD_GROUPS, SSD_STATE, SSD_WIDTH // SSD_GROUPS),
                   (ML_HEADS, ML_HEAD_DIM, ML_HEAD_DIM), (ML_HEADS, ML_HEAD_DIM), (1, LANE))


def _l0_prompt(x, weights, init, front_pad):
    bsz, length, _ = x.shape
    q_len = min(CHUNK, length)
    assert length % q_len == 0
    rows = _rows_per_step(bsz, L0_ROWS)
    assert len(weights) == N_L0_W and len(init) == N_L0_S
    grid = (bsz // rows, length // q_len)
    last = length // q_len - 1
    x_spec = pl.BlockSpec((rows, q_len, D_MODEL), lambda b, c: (b, c, 0))
    next_spec = pl.BlockSpec((rows, q_len, D_MODEL), lambda b, c: (b, jnp.minimum(c + 1, last), 0))
    in_specs = ([x_spec, next_spec] + [_const_spec(w.shape) for w in weights]
                + [_state_spec(s, 0) for s in L0_STATE_SHAPES])
    out_shape = ([jax.ShapeDtypeStruct((bsz, length, D_MODEL), F32)]
                 + [jax.ShapeDtypeStruct((bsz,) + s, F32) for s in L0_STATE_SHAPES])
    out_specs = [x_spec] + [_state_spec(s, rows) for s in L0_STATE_SHAPES]
    scratch = [pltpu.VMEM((rows,) + s, F32) for s in L0_CARRY_SHAPES + ((q_len, IN_MIX_PAD),)]
    return pl.pallas_call(
        functools.partial(_l0_prompt_kernel, front_pad=front_pad, rows=rows),
        grid=grid, in_specs=in_specs, out_specs=out_specs, out_shape=out_shape, scratch_shapes=scratch,
        compiler_params=pltpu.CompilerParams(dimension_semantics=("arbitrary", "arbitrary"),
                                             vmem_limit_bytes=VMEM_LIMIT),
        name="l0_prompt",
    )(x, x, *weights, *init)


def _rglru_gates(xc, ra, ix, ba_ref, bx_ref, lam_ref):
    r = _sigmoid(ra + ba_ref[...])
    i = _sigmoid(ix + bx_ref[...])
    log_a = r * (-LRU_C * _softplus(-lam_ref[...]))
    a = jnp.exp(log_a)
    u = jnp.sqrt(1.0 - a * a) * (i * xc)
    return a, u


def _perm_time(n, row=False):
    p = lax.broadcasted_iota(jnp.int32, (1, n) if row else (n, 1), 1 if row else 0)
    return (n // SUBLANE) * (p & (SUBLANE - 1)) + lax.shift_right_logical(p, 3)


def _perm_matrices(n):
    nb = n // SUBLANE
    r = lax.broadcasted_iota(jnp.int32, (n, n), 0)
    c = lax.broadcasted_iota(jnp.int32, (n, n), 1)
    to_perm = jnp.where(c == nb * (r & (SUBLANE - 1)) + lax.shift_right_logical(r, 3), 1.0, 0.0)
    to_time = jnp.where(r == nb * (c & (SUBLANE - 1)) + lax.shift_right_logical(c, 3), 1.0, 0.0)
    return to_perm.astype(BF16), to_time.astype(BF16)


def _move_rows(sel, x_bf16):
    return jnp.dot(sel, x_bf16, preferred_element_type=F32).astype(BF16)


def _conv_perm(tail_ref, x, w_ref, b_ref):
    n, ch = x.shape
    nb = n // SUBLANE
    x3 = x.reshape(nb, SUBLANE, ch)
    tail8 = tail_ref[...]
    sub = lax.broadcasted_iota(jnp.int32, (SUBLANE, ch), 0)
    y = b_ref[...].reshape(1, 1, ch) + w_ref[TAIL:TAIL + 1, :].reshape(1, 1, ch) * x3
    wrapped = [jnp.where(sub >= 1, pltpu.roll(x3[nb - d], 1, 0), tail8[SUBLANE - d:SUBLANE - d + 1, :])
               for d in range(1, CONV_W)]
    for back in range(1, CONV_W):
        head = jnp.stack([wrapped[back - j - 1] for j in range(back)], axis=0)
        shifted = jnp.concatenate([head, x3[:nb - back]], axis=0)
        y = y + w_ref[TAIL - back:TAIL - back + 1, :].reshape(1, 1, ch) * shifted
    for d in range(1, CONV_W):
        tail_ref[SUBLANE - d:SUBLANE - d + 1, :] = x3[nb - d][SUBLANE - 1:SUBLANE, :]
    return y.reshape(n, ch)


def _scan_perm(a, u, h_prev):
    n, ch = a.shape
    nb = n // SUBLANE
    a3 = a.reshape(nb, SUBLANE, ch)
    u3 = u.reshape(nb, SUBLANE, ch)
    local = [u3[0]]
    decay = [a3[0]]
    for j in range(1, nb):
        local.append(a3[j] * local[-1] + u3[j])
        decay.append(a3[j] * decay[-1])
    seg_u, seg_a = local[-1], decay[-1]
    sub = lax.broadcasted_iota(jnp.int32, (SUBLANE, ch), 0)
    shift = 1
    while shift < SUBLANE:
        keep = sub >= shift
        seg_u = seg_u + seg_a * jnp.where(keep, pltpu.roll(seg_u, shift, 0), 0.0)
        seg_a = seg_a * jnp.where(keep, pltpu.roll(seg_a, shift, 0), 1.0)
        shift *= 2
    seg_end = seg_a * h_prev + seg_u
    carry = jnp.where(sub >= 1, pltpu.roll(seg_end, 1, 0), h_prev)
    h3 = jnp.stack([local[j] + decay[j] * carry for j in range(nb)], axis=0)
    return h3.reshape(n, ch), seg_end[SUBLANE - 1:SUBLANE, :]


N_L1_W = 10
N_L1_S = 2


def _l1_prompt_kernel(h_ref, hnext_ref, *refs, front_pad, rows):
    w_refs = refs[:N_L1_W]
    init_refs = refs[N_L1_W:N_L1_W + N_L1_S]
    out_refs = refs[N_L1_W + N_L1_S:N_L1_W + 2 * N_L1_S + 1]
    scratch = refs[N_L1_W + 2 * N_L1_S + 1:]
    c = pl.program_id(1)

    def each_row(phase):
        return [_l1_prompt_row(h_ref.at[r], hnext_ref.at[r], *w_refs, *init_refs, *(o.at[r] for o in out_refs),
                               *(s.at[r] for s in scratch), front_pad=front_pad, phase=phase)
                for r in range(rows)]

    @pl.when(c == 0)
    def _():
        each_row("init")

    bodies = each_row("body")
    _run_staggered([gen for gens, _ in bodies for gen in gens])
    for _, finish in bodies:
        finish()

    @pl.when(c == pl.num_programs(1) - 1)
    def _():
        each_row("final")


def _l1_in_proj(h_val, nw_ref, win_ref, to_perm):
    hn = _move_rows(to_perm, _rms(h_val, nw_ref[...]).astype(BF16))
    return _bdot(hn, _wload(win_ref[...]))


def _l1_prompt_row(h_ref, hnext_ref, nw_ref, fnw_ref, win_ref, wout_ref, cw_ref, cb_ref,
                   wax_ref, ba_ref, bx_ref, lam_ref, ilc_ref, ilh_ref,
                   y_ref, olc_ref, olh_ref, lbuf, h_st, proj_s, *, front_pad, phase):
    if phase == "init":
        lbuf[...] = jnp.zeros(lbuf.shape, F32)
        lbuf[SUBLANE - TAIL:SUBLANE, :] = ilc_ref[0]
        h_st[...] = ilh_ref[0]
        proj_s[...] = _l1_in_proj(h_ref[...], nw_ref, win_ref, _perm_matrices(h_ref.shape[0])[0])
        return None
    if phase == "final":
        olc_ref[...] = lbuf[SUBLANE - TAIL:SUBLANE, :]
        olh_ref[...] = h_st[...]
        return None
    return _l1_row_body(h_ref, hnext_ref, nw_ref, fnw_ref, win_ref, wout_ref, cw_ref, cb_ref,
                        wax_ref, ba_ref, bx_ref, lam_ref, y_ref, lbuf, h_st, proj_s, front_pad)


def _l1_row_body(h_ref, hnext_ref, nw_ref, fnw_ref, win_ref, wout_ref, cw_ref, cb_ref,
                 wax_ref, ba_ref, bx_ref, lam_ref, y_ref, lbuf, h_st, proj_s, front_pad):
    q_len = h_ref.shape[0]
    h_in = h_ref[...]
    to_perm, to_time = _perm_matrices(q_len)
    hn_next = _move_rows(to_perm, _rms(hnext_ref[...], nw_ref[...]).astype(BF16))
    if front_pad:
        valid = _perm_time(q_len) >= front_pad
    gw = LRU_WIDTH // L1_GROUPS
    tiles = gw // LANE
    partial_out = []

    def group(g):
        cg = slice(g * gw, (g + 1) * gw)
        cx = slice(LRU_WIDTH + g * gw, LRU_WIDTH + (g + 1) * gw)
        gate = proj_s[:, cg]
        xr = proj_s[:, cx]
        xc = _conv_perm(lbuf.at[:, cg], xr, cw_ref.at[:, cg], cb_ref.at[:, cg])
        ra, ix = _blockdiag_tiles(xc, wax_ref.at[g * tiles:(g + 1) * tiles])
        yield
        proj_s[:, cg] = _bdot(hn_next, _wload(win_ref[:, cg]))
        proj_s[:, cx] = _bdot(hn_next, _wload(win_ref[:, cx]))
        a, u = _rglru_gates(xc, ra, ix, ba_ref.at[:, cg], bx_ref.at[:, cg], lam_ref.at[:, cg])
        if front_pad:
            a = jnp.where(valid, a, 1.0)
            u = jnp.where(valid, u, 0.0)
        yield
        h, h_last = _scan_perm(a, u, h_st[:, cg])
        h_st[:, cg] = h_last
        yield
        y = _move_rows(to_time, (h * _silu(gate)).astype(BF16))
        partial_out.append(_bdot(y, _wload(wout_ref[g * gw // 2:(g + 1) * gw // 2, :])))

    def finish():
        h2 = h_in
        for part in partial_out:
            h2 = h2 + part
        y_ref[...] = _rms(h2, fnw_ref[...])

    return [group(g) for g in range(L1_GROUPS)], finish


def _run_staggered(gens):
    live = []
    pending = list(gens)
    while pending or live:
        if pending:
            live.append(pending.pop(0))
        for gen in list(live):
            if next(gen, "done") == "done":
                live.remove(gen)


L1_STATE_SHAPES = ((TAIL, LRU_WIDTH), (1, LRU_WIDTH))


def _l1_prompt(h1, weights, init, front_pad):
    bsz, length, _ = h1.shape
    q_len = min(CHUNK, length)
    assert length % q_len == 0
    rows = _rows_per_step(bsz, L1_ROWS)
    assert len(weights) == N_L1_W and len(init) == N_L1_S
    grid = (bsz // rows, length // q_len)
    last = length // q_len - 1
    x_spec = pl.BlockSpec((rows, q_len, D_MODEL), lambda b, c: (b, c, 0))
    next_spec = pl.BlockSpec((rows, q_len, D_MODEL), lambda b, c: (b, jnp.minimum(c + 1, last), 0))
    in_specs = ([x_spec, next_spec] + [_const_spec(w.shape) for w in weights]
                + [_state_spec(s, 0) for s in L1_STATE_SHAPES])
    out_shape = ([jax.ShapeDtypeStruct((bsz, length, D_MODEL), F32)]
                 + [jax.ShapeDtypeStruct((bsz,) + s, F32) for s in L1_STATE_SHAPES])
    out_specs = [x_spec] + [_state_spec(s, rows) for s in L1_STATE_SHAPES]
    scratch = [pltpu.VMEM((rows, SUBLANE, LRU_WIDTH), F32), pltpu.VMEM((rows, 1, LRU_WIDTH), F32),
               pltpu.VMEM((rows, q_len, 2 * LRU_WIDTH), F32)]
    return pl.pallas_call(
        functools.partial(_l1_prompt_kernel, front_pad=front_pad, rows=rows),
        grid=grid, in_specs=in_specs, out_specs=out_specs, out_shape=out_shape, scratch_shapes=scratch,
        compiler_params=pltpu.CompilerParams(dimension_semantics=("arbitrary", "arbitrary"),
                                             vmem_limit_bytes=VMEM_LIMIT),
        name="l1_prompt",
    )(h1, h1, *weights, *init)


def _conv_step(buf_ref, x, w_ref, b_ref, newbuf_ref):
    y = b_ref[...] + w_ref[3:4, :] * x
    for tap in range(TAIL):
        y = y + w_ref[tap:tap + 1, :] * buf_ref[tap]
    for tap in range(TAIL - 1):
        newbuf_ref[tap] = buf_ref[tap + 1]
    newbuf_ref[TAIL - 1] = x
    return y


def _l0_sample_pre_kernel(x_ref, nw_ref, win_ref, scw_ref, scb_ref, dtb_ref, alog_ref,
                          mcw_ref, mcb_ref, wqk_ref, wv_ref, wg_ref, bg_ref,
                          sbuf_ref, mbuf_ref, m0_ref, n0_ref,
                          nsb_ref, nmb_ref, zs_ref, xs_ref, bm_ref, cm_ref, xdt_t_ref, dec_t_ref,
                          zm_ref, xc_ref, q_ref, isv_t_ref, fs_t_ref, k_ref, mnew_ref, nnew_ref, den_ref):
    x = x_ref[...]
    hn = _rms(x, nw_ref[...])
    proj = _bdot(hn, _wload(win_ref[...]))
    zs_ref[...] = proj[:, OFF_ZS:OFF_ZS + SSD_WIDTH]
    zm_ref[...] = proj[:, OFF_ZM:OFF_ZM + ML_WIDTH]
    xbc = proj[:, OFF_XBC:OFF_XBC + SSD_CONV_CH]
    dt_raw = proj[:, OFF_DT:OFF_DT + LANE]
    xm = proj[:, OFF_XM:OFF_XM + ML_WIDTH]
    expand = _expand_matrix()

    xbc = _silu(_conv_step(sbuf_ref, xbc, scw_ref, scb_ref, nsb_ref))
    xs = xbc[:, :SSD_WIDTH]
    xs_ref[...] = xs
    bm_ref[...] = xbc[:, SSD_WIDTH:SSD_WIDTH + SSD_GROUPS * SSD_STATE]
    cm_ref[...] = xbc[:, SSD_WIDTH + SSD_GROUPS * SSD_STATE:]
    dt = _softplus(dt_raw + dtb_ref[...])
    log_a = -dt * jnp.exp(alog_ref[...])
    xdt_t_ref[...] = xs * _expand_heads(dt, expand)
    dec_t_ref[...] = _expand_heads(jnp.exp(log_a), expand)

    xc = _silu(_conv_step(mbuf_ref, xm, mcw_ref, mcb_ref, nmb_ref))
    xc_ref[...] = xc
    q, k, v, ig, logf = _mlstm_qkv_gates(xm, xc, wqk_ref, wv_ref, wg_ref, bg_ref)
    m0 = m0_ref[...]
    m_new = jnp.maximum(logf + m0, ig)
    fs = jnp.exp(logf + m0 - m_new)
    is_ = jnp.exp(ig - m_new)
    mnew_ref[...] = m_new
    r = lax.broadcasted_iota(jnp.int32, (LANE, ML_WIDTH), 0)
    cidx = lax.broadcasted_iota(jnp.int32, (LANE, ML_WIDTH), 1)
    expand_m = jnp.where(lax.shift_right_logical(cidx, 8) == r, 1.0, 0.0).astype(BF16)
    fs_e = _expand_heads(fs, expand_m)
    is_e = _expand_heads(is_, expand_m)
    n_new = fs_e * n0_ref[...] + is_e * k
    nnew_ref[...] = n_new
    q_ref[...] = q
    k_ref[...] = k
    isv_t_ref[...] = is_e * v
    fs_t_ref[...] = fs_e
    nq = n_new * q
    floor = jnp.exp(-m_new)
    for hd in range(ML_HEADS):
        den = jnp.sum(nq[:, hd * ML_HEAD_DIM:(hd + 1) * ML_HEAD_DIM], axis=-1, keepdims=True)
        den_ref[:, hd:hd + 1] = jnp.maximum(jnp.abs(den), floor[:, hd:hd + 1])


BT_S = 8
BT_C = 8


def _ssd_state_kernel(s_ref, xdt_ref, dec_ref, bm_ref, cm_ref, snew_ref, y_ref):
    for i in range(BT_S):
        x_col = xdt_ref[0, :, i:i + 1].reshape(SSD_HEADS, SSD_HEAD_DIM, 1)
        d_col = dec_ref[0, :, i:i + 1].reshape(SSD_HEADS, SSD_HEAD_DIM, 1)
        ys = []
        for g in range(SSD_GROUPS):
            hs = slice(g * SSD_HPG, (g + 1) * SSD_HPG)
            b_row = bm_ref[i:i + 1, g * SSD_STATE:(g + 1) * SSD_STATE].reshape(1, 1, SSD_STATE)
            c_row = cm_ref[i:i + 1, g * SSD_STATE:(g + 1) * SSD_STATE].reshape(1, 1, SSD_STATE)
            s_new = d_col[hs] * s_ref[i, hs] + x_col[hs] * b_row
            snew_ref[i, hs] = s_new
            ys.append(jnp.sum(s_new * c_row, axis=-1, keepdims=True))
        y_ref[0, :, i:i + 1] = jnp.concatenate(ys, axis=0).reshape(SSD_WIDTH, 1)


def _mlstm_state_kernel(c_ref, isv_ref, fs_ref, k_ref, q_ref, cnew_ref, num_ref):
    for i in range(BT_C):
        for hd in range(ML_HEADS):
            sl = slice(hd * ML_HEAD_DIM, (hd + 1) * ML_HEAD_DIM)
            v_col = isv_ref[0, sl, i:i + 1]
            f_col = fs_ref[0, sl, i:i + 1]
            c_new = f_col * c_ref[i, hd] + v_col * k_ref[0, i:i + 1, sl]
            cnew_ref[i, hd] = c_new
            num_ref[0, sl, i:i + 1] = jnp.sum(c_new * q_ref[0, i:i + 1, sl], axis=-1, keepdims=True)


def _sample_post_kernel(x_ref, ys_t_ref, num_t_ref, den_ref, zs_ref, xs_ref, zm_ref, xc_ref,
                        dsk_ref, snw_ref, msk_ref, mnw_ref, wout_ref,
                        nw1_ref, fnw_ref, win1_ref, wout1_ref, cw_ref, cb_ref,
                        wax_ref, ba_ref, bx_ref, lam_ref, lbuf_ref, h0_ref,
                        y_ref, nlb_ref, hnew_ref):
    xs = xs_ref[...]
    y_s = ys_t_ref[...] + dsk_ref[...] * xs
    y_s = _group_rmsnorm(y_s * _silu(zs_ref[...]), snw_ref[...])
    num = num_t_ref[...]
    den = den_ref[...]
    h_m = jnp.concatenate(
        [num[:, hd * ML_HEAD_DIM:(hd + 1) * ML_HEAD_DIM] / den[:, hd:hd + 1] for hd in range(ML_HEADS)], axis=-1)
    h_m = _head_layernorm(h_m) * mnw_ref[...]
    y_m = (h_m + msk_ref[...] * xc_ref[...]) * _silu(zm_ref[...])
    h1 = x_ref[...] + _bdot(jnp.concatenate([y_s, y_m], axis=-1), _wload(wout_ref[...]))

    hn = _rms(h1, nw1_ref[...])
    proj = _bdot(hn, _wload(win1_ref[...]))
    gate = proj[:, :LRU_WIDTH]
    xr = proj[:, LRU_WIDTH:]
    xc = _conv_step(lbuf_ref, xr, cw_ref, cb_ref, nlb_ref)
    ra, ix = _blockdiag_tiles(xc, wax_ref)
    a, u = _rglru_gates(xc, ra, ix, ba_ref, bx_ref, lam_ref)
    h = a * h0_ref[...] + u
    hnew_ref[...] = h
    h2 = h1 + _bdot(h * _silu(gate), _wload(wout1_ref[...]))
    y_ref[...] = _rms(h2, fnw_ref[...])


def _full_call(kernel_fn, out_shapes, args, name):
    return pl.pallas_call(
        kernel_fn,
        out_shape=[jax.ShapeDtypeStruct(s, F32) for s in out_shapes],
        compiler_params=pltpu.CompilerParams(vmem_limit_bytes=VMEM_LIMIT),
        name=name,
    )(*args)


def _to_cols(a, bt):
    rows, ch = a.shape
    return a.reshape(rows // bt, bt, ch).transpose(0, 2, 1)


def _from_cols(a):
    tiles, ch, bt = a.shape
    return a.transpose(0, 2, 1).reshape(tiles * bt, ch)


def _row(v, width=None):
    v = v.reshape(1, -1).astype(F32)
    if width is not None and v.shape[1] < width:
        v = jnp.pad(v, ((0, 0), (0, width - v.shape[1])))
    return v


PACK_STEPS = 8


def _pack_all(weights):
    flats = [w.reshape(-1, w.shape[-1]) for w in weights]
    for f in flats:
        assert f.shape[0] % (2 * SUBLANE * PACK_STEPS) == 0
    packed = pl.pallas_call(
        _pack_kernel,
        grid=(PACK_STEPS,),
        in_specs=[pl.BlockSpec((f.shape[0] // PACK_STEPS, f.shape[1]), lambda i: (i, 0)) for f in flats],
        out_specs=[pl.BlockSpec((f.shape[0] // PACK_STEPS // 2, f.shape[1]), lambda i: (i, 0)) for f in flats],
        out_shape=[jax.ShapeDtypeStruct((f.shape[0] // 2, f.shape[1]), jnp.uint32) for f in flats],
        compiler_params=pltpu.CompilerParams(vmem_limit_bytes=VMEM_LIMIT),
        name="pack_weights",
    )(*flats)
    return [p.reshape(w.shape[:-2] + (w.shape[-2] // 2, w.shape[-1])) for p, w in zip(packed, weights)]


def _pack_kernel(*refs):
    n = len(refs) // 2
    for w_ref, o_ref in zip(refs[:n], refs[n:]):
        o_ref[...] = pltpu.bitcast(w_ref[...].astype(BF16), jnp.uint32)


def _dense_block_tiles(w):
    nb, bi, bo = w.shape
    per = LANE // bi
    rows = w.reshape(nb // per, per * bi, bo)
    col = jnp.arange(per * bo)
    spread = (col[None, :] % bo == jnp.arange(bo)[:, None]).astype(w.dtype)
    rep = jnp.einsum('tro,oc->trc', rows, spread)
    same_block = (jnp.arange(per * bi)[:, None] // bi) == (col[None, :] // bo)
    return jnp.where(same_block, rep, 0.0)


def kernel(x_prompt, x_sample, state_ssd_conv, state_ssd, state_mlstm_conv, state_mlstm_C, state_mlstm_n,
           state_mlstm_m, state_lru_conv, state_lru_h, meta_tokens, norm_w, final_norm_w, w_in_mix, w_out_mix,
           ssd_conv_w, ssd_conv_b, ssd_dt_bias, ssd_a_log, ssd_d, ssd_norm_w, ml_conv_w, ml_conv_b, ml_wq, ml_wk,
           ml_wv, ml_w_gate, ml_b_gate, ml_skip, ml_norm_w, lru_w_in, lru_w_out, lru_conv_w, lru_conv_b, lru_wa,
           lru_ba, lru_wx, lru_bx, lru_lambda):
    bsz = x_prompt.shape[0]
    dec = x_sample.shape[0]

    w_in = w_in_mix[0]
    o1 = SSD_WIDTH
    o2 = o1 + SSD_CONV_CH
    o3 = o2 + SSD_HEADS
    win = jnp.concatenate([w_in[:, :o2], jnp.pad(w_in[:, o2:o3], ((0, 0), (0, LANE - SSD_HEADS))), w_in[:, o3:]],
                          axis=1)
    wout = w_out_mix[0]
    nw0 = _row(norm_w[0])
    nw1 = _row(norm_w[1])
    fnw = _row(final_norm_w)
    scw = ssd_conv_w[0]
    scb = _row(ssd_conv_b[0])
    dtb = _row(ssd_dt_bias[0], LANE)
    alog = _row(ssd_a_log[0], LANE)
    dsk = _row(jnp.repeat(ssd_d[0], SSD_HEAD_DIM))
    snw = _row(ssd_norm_w[0])
    mcw = ml_conv_w[0]
    mcb = _row(ml_conv_b[0])
    wqk = jnp.concatenate([_dense_block_tiles(ml_wq[0]), _dense_block_tiles(ml_wk[0])], axis=2)
    wv = _dense_block_tiles(ml_wv[0])
    wg_raw = ml_w_gate[0]
    wg = jnp.concatenate([jnp.pad(wg_raw[:, :ML_HEADS], ((0, 0), (0, LANE - ML_HEADS))),
                          jnp.pad(wg_raw[:, ML_HEADS:], ((0, 0), (0, LANE - ML_HEADS)))], axis=1)
    bg = jnp.concatenate([_row(ml_b_gate[0, :ML_HEADS], LANE), _row(ml_b_gate[0, ML_HEADS:], LANE)], axis=1)
    msk = _row(ml_skip[0])
    mnw = _row(ml_norm_w[0])
    win1 = lru_w_in[0]
    wout1 = lru_w_out[0]
    lcw = lru_conv_w[0]
    lcb = _row(lru_conv_b[0])
    wax = jnp.concatenate([lru_wa[0], lru_wx[0]], axis=2)
    r_idx = lax.broadcasted_iota(jnp.int32, (LANE, SSD_WIDTH), 0)
    c_idx = lax.broadcasted_iota(jnp.int32, (LANE, SSD_WIDTH), 1)
    expand = (c_idx // SSD_HEAD_DIM == r_idx).astype(F32)
    ba = _row(lru_ba[0])
    bx = _row(lru_bx[0])
    lam = _row(lru_lambda[0])

    win, wout, wqk, wv, wg, expand, win1, wout1, wax = _pack_all(
        [win, wout, wqk, wv, wg, expand, win1, wout1, wax])
    l0_w = (nw0, win, wout, scw, scb, dtb, alog, dsk, snw, mcw, mcb, wqk, wv, wg, bg, msk, mnw, expand)
    l1_w = (nw1, fnw, win1, wout1, lcw, lcb, wax, ba, bx, lam)

    zero0 = tuple(jnp.zeros((1,) + s, F32) for s in L0_STATE_SHAPES)
    zero1 = tuple(jnp.zeros((1,) + s, F32) for s in L1_STATE_SHAPES)
    meta = jnp.pad(meta_tokens.astype(F32), ((CHUNK - N_META, 0), (0, 0)))[None]
    meta_out = _l0_prompt(meta, l0_w, zero0, CHUNK - N_META)
    meta1_out = _l1_prompt(meta_out[0], l1_w, zero1, CHUNK - N_META)
    l0_out = _l0_prompt(x_prompt, l0_w, tuple(meta_out[1:]), 0)
    h1_p, p_sc, p_s, p_mc, p_c, p_n, p_m = l0_out
    y_prompt, p_lc, p_lh = _l1_prompt(h1_p, l1_w, tuple(meta1_out[1:]), 0)

    p_m = p_m[:, 0, :ML_HEADS]
    p_lh = p_lh[:, 0]

    xs2 = x_sample[:, 0]
    sbuf = jnp.moveaxis(state_ssd_conv[0], 1, 0)
    mbuf = jnp.moveaxis(state_mlstm_conv[0], 1, 0)
    lbuf = jnp.moveaxis(state_lru_conv[0], 1, 0)
    m0 = jnp.pad(state_mlstm_m[0], ((0, 0), (0, LANE - ML_HEADS)))
    n0 = state_mlstm_n[0].reshape(dec, ML_WIDTH)
    pre_shapes = ((TAIL, dec, SSD_CONV_CH), (TAIL, dec, ML_WIDTH), (dec, SSD_WIDTH), (dec, SSD_WIDTH),
                  (dec, SSD_GROUPS * SSD_STATE), (dec, SSD_GROUPS * SSD_STATE), (dec, SSD_WIDTH), (dec, SSD_WIDTH),
                  (dec, ML_WIDTH), (dec, ML_WIDTH), (dec, ML_WIDTH), (dec, ML_WIDTH), (dec, ML_WIDTH),
                  (dec, ML_WIDTH), (dec, LANE), (dec, ML_WIDTH), (dec, ML_HEADS))
    (nsb, nmb, zs, xs_c, bm, cm, xdt_t, dec_t, zm, xc_m, q, isv_t, fs_t, k, m_new, n_new, den) = _full_call(
        _l0_sample_pre_kernel, pre_shapes,
        (xs2, nw0, win, scw, scb, dtb, alog, mcw, mcb, wqk, wv, wg, bg, sbuf, mbuf, m0, n0), "l0_sample_pre")

    xdt_c = _to_cols(xdt_t, BT_S)
    dec_c = _to_cols(dec_t, BT_S)
    s_new, ys_c = pl.pallas_call(
        _ssd_state_kernel,
        grid=(dec // BT_S,),
        in_specs=[pl.BlockSpec((BT_S, SSD_HEADS, SSD_HEAD_DIM, SSD_STATE), lambda i: (i, 0, 0, 0)),
                  pl.BlockSpec((1, SSD_WIDTH, BT_S), lambda i: (i, 0, 0)),
                  pl.BlockSpec((1, SSD_WIDTH, BT_S), lambda i: (i, 0, 0)),
                  pl.BlockSpec((BT_S, SSD_GROUPS * SSD_STATE), lambda i: (i, 0)),
                  pl.BlockSpec((BT_S, SSD_GROUPS * SSD_STATE), lambda i: (i, 0))],
        out_specs=[pl.BlockSpec((BT_S, SSD_HEADS, SSD_HEAD_DIM, SSD_STATE), lambda i: (i, 0, 0, 0)),
                   pl.BlockSpec((1, SSD_WIDTH, BT_S), lambda i: (i, 0, 0))],
        out_shape=[jax.ShapeDtypeStruct((dec, SSD_HEADS, SSD_HEAD_DIM, SSD_STATE), F32),
                   jax.ShapeDtypeStruct((dec // BT_S, SSD_WIDTH, BT_S), F32)],
        compiler_params=pltpu.CompilerParams(dimension_semantics=("arbitrary",), vmem_limit_bytes=VMEM_LIMIT),
        name="ssd_state",
    )(state_ssd[0], xdt_c, dec_c, bm, cm)

    isv_c = _to_cols(isv_t, BT_C)
    fs_c = _to_cols(fs_t, BT_C)
    c_new, num_c = pl.pallas_call(
        _mlstm_state_kernel,
        grid=(dec // BT_C,),
        in_specs=[pl.BlockSpec((BT_C, ML_HEADS, ML_HEAD_DIM, ML_HEAD_DIM), lambda i: (i, 0, 0, 0)),
                  pl.BlockSpec((1, ML_WIDTH, BT_C), lambda i: (i, 0, 0)),
                  pl.BlockSpec((1, ML_WIDTH, BT_C), lambda i: (i, 0, 0)),
                  pl.BlockSpec((1, BT_C, ML_WIDTH), lambda i: (i, 0, 0)),
                  pl.BlockSpec((1, BT_C, ML_WIDTH), lambda i: (i, 0, 0))],
        out_specs=[pl.BlockSpec((BT_C, ML_HEADS, ML_HEAD_DIM, ML_HEAD_DIM), lambda i: (i, 0, 0, 0)),
                   pl.BlockSpec((1, ML_WIDTH, BT_C), lambda i: (i, 0, 0))],
        out_shape=[jax.ShapeDtypeStruct((dec, ML_HEADS, ML_HEAD_DIM, ML_HEAD_DIM), F32),
                   jax.ShapeDtypeStruct((dec // BT_C, ML_WIDTH, BT_C), F32)],
        compiler_params=pltpu.CompilerParams(dimension_semantics=("arbitrary",), vmem_limit_bytes=VMEM_LIMIT),
        name="mlstm_state",
    )(state_mlstm_C[0], isv_c, fs_c, k.reshape(dec // BT_C, BT_C, ML_WIDTH), q.reshape(dec // BT_C, BT_C, ML_WIDTH))

    post_shapes = ((dec, D_MODEL), (TAIL, dec, LRU_WIDTH), (dec, LRU_WIDTH))
    y_s2, nlb, h_new = _full_call(
        _sample_post_kernel, post_shapes,
        (xs2, _from_cols(ys_c), _from_cols(num_c), den, zs, xs_c, zm, xc_m, dsk, snw, msk, mnw, wout,
         nw1, fnw, win1, wout1, lcw, lcb, wax, ba, bx, lam, lbuf, state_lru_h[0]), "sample_post")

    s_sc = jnp.moveaxis(nsb, 0, 1)[None]
    s_mc = jnp.moveaxis(nmb, 0, 1)[None]
    s_lc = jnp.moveaxis(nlb, 0, 1)[None]
    return (y_prompt, y_s2[:, None, :],
            p_sc[None], p_s[None], p_mc[None], p_c[None], p_n[None], p_m[None], p_lc[None], p_lh[None],
            s_sc, s_new[None], s_mc, c_new[None], n_new.reshape(dec, ML_HEADS, ML_HEAD_DIM)[None],
            m_new[:, :ML_HEADS][None], s_lc, h_new[None])
```

```python
import functools

import jax
import jax.numpy as jnp
from jax import lax
from jax.experimental import pallas as pl
from jax.experimental.pallas import tpu as pltpu

F32 = jnp.float32
BF16 = jnp.bfloat16

D_MODEL = 1024
N_META = 16
CONV_W = 4
EPS = 1e-6
NEG = -1e30
SSD_WIDTH = 1024
SSD_HEAD_DIM = 64
SSD_HEADS = 16
SSD_GROUPS = 2
SSD_HPG = 8
SSD_STATE = 128
SSD_CONV_CH = 1536
ML_WIDTH = 1024
ML_HEADS = 4
ML_HEAD_DIM = 256
ML_QKV_BLOCK = 4
LRU_WIDTH = 2048
LRU_BLOCKS = 16
LRU_BLOCK = 128
LRU_C = 8.0

LANE = 128
SUBLANE = 8
CHUNK = 128
L0_ROWS = 2
L1_ROWS = 2
L1_GROUPS = 4
TAIL = CONV_W - 1

OFF_ZS = 0
OFF_XBC = OFF_ZS + SSD_WIDTH
OFF_DT = OFF_XBC + SSD_CONV_CH
OFF_ZM = OFF_DT + LANE
OFF_XM = OFF_ZM + ML_WIDTH
IN_MIX_PAD = OFF_XM + ML_WIDTH

VMEM_LIMIT = 56 * 1024 * 1024


def _sigmoid(x):
    return 1.0 / (1.0 + jnp.exp(-x))


def _silu(x):
    return x * _sigmoid(x)


def _softplus(x):
    return jnp.maximum(x, 0.0) + jnp.log1p(jnp.exp(-jnp.abs(x)))


def _rms(x, w):
    return x * lax.rsqrt(jnp.mean(x * x, axis=-1, keepdims=True) + EPS) * w


def _bdot(a, b):
    return jnp.dot(a.astype(BF16), b.astype(BF16), preferred_element_type=F32)


def _wload(w):
    return pltpu.bitcast(w, BF16)


def _split3(x):
    hi = x.astype(BF16)
    r = x - hi.astype(F32)
    mid = r.astype(BF16)
    lo = (r - mid.astype(F32)).astype(BF16)
    return hi, mid, lo


def _cumsum_rows(x, tril):
    hi, mid, lo = _split3(x)
    d = functools.partial(jnp.dot, preferred_element_type=F32)
    return d(tril, hi) + d(tril, mid) + d(tril, lo)


def _expand_heads(x, expand):
    hi, mid, _ = _split3(x)
    d = functools.partial(jnp.dot, preferred_element_type=F32)
    return d(hi, expand) + d(mid, expand)


def _expand_matrix():
    r = lax.broadcasted_iota(jnp.int32, (LANE, SSD_WIDTH), 0)
    c = lax.broadcasted_iota(jnp.int32, (LANE, SSD_WIDTH), 1)
    return jnp.where(lax.shift_right_logical(c, 6) == r, 1.0, 0.0).astype(BF16)


def _blockdiag_tiles(x, w_ref):
    k = w_ref.shape[0]
    m = w_ref.shape[2] // LANE
    prods = [_bdot(x[:, t * LANE:(t + 1) * LANE], _wload(w_ref[t])) for t in range(k)]
    return [jnp.concatenate([p[:, j * LANE:(j + 1) * LANE] for p in prods], axis=-1) for j in range(m)]


def _group_rmsnorm(y, w):
    half = SSD_WIDTH // SSD_GROUPS
    parts = []
    for g in range(SSD_GROUPS):
        yg = y[:, g * half:(g + 1) * half]
        parts.append(yg * lax.rsqrt(jnp.mean(yg * yg, axis=-1, keepdims=True) + EPS))
    return jnp.concatenate(parts, axis=-1) * w


def _head_layernorm(h):
    parts = []
    for k in range(ML_HEADS):
        hk = h[:, k * ML_HEAD_DIM:(k + 1) * ML_HEAD_DIM]
        mu = jnp.mean(hk, axis=-1, keepdims=True)
        d = hk - mu
        var = jnp.mean(d * d, axis=-1, keepdims=True)
        parts.append(d * lax.rsqrt(var + EPS))
    return jnp.concatenate(parts, axis=-1)


def _mlstm_qkv_gates(xm, xc, wqk_ref, wv_ref, wg_ref, bg_ref):
    q, k = _blockdiag_tiles(xc, wqk_ref)
    v, = _blockdiag_tiles(xm, wv_ref)
    gates = _bdot(jnp.concatenate([q, k, v], axis=-1), _wload(wg_ref[...])) + bg_ref[...]
    ig = gates[:, :LANE]
    logf = -_softplus(-gates[:, LANE:])
    return q, k * (ML_HEAD_DIM ** -0.5), v, ig, logf


N_L0_W = 18
N_L0_S = 6


def _l0_prompt_kernel(x_ref, xnext_ref, *refs, front_pad, rows):
    w_refs = refs[:N_L0_W]
    init_refs = refs[N_L0_W:N_L0_W + N_L0_S]
    out_refs = refs[N_L0_W + N_L0_S:N_L0_W + 2 * N_L0_S + 1]
    scratch = refs[N_L0_W + 2 * N_L0_S + 1:]
    per_row = len(scratch) // rows
    c = pl.program_id(1)
    win_ref, wout_ref = w_refs[1], w_refs[2]
    q_len = x_ref.shape[1]
    pieces = {}

    def each_row(phase):
        return [_l0_prompt_row(x_ref.at[r], xnext_ref.at[r], *w_refs, *init_refs, *(o.at[r] for o in out_refs),
                               *scratch[r * per_row:(r + 1) * per_row], front_pad=front_pad, phase=phase,
                               emit=lambda k0, y, r=r: pieces.setdefault(k0, {}).__setitem__(r, y))
                for r in range(rows)]

    @pl.when(c == 0)
    def _():
        each_row("init")

    bodies = each_row("body")
    proj_refs = [scratch[r * per_row + per_row - 1] for r in range(rows)]
    _, to_time = _perm_matrices(q_len)
    partials = []

    def in_proj():
        lhs = jnp.concatenate([hn_next for _, hn_next, _ in bodies], axis=0)
        for lo, hi in L0_PROJ_PIECES:
            res = _bdot(lhs, _wload(win_ref[:, lo:hi]))
            for r in range(rows):
                proj_refs[r][:, lo:hi] = res[r * q_len:(r + 1) * q_len]
            yield

    def out_proj():
        pending = [(0, SSD_WIDTH)] + [(SSD_WIDTH + hd * ML_HEAD_DIM, ML_HEAD_DIM) for hd in range(ML_HEADS)]
        while pending:
            for k0, width in list(pending):
                if len(pieces.get(k0, ())) == rows:
                    y_t = jnp.concatenate([_move_rows(to_time, pieces[k0][r].astype(BF16)) for r in range(rows)],
                                          axis=0)
                    partials.append(_bdot(y_t, _wload(wout_ref[k0 // 2:(k0 + width) // 2, :])))
                    pending.remove((k0, width))
            yield

    _run_round_robin([gen for gens, _, _ in bodies for gen in gens] + [in_proj(), out_proj()])
    total = partials[0]
    for part in partials[1:]:
        total = total + part
    for r, (_, _, x) in enumerate(bodies):
        h1 = x + total[r * q_len:(r + 1) * q_len]
        if front_pad:
            h1 = jnp.where(lax.broadcasted_iota(jnp.int32, (q_len, 1), 0) >= front_pad, h1, 0.0)
        out_refs[0][r] = h1

    @pl.when(c == pl.num_programs(1) - 1)
    def _():
        each_row("final")


L0_PROJ_PIECES = ((OFF_XBC, OFF_ZM), (OFF_XM, IN_MIX_PAD), (OFF_ZM, OFF_XM), (OFF_ZS, OFF_XBC))


def _run_round_robin(gens):
    live = list(gens)
    while live:
        for gen in list(live):
            if next(gen, "done") == "done":
                live.remove(gen)


def _l0_prompt_row(x_ref, xnext_ref, nw_ref, win_ref, wout_ref,
                   scw_ref, scb_ref, dtb_ref, alog_ref, dsk_ref, snw_ref,
                   mcw_ref, mcb_ref, wqk_ref, wv_ref, wg_ref, bg_ref, msk_ref, mnw_ref, expand_ref,
                   isc_ref, iss_ref, imc_ref, ict_ref, inn_ref, imm_ref,
                   h1_ref, osc_ref, oss_ref, omc_ref, oct_ref, onn_ref, omm_ref,
                   sbuf, mbuf, s_st, ct_st, n_st, m_st, proj_s, *, front_pad, phase, emit):
    q_len = x_ref.shape[0]

    if phase == "init":
        sbuf[...] = jnp.zeros(sbuf.shape, F32)
        mbuf[...] = jnp.zeros(mbuf.shape, F32)
        sbuf[SUBLANE - TAIL:SUBLANE, :] = isc_ref[0]
        mbuf[SUBLANE - TAIL:SUBLANE, :] = imc_ref[0]
        for g in range(SSD_GROUPS):
            heads = iss_ref[0, g * SSD_HPG:(g + 1) * SSD_HPG]
            s_st[g] = heads.reshape(SSD_HPG * SSD_HEAD_DIM, SSD_STATE).T
        for hd in range(ML_HEADS):
            ct_st[hd] = ict_ref[0, hd].T
        n_st[...] = inn_ref[0]
        m_st[...] = imm_ref[0]
        hn0 = _move_rows(_perm_matrices(q_len)[0], _rms(x_ref[...], nw_ref[...]).astype(BF16))
        proj_s[...] = _bdot(hn0, _wload(win_ref[...]))
        return None
    if phase == "final":
        osc_ref[...] = sbuf[SUBLANE - TAIL:SUBLANE, :]
        omc_ref[...] = mbuf[SUBLANE - TAIL:SUBLANE, :]
        for g in range(SSD_GROUPS):
            oss_ref[g * SSD_HPG:(g + 1) * SSD_HPG] = s_st[g].T.reshape(SSD_HPG, SSD_HEAD_DIM, SSD_STATE)
        for hd in range(ML_HEADS):
            oct_ref[hd] = ct_st[hd].T
        onn_ref[...] = n_st[...]
        omm_ref[...] = m_st[...]
        return None

    x = x_ref[...]
    to_perm, to_time = _perm_matrices(q_len)
    t_col = _perm_time(q_len)
    t_row = _perm_time(q_len, row=True)
    causal = t_col >= t_row
    tril = jnp.where(causal, 1.0, 0.0).astype(BF16)
    valid = (t_col >= front_pad) if front_pad else None
    hn_next = _move_rows(to_perm, _rms(xnext_ref[...], nw_ref[...]).astype(BF16))
    xbc_raw = proj_s[:, OFF_XBC:OFF_XBC + SSD_CONV_CH]
    dt_raw = proj_s[:, OFF_DT:OFF_DT + LANE]
    xm = proj_s[:, OFF_XM:OFF_XM + ML_WIDTH]
    z_s = proj_s[:, OFF_ZS:OFF_ZS + SSD_WIDTH]
    z_m = proj_s[:, OFF_ZM:OFF_ZM + ML_WIDTH]

    def ssd():
        xbc = _silu(_conv_perm(sbuf, xbc_raw, scw_ref, scb_ref))
        yield
        xs = xbc[:, :SSD_WIDTH]
        bm = xbc[:, SSD_WIDTH:SSD_WIDTH + SSD_GROUPS * SSD_STATE]
        cm = xbc[:, SSD_WIDTH + SSD_GROUPS * SSD_STATE:]
        dt = _softplus(dt_raw + dtb_ref[...])
        if front_pad:
            dt = jnp.where(valid, dt, 0.0)
        log_a = -dt * jnp.exp(alog_ref[...])
        a_cs = _cumsum_rows(log_a, tril)
        yield
        a_last = a_cs[q_len - 1:q_len, :]
        expand = _wload(expand_ref[...])
        w_state = _expand_heads(dt * jnp.exp(a_last - a_cs), expand)
        e_acs = _expand_heads(jnp.exp(a_cs), expand)
        a_cs_t = a_cs.T
        dt_t = dt.T
        yield
        pair_lo = lax.broadcasted_iota(jnp.int32, (q_len, LANE), 1) < SSD_HEAD_DIM
        half = SSD_WIDTH // SSD_GROUPS
        y_groups = []
        for g in range(SSD_GROUPS):
            bg = bm[:, g * SSD_STATE:(g + 1) * SSD_STATE]
            cg = cm[:, g * SSD_STATE:(g + 1) * SSD_STATE]
            bg_t = bg.T
            xg = xs[:, g * half:(g + 1) * half]
            eg = e_acs[:, g * half:(g + 1) * half]
            s_old = s_st[g]
            cb = _bdot(cg, bg_t)
            y_off = _bdot(cg, s_old) * eg
            s_st[g] = eg[q_len - 1:q_len, :] * s_old + _bdot(bg_t, xg * w_state[:, g * half:(g + 1) * half])
            yield
            y_pairs = []
            for pr in range(SSD_HPG // 2):
                ms = []
                for e in (2 * pr, 2 * pr + 1):
                    hd = g * SSD_HPG + e
                    seg = jnp.exp(jnp.where(causal, a_cs[:, hd:hd + 1] - a_cs_t[hd:hd + 1, :], -jnp.inf))
                    ms.append(cb * seg * dt_t[hd:hd + 1, :])
                xp = xg[:, pr * LANE:(pr + 1) * LANE]
                rhs = jnp.concatenate([jnp.where(pair_lo, xp, 0.0), jnp.where(pair_lo, 0.0, xp)], axis=0)
                y_pairs.append(_bdot(jnp.concatenate(ms, axis=-1), rhs))
                yield
            y_groups.append(jnp.concatenate(y_pairs, axis=-1) + y_off)
        y_s = jnp.concatenate(y_groups, axis=-1) + dsk_ref[...] * xs
        emit(0, _group_rmsnorm(y_s * _silu(z_s), snw_ref[...]))

    def mlstm():
        xc = _silu(_conv_perm(mbuf, xm, mcw_ref, mcb_ref))
        yield
        q, k = _blockdiag_tiles(xc, wqk_ref)
        v, = _blockdiag_tiles(xm, wv_ref)
        yield
        gates = _bdot(jnp.concatenate([q, k, v], axis=-1), _wload(wg_ref[...])) + bg_ref[...]
        k = k * (ML_HEAD_DIM ** -0.5)
        yield
        ig = gates[:, :LANE]
        logf = -_softplus(-gates[:, LANE:])
        if front_pad:
            ig = jnp.where(valid, ig, NEG)
            logf = jnp.where(valid, logf, 0.0)
        bcum = _cumsum_rows(logf, tril)
        yield
        ftot = bcum[q_len - 1:q_len, :]
        m_prev = m_st[...]
        w_end = ftot - bcum + ig
        m_new = jnp.maximum(ftot + m_prev, jnp.max(w_end, axis=0, keepdims=True))
        sc = jnp.exp(ftot + m_prev - m_new)
        wexp = jnp.exp(w_end - m_new)
        inter = bcum + m_prev
        bcum_t = bcum.T
        ig_t = ig.T
        m_st[...] = m_new
        yield
        for hd in range(ML_HEADS):
            sl = slice(hd * ML_HEAD_DIM, (hd + 1) * ML_HEAD_DIM)
            q_h, k_h, v_h = q[:, sl], k[:, sl], v[:, sl]
            k_t = k_h.T
            dmat = jnp.where(causal, bcum[:, hd:hd + 1] - bcum_t[hd:hd + 1, :] + ig_t[hd:hd + 1, :], -jnp.inf)
            inter_h = inter[:, hd:hd + 1]
            m_t = jnp.maximum(inter_h, jnp.max(dmat, axis=-1, keepdims=True))
            dexp = jnp.exp(dmat - m_t)
            inter_sc = jnp.exp(inter_h - m_t)
            s = _bdot(q_h, k_t) * dexp
            yield
            ct_old = ct_st[hd]
            n_old = n_st[hd:hd + 1, :]
            num = _bdot(s, v_h) + inter_sc * _bdot(q_h, ct_old)
            den = jnp.sum(s, axis=-1, keepdims=True) + inter_sc * jnp.sum(q_h * n_old, axis=-1, keepdims=True)
            h_h = num / jnp.maximum(jnp.abs(den), jnp.exp(-m_t))
            w_col = wexp[:, hd:hd + 1]
            sc_h = sc[:, hd:hd + 1]
            ct_st[hd] = sc_h * ct_old + _bdot(k_t, v_h * w_col)
            n_st[hd:hd + 1, :] = sc_h * n_old + jnp.sum(k_h * w_col, axis=0, keepdims=True)
            yield
            mu = jnp.mean(h_h, axis=-1, keepdims=True)
            dev = h_h - mu
            var = jnp.mean(dev * dev, axis=-1, keepdims=True)
            h_h = dev * lax.rsqrt(var + EPS) * mnw_ref[:, sl]
            emit(SSD_WIDTH + hd * ML_HEAD_DIM, (h_h + msk_ref[:, sl] * xc[:, sl]) * _silu(z_m[:, sl]))
            yield

    return [ssd(), mlstm()], hn_next, x


def _const_spec(shape):
    nd = len(shape)
    return pl.BlockSpec(shape, lambda b, c: (0,) * nd)


def _state_spec(shape, rows):
    nd = len(shape)
    if rows:
        return pl.BlockSpec((rows,) + shape, lambda b, c: (b,) + (0,) * nd)
    return pl.BlockSpec((1,) + shape, lambda b, c: (0,) * (nd + 1))


def _rows_per_step(bsz, want):
    return want if bsz % want == 0 else 1


L0_STATE_SHAPES = ((TAIL, SSD_CONV_CH), (SSD_HEADS, SSD_HEAD_DIM, SSD_STATE), (TAIL, ML_WIDTH),
                   (ML_HEADS, ML_HEAD_DIM, ML_HEAD_DIM), (ML_HEADS, ML_HEAD_DIM), (1, LANE))
L0_CARRY_SHAPES = ((SUBLANE, SSD_CONV_CH), (SUBLANE, ML_WIDTH), (SSD_GROUPS, SSD_STATE, SSD_WIDTH // SSD_GROUPS),
                   (ML_HEADS, ML_HEAD_DIM, ML_HEAD_DIM), (ML_HEADS, ML_HEAD_DIM), (1, LANE))


def _l0_prompt(x, weights, init, front_pad):
    bsz, length, _ = x.shape
    q_len = min(CHUNK, length)
    assert length % q_len == 0
    rows = _rows_per_step(bsz, L0_ROWS)
    assert len(weights) == N_L0_W and len(init) == N_L0_S
    grid = (bsz // rows, length // q_len)
    last = length // q_len - 1
    x_spec = pl.BlockSpec((rows, q_len, D_MODEL), lambda b, c: (b, c, 0))
    next_spec = pl.BlockSpec((rows, q_len, D_MODEL), lambda b, c: (b, jnp.minimum(c + 1, last), 0))
    in_specs = ([x_spec, next_spec] + [_const_spec(w.shape) for w in weights]
                + [_state_spec(s, 0) for s in L0_STATE_SHAPES])
    out_shape = ([jax.ShapeDtypeStruct((bsz, length, D_MODEL), F32)]
                 + [jax.ShapeDtypeStruct((bsz,) + s, F32) for s in L0_STATE_SHAPES])
    out_specs = [x_spec] + [_state_spec(s, rows) for s in L0_STATE_SHAPES]
    row_scratch = L0_CARRY_SHAPES + ((q_len, IN_MIX_PAD),)
    scratch = [pltpu.VMEM(s, F32) for _ in range(rows) for s in row_scratch]
    return pl.pallas_call(
        functools.partial(_l0_prompt_kernel, front_pad=front_pad, rows=rows),
        grid=grid, in_specs=in_specs, out_specs=out_specs, out_shape=out_shape, scratch_shapes=scratch,
        compiler_params=pltpu.CompilerParams(dimension_semantics=("arbitrary", "arbitrary"),
                                             vmem_limit_bytes=VMEM_LIMIT),
        name="l0_prompt",
    )(x, x, *weights, *init)


def _rglru_gates(xc, ra, ix, ba_ref, bx_ref, lam_ref):
    r = _sigmoid(ra + ba_ref[...])
    i = _sigmoid(ix + bx_ref[...])
    log_a = r * (-LRU_C * _softplus(-lam_ref[...]))
    a = jnp.exp(log_a)
    u = jnp.sqrt(1.0 - a * a) * (i * xc)
    return a, u


def _perm_time(n, row=False):
    p = lax.broadcasted_iota(jnp.int32, (1, n) if row else (n, 1), 1 if row else 0)
    return (n // SUBLANE) * (p & (SUBLANE - 1)) + lax.shift_right_logical(p, 3)


def _perm_matrices(n):
    nb = n // SUBLANE
    r = lax.broadcasted_iota(jnp.int32, (n, n), 0)
    c = lax.broadcasted_iota(jnp.int32, (n, n), 1)
    to_perm = jnp.where(c == nb * (r & (SUBLANE - 1)) + lax.shift_right_logical(r, 3), 1.0, 0.0)
    to_time = jnp.where(r == nb * (c & (SUBLANE - 1)) + lax.shift_right_logical(c, 3), 1.0, 0.0)
    return to_perm.astype(BF16), to_time.astype(BF16)


def _move_rows(sel, x_bf16):
    return jnp.dot(sel, x_bf16, preferred_element_type=F32).astype(BF16)


def _conv_perm(tail_ref, x, w_ref, b_ref):
    n, ch = x.shape
    nb = n // SUBLANE
    x3 = x.reshape(nb, SUBLANE, ch)
    tail8 = tail_ref[...]
    sub = lax.broadcasted_iota(jnp.int32, (SUBLANE, ch), 0)
    y = b_ref[...].reshape(1, 1, ch) + w_ref[TAIL:TAIL + 1, :].reshape(1, 1, ch) * x3
    wrapped = [jnp.where(sub >= 1, pltpu.roll(x3[nb - d], 1, 0), tail8[SUBLANE - d:SUBLANE - d + 1, :])
               for d in range(1, CONV_W)]
    for back in range(1, CONV_W):
        head = jnp.stack([wrapped[back - j - 1] for j in range(back)], axis=0)
        shifted = jnp.concatenate([head, x3[:nb - back]], axis=0)
        y = y + w_ref[TAIL - back:TAIL - back + 1, :].reshape(1, 1, ch) * shifted
    for d in range(1, CONV_W):
        tail_ref[SUBLANE - d:SUBLANE - d + 1, :] = x3[nb - d][SUBLANE - 1:SUBLANE, :]
    return y.reshape(n, ch)


def _scan_perm(a, u, h_prev):
    n, ch = a.shape
    nb = n // SUBLANE
    a3 = a.reshape(nb, SUBLANE, ch)
    u3 = u.reshape(nb, SUBLANE, ch)
    local = [u3[0]]
    decay = [a3[0]]
    for j in range(1, nb):
        local.append(a3[j] * local[-1] + u3[j])
        decay.append(a3[j] * decay[-1])
    seg_u, seg_a = local[-1], decay[-1]
    sub = lax.broadcasted_iota(jnp.int32, (SUBLANE, ch), 0)
    shift = 1
    while shift < SUBLANE:
        keep = sub >= shift
        seg_u = seg_u + seg_a * jnp.where(keep, pltpu.roll(seg_u, shift, 0), 0.0)
        seg_a = seg_a * jnp.where(keep, pltpu.roll(seg_a, shift, 0), 1.0)
        shift *= 2
    seg_end = seg_a * h_prev + seg_u
    carry = jnp.where(sub >= 1, pltpu.roll(seg_end, 1, 0), h_prev)
    h3 = jnp.stack([local[j] + decay[j] * carry for j in range(nb)], axis=0)
    return h3.reshape(n, ch), seg_end[SUBLANE - 1:SUBLANE, :]


N_L1_W = 10
N_L1_S = 2


def _l1_prompt_kernel(h_ref, hnext_ref, *refs, front_pad, rows):
    w_refs = refs[:N_L1_W]
    init_refs = refs[N_L1_W:N_L1_W + N_L1_S]
    out_refs = refs[N_L1_W + N_L1_S:N_L1_W + 2 * N_L1_S + 1]
    scratch = refs[N_L1_W + 2 * N_L1_S + 1:]
    per_row = len(scratch) // rows
    c = pl.program_id(1)
    fnw_ref, win_ref, wout_ref = w_refs[1], w_refs[2], w_refs[3]
    q_len = h_ref.shape[1]
    gw = LRU_WIDTH // L1_GROUPS
    pieces = {}

    def each_row(phase):
        return [_l1_prompt_row(h_ref.at[r], hnext_ref.at[r], *w_refs, *init_refs, *(o.at[r] for o in out_refs),
                               *scratch[r * per_row:(r + 1) * per_row], front_pad=front_pad, phase=phase,
                               emit=lambda g, y, r=r: pieces.setdefault(g, {}).__setitem__(r, y))
                for r in range(rows)]

    @pl.when(c == 0)
    def _():
        each_row("init")

    bodies = each_row("body")
    proj_refs = [scratch[r * per_row + per_row - 1] for r in range(rows)]
    _, to_time = _perm_matrices(q_len)
    partials = []

    def in_proj():
        lhs = jnp.concatenate([hn_next for _, hn_next, _ in bodies], axis=0)
        for g in range(L1_GROUPS):
            for lo in (g * gw, LRU_WIDTH + g * gw):
                res = _bdot(lhs, _wload(win_ref[:, lo:lo + gw]))
                for r in range(rows):
                    proj_refs[r][:, lo:lo + gw] = res[r * q_len:(r + 1) * q_len]
            yield

    def out_proj():
        pending = list(range(L1_GROUPS))
        while pending:
            for g in list(pending):
                if len(pieces.get(g, ())) == rows:
                    y_t = jnp.concatenate([_move_rows(to_time, pieces[g][r].astype(BF16)) for r in range(rows)],
                                          axis=0)
                    partials.append(_bdot(y_t, _wload(wout_ref[g * gw // 2:(g + 1) * gw // 2, :])))
                    pending.remove(g)
            yield

    chains = [bodies[r][0][g] for g in range(L1_GROUPS) for r in range(rows)]
    _run_staggered([in_proj()] + chains + [out_proj()])
    total = partials[0]
    for part in partials[1:]:
        total = total + part
    for r, (_, _, h_in) in enumerate(bodies):
        out_refs[0][r] = _rms(h_in + total[r * q_len:(r + 1) * q_len], fnw_ref[...])

    @pl.when(c == pl.num_programs(1) - 1)
    def _():
        each_row("final")


def _l1_in_proj(h_val, nw_ref, win_ref, to_perm):
    hn = _move_rows(to_perm, _rms(h_val, nw_ref[...]).astype(BF16))
    return _bdot(hn, _wload(win_ref[...]))


def _l1_prompt_row(h_ref, hnext_ref, nw_ref, fnw_ref, win_ref, wout_ref, cw_ref, cb_ref,
                   wax_ref, ba_ref, bx_ref, lam_ref, ilc_ref, ilh_ref,
                   y_ref, olc_ref, olh_ref, lbuf, h_st, proj_s, *, front_pad, phase, emit):
    if phase == "init":
        lbuf[...] = jnp.zeros(lbuf.shape, F32)
        lbuf[SUBLANE - TAIL:SUBLANE, :] = ilc_ref[0]
        h_st[...] = ilh_ref[0]
        proj_s[...] = _l1_in_proj(h_ref[...], nw_ref, win_ref, _perm_matrices(h_ref.shape[0])[0])
        return None
    if phase == "final":
        olc_ref[...] = lbuf[SUBLANE - TAIL:SUBLANE, :]
        olh_ref[...] = h_st[...]
        return None

    q_len = h_ref.shape[0]
    h_in = h_ref[...]
    to_perm, _ = _perm_matrices(q_len)
    hn_next = _move_rows(to_perm, _rms(hnext_ref[...], nw_ref[...]).astype(BF16))
    if front_pad:
        valid = _perm_time(q_len) >= front_pad
    gw = LRU_WIDTH // L1_GROUPS
    tiles = gw // LANE
    gates = [proj_s[:, g * gw:(g + 1) * gw] for g in range(L1_GROUPS)]
    xrs = [proj_s[:, LRU_WIDTH + g * gw:LRU_WIDTH + (g + 1) * gw] for g in range(L1_GROUPS)]

    def group(g):
        cg = slice(g * gw, (g + 1) * gw)
        xc = _conv_perm(lbuf.at[:, cg], xrs[g], cw_ref.at[:, cg], cb_ref.at[:, cg])
        ra, ix = _blockdiag_tiles(xc, wax_ref.at[g * tiles:(g + 1) * tiles])
        yield
        a, u = _rglru_gates(xc, ra, ix, ba_ref.at[:, cg], bx_ref.at[:, cg], lam_ref.at[:, cg])
        if front_pad:
            a = jnp.where(valid, a, 1.0)
            u = jnp.where(valid, u, 0.0)
        yield
        h, h_last = _scan_perm(a, u, h_st[:, cg])
        h_st[:, cg] = h_last
        yield
        emit(g, h * _silu(gates[g]))

    return [group(g) for g in range(L1_GROUPS)], hn_next, h_in


def _run_staggered(gens):
    live = []
    pending = list(gens)
    while pending or live:
        if pending:
            live.append(pending.pop(0))
        for gen in list(live):
            if next(gen, "done") == "done":
                live.remove(gen)


L1_STATE_SHAPES = ((TAIL, LRU_WIDTH), (1, LRU_WIDTH))


def _l1_prompt(h1, weights, init, front_pad):
    bsz, length, _ = h1.shape
    q_len = min(CHUNK, length)
    assert length % q_len == 0
    rows = _rows_per_step(bsz, L1_ROWS)
    assert len(weights) == N_L1_W and len(init) == N_L1_S
    grid = (bsz // rows, length // q_len)
    last = length // q_len - 1
    x_spec = pl.BlockSpec((rows, q_len, D_MODEL), lambda b, c: (b, c, 0))
    next_spec = pl.BlockSpec((rows, q_len, D_MODEL), lambda b, c: (b, jnp.minimum(c + 1, last), 0))
    in_specs = ([x_spec, next_spec] + [_const_spec(w.shape) for w in weights]
                + [_state_spec(s, 0) for s in L1_STATE_SHAPES])
    out_shape = ([jax.ShapeDtypeStruct((bsz, length, D_MODEL), F32)]
                 + [jax.ShapeDtypeStruct((bsz,) + s, F32) for s in L1_STATE_SHAPES])
    out_specs = [x_spec] + [_state_spec(s, rows) for s in L1_STATE_SHAPES]
    row_scratch = ((SUBLANE, LRU_WIDTH), (1, LRU_WIDTH), (q_len, 2 * LRU_WIDTH))
    scratch = [pltpu.VMEM(s, F32) for _ in range(rows) for s in row_scratch]
    return pl.pallas_call(
        functools.partial(_l1_prompt_kernel, front_pad=front_pad, rows=rows),
        grid=grid, in_specs=in_specs, out_specs=out_specs, out_shape=out_shape, scratch_shapes=scratch,
        compiler_params=pltpu.CompilerParams(dimension_semantics=("arbitrary", "arbitrary"),
                                             vmem_limit_bytes=VMEM_LIMIT),
        name="l1_prompt",
    )(h1, h1, *weights, *init)


def _conv_step(buf_ref, x, w_ref, b_ref, newbuf_ref):
    y = b_ref[...] + w_ref[3:4, :] * x
    for tap in range(TAIL):
        y = y + w_ref[tap:tap + 1, :] * buf_ref[tap]
    for tap in range(TAIL - 1):
        newbuf_ref[tap] = buf_ref[tap + 1]
    newbuf_ref[TAIL - 1] = x
    return y


def _l0_sample_pre_kernel(x_ref, nw_ref, win_ref, scw_ref, scb_ref, dtb_ref, alog_ref,
                          mcw_ref, mcb_ref, wqk_ref, wv_ref, wg_ref, bg_ref,
                          sbuf_ref, mbuf_ref, m0_ref, n0_ref,
                          nsb_ref, nmb_ref, zs_ref, xs_ref, bm_ref, cm_ref, xdt_t_ref, dec_t_ref,
                          zm_ref, xc_ref, q_ref, isv_t_ref, fs_t_ref, k_ref, mnew_ref, nnew_ref, den_ref):
    x = x_ref[...]
    hn = _rms(x, nw_ref[...])
    proj = _bdot(hn, _wload(win_ref[...]))
    zs_ref[...] = proj[:, OFF_ZS:OFF_ZS + SSD_WIDTH]
    zm_ref[...] = proj[:, OFF_ZM:OFF_ZM + ML_WIDTH]
    xbc = proj[:, OFF_XBC:OFF_XBC + SSD_CONV_CH]
    dt_raw = proj[:, OFF_DT:OFF_DT + LANE]
    xm = proj[:, OFF_XM:OFF_XM + ML_WIDTH]
    expand = _expand_matrix()

    xbc = _silu(_conv_step(sbuf_ref, xbc, scw_ref, scb_ref, nsb_ref))
    xs = xbc[:, :SSD_WIDTH]
    xs_ref[...] = xs
    bm_ref[...] = xbc[:, SSD_WIDTH:SSD_WIDTH + SSD_GROUPS * SSD_STATE]
    cm_ref[...] = xbc[:, SSD_WIDTH + SSD_GROUPS * SSD_STATE:]
    dt = _softplus(dt_raw + dtb_ref[...])
    log_a = -dt * jnp.exp(alog_ref[...])
    xdt_t_ref[...] = xs * _expand_heads(dt, expand)
    dec_t_ref[...] = _expand_heads(jnp.exp(log_a), expand)

    xc = _silu(_conv_step(mbuf_ref, xm, mcw_ref, mcb_ref, nmb_ref))
    xc_ref[...] = xc
    q, k, v, ig, logf = _mlstm_qkv_gates(xm, xc, wqk_ref, wv_ref, wg_ref, bg_ref)
    m0 = m0_ref[...]
    m_new = jnp.maximum(logf + m0, ig)
    fs = jnp.exp(logf + m0 - m_new)
    is_ = jnp.exp(ig - m_new)
    mnew_ref[...] = m_new
    r = lax.broadcasted_iota(jnp.int32, (LANE, ML_WIDTH), 0)
    cidx = lax.broadcasted_iota(jnp.int32, (LANE, ML_WIDTH), 1)
    expand_m = jnp.where(lax.shift_right_logical(cidx, 8) == r, 1.0, 0.0).astype(BF16)
    fs_e = _expand_heads(fs, expand_m)
    is_e = _expand_heads(is_, expand_m)
    n_new = fs_e * n0_ref[...] + is_e * k
    nnew_ref[...] = n_new
    q_ref[...] = q
    k_ref[...] = k
    isv_t_ref[...] = is_e * v
    fs_t_ref[...] = fs_e
    nq = n_new * q
    floor = jnp.exp(-m_new)
    for hd in range(ML_HEADS):
        den = jnp.sum(nq[:, hd * ML_HEAD_DIM:(hd + 1) * ML_HEAD_DIM], axis=-1, keepdims=True)
        den_ref[:, hd:hd + 1] = jnp.maximum(jnp.abs(den), floor[:, hd:hd + 1])


BT_S = 8
BT_C = 8


def _ssd_state_kernel(s_ref, xdt_ref, dec_ref, bm_ref, cm_ref, snew_ref, y_ref):
    for i in range(BT_S):
        x_col = xdt_ref[0, :, i:i + 1].reshape(SSD_HEADS, SSD_HEAD_DIM, 1)
        d_col = dec_ref[0, :, i:i + 1].reshape(SSD_HEADS, SSD_HEAD_DIM, 1)
        ys = []
        for g in range(SSD_GROUPS):
            hs = slice(g * SSD_HPG, (g + 1) * SSD_HPG)
            b_row = bm_ref[i:i + 1, g * SSD_STATE:(g + 1) * SSD_STATE].reshape(1, 1, SSD_STATE)
            c_row = cm_ref[i:i + 1, g * SSD_STATE:(g + 1) * SSD_STATE].reshape(1, 1, SSD_STATE)
            s_new = d_col[hs] * s_ref[i, hs] + x_col[hs] * b_row
            snew_ref[i, hs] = s_new
            ys.append(jnp.sum(s_new * c_row, axis=-1, keepdims=True))
        y_ref[0, :, i:i + 1] = jnp.concatenate(ys, axis=0).reshape(SSD_WIDTH, 1)


def _mlstm_state_kernel(c_ref, isv_ref, fs_ref, k_ref, q_ref, cnew_ref, num_ref):
    for i in range(BT_C):
        for hd in range(ML_HEADS):
            sl = slice(hd * ML_HEAD_DIM, (hd + 1) * ML_HEAD_DIM)
            v_col = isv_ref[0, sl, i:i + 1]
            f_col = fs_ref[0, sl, i:i + 1]
            c_new = f_col * c_ref[i, hd] + v_col * k_ref[0, i:i + 1, sl]
            cnew_ref[i, hd] = c_new
            num_ref[0, sl, i:i + 1] = jnp.sum(c_new * q_ref[0, i:i + 1, sl], axis=-1, keepdims=True)


def _sample_post_kernel(x_ref, ys_t_ref, num_t_ref, den_ref, zs_ref, xs_ref, zm_ref, xc_ref,
                        dsk_ref, snw_ref, msk_ref, mnw_ref, wout_ref,
                        nw1_ref, fnw_ref, win1_ref, wout1_ref, cw_ref, cb_ref,
                        wax_ref, ba_ref, bx_ref, lam_ref, lbuf_ref, h0_ref,
                        y_ref, nlb_ref, hnew_ref):
    xs = xs_ref[...]
    y_s = ys_t_ref[...] + dsk_ref[...] * xs
    y_s = _group_rmsnorm(y_s * _silu(zs_ref[...]), snw_ref[...])
    num = num_t_ref[...]
    den = den_ref[...]
    h_m = jnp.concatenate(
        [num[:, hd * ML_HEAD_DIM:(hd + 1) * ML_HEAD_DIM] / den[:, hd:hd + 1] for hd in range(ML_HEADS)], axis=-1)
    h_m = _head_layernorm(h_m) * mnw_ref[...]
    y_m = (h_m + msk_ref[...] * xc_ref[...]) * _silu(zm_ref[...])
    h1 = x_ref[...] + _bdot(jnp.concatenate([y_s, y_m], axis=-1), _wload(wout_ref[...]))

    hn = _rms(h1, nw1_ref[...])
    proj = _bdot(hn, _wload(win1_ref[...]))
    gate = proj[:, :LRU_WIDTH]
    xr = proj[:, LRU_WIDTH:]
    xc = _conv_step(lbuf_ref, xr, cw_ref, cb_ref, nlb_ref)
    ra, ix = _blockdiag_tiles(xc, wax_ref)
    a, u = _rglru_gates(xc, ra, ix, ba_ref, bx_ref, lam_ref)
    h = a * h0_ref[...] + u
    hnew_ref[...] = h
    h2 = h1 + _bdot(h * _silu(gate), _wload(wout1_ref[...]))
    y_ref[...] = _rms(h2, fnw_ref[...])


def _full_call(kernel_fn, out_shapes, args, name):
    return pl.pallas_call(
        kernel_fn,
        out_shape=[jax.ShapeDtypeStruct(s, F32) for s in out_shapes],
        compiler_params=pltpu.CompilerParams(vmem_limit_bytes=VMEM_LIMIT),
        name=name,
    )(*args)


def _to_cols(a, bt):
    rows, ch = a.shape
    return a.reshape(rows // bt, bt, ch).transpose(0, 2, 1)


def _from_cols(a):
    tiles, ch, bt = a.shape
    return a.transpose(0, 2, 1).reshape(tiles * bt, ch)


def _row(v, width=None):
    v = v.reshape(1, -1).astype(F32)
    if width is not None and v.shape[1] < width:
        v = jnp.pad(v, ((0, 0), (0, width - v.shape[1])))
    return v


PACK_STEPS = 8


def _pack_all(weights):
    flats = [w.reshape(-1, w.shape[-1]) for w in weights]
    for f in flats:
        assert f.shape[0] % (2 * SUBLANE * PACK_STEPS) == 0
    packed = pl.pallas_call(
        _pack_kernel,
        grid=(PACK_STEPS,),
        in_specs=[pl.BlockSpec((f.shape[0] // PACK_STEPS, f.shape[1]), lambda i: (i, 0)) for f in flats],
        out_specs=[pl.BlockSpec((f.shape[0] // PACK_STEPS // 2, f.shape[1]), lambda i: (i, 0)) for f in flats],
        out_shape=[jax.ShapeDtypeStruct((f.shape[0] // 2, f.shape[1]), jnp.uint32) for f in flats],
        compiler_params=pltpu.CompilerParams(vmem_limit_bytes=VMEM_LIMIT),
        name="pack_weights",
    )(*flats)
    return [p.reshape(w.shape[:-2] + (w.shape[-2] // 2, w.shape[-1])) for p, w in zip(packed, weights)]


def _pack_kernel(*refs):
    n = len(refs) // 2
    for w_ref, o_ref in zip(refs[:n], refs[n:]):
        o_ref[...] = pltpu.bitcast(w_ref[...].astype(BF16), jnp.uint32)


def _dense_block_tiles(w):
    nb, bi, bo = w.shape
    per = LANE // bi
    rows = w.reshape(nb // per, per * bi, bo)
    col = jnp.arange(per * bo)
    spread = (col[None, :] % bo == jnp.arange(bo)[:, None]).astype(w.dtype)
    rep = jnp.einsum('tro,oc->trc', rows, spread)
    same_block = (jnp.arange(per * bi)[:, None] // bi) == (col[None, :] // bo)
    return jnp.where(same_block, rep, 0.0)


def kernel(x_prompt, x_sample, state_ssd_conv, state_ssd, state_mlstm_conv, state_mlstm_C, state_mlstm_n,
           state_mlstm_m, state_lru_conv, state_lru_h, meta_tokens, norm_w, final_norm_w, w_in_mix, w_out_mix,
           ssd_conv_w, ssd_conv_b, ssd_dt_bias, ssd_a_log, ssd_d, ssd_norm_w, ml_conv_w, ml_conv_b, ml_wq, ml_wk,
           ml_wv, ml_w_gate, ml_b_gate, ml_skip, ml_norm_w, lru_w_in, lru_w_out, lru_conv_w, lru_conv_b, lru_wa,
           lru_ba, lru_wx, lru_bx, lru_lambda):
    bsz = x_prompt.shape[0]
    dec = x_sample.shape[0]

    w_in = w_in_mix[0]
    o1 = SSD_WIDTH
    o2 = o1 + SSD_CONV_CH
    o3 = o2 + SSD_HEADS
    win = jnp.concatenate([w_in[:, :o2], jnp.pad(w_in[:, o2:o3], ((0, 0), (0, LANE - SSD_HEADS))), w_in[:, o3:]],
                          axis=1)
    wout = w_out_mix[0]
    nw0 = _row(norm_w[0])
    nw1 = _row(norm_w[1])
    fnw = _row(final_norm_w)
    scw = ssd_conv_w[0]
    scb = _row(ssd_conv_b[0])
    dtb = _row(ssd_dt_bias[0], LANE)
    alog = _row(ssd_a_log[0], LANE)
    dsk = _row(jnp.repeat(ssd_d[0], SSD_HEAD_DIM))
    snw = _row(ssd_norm_w[0])
    mcw = ml_conv_w[0]
    mcb = _row(ml_conv_b[0])
    wqk = jnp.concatenate([_dense_block_tiles(ml_wq[0]), _dense_block_tiles(ml_wk[0])], axis=2)
    wv = _dense_block_tiles(ml_wv[0])
    wg_raw = ml_w_gate[0]
    wg = jnp.concatenate([jnp.pad(wg_raw[:, :ML_HEADS], ((0, 0), (0, LANE - ML_HEADS))),
                          jnp.pad(wg_raw[:, ML_HEADS:], ((0, 0), (0, LANE - ML_HEADS)))], axis=1)
    bg = jnp.concatenate([_row(ml_b_gate[0, :ML_HEADS], LANE), _row(ml_b_gate[0, ML_HEADS:], LANE)], axis=1)
    msk = _row(ml_skip[0])
    mnw = _row(ml_norm_w[0])
    win1 = lru_w_in[0]
    wout1 = lru_w_out[0]
    lcw = lru_conv_w[0]
    lcb = _row(lru_conv_b[0])
    wax = jnp.concatenate([lru_wa[0], lru_wx[0]], axis=2)
    r_idx = lax.broadcasted_iota(jnp.int32, (LANE, SSD_WIDTH), 0)
    c_idx = lax.broadcasted_iota(jnp.int32, (LANE, SSD_WIDTH), 1)
    expand = (c_idx // SSD_HEAD_DIM == r_idx).astype(F32)
    ba = _row(lru_ba[0])
    bx = _row(lru_bx[0])
    lam = _row(lru_lambda[0])

    win, wout, wqk, wv, wg, expand, win1, wout1, wax = _pack_all(
        [win, wout, wqk, wv, wg, expand, win1, wout1, wax])
    l0_w = (nw0, win, wout, scw, scb, dtb, alog, dsk, snw, mcw, mcb, wqk, wv, wg, bg, msk, mnw, expand)
    l1_w = (nw1, fnw, win1, wout1, lcw, lcb, wax, ba, bx, lam)

    zero0 = tuple(jnp.zeros((1,) + s, F32) for s in L0_STATE_SHAPES)
    zero1 = tuple(jnp.zeros((1,) + s, F32) for s in L1_STATE_SHAPES)
    meta = jnp.pad(meta_tokens.astype(F32), ((CHUNK - N_META, 0), (0, 0)))[None]
    meta_out = _l0_prompt(meta, l0_w, zero0, CHUNK - N_META)
    meta1_out = _l1_prompt(meta_out[0], l1_w, zero1, CHUNK - N_META)
    l0_out = _l0_prompt(x_prompt, l0_w, tuple(meta_out[1:]), 0)
    h1_p, p_sc, p_s, p_mc, p_c, p_n, p_m = l0_out
    y_prompt, p_lc, p_lh = _l1_prompt(h1_p, l1_w, tuple(meta1_out[1:]), 0)

    p_m = p_m[:, 0, :ML_HEADS]
    p_lh = p_lh[:, 0]

    xs2 = x_sample[:, 0]
    sbuf = jnp.moveaxis(state_ssd_conv[0], 1, 0)
    mbuf = jnp.moveaxis(state_mlstm_conv[0], 1, 0)
    lbuf = jnp.moveaxis(state_lru_conv[0], 1, 0)
    m0 = jnp.pad(state_mlstm_m[0], ((0, 0), (0, LANE - ML_HEADS)))
    n0 = state_mlstm_n[0].reshape(dec, ML_WIDTH)
    pre_shapes = ((TAIL, dec, SSD_CONV_CH), (TAIL, dec, ML_WIDTH), (dec, SSD_WIDTH), (dec, SSD_WIDTH),
                  (dec, SSD_GROUPS * SSD_STATE), (dec, SSD_GROUPS * SSD_STATE), (dec, SSD_WIDTH), (dec, SSD_WIDTH),
                  (dec, ML_WIDTH), (dec, ML_WIDTH), (dec, ML_WIDTH), (dec, ML_WIDTH), (dec, ML_WIDTH),
                  (dec, ML_WIDTH), (dec, LANE), (dec, ML_WIDTH), (dec, ML_HEADS))
    (nsb, nmb, zs, xs_c, bm, cm, xdt_t, dec_t, zm, xc_m, q, isv_t, fs_t, k, m_new, n_new, den) = _full_call(
        _l0_sample_pre_kernel, pre_shapes,
        (xs2, nw0, win, scw, scb, dtb, alog, mcw, mcb, wqk, wv, wg, bg, sbuf, mbuf, m0, n0), "l0_sample_pre")

    xdt_c = _to_cols(xdt_t, BT_S)
    dec_c = _to_cols(dec_t, BT_S)
    s_new, ys_c = pl.pallas_call(
        _ssd_state_kernel,
        grid=(dec // BT_S,),
        in_specs=[pl.BlockSpec((BT_S, SSD_HEADS, SSD_HEAD_DIM, SSD_STATE), lambda i: (i, 0, 0, 0)),
                  pl.BlockSpec((1, SSD_WIDTH, BT_S), lambda i: (i, 0, 0)),
                  pl.BlockSpec((1, SSD_WIDTH, BT_S), lambda i: (i, 0, 0)),
                  pl.BlockSpec((BT_S, SSD_GROUPS * SSD_STATE), lambda i: (i, 0)),
                  pl.BlockSpec((BT_S, SSD_GROUPS * SSD_STATE), lambda i: (i, 0))],
        out_specs=[pl.BlockSpec((BT_S, SSD_HEADS, SSD_HEAD_DIM, SSD_STATE), lambda i: (i, 0, 0, 0)),
                   pl.BlockSpec((1, SSD_WIDTH, BT_S), lambda i: (i, 0, 0))],
        out_shape=[jax.ShapeDtypeStruct((dec, SSD_HEADS, SSD_HEAD_DIM, SSD_STATE), F32),
                   jax.ShapeDtypeStruct((dec // BT_S, SSD_WIDTH, BT_S), F32)],
        compiler_params=pltpu.CompilerParams(dimension_semantics=("arbitrary",), vmem_limit_bytes=VMEM_LIMIT),
        name="ssd_state",
    )(state_ssd[0], xdt_c, dec_c, bm, cm)

    isv_c = _to_cols(isv_t, BT_C)
    fs_c = _to_cols(fs_t, BT_C)
    c_new, num_c = pl.pallas_call(
        _mlstm_state_kernel,
        grid=(dec // BT_C,),
        in_specs=[pl.BlockSpec((BT_C, ML_HEADS, ML_HEAD_DIM, ML_HEAD_DIM), lambda i: (i, 0, 0, 0)),
                  pl.BlockSpec((1, ML_WIDTH, BT_C), lambda i: (i, 0, 0)),
                  pl.BlockSpec((1, ML_WIDTH, BT_C), lambda i: (i, 0, 0)),
                  pl.BlockSpec((1, BT_C, ML_WIDTH), lambda i: (i, 0, 0)),
                  pl.BlockSpec((1, BT_C, ML_WIDTH), lambda i: (i, 0, 0))],
        out_specs=[pl.BlockSpec((BT_C, ML_HEADS, ML_HEAD_DIM, ML_HEAD_DIM), lambda i: (i, 0, 0, 0)),
                   pl.BlockSpec((1, ML_WIDTH, BT_C), lambda i: (i, 0, 0))],
        out_shape=[jax.ShapeDtypeStruct((dec, ML_HEADS, ML_HEAD_DIM, ML_HEAD_DIM), F32),
                   jax.ShapeDtypeStruct((dec // BT_C, ML_WIDTH, BT_C), F32)],
        compiler_params=pltpu.CompilerParams(dimension_semantics=("arbitrary",), vmem_limit_bytes=VMEM_LIMIT),
        name="mlstm_state",
    )(state_mlstm_C[0], isv_c, fs_c, k.reshape(dec // BT_C, BT_C, ML_WIDTH), q.reshape(dec // BT_C, BT_C, ML_WIDTH))

    post_shapes = ((dec, D_MODEL), (TAIL, dec, LRU_WIDTH), (dec, LRU_WIDTH))
    y_s2, nlb, h_new = _full_call(
        _sample_post_kernel, post_shapes,
        (xs2, _from_cols(ys_c), _from_cols(num_c), den, zs, xs_c, zm, xc_m, dsk, snw, msk, mnw, wout,
         nw1, fnw, win1, wout1, lcw, lcb, wax, ba, bx, lam, lbuf, state_lru_h[0]), "sample_post")

    s_sc = jnp.moveaxis(nsb, 0, 1)[None]
    s_mc = jnp.moveaxis(nmb, 0, 1)[None]
    s_lc = jnp.moveaxis(nlb, 0, 1)[None]
    return (y_prompt, y_s2[:, None, :],
            p_sc[None], p_s[None], p_mc[None], p_c[None], p_n[None], p_m[None], p_lc[None], p_lh[None],
            s_sc, s_new[None], s_mc, c_new[None], n_new.reshape(dec, ML_HEADS, ML_HEAD_DIM)[None],
            m_new[:, :ML_HEADS][None], s_lc, h_new[None])
```

```python
import functools

import jax
import jax.numpy as jnp
from jax import lax
from jax.experimental import pallas as pl
from jax.experimental.pallas import tpu as pltpu

F32 = jnp.float32
BF16 = jnp.bfloat16

D_MODEL = 1024
N_META = 16
CONV_W = 4
EPS = 1e-6
NEG = -1e30
SSD_WIDTH = 1024
SSD_HEAD_DIM = 64
SSD_HEADS = 16
SSD_GROUPS = 2
SSD_HPG = 8
SSD_STATE = 128
SSD_CONV_CH = 1536
ML_WIDTH = 1024
ML_HEADS = 4
ML_HEAD_DIM = 256
ML_QKV_BLOCK = 4
LRU_WIDTH = 2048
LRU_BLOCKS = 16
LRU_BLOCK = 128
LRU_C = 8.0

LANE = 128
SUBLANE = 8
CHUNK = 128
L0_ROWS = 2
L1_ROWS = 4
L1_GROUPS = 4
TAIL = CONV_W - 1

OFF_ZS = 0
OFF_XBC = OFF_ZS + SSD_WIDTH
OFF_DT = OFF_XBC + SSD_CONV_CH
OFF_ZM = OFF_DT + LANE
OFF_XM = OFF_ZM + ML_WIDTH
IN_MIX_PAD = OFF_XM + ML_WIDTH

VMEM_LIMIT = 56 * 1024 * 1024


def _sigmoid(x):
    return 1.0 / (1.0 + jnp.exp(-x))


def _silu(x):
    return x * _sigmoid(x)


def _softplus(x):
    return jnp.maximum(x, 0.0) + jnp.log1p(jnp.exp(-jnp.abs(x)))


def _rms(x, w):
    return x * lax.rsqrt(jnp.mean(x * x, axis=-1, keepdims=True) + EPS) * w


def _bdot(a, b):
    return jnp.dot(a.astype(BF16), b.astype(BF16), preferred_element_type=F32)


def _wload(w):
    return pltpu.bitcast(w, BF16)


def _split3(x):
    hi = x.astype(BF16)
    r = x - hi.astype(F32)
    mid = r.astype(BF16)
    lo = (r - mid.astype(F32)).astype(BF16)
    return hi, mid, lo


def _cumsum_rows(x, tril):
    hi, mid, lo = _split3(x)
    d = functools.partial(jnp.dot, preferred_element_type=F32)
    return d(tril, hi) + d(tril, mid) + d(tril, lo)


def _expand_heads(x, expand):
    hi, mid, _ = _split3(x)
    d = functools.partial(jnp.dot, preferred_element_type=F32)
    return d(hi, expand) + d(mid, expand)


def _expand_matrix():
    r = lax.broadcasted_iota(jnp.int32, (LANE, SSD_WIDTH), 0)
    c = lax.broadcasted_iota(jnp.int32, (LANE, SSD_WIDTH), 1)
    return jnp.where(lax.shift_right_logical(c, 6) == r, 1.0, 0.0).astype(BF16)


def _blockdiag_tiles(x, w_ref):
    k = w_ref.shape[0]
    m = w_ref.shape[2] // LANE
    prods = [_bdot(x[:, t * LANE:(t + 1) * LANE], _wload(w_ref[t])) for t in range(k)]
    return [jnp.concatenate([p[:, j * LANE:(j + 1) * LANE] for p in prods], axis=-1) for j in range(m)]


def _group_rmsnorm(y, w):
    half = SSD_WIDTH // SSD_GROUPS
    parts = []
    for g in range(SSD_GROUPS):
        yg = y[:, g * half:(g + 1) * half]
        parts.append(yg * lax.rsqrt(jnp.mean(yg * yg, axis=-1, keepdims=True) + EPS))
    return jnp.concatenate(parts, axis=-1) * w


def _head_layernorm(h):
    parts = []
    for k in range(ML_HEADS):
        hk = h[:, k * ML_HEAD_DIM:(k + 1) * ML_HEAD_DIM]
        mu = jnp.mean(hk, axis=-1, keepdims=True)
        d = hk - mu
        var = jnp.mean(d * d, axis=-1, keepdims=True)
        parts.append(d * lax.rsqrt(var + EPS))
    return jnp.concatenate(parts, axis=-1)


def _mlstm_qkv_gates(xm, xc, wqk_ref, wv_ref, wg_ref, bg_ref):
    q, k = _blockdiag_tiles(xc, wqk_ref)
    v, = _blockdiag_tiles(xm, wv_ref)
    gates = _bdot(jnp.concatenate([q, k, v], axis=-1), _wload(wg_ref[...])) + bg_ref[...]
    ig = gates[:, :LANE]
    logf = -_softplus(-gates[:, LANE:])
    return q, k * (ML_HEAD_DIM ** -0.5), v, ig, logf


N_L0_W = 18
N_L0_S = 6


def _l0_prompt_kernel(x_ref, xnext_ref, *refs, front_pad, rows):
    w_refs = refs[:N_L0_W]
    init_refs = refs[N_L0_W:N_L0_W + N_L0_S]
    out_refs = refs[N_L0_W + N_L0_S:N_L0_W + 2 * N_L0_S + 1]
    scratch = refs[N_L0_W + 2 * N_L0_S + 1:]
    per_row = len(scratch) // rows
    c = pl.program_id(1)
    win_ref, wout_ref = w_refs[1], w_refs[2]
    q_len = x_ref.shape[1]
    pieces = {}

    def each_row(phase):
        return [_l0_prompt_row(x_ref.at[r], xnext_ref.at[r], *w_refs, *init_refs, *(o.at[r] for o in out_refs),
                               *scratch[r * per_row:(r + 1) * per_row], front_pad=front_pad, phase=phase,
                               emit=lambda k0, y, r=r: pieces.setdefault(k0, {}).__setitem__(r, y))
                for r in range(rows)]

    @pl.when(c == 0)
    def _():
        each_row("init")

    bodies = each_row("body")
    proj_refs = [scratch[r * per_row + per_row - 1] for r in range(rows)]
    _, to_time = _perm_matrices(q_len)
    partials = []

    def in_proj():
        lhs = jnp.concatenate([hn_next for _, hn_next, _ in bodies], axis=0)
        for lo, hi in L0_PROJ_PIECES:
            res = _bdot(lhs, _wload(win_ref[:, lo:hi]))
            for r in range(rows):
                proj_refs[r][:, lo:hi] = res[r * q_len:(r + 1) * q_len]
            yield

    def out_proj():
        pending = [(0, SSD_WIDTH)] + [(SSD_WIDTH + hd * ML_HEAD_DIM, ML_HEAD_DIM) for hd in range(ML_HEADS)]
        while pending:
            for k0, width in list(pending):
                if len(pieces.get(k0, ())) == rows:
                    y_t = jnp.concatenate([_move_rows(to_time, pieces[k0][r].astype(BF16)) for r in range(rows)],
                                          axis=0)
                    partials.append(_bdot(y_t, _wload(wout_ref[k0 // 2:(k0 + width) // 2, :])))
                    pending.remove((k0, width))
            yield

    _run_round_robin([gen for gens, _, _ in bodies for gen in gens] + [in_proj(), out_proj()])
    total = partials[0]
    for part in partials[1:]:
        total = total + part
    for r, (_, _, x) in enumerate(bodies):
        h1 = x + total[r * q_len:(r + 1) * q_len]
        if front_pad:
            h1 = jnp.where(lax.broadcasted_iota(jnp.int32, (q_len, 1), 0) >= front_pad, h1, 0.0)
        out_refs[0][r] = h1

    @pl.when(c == pl.num_programs(1) - 1)
    def _():
        each_row("final")


L0_PROJ_PIECES = ((OFF_XBC, OFF_ZM), (OFF_XM, IN_MIX_PAD), (OFF_ZM, OFF_XM), (OFF_ZS, OFF_XBC))


def _run_round_robin(gens):
    live = list(gens)
    while live:
        for gen in list(live):
            if next(gen, "done") == "done":
                live.remove(gen)


def _l0_prompt_row(x_ref, xnext_ref, nw_ref, win_ref, wout_ref,
                   scw_ref, scb_ref, dtb_ref, alog_ref, dsk_ref, snw_ref,
                   mcw_ref, mcb_ref, wqk_ref, wv_ref, wg_ref, bg_ref, msk_ref, mnw_ref, expand_ref,
                   isc_ref, iss_ref, imc_ref, ict_ref, inn_ref, imm_ref,
                   h1_ref, osc_ref, oss_ref, omc_ref, oct_ref, onn_ref, omm_ref,
                   sbuf, mbuf, s_st, ct_st, n_st, m_st, proj_s, *, front_pad, phase, emit):
    q_len = x_ref.shape[0]

    if phase == "init":
        sbuf[...] = jnp.zeros(sbuf.shape, F32)
        mbuf[...] = jnp.zeros(mbuf.shape, F32)
        sbuf[SUBLANE - TAIL:SUBLANE, :] = isc_ref[0]
        mbuf[SUBLANE - TAIL:SUBLANE, :] = imc_ref[0]
        for g in range(SSD_GROUPS):
            heads = iss_ref[0, g * SSD_HPG:(g + 1) * SSD_HPG]
            s_st[g] = heads.reshape(SSD_HPG * SSD_HEAD_DIM, SSD_STATE).T
        for hd in range(ML_HEADS):
            ct_st[hd] = ict_ref[0, hd].T
        n_st[...] = inn_ref[0]
        m_st[...] = imm_ref[0]
        hn0 = _move_rows(_perm_matrices(q_len)[0], _rms(x_ref[...], nw_ref[...]).astype(BF16))
        proj_s[...] = _bdot(hn0, _wload(win_ref[...]))
        return None
    if phase == "final":
        osc_ref[...] = sbuf[SUBLANE - TAIL:SUBLANE, :]
        omc_ref[...] = mbuf[SUBLANE - TAIL:SUBLANE, :]
        for g in range(SSD_GROUPS):
            oss_ref[g * SSD_HPG:(g + 1) * SSD_HPG] = s_st[g].T.reshape(SSD_HPG, SSD_HEAD_DIM, SSD_STATE)
        for hd in range(ML_HEADS):
            oct_ref[hd] = ct_st[hd].T
        onn_ref[...] = n_st[...]
        omm_ref[...] = m_st[...]
        return None

    x = x_ref[...]
    to_perm, to_time = _perm_matrices(q_len)
    t_col = _perm_time(q_len)
    t_row = _perm_time(q_len, row=True)
    causal = t_col >= t_row
    tril = jnp.where(causal, 1.0, 0.0).astype(BF16)
    valid = (t_col >= front_pad) if front_pad else None
    hn_next = _move_rows(to_perm, _rms(xnext_ref[...], nw_ref[...]).astype(BF16))
    xbc_raw = proj_s[:, OFF_XBC:OFF_XBC + SSD_CONV_CH]
    dt_raw = proj_s[:, OFF_DT:OFF_DT + LANE]
    xm = proj_s[:, OFF_XM:OFF_XM + ML_WIDTH]
    z_s = proj_s[:, OFF_ZS:OFF_ZS + SSD_WIDTH]
    z_m = proj_s[:, OFF_ZM:OFF_ZM + ML_WIDTH]

    def ssd():
        xbc = _silu(_conv_perm(sbuf, xbc_raw, scw_ref, scb_ref))
        yield
        xs = xbc[:, :SSD_WIDTH]
        bm = xbc[:, SSD_WIDTH:SSD_WIDTH + SSD_GROUPS * SSD_STATE]
        cm = xbc[:, SSD_WIDTH + SSD_GROUPS * SSD_STATE:]
        dt = _softplus(dt_raw + dtb_ref[...])
        if front_pad:
            dt = jnp.where(valid, dt, 0.0)
        log_a = -dt * jnp.exp(alog_ref[...])
        a_cs = _cumsum_rows(log_a, tril)
        yield
        a_last = a_cs[q_len - 1:q_len, :]
        expand = _wload(expand_ref[...])
        w_state = _expand_heads(dt * jnp.exp(a_last - a_cs), expand)
        e_acs = _expand_heads(jnp.exp(a_cs), expand)
        a_cs_t = a_cs.T
        dt_t = dt.T
        yield
        pair_lo = lax.broadcasted_iota(jnp.int32, (q_len, LANE), 1) < SSD_HEAD_DIM
        half = SSD_WIDTH // SSD_GROUPS
        y_groups = []
        for g in range(SSD_GROUPS):
            bg = bm[:, g * SSD_STATE:(g + 1) * SSD_STATE]
            cg = cm[:, g * SSD_STATE:(g + 1) * SSD_STATE]
            bg_t = bg.T
            xg = xs[:, g * half:(g + 1) * half]
            eg = e_acs[:, g * half:(g + 1) * half]
            s_old = s_st[g]
            cb = _bdot(cg, bg_t)
            y_off = _bdot(cg, s_old) * eg
            s_st[g] = eg[q_len - 1:q_len, :] * s_old + _bdot(bg_t, xg * w_state[:, g * half:(g + 1) * half])
            yield
            y_pairs = []
            for pr in range(SSD_HPG // 2):
                ms = []
                for e in (2 * pr, 2 * pr + 1):
                    hd = g * SSD_HPG + e
                    seg = jnp.exp(jnp.where(causal, a_cs[:, hd:hd + 1] - a_cs_t[hd:hd + 1, :], -jnp.inf))
                    ms.append(cb * seg * dt_t[hd:hd + 1, :])
                xp = xg[:, pr * LANE:(pr + 1) * LANE]
                rhs = jnp.concatenate([jnp.where(pair_lo, xp, 0.0), jnp.where(pair_lo, 0.0, xp)], axis=0)
                y_pairs.append(_bdot(jnp.concatenate(ms, axis=-1), rhs))
                yield
            y_groups.append(jnp.concatenate(y_pairs, axis=-1) + y_off)
        y_s = jnp.concatenate(y_groups, axis=-1) + dsk_ref[...] * xs
        emit(0, _group_rmsnorm(y_s * _silu(z_s), snw_ref[...]))

    def mlstm():
        xc = _silu(_conv_perm(mbuf, xm, mcw_ref, mcb_ref))
        yield
        q, k = _blockdiag_tiles(xc, wqk_ref)
        v, = _blockdiag_tiles(xm, wv_ref)
        yield
        gates = _bdot(jnp.concatenate([q, k, v], axis=-1), _wload(wg_ref[...])) + bg_ref[...]
        k = k * (ML_HEAD_DIM ** -0.5)
        yield
        ig = gates[:, :LANE]
        logf = -_softplus(-gates[:, LANE:])
        if front_pad:
            ig = jnp.where(valid, ig, NEG)
            logf = jnp.where(valid, logf, 0.0)
        bcum = _cumsum_rows(logf, tril)
        yield
        ftot = bcum[q_len - 1:q_len, :]
        m_prev = m_st[...]
        w_end = ftot - bcum + ig
        m_new = jnp.maximum(ftot + m_prev, jnp.max(w_end, axis=0, keepdims=True))
        sc = jnp.exp(ftot + m_prev - m_new)
        wexp = jnp.exp(w_end - m_new)
        inter = bcum + m_prev
        bcum_t = bcum.T
        ig_t = ig.T
        m_st[...] = m_new
        yield
        for hd in range(ML_HEADS):
            sl = slice(hd * ML_HEAD_DIM, (hd + 1) * ML_HEAD_DIM)
            q_h, k_h, v_h = q[:, sl], k[:, sl], v[:, sl]
            k_t = k_h.T
            dmat = jnp.where(causal, bcum[:, hd:hd + 1] - bcum_t[hd:hd + 1, :] + ig_t[hd:hd + 1, :], -jnp.inf)
            inter_h = inter[:, hd:hd + 1]
            m_t = jnp.maximum(inter_h, jnp.max(dmat, axis=-1, keepdims=True))
            dexp = jnp.exp(dmat - m_t)
            inter_sc = jnp.exp(inter_h - m_t)
            s = _bdot(q_h, k_t) * dexp
            yield
            ct_old = ct_st[hd]
            n_old = n_st[hd:hd + 1, :]
            num = _bdot(s, v_h) + inter_sc * _bdot(q_h, ct_old)
            den = jnp.sum(s, axis=-1, keepdims=True) + inter_sc * jnp.sum(q_h * n_old, axis=-1, keepdims=True)
            h_h = num / jnp.maximum(jnp.abs(den), jnp.exp(-m_t))
            w_col = wexp[:, hd:hd + 1]
            sc_h = sc[:, hd:hd + 1]
            ct_st[hd] = sc_h * ct_old + _bdot(k_t, v_h * w_col)
            n_st[hd:hd + 1, :] = sc_h * n_old + jnp.sum(k_h * w_col, axis=0, keepdims=True)
            yield
            mu = jnp.mean(h_h, axis=-1, keepdims=True)
            dev = h_h - mu
            var = jnp.mean(dev * dev, axis=-1, keepdims=True)
            h_h = dev * lax.rsqrt(var + EPS) * mnw_ref[:, sl]
            emit(SSD_WIDTH + hd * ML_HEAD_DIM, (h_h + msk_ref[:, sl] * xc[:, sl]) * _silu(z_m[:, sl]))
            yield

    return [ssd(), mlstm()], hn_next, x


def _const_spec(shape):
    nd = len(shape)
    return pl.BlockSpec(shape, lambda b, c: (0,) * nd)


def _state_spec(shape, rows):
    nd = len(shape)
    if rows:
        return pl.BlockSpec((rows,) + shape, lambda b, c: (b,) + (0,) * nd)
    return pl.BlockSpec((1,) + shape, lambda b, c: (0,) * (nd + 1))


def _rows_per_step(bsz, want):
    return want if bsz % want == 0 else 1


L0_STATE_SHAPES = ((TAIL, SSD_CONV_CH), (SSD_HEADS, SSD_HEAD_DIM, SSD_STATE), (TAIL, ML_WIDTH),
                   (ML_HEADS, ML_HEAD_DIM, ML_HEAD_DIM), (ML_HEADS, ML_HEAD_DIM), (1, LANE))
L0_CARRY_SHAPES = ((SUBLANE, SSD_CONV_CH), (SUBLANE, ML_WIDTH), (SSD_GROUPS, SSD_STATE, SSD_WIDTH // SSD_GROUPS),
                   (ML_HEADS, ML_HEAD_DIM, ML_HEAD_DIM), (ML_HEADS, ML_HEAD_DIM), (1, LANE))


def _l0_prompt(x, weights, init, front_pad):
    bsz, length, _ = x.shape
    q_len = min(CHUNK, length)
    assert length % q_len == 0
    rows = _rows_per_step(bsz, L0_ROWS)
    assert len(weights) == N_L0_W and len(init) == N_L0_S
    grid = (bsz // rows, length // q_len)
    last = length // q_len - 1
    x_spec = pl.BlockSpec((rows, q_len, D_MODEL), lambda b, c: (b, c, 0))
    next_spec = pl.BlockSpec((rows, q_len, D_MODEL), lambda b, c: (b, jnp.minimum(c + 1, last), 0))
    in_specs = ([x_spec, next_spec] + [_const_spec(w.shape) for w in weights]
                + [_state_spec(s, 0) for s in L0_STATE_SHAPES])
    out_shape = ([jax.ShapeDtypeStruct((bsz, length, D_MODEL), F32)]
                 + [jax.ShapeDtypeStruct((bsz,) + s, F32) for s in L0_STATE_SHAPES])
    out_specs = [x_spec] + [_state_spec(s, rows) for s in L0_STATE_SHAPES]
    row_scratch = L0_CARRY_SHAPES + ((q_len, IN_MIX_PAD),)
    scratch = [pltpu.VMEM(s, F32) for _ in range(rows) for s in row_scratch]
    return pl.pallas_call(
        functools.partial(_l0_prompt_kernel, front_pad=front_pad, rows=rows),
        grid=grid, in_specs=in_specs, out_specs=out_specs, out_shape=out_shape, scratch_shapes=scratch,
        compiler_params=pltpu.CompilerParams(dimension_semantics=("arbitrary", "arbitrary"),
                                             vmem_limit_bytes=VMEM_LIMIT),
        name="l0_prompt",
    )(x, x, *weights, *init)


def _rglru_gates(xc, ra, ix, ba_ref, bx_ref, lam_ref):
    r = _sigmoid(ra + ba_ref[...])
    i = _sigmoid(ix + bx_ref[...])
    log_a = r * (-LRU_C * _softplus(-lam_ref[...]))
    a = jnp.exp(log_a)
    u = jnp.sqrt(1.0 - a * a) * (i * xc)
    return a, u


def _perm_time(n, row=False):
    p = lax.broadcasted_iota(jnp.int32, (1, n) if row else (n, 1), 1 if row else 0)
    return (n // SUBLANE) * (p & (SUBLANE - 1)) + lax.shift_right_logical(p, 3)


def _perm_matrices(n):
    nb = n // SUBLANE
    r = lax.broadcasted_iota(jnp.int32, (n, n), 0)
    c = lax.broadcasted_iota(jnp.int32, (n, n), 1)
    to_perm = jnp.where(c == nb * (r & (SUBLANE - 1)) + lax.shift_right_logical(r, 3), 1.0, 0.0)
    to_time = jnp.where(r == nb * (c & (SUBLANE - 1)) + lax.shift_right_logical(c, 3), 1.0, 0.0)
    return to_perm.astype(BF16), to_time.astype(BF16)


def _move_rows(sel, x_bf16):
    return jnp.dot(sel, x_bf16, preferred_element_type=F32).astype(BF16)


def _conv_perm(tail_ref, x, w_ref, b_ref):
    n, ch = x.shape
    nb = n // SUBLANE
    x3 = x.reshape(nb, SUBLANE, ch)
    tail8 = tail_ref[...]
    sub = lax.broadcasted_iota(jnp.int32, (SUBLANE, ch), 0)
    y = b_ref[...].reshape(1, 1, ch) + w_ref[TAIL:TAIL + 1, :].reshape(1, 1, ch) * x3
    wrapped = [jnp.where(sub >= 1, pltpu.roll(x3[nb - d], 1, 0), tail8[SUBLANE - d:SUBLANE - d + 1, :])
               for d in range(1, CONV_W)]
    for back in range(1, CONV_W):
        head = jnp.stack([wrapped[back - j - 1] for j in range(back)], axis=0)
        shifted = jnp.concatenate([head, x3[:nb - back]], axis=0)
        y = y + w_ref[TAIL - back:TAIL - back + 1, :].reshape(1, 1, ch) * shifted
    for d in range(1, CONV_W):
        tail_ref[SUBLANE - d:SUBLANE - d + 1, :] = x3[nb - d][SUBLANE - 1:SUBLANE, :]
    return y.reshape(n, ch)


def _scan_perm(a, u, h_prev):
    n, ch = a.shape
    nb = n // SUBLANE
    a3 = a.reshape(nb, SUBLANE, ch)
    u3 = u.reshape(nb, SUBLANE, ch)
    local = [u3[0]]
    decay = [a3[0]]
    for j in range(1, nb):
        local.append(a3[j] * local[-1] + u3[j])
        decay.append(a3[j] * decay[-1])
    seg_u, seg_a = local[-1], decay[-1]
    sub = lax.broadcasted_iota(jnp.int32, (SUBLANE, ch), 0)
    shift = 1
    while shift < SUBLANE:
        keep = sub >= shift
        seg_u = seg_u + seg_a * jnp.where(keep, pltpu.roll(seg_u, shift, 0), 0.0)
        seg_a = seg_a * jnp.where(keep, pltpu.roll(seg_a, shift, 0), 1.0)
        shift *= 2
    seg_end = seg_a * h_prev + seg_u
    carry = jnp.where(sub >= 1, pltpu.roll(seg_end, 1, 0), h_prev)
    h3 = jnp.stack([local[j] + decay[j] * carry for j in range(nb)], axis=0)
    return h3.reshape(n, ch), seg_end[SUBLANE - 1:SUBLANE, :]


N_L1_W = 10
N_L1_S = 2


def _l1_prompt_kernel(h_ref, hnext_ref, *refs, front_pad, rows):
    w_refs = refs[:N_L1_W]
    init_refs = refs[N_L1_W:N_L1_W + N_L1_S]
    out_refs = refs[N_L1_W + N_L1_S:N_L1_W + 2 * N_L1_S + 1]
    scratch = refs[N_L1_W + 2 * N_L1_S + 1:]
    per_row = len(scratch) // rows
    c = pl.program_id(1)
    fnw_ref, win_ref, wout_ref = w_refs[1], w_refs[2], w_refs[3]
    q_len = h_ref.shape[1]
    gw = LRU_WIDTH // L1_GROUPS
    pieces = {}

    def each_row(phase):
        return [_l1_prompt_row(h_ref.at[r], hnext_ref.at[r], *w_refs, *init_refs, *(o.at[r] for o in out_refs),
                               *scratch[r * per_row:(r + 1) * per_row], front_pad=front_pad, phase=phase,
                               emit=lambda g, y, r=r: pieces.setdefault(g, {}).__setitem__(r, y))
                for r in range(rows)]

    @pl.when(c == 0)
    def _():
        each_row("init")

    bodies = each_row("body")
    proj_refs = [scratch[r * per_row + per_row - 1] for r in range(rows)]
    _, to_time = _perm_matrices(q_len)
    partials = []

    def in_proj():
        lhs = jnp.concatenate([hn_next for _, hn_next, _ in bodies], axis=0)
        for g in range(L1_GROUPS):
            for lo in (g * gw, LRU_WIDTH + g * gw):
                res = _bdot(lhs, _wload(win_ref[:, lo:lo + gw]))
                for r in range(rows):
                    proj_refs[r][:, lo:lo + gw] = res[r * q_len:(r + 1) * q_len]
            yield

    def out_proj():
        pending = list(range(L1_GROUPS))
        while pending:
            for g in list(pending):
                if len(pieces.get(g, ())) == rows:
                    y_t = jnp.concatenate([_move_rows(to_time, pieces[g][r].astype(BF16)) for r in range(rows)],
                                          axis=0)
                    partials.append(_bdot(y_t, _wload(wout_ref[g * gw // 2:(g + 1) * gw // 2, :])))
                    pending.remove(g)
            yield

    chains = [bodies[r][0][g] for g in range(L1_GROUPS) for r in range(rows)]
    _run_staggered([in_proj()] + chains + [out_proj()])
    total = partials[0]
    for part in partials[1:]:
        total = total + part
    for r, (_, _, h_in) in enumerate(bodies):
        out_refs[0][r] = _rms(h_in + total[r * q_len:(r + 1) * q_len], fnw_ref[...])

    @pl.when(c == pl.num_programs(1) - 1)
    def _():
        each_row("final")


def _l1_in_proj(h_val, nw_ref, win_ref, to_perm):
    hn = _move_rows(to_perm, _rms(h_val, nw_ref[...]).astype(BF16))
    return _bdot(hn, _wload(win_ref[...]))


def _l1_prompt_row(h_ref, hnext_ref, nw_ref, fnw_ref, win_ref, wout_ref, cw_ref, cb_ref,
                   wax_ref, ba_ref, bx_ref, lam_ref, ilc_ref, ilh_ref,
                   y_ref, olc_ref, olh_ref, lbuf, h_st, proj_s, *, front_pad, phase, emit):
    if phase == "init":
        lbuf[...] = jnp.zeros(lbuf.shape, F32)
        lbuf[SUBLANE - TAIL:SUBLANE, :] = ilc_ref[0]
        h_st[...] = ilh_ref[0]
        proj_s[...] = _l1_in_proj(h_ref[...], nw_ref, win_ref, _perm_matrices(h_ref.shape[0])[0])
        return None
    if phase == "final":
        olc_ref[...] = lbuf[SUBLANE - TAIL:SUBLANE, :]
        olh_ref[...] = h_st[...]
        return None

    q_len = h_ref.shape[0]
    h_in = h_ref[...]
    to_perm, _ = _perm_matrices(q_len)
    hn_next = _move_rows(to_perm, _rms(hnext_ref[...], nw_ref[...]).astype(BF16))
    if front_pad:
        valid = _perm_time(q_len) >= front_pad
    gw = LRU_WIDTH // L1_GROUPS
    tiles = gw // LANE
    gates = [proj_s[:, g * gw:(g + 1) * gw] for g in range(L1_GROUPS)]
    xrs = [proj_s[:, LRU_WIDTH + g * gw:LRU_WIDTH + (g + 1) * gw] for g in range(L1_GROUPS)]

    def group(g):
        cg = slice(g * gw, (g + 1) * gw)
        xc = _conv_perm(lbuf.at[:, cg], xrs[g], cw_ref.at[:, cg], cb_ref.at[:, cg])
        ra, ix = _blockdiag_tiles(xc, wax_ref.at[g * tiles:(g + 1) * tiles])
        yield
        a, u = _rglru_gates(xc, ra, ix, ba_ref.at[:, cg], bx_ref.at[:, cg], lam_ref.at[:, cg])
        if front_pad:
            a = jnp.where(valid, a, 1.0)
            u = jnp.where(valid, u, 0.0)
        yield
        h, h_last = _scan_perm(a, u, h_st[:, cg])
        h_st[:, cg] = h_last
        yield
        emit(g, h * _silu(gates[g]))

    return [group(g) for g in range(L1_GROUPS)], hn_next, h_in


def _run_staggered(gens):
    live = []
    pending = list(gens)
    while pending or live:
        if pending:
            live.append(pending.pop(0))
        for gen in list(live):
            if next(gen, "done") == "done":
                live.remove(gen)


L1_STATE_SHAPES = ((TAIL, LRU_WIDTH), (1, LRU_WIDTH))


def _l1_prompt(h1, weights, init, front_pad):
    bsz, length, _ = h1.shape
    q_len = min(CHUNK, length)
    assert length % q_len == 0
    rows = _rows_per_step(bsz, L1_ROWS)
    assert len(weights) == N_L1_W and len(init) == N_L1_S
    grid = (bsz // rows, length // q_len)
    last = length // q_len - 1
    x_spec = pl.BlockSpec((rows, q_len, D_MODEL), lambda b, c: (b, c, 0))
    next_spec = pl.BlockSpec((rows, q_len, D_MODEL), lambda b, c: (b, jnp.minimum(c + 1, last), 0))
    in_specs = ([x_spec, next_spec] + [_const_spec(w.shape) for w in weights]
                + [_state_spec(s, 0) for s in L1_STATE_SHAPES])
    out_shape = ([jax.ShapeDtypeStruct((bsz, length, D_MODEL), F32)]
                 + [jax.ShapeDtypeStruct((bsz,) + s, F32) for s in L1_STATE_SHAPES])
    out_specs = [x_spec] + [_state_spec(s, rows) for s in L1_STATE_SHAPES]
    row_scratch = ((SUBLANE, LRU_WIDTH), (1, LRU_WIDTH), (q_len, 2 * LRU_WIDTH))
    scratch = [pltpu.VMEM(s, F32) for _ in range(rows) for s in row_scratch]
    return pl.pallas_call(
        functools.partial(_l1_prompt_kernel, front_pad=front_pad, rows=rows),
        grid=grid, in_specs=in_specs, out_specs=out_specs, out_shape=out_shape, scratch_shapes=scratch,
        compiler_params=pltpu.CompilerParams(dimension_semantics=("arbitrary", "arbitrary"),
                                             vmem_limit_bytes=VMEM_LIMIT),
        name="l1_prompt",
    )(h1, h1, *weights, *init)


def _conv_step(buf_ref, x, w_ref, b_ref, newbuf_ref):
    y = b_ref[...] + w_ref[3:4, :] * x
    for tap in range(TAIL):
        y = y + w_ref[tap:tap + 1, :] * buf_ref[tap]
    for tap in range(TAIL - 1):
        newbuf_ref[tap] = buf_ref[tap + 1]
    newbuf_ref[TAIL - 1] = x
    return y


def _l0_sample_pre_kernel(x_ref, nw_ref, win_ref, scw_ref, scb_ref, dtb_ref, alog_ref,
                          mcw_ref, mcb_ref, wqk_ref, wv_ref, wg_ref, bg_ref,
                          sbuf_ref, mbuf_ref, m0_ref, n0_ref,
                          nsb_ref, nmb_ref, zs_ref, xs_ref, bm_ref, cm_ref, xdt_t_ref, dec_t_ref,
                          zm_ref, xc_ref, q_ref, isv_t_ref, fs_t_ref, k_ref, mnew_ref, nnew_ref, den_ref):
    x = x_ref[...]
    hn = _rms(x, nw_ref[...])
    proj = _bdot(hn, _wload(win_ref[...]))
    zs_ref[...] = proj[:, OFF_ZS:OFF_ZS + SSD_WIDTH]
    zm_ref[...] = proj[:, OFF_ZM:OFF_ZM + ML_WIDTH]
    xbc = proj[:, OFF_XBC:OFF_XBC + SSD_CONV_CH]
    dt_raw = proj[:, OFF_DT:OFF_DT + LANE]
    xm = proj[:, OFF_XM:OFF_XM + ML_WIDTH]
    expand = _expand_matrix()

    xbc = _silu(_conv_step(sbuf_ref, xbc, scw_ref, scb_ref, nsb_ref))
    xs = xbc[:, :SSD_WIDTH]
    xs_ref[...] = xs
    bm_ref[...] = xbc[:, SSD_WIDTH:SSD_WIDTH + SSD_GROUPS * SSD_STATE]
    cm_ref[...] = xbc[:, SSD_WIDTH + SSD_GROUPS * SSD_STATE:]
    dt = _softplus(dt_raw + dtb_ref[...])
    log_a = -dt * jnp.exp(alog_ref[...])
    xdt_t_ref[...] = xs * _expand_heads(dt, expand)
    dec_t_ref[...] = jnp.exp(log_a)

    xc = _silu(_conv_step(mbuf_ref, xm, mcw_ref, mcb_ref, nmb_ref))
    xc_ref[...] = xc
    q, k, v, ig, logf = _mlstm_qkv_gates(xm, xc, wqk_ref, wv_ref, wg_ref, bg_ref)
    m0 = m0_ref[...]
    m_new = jnp.maximum(logf + m0, ig)
    fs = jnp.exp(logf + m0 - m_new)
    is_ = jnp.exp(ig - m_new)
    mnew_ref[...] = m_new
    r = lax.broadcasted_iota(jnp.int32, (LANE, ML_WIDTH), 0)
    cidx = lax.broadcasted_iota(jnp.int32, (LANE, ML_WIDTH), 1)
    expand_m = jnp.where(lax.shift_right_logical(cidx, 8) == r, 1.0, 0.0).astype(BF16)
    fs_e = _expand_heads(fs, expand_m)
    is_e = _expand_heads(is_, expand_m)
    n_new = fs_e * n0_ref[...] + is_e * k
    nnew_ref[...] = n_new
    q_ref[...] = q
    k_ref[...] = k
    isv_t_ref[...] = is_e * v
    fs_t_ref[...] = fs
    nq = n_new * q
    floor = jnp.exp(-m_new)
    for hd in range(ML_HEADS):
        den = jnp.sum(nq[:, hd * ML_HEAD_DIM:(hd + 1) * ML_HEAD_DIM], axis=-1, keepdims=True)
        den_ref[:, hd:hd + 1] = jnp.maximum(jnp.abs(den), floor[:, hd:hd + 1])


BT_S = 8
BT_C = 8


def _ssd_state_kernel(dec_ref, s_ref, xdt_ref, bm_ref, cm_ref, snew_ref, y_ref):
    base = pl.program_id(0) * BT_S
    for i in range(BT_S):
        x_col = xdt_ref[0, :, i:i + 1].reshape(SSD_HEADS, SSD_HEAD_DIM, 1)
        ys = []
        for g in range(SSD_GROUPS):
            hs = slice(g * SSD_HPG, (g + 1) * SSD_HPG)
            b_row = bm_ref[i:i + 1, g * SSD_STATE:(g + 1) * SSD_STATE].reshape(1, 1, SSD_STATE)
            c_row = cm_ref[i:i + 1, g * SSD_STATE:(g + 1) * SSD_STATE].reshape(1, 1, SSD_STATE)
            decay = jnp.stack([jnp.full((1, 1), dec_ref[base + i, hd], F32)
                               for hd in range(hs.start, hs.stop)], axis=0)
            s_new = decay * s_ref[i, hs] + x_col[hs] * b_row
            snew_ref[i, hs] = s_new
            ys.append(jnp.sum(s_new * c_row, axis=-1, keepdims=True))
        y_ref[0, :, i:i + 1] = jnp.concatenate(ys, axis=0).reshape(SSD_WIDTH, 1)


def _mlstm_state_kernel(fs_ref, c_ref, isv_ref, k_ref, q_ref, cnew_ref, num_ref):
    base = pl.program_id(0) * BT_C
    for i in range(BT_C):
        for hd in range(ML_HEADS):
            sl = slice(hd * ML_HEAD_DIM, (hd + 1) * ML_HEAD_DIM)
            v_col = isv_ref[0, sl, i:i + 1]
            c_new = fs_ref[base + i, hd] * c_ref[i, hd] + v_col * k_ref[0, i:i + 1, sl]
            cnew_ref[i, hd] = c_new
            num_ref[0, sl, i:i + 1] = jnp.sum(c_new * q_ref[0, i:i + 1, sl], axis=-1, keepdims=True)


def _sample_post_kernel(x_ref, ys_t_ref, num_t_ref, den_ref, zs_ref, xs_ref, zm_ref, xc_ref,
                        dsk_ref, snw_ref, msk_ref, mnw_ref, wout_ref,
                        nw1_ref, fnw_ref, win1_ref, wout1_ref, cw_ref, cb_ref,
                        wax_ref, ba_ref, bx_ref, lam_ref, lbuf_ref, h0_ref,
                        y_ref, nlb_ref, hnew_ref):
    xs = xs_ref[...]
    y_s = ys_t_ref[...] + dsk_ref[...] * xs
    y_s = _group_rmsnorm(y_s * _silu(zs_ref[...]), snw_ref[...])
    num = num_t_ref[...]
    den = den_ref[...]
    h_m = jnp.concatenate(
        [num[:, hd * ML_HEAD_DIM:(hd + 1) * ML_HEAD_DIM] / den[:, hd:hd + 1] for hd in range(ML_HEADS)], axis=-1)
    h_m = _head_layernorm(h_m) * mnw_ref[...]
    y_m = (h_m + msk_ref[...] * xc_ref[...]) * _silu(zm_ref[...])
    h1 = x_ref[...] + _bdot(jnp.concatenate([y_s, y_m], axis=-1), _wload(wout_ref[...]))

    hn = _rms(h1, nw1_ref[...])
    proj = _bdot(hn, _wload(win1_ref[...]))
    gate = proj[:, :LRU_WIDTH]
    xr = proj[:, LRU_WIDTH:]
    xc = _conv_step(lbuf_ref, xr, cw_ref, cb_ref, nlb_ref)
    ra, ix = _blockdiag_tiles(xc, wax_ref)
    a, u = _rglru_gates(xc, ra, ix, ba_ref, bx_ref, lam_ref)
    h = a * h0_ref[...] + u
    hnew_ref[...] = h
    h2 = h1 + _bdot(h * _silu(gate), _wload(wout1_ref[...]))
    y_ref[...] = _rms(h2, fnw_ref[...])


def _full_call(kernel_fn, out_shapes, args, name):
    return pl.pallas_call(
        kernel_fn,
        out_shape=[jax.ShapeDtypeStruct(s, F32) for s in out_shapes],
        compiler_params=pltpu.CompilerParams(vmem_limit_bytes=VMEM_LIMIT),
        name=name,
    )(*args)


def _to_cols(a, bt):
    rows, ch = a.shape
    return a.reshape(rows // bt, bt, ch).transpose(0, 2, 1)


def _from_cols(a):
    tiles, ch, bt = a.shape
    return a.transpose(0, 2, 1).reshape(tiles * bt, ch)


def _row(v, width=None):
    v = v.reshape(1, -1).astype(F32)
    if width is not None and v.shape[1] < width:
        v = jnp.pad(v, ((0, 0), (0, width - v.shape[1])))
    return v


PACK_STEPS = 8


def _pack_all(weights):
    flats = [w.reshape(-1, w.shape[-1]) for w in weights]
    for f in flats:
        assert f.shape[0] % (2 * SUBLANE * PACK_STEPS) == 0
    packed = pl.pallas_call(
        _pack_kernel,
        grid=(PACK_STEPS,),
        in_specs=[pl.BlockSpec((f.shape[0] // PACK_STEPS, f.shape[1]), lambda i: (i, 0)) for f in flats],
        out_specs=[pl.BlockSpec((f.shape[0] // PACK_STEPS // 2, f.shape[1]), lambda i: (i, 0)) for f in flats],
        out_shape=[jax.ShapeDtypeStruct((f.shape[0] // 2, f.shape[1]), jnp.uint32) for f in flats],
        compiler_params=pltpu.CompilerParams(vmem_limit_bytes=VMEM_LIMIT),
        name="pack_weights",
    )(*flats)
    return [p.reshape(w.shape[:-2] + (w.shape[-2] // 2, w.shape[-1])) for p, w in zip(packed, weights)]


def _pack_kernel(*refs):
    n = len(refs) // 2
    for w_ref, o_ref in zip(refs[:n], refs[n:]):
        o_ref[...] = pltpu.bitcast(w_ref[...].astype(BF16), jnp.uint32)


def _dense_block_tiles(w):
    nb, bi, bo = w.shape
    per = LANE // bi
    rows = w.reshape(nb // per, per * bi, bo)
    col = jnp.arange(per * bo)
    spread = (col[None, :] % bo == jnp.arange(bo)[:, None]).astype(w.dtype)
    rep = jnp.einsum('tro,oc->trc', rows, spread)
    same_block = (jnp.arange(per * bi)[:, None] // bi) == (col[None, :] // bo)
    return jnp.where(same_block, rep, 0.0)


def kernel(x_prompt, x_sample, state_ssd_conv, state_ssd, state_mlstm_conv, state_mlstm_C, state_mlstm_n,
           state_mlstm_m, state_lru_conv, state_lru_h, meta_tokens, norm_w, final_norm_w, w_in_mix, w_out_mix,
           ssd_conv_w, ssd_conv_b, ssd_dt_bias, ssd_a_log, ssd_d, ssd_norm_w, ml_conv_w, ml_conv_b, ml_wq, ml_wk,
           ml_wv, ml_w_gate, ml_b_gate, ml_skip, ml_norm_w, lru_w_in, lru_w_out, lru_conv_w, lru_conv_b, lru_wa,
           lru_ba, lru_wx, lru_bx, lru_lambda):
    bsz = x_prompt.shape[0]
    dec = x_sample.shape[0]

    w_in = w_in_mix[0]
    o1 = SSD_WIDTH
    o2 = o1 + SSD_CONV_CH
    o3 = o2 + SSD_HEADS
    win = jnp.concatenate([w_in[:, :o2], jnp.pad(w_in[:, o2:o3], ((0, 0), (0, LANE - SSD_HEADS))), w_in[:, o3:]],
                          axis=1)
    wout = w_out_mix[0]
    nw0 = _row(norm_w[0])
    nw1 = _row(norm_w[1])
    fnw = _row(final_norm_w)
    scw = ssd_conv_w[0]
    scb = _row(ssd_conv_b[0])
    dtb = _row(ssd_dt_bias[0], LANE)
    alog = _row(ssd_a_log[0], LANE)
    dsk = _row(jnp.repeat(ssd_d[0], SSD_HEAD_DIM))
    snw = _row(ssd_norm_w[0])
    mcw = ml_conv_w[0]
    mcb = _row(ml_conv_b[0])
    wqk = jnp.concatenate([_dense_block_tiles(ml_wq[0]), _dense_block_tiles(ml_wk[0])], axis=2)
    wv = _dense_block_tiles(ml_wv[0])
    wg_raw = ml_w_gate[0]
    wg = jnp.concatenate([jnp.pad(wg_raw[:, :ML_HEADS], ((0, 0), (0, LANE - ML_HEADS))),
                          jnp.pad(wg_raw[:, ML_HEADS:], ((0, 0), (0, LANE - ML_HEADS)))], axis=1)
    bg = jnp.concatenate([_row(ml_b_gate[0, :ML_HEADS], LANE), _row(ml_b_gate[0, ML_HEADS:], LANE)], axis=1)
    msk = _row(ml_skip[0])
    mnw = _row(ml_norm_w[0])
    win1 = lru_w_in[0]
    wout1 = lru_w_out[0]
    lcw = lru_conv_w[0]
    lcb = _row(lru_conv_b[0])
    wax = jnp.concatenate([lru_wa[0], lru_wx[0]], axis=2)
    r_idx = lax.broadcasted_iota(jnp.int32, (LANE, SSD_WIDTH), 0)
    c_idx = lax.broadcasted_iota(jnp.int32, (LANE, SSD_WIDTH), 1)
    expand = (c_idx // SSD_HEAD_DIM == r_idx).astype(F32)
    ba = _row(lru_ba[0])
    bx = _row(lru_bx[0])
    lam = _row(lru_lambda[0])

    win, wout, wqk, wv, wg, expand, win1, wout1, wax = _pack_all(
        [win, wout, wqk, wv, wg, expand, win1, wout1, wax])
    l0_w = (nw0, win, wout, scw, scb, dtb, alog, dsk, snw, mcw, mcb, wqk, wv, wg, bg, msk, mnw, expand)
    l1_w = (nw1, fnw, win1, wout1, lcw, lcb, wax, ba, bx, lam)

    zero0 = tuple(jnp.zeros((1,) + s, F32) for s in L0_STATE_SHAPES)
    zero1 = tuple(jnp.zeros((1,) + s, F32) for s in L1_STATE_SHAPES)
    meta = jnp.pad(meta_tokens.astype(F32), ((CHUNK - N_META, 0), (0, 0)))[None]
    meta_out = _l0_prompt(meta, l0_w, zero0, CHUNK - N_META)
    meta1_out = _l1_prompt(meta_out[0], l1_w, zero1, CHUNK - N_META)
    l0_out = _l0_prompt(x_prompt, l0_w, tuple(meta_out[1:]), 0)
    h1_p, p_sc, p_s, p_mc, p_c, p_n, p_m = l0_out
    y_prompt, p_lc, p_lh = _l1_prompt(h1_p, l1_w, tuple(meta1_out[1:]), 0)

    p_m = p_m[:, 0, :ML_HEADS]
    p_lh = p_lh[:, 0]

    xs2 = x_sample[:, 0]
    sbuf = jnp.moveaxis(state_ssd_conv[0], 1, 0)
    mbuf = jnp.moveaxis(state_mlstm_conv[0], 1, 0)
    lbuf = jnp.moveaxis(state_lru_conv[0], 1, 0)
    m0 = jnp.pad(state_mlstm_m[0], ((0, 0), (0, LANE - ML_HEADS)))
    n0 = state_mlstm_n[0].reshape(dec, ML_WIDTH)
    pre_shapes = ((TAIL, dec, SSD_CONV_CH), (TAIL, dec, ML_WIDTH), (dec, SSD_WIDTH), (dec, SSD_WIDTH),
                  (dec, SSD_GROUPS * SSD_STATE), (dec, SSD_GROUPS * SSD_STATE), (dec, SSD_WIDTH), (dec, LANE),
                  (dec, ML_WIDTH), (dec, ML_WIDTH), (dec, ML_WIDTH), (dec, ML_WIDTH), (dec, LANE),
                  (dec, ML_WIDTH), (dec, LANE), (dec, ML_WIDTH), (dec, ML_HEADS))
    (nsb, nmb, zs, xs_c, bm, cm, xdt_t, dec_t, zm, xc_m, q, isv_t, fs_t, k, m_new, n_new, den) = _full_call(
        _l0_sample_pre_kernel, pre_shapes,
        (xs2, nw0, win, scw, scb, dtb, alog, mcw, mcb, wqk, wv, wg, bg, sbuf, mbuf, m0, n0), "l0_sample_pre")

    xdt_c = _to_cols(xdt_t, BT_S)
    s_new, ys_c = pl.pallas_call(
        _ssd_state_kernel,
        grid=(dec // BT_S,),
        in_specs=[pl.BlockSpec(memory_space=pltpu.SMEM),
                  pl.BlockSpec((BT_S, SSD_HEADS, SSD_HEAD_DIM, SSD_STATE), lambda i: (i, 0, 0, 0)),
                  pl.BlockSpec((1, SSD_WIDTH, BT_S), lambda i: (i, 0, 0)),
                  pl.BlockSpec((BT_S, SSD_GROUPS * SSD_STATE), lambda i: (i, 0)),
                  pl.BlockSpec((BT_S, SSD_GROUPS * SSD_STATE), lambda i: (i, 0))],
        out_specs=[pl.BlockSpec((BT_S, SSD_HEADS, SSD_HEAD_DIM, SSD_STATE), lambda i: (i, 0, 0, 0)),
                   pl.BlockSpec((1, SSD_WIDTH, BT_S), lambda i: (i, 0, 0))],
        out_shape=[jax.ShapeDtypeStruct((dec, SSD_HEADS, SSD_HEAD_DIM, SSD_STATE), F32),
                   jax.ShapeDtypeStruct((dec // BT_S, SSD_WIDTH, BT_S), F32)],
        compiler_params=pltpu.CompilerParams(dimension_semantics=("arbitrary",), vmem_limit_bytes=VMEM_LIMIT),
        name="ssd_state",
    )(dec_t[:, :SSD_HEADS], state_ssd[0], xdt_c, bm, cm)

    isv_c = _to_cols(isv_t, BT_C)
    c_new, num_c = pl.pallas_call(
        _mlstm_state_kernel,
        grid=(dec // BT_C,),
        in_specs=[pl.BlockSpec(memory_space=pltpu.SMEM),
                  pl.BlockSpec((BT_C, ML_HEADS, ML_HEAD_DIM, ML_HEAD_DIM), lambda i: (i, 0, 0, 0)),
                  pl.BlockSpec((1, ML_WIDTH, BT_C), lambda i: (i, 0, 0)),
                  pl.BlockSpec((1, BT_C, ML_WIDTH), lambda i: (i, 0, 0)),
                  pl.BlockSpec((1, BT_C, ML_WIDTH), lambda i: (i, 0, 0))],
        out_specs=[pl.BlockSpec((BT_C, ML_HEADS, ML_HEAD_DIM, ML_HEAD_DIM), lambda i: (i, 0, 0, 0)),
                   pl.BlockSpec((1, ML_WIDTH, BT_C), lambda i: (i, 0, 0))],
        out_shape=[jax.ShapeDtypeStruct((dec, ML_HEADS, ML_HEAD_DIM, ML_HEAD_DIM), F32),
                   jax.ShapeDtypeStruct((dec // BT_C, ML_WIDTH, BT_C), F32)],
        compiler_params=pltpu.CompilerParams(dimension_semantics=("arbitrary",), vmem_limit_bytes=VMEM_LIMIT),
        name="mlstm_state",
    )(fs_t[:, :ML_HEADS], state_mlstm_C[0], isv_c, k.reshape(dec // BT_C, BT_C, ML_WIDTH),
      q.reshape(dec // BT_C, BT_C, ML_WIDTH))

    post_shapes = ((dec, D_MODEL), (TAIL, dec, LRU_WIDTH), (dec, LRU_WIDTH))
    y_s2, nlb, h_new = _full_call(
        _sample_post_kernel, post_shapes,
        (xs2, _from_cols(ys_c), _from_cols(num_c), den, zs, xs_c, zm, xc_m, dsk, snw, msk, mnw, wout,
         nw1, fnw, win1, wout1, lcw, lcb, wax, ba, bx, lam, lbuf, state_lru_h[0]), "sample_post")

    s_sc = jnp.moveaxis(nsb, 0, 1)[None]
    s_mc = jnp.moveaxis(nmb, 0, 1)[None]
    s_lc = jnp.moveaxis(nlb, 0, 1)[None]
    return (y_prompt, y_s2[:, None, :],
            p_sc[None], p_s[None], p_mc[None], p_c[None], p_n[None], p_m[None], p_lc[None], p_lh[None],
            s_sc, s_new[None], s_mc, c_new[None], n_new.reshape(dec, ML_HEADS, ML_HEAD_DIM)[None],
            m_new[:, :ML_HEADS][None], s_lc, h_new[None])
```

```python
import functools

import jax
import jax.numpy as jnp
from jax import lax
from jax.experimental import pallas as pl
from jax.experimental.pallas import tpu as pltpu

F32 = jnp.float32
BF16 = jnp.bfloat16

D_MODEL = 1024
N_META = 16
CONV_W = 4
EPS = 1e-6
NEG = -1e30
SSD_WIDTH = 1024
SSD_HEAD_DIM = 64
SSD_HEADS = 16
SSD_GROUPS = 2
SSD_HPG = 8
SSD_STATE = 128
SSD_CONV_CH = 1536
ML_WIDTH = 1024
ML_HEADS = 4
ML_HEAD_DIM = 256
ML_QKV_BLOCK = 4
LRU_WIDTH = 2048
LRU_BLOCKS = 16
LRU_BLOCK = 128
LRU_C = 8.0

LANE = 128
SUBLANE = 8
CHUNK = 128
L0_ROWS = 2
L1_ROWS = 4
L1_GROUPS = 4
TAIL = CONV_W - 1

OFF_ZS = 0
OFF_XBC = OFF_ZS + SSD_WIDTH
OFF_DT = OFF_XBC + SSD_CONV_CH
OFF_ZM = OFF_DT + LANE
OFF_XM = OFF_ZM + ML_WIDTH
IN_MIX_PAD = OFF_XM + ML_WIDTH

VMEM_LIMIT = 56 * 1024 * 1024


def _sigmoid(x):
    return 1.0 / (1.0 + jnp.exp(-x))


def _silu(x):
    return x * _sigmoid(x)


def _softplus(x):
    return jnp.maximum(x, 0.0) + jnp.log1p(jnp.exp(-jnp.abs(x)))


def _rms(x, w):
    return x * lax.rsqrt(jnp.mean(x * x, axis=-1, keepdims=True) + EPS) * w


def _bdot(a, b):
    return jnp.dot(a.astype(BF16), b.astype(BF16), preferred_element_type=F32)


def _bdot_nt(a, b):
    return lax.dot_general(a.astype(BF16), b.astype(BF16), (((1,), (1,)), ((), ())), preferred_element_type=F32)


def _wload(w):
    return pltpu.bitcast(w, BF16)


def _split3(x):
    hi = x.astype(BF16)
    r = x - hi.astype(F32)
    mid = r.astype(BF16)
    lo = (r - mid.astype(F32)).astype(BF16)
    return hi, mid, lo


def _cumsum_rows(x, tril):
    hi, mid, lo = _split3(x)
    d = functools.partial(jnp.dot, preferred_element_type=F32)
    return d(tril, hi) + d(tril, mid) + d(tril, lo)


def _expand_heads(x, expand):
    hi, mid, _ = _split3(x)
    d = functools.partial(jnp.dot, preferred_element_type=F32)
    return d(hi, expand) + d(mid, expand)


def _expand_matrix():
    r = lax.broadcasted_iota(jnp.int32, (LANE, SSD_WIDTH), 0)
    c = lax.broadcasted_iota(jnp.int32, (LANE, SSD_WIDTH), 1)
    return jnp.where(lax.shift_right_logical(c, 6) == r, 1.0, 0.0).astype(BF16)


def _blockdiag_tiles(x, w_ref):
    k = w_ref.shape[0]
    m = w_ref.shape[2] // LANE
    prods = [_bdot(x[:, t * LANE:(t + 1) * LANE], _wload(w_ref[t])) for t in range(k)]
    return [jnp.concatenate([p[:, j * LANE:(j + 1) * LANE] for p in prods], axis=-1) for j in range(m)]


def _group_rmsnorm(y, w):
    half = SSD_WIDTH // SSD_GROUPS
    parts = []
    for g in range(SSD_GROUPS):
        yg = y[:, g * half:(g + 1) * half]
        parts.append(yg * lax.rsqrt(jnp.mean(yg * yg, axis=-1, keepdims=True) + EPS))
    return jnp.concatenate(parts, axis=-1) * w


def _head_layernorm(h):
    parts = []
    for k in range(ML_HEADS):
        hk = h[:, k * ML_HEAD_DIM:(k + 1) * ML_HEAD_DIM]
        mu = jnp.mean(hk, axis=-1, keepdims=True)
        d = hk - mu
        var = jnp.mean(d * d, axis=-1, keepdims=True)
        parts.append(d * lax.rsqrt(var + EPS))
    return jnp.concatenate(parts, axis=-1)


def _mlstm_qkv_gates(xm, xc, wqk_ref, wv_ref, wg_ref, bg_ref):
    q, k = _blockdiag_tiles(xc, wqk_ref)
    v, = _blockdiag_tiles(xm, wv_ref)
    gates = _bdot(jnp.concatenate([q, k, v], axis=-1), _wload(wg_ref[...])) + bg_ref[...]
    ig = gates[:, :LANE]
    logf = -_softplus(-gates[:, LANE:])
    return q, k * (ML_HEAD_DIM ** -0.5), v, ig, logf


N_L0_W = 18
N_L0_S = 6


def _l0_prompt_kernel(x_ref, xnext_ref, *refs, front_pad, rows):
    w_refs = refs[:N_L0_W]
    init_refs = refs[N_L0_W:N_L0_W + N_L0_S]
    out_refs = refs[N_L0_W + N_L0_S:N_L0_W + 2 * N_L0_S + 1]
    scratch = refs[N_L0_W + 2 * N_L0_S + 1:]
    per_row = len(scratch) // rows
    c = pl.program_id(1)
    win_ref, wout_ref = w_refs[1], w_refs[2]
    q_len = x_ref.shape[1]
    pieces = {}

    def each_row(phase):
        return [_l0_prompt_row(x_ref.at[r], xnext_ref.at[r], *w_refs, *init_refs, *(o.at[r] for o in out_refs),
                               *scratch[r * per_row:(r + 1) * per_row], front_pad=front_pad, phase=phase,
                               emit=lambda k0, y, r=r: pieces.setdefault(k0, {}).__setitem__(r, y))
                for r in range(rows)]

    @pl.when(c == 0)
    def _():
        each_row("init")

    bodies = each_row("body")
    proj_refs = [scratch[r * per_row + per_row - 1] for r in range(rows)]
    _, to_time = _perm_matrices(q_len)
    partials = []

    def in_proj():
        lhs = jnp.concatenate([hn_next for _, hn_next, _ in bodies], axis=0)
        for lo, hi in L0_PROJ_PIECES:
            res = _bdot(lhs, _wload(win_ref[:, lo:hi]))
            for r in range(rows):
                proj_refs[r][:, lo:hi] = res[r * q_len:(r + 1) * q_len]
            yield

    def out_proj():
        pending = [(0, SSD_WIDTH)] + [(SSD_WIDTH + hd * ML_HEAD_DIM, ML_HEAD_DIM) for hd in range(ML_HEADS)]
        while pending:
            for k0, width in list(pending):
                if len(pieces.get(k0, ())) == rows:
                    y_t = jnp.concatenate([_move_rows(to_time, pieces[k0][r].astype(BF16)) for r in range(rows)],
                                          axis=0)
                    partials.append(_bdot(y_t, _wload(wout_ref[k0 // 2:(k0 + width) // 2, :])))
                    pending.remove((k0, width))
            yield

    _run_round_robin([gen for gens, _, _ in bodies for gen in gens] + [in_proj(), out_proj()])
    total = partials[0]
    for part in partials[1:]:
        total = total + part
    for r, (_, _, x) in enumerate(bodies):
        h1 = x + total[r * q_len:(r + 1) * q_len]
        if front_pad:
            h1 = jnp.where(lax.broadcasted_iota(jnp.int32, (q_len, 1), 0) >= front_pad, h1, 0.0)
        out_refs[0][r] = h1

    @pl.when(c == pl.num_programs(1) - 1)
    def _():
        each_row("final")


L0_PROJ_PIECES = ((OFF_XBC, OFF_ZM), (OFF_XM, IN_MIX_PAD), (OFF_ZM, OFF_XM), (OFF_ZS, OFF_XBC))


def _run_round_robin(gens):
    live = list(gens)
    while live:
        for gen in list(live):
            if next(gen, "done") == "done":
                live.remove(gen)


def _l0_prompt_row(x_ref, xnext_ref, nw_ref, win_ref, wout_ref,
                   scw_ref, scb_ref, dtb_ref, alog_ref, dsk_ref, snw_ref,
                   mcw_ref, mcb_ref, wqk_ref, wv_ref, wg_ref, bg_ref, msk_ref, mnw_ref, expand_ref,
                   isc_ref, iss_ref, imc_ref, ict_ref, inn_ref, imm_ref,
                   h1_ref, osc_ref, oss_ref, omc_ref, oct_ref, onn_ref, omm_ref,
                   sbuf, mbuf, s_st, ct_st, n_st, m_st, proj_s, *, front_pad, phase, emit):
    q_len = x_ref.shape[0]

    if phase == "init":
        sbuf[...] = jnp.zeros(sbuf.shape, F32)
        mbuf[...] = jnp.zeros(mbuf.shape, F32)
        sbuf[SUBLANE - TAIL:SUBLANE, :] = isc_ref[0]
        mbuf[SUBLANE - TAIL:SUBLANE, :] = imc_ref[0]
        for g in range(SSD_GROUPS):
            heads = iss_ref[0, g * SSD_HPG:(g + 1) * SSD_HPG]
            s_st[g] = heads.reshape(SSD_HPG * SSD_HEAD_DIM, SSD_STATE).T
        for hd in range(ML_HEADS):
            ct_st[hd] = ict_ref[0, hd].T
        n_st[...] = inn_ref[0]
        m_st[...] = imm_ref[0]
        hn0 = _move_rows(_perm_matrices(q_len)[0], _rms(x_ref[...], nw_ref[...]).astype(BF16))
        proj_s[...] = _bdot(hn0, _wload(win_ref[...]))
        return None
    if phase == "final":
        osc_ref[...] = sbuf[SUBLANE - TAIL:SUBLANE, :]
        omc_ref[...] = mbuf[SUBLANE - TAIL:SUBLANE, :]
        for g in range(SSD_GROUPS):
            oss_ref[g * SSD_HPG:(g + 1) * SSD_HPG] = s_st[g].T.reshape(SSD_HPG, SSD_HEAD_DIM, SSD_STATE)
        for hd in range(ML_HEADS):
            oct_ref[hd] = ct_st[hd].T
        onn_ref[...] = n_st[...]
        omm_ref[...] = m_st[...]
        return None

    x = x_ref[...]
    to_perm, to_time = _perm_matrices(q_len)
    t_col = _perm_time(q_len)
    t_row = _perm_time(q_len, row=True)
    causal = t_col >= t_row
    tril = jnp.where(causal, 1.0, 0.0).astype(BF16)
    valid = (t_col >= front_pad) if front_pad else None
    hn_next = _move_rows(to_perm, _rms(xnext_ref[...], nw_ref[...]).astype(BF16))
    xbc_raw = proj_s[:, OFF_XBC:OFF_XBC + SSD_CONV_CH]
    dt_raw = proj_s[:, OFF_DT:OFF_DT + LANE]
    xm = proj_s[:, OFF_XM:OFF_XM + ML_WIDTH]
    z_s = proj_s[:, OFF_ZS:OFF_ZS + SSD_WIDTH]
    z_m = proj_s[:, OFF_ZM:OFF_ZM + ML_WIDTH]

    def ssd():
        xbc = _silu(_conv_perm(sbuf, xbc_raw, scw_ref, scb_ref))
        yield
        xs = xbc[:, :SSD_WIDTH]
        bm = xbc[:, SSD_WIDTH:SSD_WIDTH + SSD_GROUPS * SSD_STATE]
        cm = xbc[:, SSD_WIDTH + SSD_GROUPS * SSD_STATE:]
        dt = _softplus(dt_raw + dtb_ref[...])
        if front_pad:
            dt = jnp.where(valid, dt, 0.0)
        log_a = -dt * jnp.exp(alog_ref[...])
        a_cs = _cumsum_rows(log_a, tril)
        yield
        a_last = a_cs[q_len - 1:q_len, :]
        expand = _wload(expand_ref[...])
        w_state = _expand_heads(dt * jnp.exp(a_last - a_cs), expand)
        e_acs = _expand_heads(jnp.exp(a_cs), expand)
        a_cs_t = a_cs.T
        dt_t = dt.T
        yield
        pair_lo = lax.broadcasted_iota(jnp.int32, (q_len, LANE), 1) < SSD_HEAD_DIM
        half = SSD_WIDTH // SSD_GROUPS
        y_groups = []
        for g in range(SSD_GROUPS):
            bg = bm[:, g * SSD_STATE:(g + 1) * SSD_STATE]
            cg = cm[:, g * SSD_STATE:(g + 1) * SSD_STATE]
            bg_t = bg.T
            xg = xs[:, g * half:(g + 1) * half]
            eg = e_acs[:, g * half:(g + 1) * half]
            s_old = s_st[g]
            cb = _bdot(cg, bg_t)
            y_off = _bdot(cg, s_old) * eg
            s_st[g] = eg[q_len - 1:q_len, :] * s_old + _bdot(bg_t, xg * w_state[:, g * half:(g + 1) * half])
            yield
            y_pairs = []
            for pr in range(SSD_HPG // 2):
                ms = []
                for e in (2 * pr, 2 * pr + 1):
                    hd = g * SSD_HPG + e
                    seg = jnp.exp(jnp.where(causal, a_cs[:, hd:hd + 1] - a_cs_t[hd:hd + 1, :], -jnp.inf))
                    ms.append(cb * seg * dt_t[hd:hd + 1, :])
                xp = xg[:, pr * LANE:(pr + 1) * LANE]
                rhs = jnp.concatenate([jnp.where(pair_lo, xp, 0.0), jnp.where(pair_lo, 0.0, xp)], axis=0)
                y_pairs.append(_bdot(jnp.concatenate(ms, axis=-1), rhs))
                yield
            y_groups.append(jnp.concatenate(y_pairs, axis=-1) + y_off)
        y_s = jnp.concatenate(y_groups, axis=-1) + dsk_ref[...] * xs
        emit(0, _group_rmsnorm(y_s * _silu(z_s), snw_ref[...]))

    def mlstm():
        xc = _silu(_conv_perm(mbuf, xm, mcw_ref, mcb_ref))
        yield
        q, k = _blockdiag_tiles(xc, wqk_ref)
        v, = _blockdiag_tiles(xm, wv_ref)
        yield
        gates = _bdot(jnp.concatenate([q, k, v], axis=-1), _wload(wg_ref[...])) + bg_ref[...]
        k = k * (ML_HEAD_DIM ** -0.5)
        yield
        ig = gates[:, :LANE]
        logf = -_softplus(-gates[:, LANE:])
        if front_pad:
            ig = jnp.where(valid, ig, NEG)
            logf = jnp.where(valid, logf, 0.0)
        bcum = _cumsum_rows(logf, tril)
        yield
        ftot = bcum[q_len - 1:q_len, :]
        m_prev = m_st[...]
        w_end = ftot - bcum + ig
        m_new = jnp.maximum(ftot + m_prev, jnp.max(w_end, axis=0, keepdims=True))
        sc = jnp.exp(ftot + m_prev - m_new)
        wexp = jnp.exp(w_end - m_new)
        inter = bcum + m_prev
        bcum_t = bcum.T
        ig_t = ig.T
        m_st[...] = m_new
        yield
        for hd in range(ML_HEADS):
            sl = slice(hd * ML_HEAD_DIM, (hd + 1) * ML_HEAD_DIM)
            q_h, k_h, v_h = q[:, sl], k[:, sl], v[:, sl]
            k_t = k_h.T
            dmat = jnp.where(causal, bcum[:, hd:hd + 1] - bcum_t[hd:hd + 1, :] + ig_t[hd:hd + 1, :], -jnp.inf)
            inter_h = inter[:, hd:hd + 1]
            m_t = jnp.maximum(inter_h, jnp.max(dmat, axis=-1, keepdims=True))
            dexp = jnp.exp(dmat - m_t)
            inter_sc = jnp.exp(inter_h - m_t)
            s = _bdot(q_h, k_t) * dexp
            yield
            ct_old = ct_st[hd]
            n_old = n_st[hd:hd + 1, :]
            num = _bdot(s, v_h) + inter_sc * _bdot(q_h, ct_old)
            den = jnp.sum(s, axis=-1, keepdims=True) + inter_sc * jnp.sum(q_h * n_old, axis=-1, keepdims=True)
            h_h = num / jnp.maximum(jnp.abs(den), jnp.exp(-m_t))
            w_col = wexp[:, hd:hd + 1]
            sc_h = sc[:, hd:hd + 1]
            ct_st[hd] = sc_h * ct_old + _bdot(k_t, v_h * w_col)
            n_st[hd:hd + 1, :] = sc_h * n_old + jnp.sum(k_h * w_col, axis=0, keepdims=True)
            yield
            mu = jnp.mean(h_h, axis=-1, keepdims=True)
            dev = h_h - mu
            var = jnp.mean(dev * dev, axis=-1, keepdims=True)
            h_h = dev * lax.rsqrt(var + EPS) * mnw_ref[:, sl]
            emit(SSD_WIDTH + hd * ML_HEAD_DIM, (h_h + msk_ref[:, sl] * xc[:, sl]) * _silu(z_m[:, sl]))
            yield

    return [ssd(), mlstm()], hn_next, x


def _const_spec(shape):
    nd = len(shape)
    return pl.BlockSpec(shape, lambda b, c: (0,) * nd)


def _state_spec(shape, rows):
    nd = len(shape)
    if rows:
        return pl.BlockSpec((rows,) + shape, lambda b, c: (b,) + (0,) * nd)
    return pl.BlockSpec((1,) + shape, lambda b, c: (0,) * (nd + 1))


def _rows_per_step(bsz, want):
    return want if bsz % want == 0 else 1


L0_STATE_SHAPES = ((TAIL, SSD_CONV_CH), (SSD_HEADS, SSD_HEAD_DIM, SSD_STATE), (TAIL, ML_WIDTH),
                   (ML_HEADS, ML_HEAD_DIM, ML_HEAD_DIM), (ML_HEADS, ML_HEAD_DIM), (1, LANE))
L0_CARRY_SHAPES = ((SUBLANE, SSD_CONV_CH), (SUBLANE, ML_WIDTH), (SSD_GROUPS, SSD_STATE, SSD_WIDTH // SSD_GROUPS),
                   (ML_HEADS, ML_HEAD_DIM, ML_HEAD_DIM), (ML_HEADS, ML_HEAD_DIM), (1, LANE))


def _l0_prompt(x, weights, init, front_pad):
    bsz, length, _ = x.shape
    q_len = min(CHUNK, length)
    assert length % q_len == 0
    rows = _rows_per_step(bsz, L0_ROWS)
    assert len(weights) == N_L0_W and len(init) == N_L0_S
    grid = (bsz // rows, length // q_len)
    last = length // q_len - 1
    x_spec = pl.BlockSpec((rows, q_len, D_MODEL), lambda b, c: (b, c, 0))
    next_spec = pl.BlockSpec((rows, q_len, D_MODEL), lambda b, c: (b, jnp.minimum(c + 1, last), 0))
    in_specs = ([x_spec, next_spec] + [_const_spec(w.shape) for w in weights]
                + [_state_spec(s, 0) for s in L0_STATE_SHAPES])
    out_shape = ([jax.ShapeDtypeStruct((bsz, length, D_MODEL), F32)]
                 + [jax.ShapeDtypeStruct((bsz,) + s, F32) for s in L0_STATE_SHAPES])
    out_specs = [x_spec] + [_state_spec(s, rows) for s in L0_STATE_SHAPES]
    row_scratch = L0_CARRY_SHAPES + ((q_len, IN_MIX_PAD),)
    scratch = [pltpu.VMEM(s, F32) for _ in range(rows) for s in row_scratch]
    return pl.pallas_call(
        functools.partial(_l0_prompt_kernel, front_pad=front_pad, rows=rows),
        grid=grid, in_specs=in_specs, out_specs=out_specs, out_shape=out_shape, scratch_shapes=scratch,
        compiler_params=pltpu.CompilerParams(dimension_semantics=("arbitrary", "arbitrary"),
                                             vmem_limit_bytes=VMEM_LIMIT),
        name="l0_prompt",
    )(x, x, *weights, *init)


def _rglru_gates(xc, ra, ix, ba_ref, bx_ref, lam_ref):
    r = _sigmoid(ra + ba_ref[...])
    i = _sigmoid(ix + bx_ref[...])
    log_a = r * (-LRU_C * _softplus(-lam_ref[...]))
    a = jnp.exp(log_a)
    u = jnp.sqrt(1.0 - a * a) * (i * xc)
    return a, u


def _perm_time(n, row=False):
    p = lax.broadcasted_iota(jnp.int32, (1, n) if row else (n, 1), 1 if row else 0)
    return (n // SUBLANE) * (p & (SUBLANE - 1)) + lax.shift_right_logical(p, 3)


def _perm_matrices(n):
    nb = n // SUBLANE
    r = lax.broadcasted_iota(jnp.int32, (n, n), 0)
    c = lax.broadcasted_iota(jnp.int32, (n, n), 1)
    to_perm = jnp.where(c == nb * (r & (SUBLANE - 1)) + lax.shift_right_logical(r, 3), 1.0, 0.0)
    to_time = jnp.where(r == nb * (c & (SUBLANE - 1)) + lax.shift_right_logical(c, 3), 1.0, 0.0)
    return to_perm.astype(BF16), to_time.astype(BF16)


def _move_rows(sel, x_bf16):
    return jnp.dot(sel, x_bf16, preferred_element_type=F32).astype(BF16)


def _conv_perm(tail_ref, x, w_ref, b_ref):
    n, ch = x.shape
    nb = n // SUBLANE
    x3 = x.reshape(nb, SUBLANE, ch)
    tail8 = tail_ref[...]
    sub = lax.broadcasted_iota(jnp.int32, (SUBLANE, ch), 0)
    y = b_ref[...].reshape(1, 1, ch) + w_ref[TAIL:TAIL + 1, :].reshape(1, 1, ch) * x3
    wrapped = [jnp.where(sub >= 1, pltpu.roll(x3[nb - d], 1, 0), tail8[SUBLANE - d:SUBLANE - d + 1, :])
               for d in range(1, CONV_W)]
    for back in range(1, CONV_W):
        head = jnp.stack([wrapped[back - j - 1] for j in range(back)], axis=0)
        shifted = jnp.concatenate([head, x3[:nb - back]], axis=0)
        y = y + w_ref[TAIL - back:TAIL - back + 1, :].reshape(1, 1, ch) * shifted
    for d in range(1, CONV_W):
        tail_ref[SUBLANE - d:SUBLANE - d + 1, :] = x3[nb - d][SUBLANE - 1:SUBLANE, :]
    return y.reshape(n, ch)


def _scan_perm(a, u, h_prev):
    n, ch = a.shape
    nb = n // SUBLANE
    a3 = a.reshape(nb, SUBLANE, ch)
    u3 = u.reshape(nb, SUBLANE, ch)
    local = [u3[0]]
    decay = [a3[0]]
    for j in range(1, nb):
        local.append(a3[j] * local[-1] + u3[j])
        decay.append(a3[j] * decay[-1])
    seg_u, seg_a = local[-1], decay[-1]
    sub = lax.broadcasted_iota(jnp.int32, (SUBLANE, ch), 0)
    shift = 1
    while shift < SUBLANE:
        keep = sub >= shift
        seg_u = seg_u + seg_a * jnp.where(keep, pltpu.roll(seg_u, shift, 0), 0.0)
        seg_a = seg_a * jnp.where(keep, pltpu.roll(seg_a, shift, 0), 1.0)
        shift *= 2
    seg_end = seg_a * h_prev + seg_u
    carry = jnp.where(sub >= 1, pltpu.roll(seg_end, 1, 0), h_prev)
    h3 = jnp.stack([local[j] + decay[j] * carry for j in range(nb)], axis=0)
    return h3.reshape(n, ch), seg_end[SUBLANE - 1:SUBLANE, :]


N_L1_W = 10
N_L1_S = 2


def _l1_prompt_kernel(h_ref, hnext_ref, *refs, front_pad, rows):
    w_refs = refs[:N_L1_W]
    init_refs = refs[N_L1_W:N_L1_W + N_L1_S]
    out_refs = refs[N_L1_W + N_L1_S:N_L1_W + 2 * N_L1_S + 1]
    scratch = refs[N_L1_W + 2 * N_L1_S + 1:]
    per_row = len(scratch) // rows
    c = pl.program_id(1)
    fnw_ref, win_ref, wout_ref = w_refs[1], w_refs[2], w_refs[3]
    q_len = h_ref.shape[1]
    gw = LRU_WIDTH // L1_GROUPS
    pieces = {}

    def each_row(phase):
        return [_l1_prompt_row(h_ref.at[r], hnext_ref.at[r], *w_refs, *init_refs, *(o.at[r] for o in out_refs),
                               *scratch[r * per_row:(r + 1) * per_row], front_pad=front_pad, phase=phase,
                               emit=lambda g, y, r=r: pieces.setdefault(g, {}).__setitem__(r, y))
                for r in range(rows)]

    @pl.when(c == 0)
    def _():
        each_row("init")

    bodies = each_row("body")
    proj_refs = [scratch[r * per_row + per_row - 1] for r in range(rows)]
    _, to_time = _perm_matrices(q_len)
    partials = []

    def in_proj():
        lhs = jnp.concatenate([hn_next for _, hn_next, _ in bodies], axis=0)
        for g in range(L1_GROUPS):
            for lo in (g * gw, LRU_WIDTH + g * gw):
                res = _bdot(lhs, _wload(win_ref[:, lo:lo + gw]))
                for r in range(rows):
                    proj_refs[r][:, lo:lo + gw] = res[r * q_len:(r + 1) * q_len]
            yield

    def out_proj():
        pending = list(range(L1_GROUPS))
        while pending:
            for g in list(pending):
                if len(pieces.get(g, ())) == rows:
                    y_t = jnp.concatenate([_move_rows(to_time, pieces[g][r].astype(BF16)) for r in range(rows)],
                                          axis=0)
                    partials.append(_bdot(y_t, _wload(wout_ref[g * gw // 2:(g + 1) * gw // 2, :])))
                    pending.remove(g)
            yield

    chains = [bodies[r][0][g] for g in range(L1_GROUPS) for r in range(rows)]
    _run_staggered([in_proj()] + chains + [out_proj()])
    total = partials[0]
    for part in partials[1:]:
        total = total + part
    for r, (_, _, h_in) in enumerate(bodies):
        out_refs[0][r] = _rms(h_in + total[r * q_len:(r + 1) * q_len], fnw_ref[...])

    @pl.when(c == pl.num_programs(1) - 1)
    def _():
        each_row("final")


def _l1_in_proj(h_val, nw_ref, win_ref, to_perm):
    hn = _move_rows(to_perm, _rms(h_val, nw_ref[...]).astype(BF16))
    return _bdot(hn, _wload(win_ref[...]))


def _l1_prompt_row(h_ref, hnext_ref, nw_ref, fnw_ref, win_ref, wout_ref, cw_ref, cb_ref,
                   wax_ref, ba_ref, bx_ref, lam_ref, ilc_ref, ilh_ref,
                   y_ref, olc_ref, olh_ref, lbuf, h_st, proj_s, *, front_pad, phase, emit):
    if phase == "init":
        lbuf[...] = jnp.zeros(lbuf.shape, F32)
        lbuf[SUBLANE - TAIL:SUBLANE, :] = ilc_ref[0]
        h_st[...] = ilh_ref[0]
        proj_s[...] = _l1_in_proj(h_ref[...], nw_ref, win_ref, _perm_matrices(h_ref.shape[0])[0])
        return None
    if phase == "final":
        olc_ref[...] = lbuf[SUBLANE - TAIL:SUBLANE, :]
        olh_ref[...] = h_st[...]
        return None

    q_len = h_ref.shape[0]
    h_in = h_ref[...]
    to_perm, _ = _perm_matrices(q_len)
    hn_next = _move_rows(to_perm, _rms(hnext_ref[...], nw_ref[...]).astype(BF16))
    if front_pad:
        valid = _perm_time(q_len) >= front_pad
    gw = LRU_WIDTH // L1_GROUPS
    tiles = gw // LANE
    gates = [proj_s[:, g * gw:(g + 1) * gw] for g in range(L1_GROUPS)]
    xrs = [proj_s[:, LRU_WIDTH + g * gw:LRU_WIDTH + (g + 1) * gw] for g in range(L1_GROUPS)]

    def group(g):
        cg = slice(g * gw, (g + 1) * gw)
        xc = _conv_perm(lbuf.at[:, cg], xrs[g], cw_ref.at[:, cg], cb_ref.at[:, cg])
        ra, ix = _blockdiag_tiles(xc, wax_ref.at[g * tiles:(g + 1) * tiles])
        yield
        a, u = _rglru_gates(xc, ra, ix, ba_ref.at[:, cg], bx_ref.at[:, cg], lam_ref.at[:, cg])
        if front_pad:
            a = jnp.where(valid, a, 1.0)
            u = jnp.where(valid, u, 0.0)
        yield
        h, h_last = _scan_perm(a, u, h_st[:, cg])
        h_st[:, cg] = h_last
        yield
        emit(g, h * _silu(gates[g]))

    return [group(g) for g in range(L1_GROUPS)], hn_next, h_in


def _run_staggered(gens):
    live = []
    pending = list(gens)
    while pending or live:
        if pending:
            live.append(pending.pop(0))
        for gen in list(live):
            if next(gen, "done") == "done":
                live.remove(gen)


L1_STATE_SHAPES = ((TAIL, LRU_WIDTH), (1, LRU_WIDTH))


def _l1_prompt(h1, weights, init, front_pad):
    bsz, length, _ = h1.shape
    q_len = min(CHUNK, length)
    assert length % q_len == 0
    rows = _rows_per_step(bsz, L1_ROWS)
    assert len(weights) == N_L1_W and len(init) == N_L1_S
    grid = (bsz // rows, length // q_len)
    last = length // q_len - 1
    x_spec = pl.BlockSpec((rows, q_len, D_MODEL), lambda b, c: (b, c, 0))
    next_spec = pl.BlockSpec((rows, q_len, D_MODEL), lambda b, c: (b, jnp.minimum(c + 1, last), 0))
    in_specs = ([x_spec, next_spec] + [_const_spec(w.shape) for w in weights]
                + [_state_spec(s, 0) for s in L1_STATE_SHAPES])
    out_shape = ([jax.ShapeDtypeStruct((bsz, length, D_MODEL), F32)]
                 + [jax.ShapeDtypeStruct((bsz,) + s, F32) for s in L1_STATE_SHAPES])
    out_specs = [x_spec] + [_state_spec(s, rows) for s in L1_STATE_SHAPES]
    row_scratch = ((SUBLANE, LRU_WIDTH), (1, LRU_WIDTH), (q_len, 2 * LRU_WIDTH))
    scratch = [pltpu.VMEM(s, F32) for _ in range(rows) for s in row_scratch]
    return pl.pallas_call(
        functools.partial(_l1_prompt_kernel, front_pad=front_pad, rows=rows),
        grid=grid, in_specs=in_specs, out_specs=out_specs, out_shape=out_shape, scratch_shapes=scratch,
        compiler_params=pltpu.CompilerParams(dimension_semantics=("arbitrary", "arbitrary"),
                                             vmem_limit_bytes=VMEM_LIMIT),
        name="l1_prompt",
    )(h1, h1, *weights, *init)


def _conv_step(buf_ref, x, w_ref, b_ref, newbuf_ref):
    y = b_ref[...] + w_ref[3:4, :] * x
    for tap in range(TAIL):
        y = y + w_ref[tap:tap + 1, :] * buf_ref[tap]
    for tap in range(TAIL - 1):
        newbuf_ref[tap] = buf_ref[tap + 1]
    newbuf_ref[TAIL - 1] = x
    return y


def _l0_sample_pre_kernel(x_ref, nw_ref, win_ref, scw_ref, scb_ref, dtb_ref, alog_ref,
                          mcw_ref, mcb_ref, wqk_ref, wv_ref, wg_ref, bg_ref,
                          sbuf_ref, mbuf_ref, m0_ref, n0_ref,
                          nsb_ref, nmb_ref, zs_ref, xs_ref, bm_ref, cm_ref, xdt_t_ref, dec_t_ref,
                          zm_ref, xc_ref, q_ref, isv_t_ref, fs_t_ref, k_ref, mnew_ref, nnew_ref, den_ref):
    x = x_ref[...]
    hn = _rms(x, nw_ref[...])
    proj = _bdot(hn, _wload(win_ref[...]))
    zs_ref[...] = proj[:, OFF_ZS:OFF_ZS + SSD_WIDTH]
    zm_ref[...] = proj[:, OFF_ZM:OFF_ZM + ML_WIDTH]
    xbc = proj[:, OFF_XBC:OFF_XBC + SSD_CONV_CH]
    dt_raw = proj[:, OFF_DT:OFF_DT + LANE]
    xm = proj[:, OFF_XM:OFF_XM + ML_WIDTH]
    expand = _expand_matrix()

    xbc = _silu(_conv_step(sbuf_ref, xbc, scw_ref, scb_ref, nsb_ref))
    xs = xbc[:, :SSD_WIDTH]
    xs_ref[...] = xs
    bm_ref[...] = xbc[:, SSD_WIDTH:SSD_WIDTH + SSD_GROUPS * SSD_STATE]
    cm_ref[...] = xbc[:, SSD_WIDTH + SSD_GROUPS * SSD_STATE:]
    dt = _softplus(dt_raw + dtb_ref[...])
    log_a = -dt * jnp.exp(alog_ref[...])
    xdt_t_ref[...] = xs * _expand_heads(dt, expand)
    dec_t_ref[...] = jnp.exp(log_a)

    xc = _silu(_conv_step(mbuf_ref, xm, mcw_ref, mcb_ref, nmb_ref))
    xc_ref[...] = xc
    q, k, v, ig, logf = _mlstm_qkv_gates(xm, xc, wqk_ref, wv_ref, wg_ref, bg_ref)
    m0 = m0_ref[...]
    m_new = jnp.maximum(logf + m0, ig)
    fs = jnp.exp(logf + m0 - m_new)
    is_ = jnp.exp(ig - m_new)
    mnew_ref[...] = m_new
    r = lax.broadcasted_iota(jnp.int32, (LANE, ML_WIDTH), 0)
    cidx = lax.broadcasted_iota(jnp.int32, (LANE, ML_WIDTH), 1)
    expand_m = jnp.where(lax.shift_right_logical(cidx, 8) == r, 1.0, 0.0).astype(BF16)
    fs_e = _expand_heads(fs, expand_m)
    is_e = _expand_heads(is_, expand_m)
    n_new = fs_e * n0_ref[...] + is_e * k
    nnew_ref[...] = n_new
    q_ref[...] = q
    k_ref[...] = k
    isv_t_ref[...] = is_e * v
    fs_t_ref[...] = fs
    nq = n_new * q
    floor = jnp.exp(-m_new)
    for hd in range(ML_HEADS):
        den = jnp.sum(nq[:, hd * ML_HEAD_DIM:(hd + 1) * ML_HEAD_DIM], axis=-1, keepdims=True)
        den_ref[:, hd:hd + 1] = jnp.maximum(jnp.abs(den), floor[:, hd:hd + 1])


BT_S = 8
BT_C = 8


def _ssd_state_kernel(dec_ref, s_ref, xdt_ref, bm_ref, cm_ref, snew_ref, y_ref):
    base = pl.program_id(0) * BT_S
    half = SSD_WIDTH // SSD_GROUPS
    lane = lax.broadcasted_iota(jnp.int32, (half, 2 * BT_S), 1)
    accs = [jnp.zeros((half, 2 * BT_S), F32) for _ in range(SSD_GROUPS)]
    for i in range(BT_S):
        x_col = xdt_ref[0, :, i:i + 1].reshape(SSD_HEADS, SSD_HEAD_DIM, 1)
        for g in range(SSD_GROUPS):
            hs = slice(g * SSD_HPG, (g + 1) * SSD_HPG)
            gs = slice(g * SSD_STATE, (g + 1) * SSD_STATE)
            b_row = bm_ref[i:i + 1, gs].reshape(1, 1, SSD_STATE)
            decay = jnp.stack([jnp.full((1, 1), dec_ref[base + i, hd], F32)
                               for hd in range(hs.start, hs.stop)], axis=0)
            s_new = decay * s_ref[i, hs] + x_col[hs] * b_row
            snew_ref[i, hs] = s_new
            c_rows = jnp.concatenate([cm_ref[:, gs], jnp.zeros((BT_S, SSD_STATE), F32)], axis=0)
            prod = _bdot_nt(s_new.reshape(half, SSD_STATE), c_rows)
            accs[g] = jnp.where(lane == i, prod, accs[g])
    for g in range(SSD_GROUPS):
        y_ref[0, g * half:(g + 1) * half, :] = accs[g][:, :BT_S]


def _mlstm_state_kernel(fs_ref, c_ref, isv_ref, k_ref, q_ref, cnew_ref, num_ref):
    base = pl.program_id(0) * BT_C
    lane = lax.broadcasted_iota(jnp.int32, (ML_HEAD_DIM, 2 * BT_C), 1)
    for hd in range(ML_HEADS):
        sl = slice(hd * ML_HEAD_DIM, (hd + 1) * ML_HEAD_DIM)
        q_rows = jnp.concatenate([q_ref[0, :, sl], jnp.zeros((BT_C, ML_HEAD_DIM), F32)], axis=0)
        acc = jnp.zeros((ML_HEAD_DIM, 2 * BT_C), F32)
        for i in range(BT_C):
            v_col = isv_ref[0, sl, i:i + 1]
            c_new = fs_ref[base + i, hd] * c_ref[i, hd] + v_col * k_ref[0, i:i + 1, sl]
            cnew_ref[i, hd] = c_new
            acc = jnp.where(lane == i, _bdot_nt(c_new, q_rows), acc)
        num_ref[0, sl, :] = acc[:, :BT_C]


def _sample_post_kernel(x_ref, ys_t_ref, num_t_ref, den_ref, zs_ref, xs_ref, zm_ref, xc_ref,
                        dsk_ref, snw_ref, msk_ref, mnw_ref, wout_ref,
                        nw1_ref, fnw_ref, win1_ref, wout1_ref, cw_ref, cb_ref,
                        wax_ref, ba_ref, bx_ref, lam_ref, lbuf_ref, h0_ref,
                        y_ref, nlb_ref, hnew_ref):
    xs = xs_ref[...]
    y_s = ys_t_ref[...] + dsk_ref[...] * xs
    y_s = _group_rmsnorm(y_s * _silu(zs_ref[...]), snw_ref[...])
    num = num_t_ref[...]
    den = den_ref[...]
    h_m = jnp.concatenate(
        [num[:, hd * ML_HEAD_DIM:(hd + 1) * ML_HEAD_DIM] / den[:, hd:hd + 1] for hd in range(ML_HEADS)], axis=-1)
    h_m = _head_layernorm(h_m) * mnw_ref[...]
    y_m = (h_m + msk_ref[...] * xc_ref[...]) * _silu(zm_ref[...])
    h1 = x_ref[...] + _bdot(jnp.concatenate([y_s, y_m], axis=-1), _wload(wout_ref[...]))

    hn = _rms(h1, nw1_ref[...])
    proj = _bdot(hn, _wload(win1_ref[...]))
    gate = proj[:, :LRU_WIDTH]
    xr = proj[:, LRU_WIDTH:]
    xc = _conv_step(lbuf_ref, xr, cw_ref, cb_ref, nlb_ref)
    ra, ix = _blockdiag_tiles(xc, wax_ref)
    a, u = _rglru_gates(xc, ra, ix, ba_ref, bx_ref, lam_ref)
    h = a * h0_ref[...] + u
    hnew_ref[...] = h
    h2 = h1 + _bdot(h * _silu(gate), _wload(wout1_ref[...]))
    y_ref[...] = _rms(h2, fnw_ref[...])


def _full_call(kernel_fn, out_shapes, args, name):
    return pl.pallas_call(
        kernel_fn,
        out_shape=[jax.ShapeDtypeStruct(s, F32) for s in out_shapes],
        compiler_params=pltpu.CompilerParams(vmem_limit_bytes=VMEM_LIMIT),
        name=name,
    )(*args)


def _to_cols(a, bt):
    rows, ch = a.shape
    return a.reshape(rows // bt, bt, ch).transpose(0, 2, 1)


def _from_cols(a):
    tiles, ch, bt = a.shape
    return a.transpose(0, 2, 1).reshape(tiles * bt, ch)


def _row(v, width=None):
    v = v.reshape(1, -1).astype(F32)
    if width is not None and v.shape[1] < width:
        v = jnp.pad(v, ((0, 0), (0, width - v.shape[1])))
    return v


PACK_STEPS = 8


def _pack_all(w_in, weights):
    flats = [w_in] + [w.reshape(-1, w.shape[-1]) for w in weights]
    widths = [IN_MIX_PAD] + [f.shape[1] for f in flats[1:]]
    for f in flats:
        assert f.shape[0] % (2 * SUBLANE * PACK_STEPS) == 0
    packed = pl.pallas_call(
        _pack_kernel,
        grid=(PACK_STEPS,),
        in_specs=[pl.BlockSpec((f.shape[0] // PACK_STEPS, f.shape[1]), lambda i: (i, 0)) for f in flats],
        out_specs=[pl.BlockSpec((f.shape[0] // PACK_STEPS // 2, n), lambda i: (i, 0)) for f, n in zip(flats, widths)],
        out_shape=[jax.ShapeDtypeStruct((f.shape[0] // 2, n), jnp.uint32) for f, n in zip(flats, widths)],
        compiler_params=pltpu.CompilerParams(vmem_limit_bytes=VMEM_LIMIT),
        name="pack_weights",
    )(*flats)
    return [packed[0]] + [p.reshape(w.shape[:-2] + (w.shape[-2] // 2, w.shape[-1]))
                          for p, w in zip(packed[1:], weights)]


def _pack_rows(x):
    return pltpu.bitcast(x.astype(BF16), jnp.uint32)


def _pack_kernel(*refs):
    n = len(refs) // 2
    win_ref, wino_ref = refs[0], refs[n]
    wino_ref[:, :OFF_DT] = _pack_rows(win_ref[:, :OFF_DT])
    dt_tile = win_ref[:, OFF_DT:OFF_DT + LANE]
    lane = lax.broadcasted_iota(jnp.int32, dt_tile.shape, 1)
    wino_ref[:, OFF_DT:OFF_ZM] = _pack_rows(jnp.where(lane < SSD_HEADS, dt_tile, 0.0))
    wino_ref[:, OFF_ZM:] = _pack_rows(win_ref[:, OFF_DT + SSD_HEADS:])
    for w_ref, o_ref in zip(refs[1:n], refs[n + 1:]):
        o_ref[...] = _pack_rows(w_ref[...])


def _dense_block_tiles(w):
    nb, bi, bo = w.shape
    per = LANE // bi
    rows = w.reshape(nb // per, per * bi, bo)
    col = jnp.arange(per * bo)
    spread = (col[None, :] % bo == jnp.arange(bo)[:, None]).astype(w.dtype)
    rep = jnp.einsum('tro,oc->trc', rows, spread)
    same_block = (jnp.arange(per * bi)[:, None] // bi) == (col[None, :] // bo)
    return jnp.where(same_block, rep, 0.0)


def kernel(x_prompt, x_sample, state_ssd_conv, state_ssd, state_mlstm_conv, state_mlstm_C, state_mlstm_n,
           state_mlstm_m, state_lru_conv, state_lru_h, meta_tokens, norm_w, final_norm_w, w_in_mix, w_out_mix,
           ssd_conv_w, ssd_conv_b, ssd_dt_bias, ssd_a_log, ssd_d, ssd_norm_w, ml_conv_w, ml_conv_b, ml_wq, ml_wk,
           ml_wv, ml_w_gate, ml_b_gate, ml_skip, ml_norm_w, lru_w_in, lru_w_out, lru_conv_w, lru_conv_b, lru_wa,
           lru_ba, lru_wx, lru_bx, lru_lambda):
    bsz = x_prompt.shape[0]
    dec = x_sample.shape[0]

    wout = w_out_mix[0]
    nw0 = _row(norm_w[0])
    nw1 = _row(norm_w[1])
    fnw = _row(final_norm_w)
    scw = ssd_conv_w[0]
    scb = _row(ssd_conv_b[0])
    dtb = _row(ssd_dt_bias[0], LANE)
    alog = _row(ssd_a_log[0], LANE)
    dsk = _row(jnp.repeat(ssd_d[0], SSD_HEAD_DIM))
    snw = _row(ssd_norm_w[0])
    mcw = ml_conv_w[0]
    mcb = _row(ml_conv_b[0])
    wqk = jnp.concatenate([_dense_block_tiles(ml_wq[0]), _dense_block_tiles(ml_wk[0])], axis=2)
    wv = _dense_block_tiles(ml_wv[0])
    wg_raw = ml_w_gate[0]
    wg = jnp.concatenate([jnp.pad(wg_raw[:, :ML_HEADS], ((0, 0), (0, LANE - ML_HEADS))),
                          jnp.pad(wg_raw[:, ML_HEADS:], ((0, 0), (0, LANE - ML_HEADS)))], axis=1)
    bg = jnp.concatenate([_row(ml_b_gate[0, :ML_HEADS], LANE), _row(ml_b_gate[0, ML_HEADS:], LANE)], axis=1)
    msk = _row(ml_skip[0])
    mnw = _row(ml_norm_w[0])
    win1 = lru_w_in[0]
    wout1 = lru_w_out[0]
    lcw = lru_conv_w[0]
    lcb = _row(lru_conv_b[0])
    wax = jnp.concatenate([lru_wa[0], lru_wx[0]], axis=2)
    r_idx = lax.broadcasted_iota(jnp.int32, (LANE, SSD_WIDTH), 0)
    c_idx = lax.broadcasted_iota(jnp.int32, (LANE, SSD_WIDTH), 1)
    expand = (c_idx // SSD_HEAD_DIM == r_idx).astype(F32)
    ba = _row(lru_ba[0])
    bx = _row(lru_bx[0])
    lam = _row(lru_lambda[0])

    win, wout, wqk, wv, wg, expand, win1, wout1, wax = _pack_all(
        w_in_mix[0], [wout, wqk, wv, wg, expand, win1, wout1, wax])
    l0_w = (nw0, win, wout, scw, scb, dtb, alog, dsk, snw, mcw, mcb, wqk, wv, wg, bg, msk, mnw, expand)
    l1_w = (nw1, fnw, win1, wout1, lcw, lcb, wax, ba, bx, lam)

    zero0 = tuple(jnp.zeros((1,) + s, F32) for s in L0_STATE_SHAPES)
    zero1 = tuple(jnp.zeros((1,) + s, F32) for s in L1_STATE_SHAPES)
    meta = jnp.pad(meta_tokens.astype(F32), ((CHUNK - N_META, 0), (0, 0)))[None]
    meta_out = _l0_prompt(meta, l0_w, zero0, CHUNK - N_META)
    meta1_out = _l1_prompt(meta_out[0], l1_w, zero1, CHUNK - N_META)
    l0_out = _l0_prompt(x_prompt, l0_w, tuple(meta_out[1:]), 0)
    h1_p, p_sc, p_s, p_mc, p_c, p_n, p_m = l0_out
    y_prompt, p_lc, p_lh = _l1_prompt(h1_p, l1_w, tuple(meta1_out[1:]), 0)

    p_m = p_m[:, 0, :ML_HEADS]
    p_lh = p_lh[:, 0]

    xs2 = x_sample[:, 0]
    sbuf = jnp.moveaxis(state_ssd_conv[0], 1, 0)
    mbuf = jnp.moveaxis(state_mlstm_conv[0], 1, 0)
    lbuf = jnp.moveaxis(state_lru_conv[0], 1, 0)
    m0 = jnp.pad(state_mlstm_m[0], ((0, 0), (0, LANE - ML_HEADS)))
    n0 = state_mlstm_n[0].reshape(dec, ML_WIDTH)
    pre_shapes = ((TAIL, dec, SSD_CONV_CH), (TAIL, dec, ML_WIDTH), (dec, SSD_WIDTH), (dec, SSD_WIDTH),
                  (dec, SSD_GROUPS * SSD_STATE), (dec, SSD_GROUPS * SSD_STATE), (dec, SSD_WIDTH), (dec, LANE),
                  (dec, ML_WIDTH), (dec, ML_WIDTH), (dec, ML_WIDTH), (dec, ML_WIDTH), (dec, LANE),
                  (dec, ML_WIDTH), (dec, LANE), (dec, ML_WIDTH), (dec, ML_HEADS))
    (nsb, nmb, zs, xs_c, bm, cm, xdt_t, dec_t, zm, xc_m, q, isv_t, fs_t, k, m_new, n_new, den) = _full_call(
        _l0_sample_pre_kernel, pre_shapes,
        (xs2, nw0, win, scw, scb, dtb, alog, mcw, mcb, wqk, wv, wg, bg, sbuf, mbuf, m0, n0), "l0_sample_pre")

    xdt_c = _to_cols(xdt_t, BT_S)
    s_new, ys_c = pl.pallas_call(
        _ssd_state_kernel,
        grid=(dec // BT_S,),
        in_specs=[pl.BlockSpec(memory_space=pltpu.SMEM),
                  pl.BlockSpec((BT_S, SSD_HEADS, SSD_HEAD_DIM, SSD_STATE), lambda i: (i, 0, 0, 0)),
                  pl.BlockSpec((1, SSD_WIDTH, BT_S), lambda i: (i, 0, 0)),
                  pl.BlockSpec((BT_S, SSD_GROUPS * SSD_STATE), lambda i: (i, 0)),
                  pl.BlockSpec((BT_S, SSD_GROUPS * SSD_STATE), lambda i: (i, 0))],
        out_specs=[pl.BlockSpec((BT_S, SSD_HEADS, SSD_HEAD_DIM, SSD_STATE), lambda i: (i, 0, 0, 0)),
                   pl.BlockSpec((1, SSD_WIDTH, BT_S), lambda i: (i, 0, 0))],
        out_shape=[jax.ShapeDtypeStruct((dec, SSD_HEADS, SSD_HEAD_DIM, SSD_STATE), F32),
                   jax.ShapeDtypeStruct((dec // BT_S, SSD_WIDTH, BT_S), F32)],
        compiler_params=pltpu.CompilerParams(dimension_semantics=("arbitrary",), vmem_limit_bytes=VMEM_LIMIT),
        name="ssd_state",
    )(dec_t[:, :SSD_HEADS], state_ssd[0], xdt_c, bm, cm)

    isv_c = _to_cols(isv_t, BT_C)
    c_new, num_c = pl.pallas_call(
        _mlstm_state_kernel,
        grid=(dec // BT_C,),
        in_specs=[pl.BlockSpec(memory_space=pltpu.SMEM),
                  pl.BlockSpec((BT_C, ML_HEADS, ML_HEAD_DIM, ML_HEAD_DIM), lambda i: (i, 0, 0, 0)),
                  pl.BlockSpec((1, ML_WIDTH, BT_C), lambda i: (i, 0, 0)),
                  pl.BlockSpec((1, BT_C, ML_WIDTH), lambda i: (i, 0, 0)),
                  pl.BlockSpec((1, BT_C, ML_WIDTH), lambda i: (i, 0, 0))],
        out_specs=[pl.BlockSpec((BT_C, ML_HEADS, ML_HEAD_DIM, ML_HEAD_DIM), lambda i: (i, 0, 0, 0)),
                   pl.BlockSpec((1, ML_WIDTH, BT_C), lambda i: (i, 0, 0))],
        out_shape=[jax.ShapeDtypeStruct((dec, ML_HEADS, ML_HEAD_DIM, ML_HEAD_DIM), F32),
                   jax.ShapeDtypeStruct((dec // BT_C, ML_WIDTH, BT_C), F32)],
        compiler_params=pltpu.CompilerParams(dimension_semantics=("arbitrary",), vmem_limit_bytes=VMEM_LIMIT),
        name="mlstm_state",
    )(fs_t[:, :ML_HEADS], state_mlstm_C[0], isv_c, k.reshape(dec // BT_C, BT_C, ML_WIDTH),
      q.reshape(dec // BT_C, BT_C, ML_WIDTH))

    post_shapes = ((dec, D_MODEL), (TAIL, dec, LRU_WIDTH), (dec, LRU_WIDTH))
    y_s2, nlb, h_new = _full_call(
        _sample_post_kernel, post_shapes,
        (xs2, _from_cols(ys_c), _from_cols(num_c), den, zs, xs_c, zm, xc_m, dsk, snw, msk, mnw, wout,
         nw1, fnw, win1, wout1, lcw, lcb, wax, ba, bx, lam, lbuf, state_lru_h[0]), "sample_post")

    s_sc = jnp.moveaxis(nsb, 0, 1)[None]
    s_mc = jnp.moveaxis(nmb, 0, 1)[None]
    s_lc = jnp.moveaxis(nlb, 0, 1)[None]
    return (y_prompt, y_s2[:, None, :],
            p_sc[None], p_s[None], p_mc[None], p_c[None], p_n[None], p_m[None], p_lc[None], p_lh[None],
            s_sc, s_new[None], s_mc, c_new[None], n_new.reshape(dec, ML_HEADS, ML_HEAD_DIM)[None],
            m_new[:, :ML_HEADS][None], s_lc, h_new[None])
```

```python
import functools

import jax
import jax.numpy as jnp
from jax import lax
from jax.experimental import pallas as pl
from jax.experimental.pallas import tpu as pltpu

F32 = jnp.float32
BF16 = jnp.bfloat16

D_MODEL = 1024
N_META = 16
CONV_W = 4
EPS = 1e-6
NEG = -1e30
SSD_WIDTH = 1024
SSD_HEAD_DIM = 64
SSD_HEADS = 16
SSD_GROUPS = 2
SSD_HPG = 8
SSD_STATE = 128
SSD_CONV_CH = 1536
ML_WIDTH = 1024
ML_HEADS = 4
ML_HEAD_DIM = 256
ML_QKV_BLOCK = 4
LRU_WIDTH = 2048
LRU_BLOCKS = 16
LRU_BLOCK = 128
LRU_C = 8.0

LANE = 128
SUBLANE = 8
CHUNK = 128
L0_ROWS = 2
L1_ROWS = 2
L1_GROUPS = 4
TAIL = CONV_W - 1

OFF_ZS = 0
OFF_XBC = OFF_ZS + SSD_WIDTH
OFF_DT = OFF_XBC + SSD_CONV_CH
OFF_ZM = OFF_DT + LANE
OFF_XM = OFF_ZM + ML_WIDTH
IN_MIX_PAD = OFF_XM + ML_WIDTH

VMEM_LIMIT = 56 * 1024 * 1024
L0_VMEM_LIMIT = 61 * 1024 * 1024


def _sigmoid(x):
    return 1.0 / (1.0 + jnp.exp(-x))


def _silu(x):
    return x * _sigmoid(x)


def _softplus(x):
    return jnp.maximum(x, 0.0) + jnp.log1p(jnp.exp(-jnp.abs(x)))


def _rms(x, w):
    return x * lax.rsqrt(jnp.mean(x * x, axis=-1, keepdims=True) + EPS) * w


def _bdot(a, b):
    return jnp.dot(a.astype(BF16), b.astype(BF16), preferred_element_type=F32)


def _bdot_nt(a, b):
    return lax.dot_general(a.astype(BF16), b.astype(BF16), (((1,), (1,)), ((), ())), preferred_element_type=F32)


def _wload(w):
    return pltpu.bitcast(w, BF16)


def _split3(x):
    hi = x.astype(BF16)
    r = x - hi.astype(F32)
    mid = r.astype(BF16)
    lo = (r - mid.astype(F32)).astype(BF16)
    return hi, mid, lo


def _cumsum_rows(x, tril):
    hi, mid, lo = _split3(x)
    d = functools.partial(jnp.dot, preferred_element_type=F32)
    return d(tril, hi) + d(tril, mid) + d(tril, lo)


def _expand_heads(x, expand):
    hi, mid, _ = _split3(x)
    d = functools.partial(jnp.dot, preferred_element_type=F32)
    return d(hi, expand) + d(mid, expand)


def _expand_matrix():
    r = lax.broadcasted_iota(jnp.int32, (LANE, SSD_WIDTH), 0)
    c = lax.broadcasted_iota(jnp.int32, (LANE, SSD_WIDTH), 1)
    return jnp.where(lax.shift_right_logical(c, 6) == r, 1.0, 0.0).astype(BF16)


def _blockdiag_tiles(x, w_ref):
    k = w_ref.shape[0]
    m = w_ref.shape[2] // LANE
    prods = [_bdot(x[:, t * LANE:(t + 1) * LANE], _wload(w_ref[t])) for t in range(k)]
    return [jnp.concatenate([p[:, j * LANE:(j + 1) * LANE] for p in prods], axis=-1) for j in range(m)]


def _group_rmsnorm(y, w):
    half = SSD_WIDTH // SSD_GROUPS
    parts = []
    for g in range(SSD_GROUPS):
        yg = y[:, g * half:(g + 1) * half]
        parts.append(yg * lax.rsqrt(jnp.mean(yg * yg, axis=-1, keepdims=True) + EPS))
    return jnp.concatenate(parts, axis=-1) * w


def _head_layernorm(h):
    parts = []
    for k in range(ML_HEADS):
        hk = h[:, k * ML_HEAD_DIM:(k + 1) * ML_HEAD_DIM]
        mu = jnp.mean(hk, axis=-1, keepdims=True)
        d = hk - mu
        var = jnp.mean(d * d, axis=-1, keepdims=True)
        parts.append(d * lax.rsqrt(var + EPS))
    return jnp.concatenate(parts, axis=-1)


def _mlstm_qkv_gates(xm, xc, wqk_ref, wv_ref, wg_ref, bg_ref):
    q, k = _blockdiag_tiles(xc, wqk_ref)
    v, = _blockdiag_tiles(xm, wv_ref)
    gates = _bdot(jnp.concatenate([q, k, v], axis=-1), _wload(wg_ref[...])) + bg_ref[...]
    ig = gates[:, :LANE]
    logf = -_softplus(-gates[:, LANE:])
    return q, k * (ML_HEAD_DIM ** -0.5), v, ig, logf


N_L0_W = 18
N_L0_S = 6


def _l0_prompt_kernel(x_ref, xnext_ref, *refs, front_pad, rows, stream):
    w_refs = refs[:N_L0_W]
    init_refs = refs[N_L0_W:N_L0_W + N_L0_S]
    pos = N_L0_W + N_L0_S
    stream_in = refs[pos:pos + N_STREAM_IN] if stream else ()
    pos += len(stream_in)
    out_refs = refs[pos:pos + N_L0_S + 1]
    pos += N_L0_S + 1
    stream_out = refs[pos:pos + N_STREAM_OUT] if stream else ()
    pos += len(stream_out)
    scratch = refs[pos:]
    per_row = len(scratch) // rows
    c = pl.program_id(1)
    win_ref, wout_ref = w_refs[1], w_refs[2]
    q_len = x_ref.shape[1]
    pieces = {}

    def each_row(phase):
        return [_l0_prompt_row(x_ref.at[r], xnext_ref.at[r], *w_refs, *init_refs, *(o.at[r] for o in out_refs),
                               *scratch[r * per_row:(r + 1) * per_row], front_pad=front_pad, phase=phase,
                               emit=lambda k0, y, r=r: pieces.setdefault(k0, {}).__setitem__(r, y))
                for r in range(rows)]

    @pl.when(c == 0)
    def _():
        each_row("init")

    bodies = each_row("body")
    proj_refs = [scratch[r * per_row + per_row - 1] for r in range(rows)]
    _, to_time = _perm_matrices(q_len)
    partials = []

    def in_proj():
        lhs = jnp.concatenate([hn_next for _, hn_next, _ in bodies], axis=0)
        for lo, hi in L0_PROJ_PIECES:
            res = _bdot(lhs, _wload(win_ref[:, lo:hi]))
            for r in range(rows):
                proj_refs[r][:, lo:hi] = res[r * q_len:(r + 1) * q_len]
            yield

    def out_proj():
        pending = [(0, SSD_WIDTH)] + [(SSD_WIDTH + hd * ML_HEAD_DIM, ML_HEAD_DIM) for hd in range(ML_HEADS)]
        while pending:
            for k0, width in list(pending):
                if len(pieces.get(k0, ())) == rows:
                    y_t = jnp.concatenate([_move_rows(to_time, pieces[k0][r].astype(BF16)) for r in range(rows)],
                                          axis=0)
                    partials.append(_bdot(y_t, _wload(wout_ref[k0 // 2:(k0 + width) // 2, :])))
                    pending.remove((k0, width))
            yield

    chains = [gen for gens, _, _ in bodies for gen in gens] + [in_proj(), out_proj()]
    if stream:
        step = pl.program_id(0) * pl.num_programs(1) + c
        chains.append(_ssd_state_update(*stream_in, *stream_out, base=step * stream_in[1].shape[0]))
    _run_round_robin(chains)
    total = partials[0]
    for part in partials[1:]:
        total = total + part
    for r, (_, _, x) in enumerate(bodies):
        h1 = x + total[r * q_len:(r + 1) * q_len]
        if front_pad:
            h1 = jnp.where(lax.broadcasted_iota(jnp.int32, (q_len, 1), 0) >= front_pad, h1, 0.0)
        out_refs[0][r] = h1

    @pl.when(c == pl.num_programs(1) - 1)
    def _():
        each_row("final")


L0_PROJ_PIECES = ((OFF_XBC, OFF_ZM), (OFF_XM, IN_MIX_PAD), (OFF_ZM, OFF_XM), (OFF_ZS, OFF_XBC))


def _run_round_robin(gens):
    live = list(gens)
    while live:
        for gen in list(live):
            if next(gen, "done") == "done":
                live.remove(gen)


def _l0_prompt_row(x_ref, xnext_ref, nw_ref, win_ref, wout_ref,
                   scw_ref, scb_ref, dtb_ref, alog_ref, dsk_ref, snw_ref,
                   mcw_ref, mcb_ref, wqk_ref, wv_ref, wg_ref, bg_ref, msk_ref, mnw_ref, expand_ref,
                   isc_ref, iss_ref, imc_ref, ict_ref, inn_ref, imm_ref,
                   h1_ref, osc_ref, oss_ref, omc_ref, oct_ref, onn_ref, omm_ref,
                   sbuf, mbuf, s_st, ct_st, n_st, m_st, proj_s, *, front_pad, phase, emit):
    q_len = x_ref.shape[0]

    if phase == "init":
        sbuf[...] = jnp.zeros(sbuf.shape, F32)
        mbuf[...] = jnp.zeros(mbuf.shape, F32)
        sbuf[SUBLANE - TAIL:SUBLANE, :] = isc_ref[0]
        mbuf[SUBLANE - TAIL:SUBLANE, :] = imc_ref[0]
        for g in range(SSD_GROUPS):
            heads = iss_ref[0, g * SSD_HPG:(g + 1) * SSD_HPG]
            s_st[g] = heads.reshape(SSD_HPG * SSD_HEAD_DIM, SSD_STATE).T
        for hd in range(ML_HEADS):
            ct_st[hd] = ict_ref[0, hd].T
        n_st[...] = inn_ref[0]
        m_st[...] = imm_ref[0]
        hn0 = _move_rows(_perm_matrices(q_len)[0], _rms(x_ref[...], nw_ref[...]).astype(BF16))
        proj_s[...] = _bdot(hn0, _wload(win_ref[...]))
        return None
    if phase == "final":
        osc_ref[...] = sbuf[SUBLANE - TAIL:SUBLANE, :]
        omc_ref[...] = mbuf[SUBLANE - TAIL:SUBLANE, :]
        for g in range(SSD_GROUPS):
            oss_ref[g * SSD_HPG:(g + 1) * SSD_HPG] = s_st[g].T.reshape(SSD_HPG, SSD_HEAD_DIM, SSD_STATE)
        for hd in range(ML_HEADS):
            oct_ref[hd] = ct_st[hd].T
        onn_ref[...] = n_st[...]
        omm_ref[...] = m_st[...]
        return None

    x = x_ref[...]
    to_perm, to_time = _perm_matrices(q_len)
    t_col = _perm_time(q_len)
    t_row = _perm_time(q_len, row=True)
    causal = t_col >= t_row
    tril = jnp.where(causal, 1.0, 0.0).astype(BF16)
    valid = (t_col >= front_pad) if front_pad else None
    hn_next = _move_rows(to_perm, _rms(xnext_ref[...], nw_ref[...]).astype(BF16))
    xbc_raw = proj_s[:, OFF_XBC:OFF_XBC + SSD_CONV_CH]
    dt_raw = proj_s[:, OFF_DT:OFF_DT + LANE]
    xm = proj_s[:, OFF_XM:OFF_XM + ML_WIDTH]
    z_s = proj_s[:, OFF_ZS:OFF_ZS + SSD_WIDTH]
    z_m = proj_s[:, OFF_ZM:OFF_ZM + ML_WIDTH]

    def ssd():
        xbc = _silu(_conv_perm(sbuf, xbc_raw, scw_ref, scb_ref))
        yield
        xs = xbc[:, :SSD_WIDTH]
        bm = xbc[:, SSD_WIDTH:SSD_WIDTH + SSD_GROUPS * SSD_STATE]
        cm = xbc[:, SSD_WIDTH + SSD_GROUPS * SSD_STATE:]
        dt = _softplus(dt_raw + dtb_ref[...])
        if front_pad:
            dt = jnp.where(valid, dt, 0.0)
        log_a = -dt * jnp.exp(alog_ref[...])
        a_cs = _cumsum_rows(log_a, tril)
        yield
        a_last = a_cs[q_len - 1:q_len, :]
        expand = _wload(expand_ref[...])
        w_state = _expand_heads(dt * jnp.exp(a_last - a_cs), expand)
        e_acs = _expand_heads(jnp.exp(a_cs), expand)
        a_cs_t = a_cs.T
        dt_t = dt.T
        yield
        pair_lo = lax.broadcasted_iota(jnp.int32, (q_len, LANE), 1) < SSD_HEAD_DIM
        half = SSD_WIDTH // SSD_GROUPS
        y_groups = []
        for g in range(SSD_GROUPS):
            bg = bm[:, g * SSD_STATE:(g + 1) * SSD_STATE]
            cg = cm[:, g * SSD_STATE:(g + 1) * SSD_STATE]
            bg_t = bg.T
            xg = xs[:, g * half:(g + 1) * half]
            eg = e_acs[:, g * half:(g + 1) * half]
            s_old = s_st[g]
            cb = _bdot(cg, bg_t)
            y_off = _bdot(cg, s_old) * eg
            s_st[g] = eg[q_len - 1:q_len, :] * s_old + _bdot(bg_t, xg * w_state[:, g * half:(g + 1) * half])
            yield
            y_pairs = []
            for pr in range(SSD_HPG // 2):
                ms = []
                for e in (2 * pr, 2 * pr + 1):
                    hd = g * SSD_HPG + e
                    seg = jnp.exp(jnp.where(causal, a_cs[:, hd:hd + 1] - a_cs_t[hd:hd + 1, :], -jnp.inf))
                    ms.append(cb * seg * dt_t[hd:hd + 1, :])
                xp = xg[:, pr * LANE:(pr + 1) * LANE]
                rhs = jnp.concatenate([jnp.where(pair_lo, xp, 0.0), jnp.where(pair_lo, 0.0, xp)], axis=0)
                y_pairs.append(_bdot(jnp.concatenate(ms, axis=-1), rhs))
                yield
            y_groups.append(jnp.concatenate(y_pairs, axis=-1) + y_off)
        y_s = jnp.concatenate(y_groups, axis=-1) + dsk_ref[...] * xs
        emit(0, _group_rmsnorm(y_s * _silu(z_s), snw_ref[...]))

    def mlstm():
        xc = _silu(_conv_perm(mbuf, xm, mcw_ref, mcb_ref))
        yield
        q, k = _blockdiag_tiles(xc, wqk_ref)
        v, = _blockdiag_tiles(xm, wv_ref)
        yield
        gates = _bdot(jnp.concatenate([q, k, v], axis=-1), _wload(wg_ref[...])) + bg_ref[...]
        k = k * (ML_HEAD_DIM ** -0.5)
        yield
        ig = gates[:, :LANE]
        logf = -_softplus(-gates[:, LANE:])
        if front_pad:
            ig = jnp.where(valid, ig, NEG)
            logf = jnp.where(valid, logf, 0.0)
        bcum = _cumsum_rows(logf, tril)
        yield
        ftot = bcum[q_len - 1:q_len, :]
        m_prev = m_st[...]
        w_end = ftot - bcum + ig
        m_new = jnp.maximum(ftot + m_prev, jnp.max(w_end, axis=0, keepdims=True))
        sc = jnp.exp(ftot + m_prev - m_new)
        wexp = jnp.exp(w_end - m_new)
        inter = bcum + m_prev
        bcum_t = bcum.T
        ig_t = ig.T
        m_st[...] = m_new
        yield
        for hd in range(ML_HEADS):
            sl = slice(hd * ML_HEAD_DIM, (hd + 1) * ML_HEAD_DIM)
            q_h, k_h, v_h = q[:, sl], k[:, sl], v[:, sl]
            k_t = k_h.T
            dmat = jnp.where(causal, bcum[:, hd:hd + 1] - bcum_t[hd:hd + 1, :] + ig_t[hd:hd + 1, :], -jnp.inf)
            inter_h = inter[:, hd:hd + 1]
            m_t = jnp.maximum(inter_h, jnp.max(dmat, axis=-1, keepdims=True))
            dexp = jnp.exp(dmat - m_t)
            inter_sc = jnp.exp(inter_h - m_t)
            s = _bdot(q_h, k_t) * dexp
            yield
            ct_old = ct_st[hd]
            n_old = n_st[hd:hd + 1, :]
            num = _bdot(s, v_h) + inter_sc * _bdot(q_h, ct_old)
            den = jnp.sum(s, axis=-1, keepdims=True) + inter_sc * jnp.sum(q_h * n_old, axis=-1, keepdims=True)
            h_h = num / jnp.maximum(jnp.abs(den), jnp.exp(-m_t))
            w_col = wexp[:, hd:hd + 1]
            sc_h = sc[:, hd:hd + 1]
            ct_st[hd] = sc_h * ct_old + _bdot(k_t, v_h * w_col)
            n_st[hd:hd + 1, :] = sc_h * n_old + jnp.sum(k_h * w_col, axis=0, keepdims=True)
            yield
            mu = jnp.mean(h_h, axis=-1, keepdims=True)
            dev = h_h - mu
            var = jnp.mean(dev * dev, axis=-1, keepdims=True)
            h_h = dev * lax.rsqrt(var + EPS) * mnw_ref[:, sl]
            emit(SSD_WIDTH + hd * ML_HEAD_DIM, (h_h + msk_ref[:, sl] * xc[:, sl]) * _silu(z_m[:, sl]))
            yield

    return [ssd(), mlstm()], hn_next, x


def _const_spec(shape):
    nd = len(shape)
    return pl.BlockSpec(shape, lambda b, c: (0,) * nd)


def _state_spec(shape, rows):
    nd = len(shape)
    if rows:
        return pl.BlockSpec((rows,) + shape, lambda b, c: (b,) + (0,) * nd)
    return pl.BlockSpec((1,) + shape, lambda b, c: (0,) * (nd + 1))


def _rows_per_step(bsz, want):
    return want if bsz % want == 0 else 1


L0_STATE_SHAPES = ((TAIL, SSD_CONV_CH), (SSD_HEADS, SSD_HEAD_DIM, SSD_STATE), (TAIL, ML_WIDTH),
                   (ML_HEADS, ML_HEAD_DIM, ML_HEAD_DIM), (ML_HEADS, ML_HEAD_DIM), (1, LANE))
L0_CARRY_SHAPES = ((SUBLANE, SSD_CONV_CH), (SUBLANE, ML_WIDTH), (SSD_GROUPS, SSD_STATE, SSD_WIDTH // SSD_GROUPS),
                   (ML_HEADS, ML_HEAD_DIM, ML_HEAD_DIM), (ML_HEADS, ML_HEAD_DIM), (1, LANE))


def _stream_specs(arrays, n_steps, nc):
    specs = []
    for a in arrays:
        if a.ndim == 2:
            specs.append(pl.BlockSpec(memory_space=pltpu.SMEM))
            continue
        assert a.shape[0] % n_steps == 0
        block = (a.shape[0] // n_steps,) + a.shape[1:]
        specs.append(pl.BlockSpec(block, lambda b, c, nd=a.ndim: (b * nc + c,) + (0,) * (nd - 1)))
    return specs


def _l0_prompt(x, weights, init, front_pad, stream_in=(), stream_out_shapes=()):
    bsz, length, _ = x.shape
    q_len = min(CHUNK, length)
    assert length % q_len == 0
    rows = _rows_per_step(bsz, L0_ROWS)
    assert len(weights) == N_L0_W and len(init) == N_L0_S
    nc = length // q_len
    grid = (bsz // rows, nc)
    n_steps = grid[0] * nc
    last = nc - 1
    x_spec = pl.BlockSpec((rows, q_len, D_MODEL), lambda b, c: (b, c, 0))
    next_spec = pl.BlockSpec((rows, q_len, D_MODEL), lambda b, c: (b, jnp.minimum(c + 1, last), 0))
    stream_outs = [jax.ShapeDtypeStruct(s, F32) for s in stream_out_shapes]
    in_specs = ([x_spec, next_spec] + [_const_spec(w.shape) for w in weights]
                + [_state_spec(s, 0) for s in L0_STATE_SHAPES] + _stream_specs(stream_in, n_steps, nc))
    out_shape = ([jax.ShapeDtypeStruct((bsz, length, D_MODEL), F32)]
                 + [jax.ShapeDtypeStruct((bsz,) + s, F32) for s in L0_STATE_SHAPES] + stream_outs)
    out_specs = ([x_spec] + [_state_spec(s, rows) for s in L0_STATE_SHAPES]
                 + _stream_specs(stream_outs, n_steps, nc))
    row_scratch = L0_CARRY_SHAPES + ((q_len, IN_MIX_PAD),)
    scratch = [pltpu.VMEM(s, F32) for _ in range(rows) for s in row_scratch]
    return pl.pallas_call(
        functools.partial(_l0_prompt_kernel, front_pad=front_pad, rows=rows, stream=bool(stream_in)),
        grid=grid, in_specs=in_specs, out_specs=out_specs, out_shape=out_shape, scratch_shapes=scratch,
        compiler_params=pltpu.CompilerParams(dimension_semantics=("arbitrary", "arbitrary"),
                                             vmem_limit_bytes=L0_VMEM_LIMIT),
        name="l0_prompt",
    )(x, x, *weights, *init, *stream_in)


def _rglru_gates(xc, ra, ix, ba_ref, bx_ref, lam_ref):
    r = _sigmoid(ra + ba_ref[...])
    i = _sigmoid(ix + bx_ref[...])
    log_a = r * (-LRU_C * _softplus(-lam_ref[...]))
    a = jnp.exp(log_a)
    u = jnp.sqrt(1.0 - a * a) * (i * xc)
    return a, u


def _perm_time(n, row=False):
    p = lax.broadcasted_iota(jnp.int32, (1, n) if row else (n, 1), 1 if row else 0)
    return (n // SUBLANE) * (p & (SUBLANE - 1)) + lax.shift_right_logical(p, 3)


def _perm_matrices(n):
    nb = n // SUBLANE
    r = lax.broadcasted_iota(jnp.int32, (n, n), 0)
    c = lax.broadcasted_iota(jnp.int32, (n, n), 1)
    to_perm = jnp.where(c == nb * (r & (SUBLANE - 1)) + lax.shift_right_logical(r, 3), 1.0, 0.0)
    to_time = jnp.where(r == nb * (c & (SUBLANE - 1)) + lax.shift_right_logical(c, 3), 1.0, 0.0)
    return to_perm.astype(BF16), to_time.astype(BF16)


def _move_rows(sel, x_bf16):
    return jnp.dot(sel, x_bf16, preferred_element_type=F32).astype(BF16)


def _conv_perm(tail_ref, x, w_ref, b_ref):
    n, ch = x.shape
    nb = n // SUBLANE
    x3 = x.reshape(nb, SUBLANE, ch)
    tail8 = tail_ref[...]
    sub = lax.broadcasted_iota(jnp.int32, (SUBLANE, ch), 0)
    y = b_ref[...].reshape(1, 1, ch) + w_ref[TAIL:TAIL + 1, :].reshape(1, 1, ch) * x3
    wrapped = [jnp.where(sub >= 1, pltpu.roll(x3[nb - d], 1, 0), tail8[SUBLANE - d:SUBLANE - d + 1, :])
               for d in range(1, CONV_W)]
    for back in range(1, CONV_W):
        head = jnp.stack([wrapped[back - j - 1] for j in range(back)], axis=0)
        shifted = jnp.concatenate([head, x3[:nb - back]], axis=0)
        y = y + w_ref[TAIL - back:TAIL - back + 1, :].reshape(1, 1, ch) * shifted
    for d in range(1, CONV_W):
        tail_ref[SUBLANE - d:SUBLANE - d + 1, :] = x3[nb - d][SUBLANE - 1:SUBLANE, :]
    return y.reshape(n, ch)


def _scan_perm(a, u, h_prev):
    n, ch = a.shape
    nb = n // SUBLANE
    a3 = a.reshape(nb, SUBLANE, ch)
    u3 = u.reshape(nb, SUBLANE, ch)
    local = [u3[0]]
    decay = [a3[0]]
    for j in range(1, nb):
        local.append(a3[j] * local[-1] + u3[j])
        decay.append(a3[j] * decay[-1])
    seg_u, seg_a = local[-1], decay[-1]
    sub = lax.broadcasted_iota(jnp.int32, (SUBLANE, ch), 0)
    shift = 1
    while shift < SUBLANE:
        keep = sub >= shift
        seg_u = seg_u + seg_a * jnp.where(keep, pltpu.roll(seg_u, shift, 0), 0.0)
        seg_a = seg_a * jnp.where(keep, pltpu.roll(seg_a, shift, 0), 1.0)
        shift *= 2
    seg_end = seg_a * h_prev + seg_u
    carry = jnp.where(sub >= 1, pltpu.roll(seg_end, 1, 0), h_prev)
    h3 = jnp.stack([local[j] + decay[j] * carry for j in range(nb)], axis=0)
    return h3.reshape(n, ch), seg_end[SUBLANE - 1:SUBLANE, :]


N_L1_W = 10
N_L1_S = 2


N_STREAM_IN = 5
N_STREAM_OUT = 2


def _l1_prompt_kernel(h_ref, hnext_ref, *refs, front_pad, rows, stream):
    w_refs = refs[:N_L1_W]
    init_refs = refs[N_L1_W:N_L1_W + N_L1_S]
    pos = N_L1_W + N_L1_S
    stream_in = refs[pos:pos + N_STREAM_IN] if stream else ()
    pos += len(stream_in)
    out_refs = refs[pos:pos + N_L1_S + 1]
    pos += N_L1_S + 1
    stream_out = refs[pos:pos + N_STREAM_OUT] if stream else ()
    pos += len(stream_out)
    scratch = refs[pos:]
    per_row = len(scratch) // rows
    c = pl.program_id(1)
    fnw_ref, win_ref, wout_ref = w_refs[1], w_refs[2], w_refs[3]
    q_len = h_ref.shape[1]
    gw = LRU_WIDTH // L1_GROUPS
    pieces = {}

    def each_row(phase):
        return [_l1_prompt_row(h_ref.at[r], hnext_ref.at[r], *w_refs, *init_refs, *(o.at[r] for o in out_refs),
                               *scratch[r * per_row:(r + 1) * per_row], front_pad=front_pad, phase=phase,
                               emit=lambda g, y, r=r: pieces.setdefault(g, {}).__setitem__(r, y))
                for r in range(rows)]

    @pl.when(c == 0)
    def _():
        each_row("init")

    bodies = each_row("body")
    proj_refs = [scratch[r * per_row + per_row - 1] for r in range(rows)]
    _, to_time = _perm_matrices(q_len)
    partials = []

    def in_proj():
        lhs = jnp.concatenate([hn_next for _, hn_next, _ in bodies], axis=0)
        for g in range(L1_GROUPS):
            for lo in (g * gw, LRU_WIDTH + g * gw):
                res = _bdot(lhs, _wload(win_ref[:, lo:lo + gw]))
                for r in range(rows):
                    proj_refs[r][:, lo:lo + gw] = res[r * q_len:(r + 1) * q_len]
            yield

    def out_proj():
        pending = list(range(L1_GROUPS))
        while pending:
            for g in list(pending):
                if len(pieces.get(g, ())) == rows:
                    y_t = jnp.concatenate([_move_rows(to_time, pieces[g][r].astype(BF16)) for r in range(rows)],
                                          axis=0)
                    partials.append(_bdot(y_t, _wload(wout_ref[g * gw // 2:(g + 1) * gw // 2, :])))
                    pending.remove(g)
            yield

    chains = [bodies[r][0][g] for g in range(L1_GROUPS) for r in range(rows)]
    if stream:
        step = pl.program_id(0) * pl.num_programs(1) + c
        chains.append(_mlstm_state_update(*stream_in, *stream_out, base=step * stream_in[1].shape[0]))
    _run_staggered([in_proj()] + chains + [out_proj()])
    total = partials[0]
    for part in partials[1:]:
        total = total + part
    for r, (_, _, h_in) in enumerate(bodies):
        out_refs[0][r] = _rms(h_in + total[r * q_len:(r + 1) * q_len], fnw_ref[...])

    @pl.when(c == pl.num_programs(1) - 1)
    def _():
        each_row("final")


def _l1_in_proj(h_val, nw_ref, win_ref, to_perm):
    hn = _move_rows(to_perm, _rms(h_val, nw_ref[...]).astype(BF16))
    return _bdot(hn, _wload(win_ref[...]))


def _l1_prompt_row(h_ref, hnext_ref, nw_ref, fnw_ref, win_ref, wout_ref, cw_ref, cb_ref,
                   wax_ref, ba_ref, bx_ref, lam_ref, ilc_ref, ilh_ref,
                   y_ref, olc_ref, olh_ref, lbuf, h_st, proj_s, *, front_pad, phase, emit):
    if phase == "init":
        lbuf[...] = jnp.zeros(lbuf.shape, F32)
        lbuf[SUBLANE - TAIL:SUBLANE, :] = ilc_ref[0]
        h_st[...] = ilh_ref[0]
        proj_s[...] = _l1_in_proj(h_ref[...], nw_ref, win_ref, _perm_matrices(h_ref.shape[0])[0])
        return None
    if phase == "final":
        olc_ref[...] = lbuf[SUBLANE - TAIL:SUBLANE, :]
        olh_ref[...] = h_st[...]
        return None

    q_len = h_ref.shape[0]
    h_in = h_ref[...]
    to_perm, _ = _perm_matrices(q_len)
    hn_next = _move_rows(to_perm, _rms(hnext_ref[...], nw_ref[...]).astype(BF16))
    if front_pad:
        valid = _perm_time(q_len) >= front_pad
    gw = LRU_WIDTH // L1_GROUPS
    tiles = gw // LANE
    gates = [proj_s[:, g * gw:(g + 1) * gw] for g in range(L1_GROUPS)]
    xrs = [proj_s[:, LRU_WIDTH + g * gw:LRU_WIDTH + (g + 1) * gw] for g in range(L1_GROUPS)]

    def group(g):
        cg = slice(g * gw, (g + 1) * gw)
        xc = _conv_perm(lbuf.at[:, cg], xrs[g], cw_ref.at[:, cg], cb_ref.at[:, cg])
        ra, ix = _blockdiag_tiles(xc, wax_ref.at[g * tiles:(g + 1) * tiles])
        yield
        a, u = _rglru_gates(xc, ra, ix, ba_ref.at[:, cg], bx_ref.at[:, cg], lam_ref.at[:, cg])
        if front_pad:
            a = jnp.where(valid, a, 1.0)
            u = jnp.where(valid, u, 0.0)
        yield
        h, h_last = _scan_perm(a, u, h_st[:, cg])
        h_st[:, cg] = h_last
        yield
        emit(g, h * _silu(gates[g]))

    return [group(g) for g in range(L1_GROUPS)], hn_next, h_in


def _run_staggered(gens):
    live = []
    pending = list(gens)
    while pending or live:
        if pending:
            live.append(pending.pop(0))
        for gen in list(live):
            if next(gen, "done") == "done":
                live.remove(gen)


L1_STATE_SHAPES = ((TAIL, LRU_WIDTH), (1, LRU_WIDTH))


def _l1_prompt(h1, weights, init, front_pad, stream_in=(), stream_out_shapes=()):
    bsz, length, _ = h1.shape
    q_len = min(CHUNK, length)
    assert length % q_len == 0
    rows = _rows_per_step(bsz, L1_ROWS)
    assert len(weights) == N_L1_W and len(init) == N_L1_S
    nc = length // q_len
    grid = (bsz // rows, nc)
    n_steps = grid[0] * nc
    last = nc - 1
    x_spec = pl.BlockSpec((rows, q_len, D_MODEL), lambda b, c: (b, c, 0))
    next_spec = pl.BlockSpec((rows, q_len, D_MODEL), lambda b, c: (b, jnp.minimum(c + 1, last), 0))
    stream_outs = [jax.ShapeDtypeStruct(s, F32) for s in stream_out_shapes]
    in_specs = ([x_spec, next_spec] + [_const_spec(w.shape) for w in weights]
                + [_state_spec(s, 0) for s in L1_STATE_SHAPES] + _stream_specs(stream_in, n_steps, nc))
    out_shape = ([jax.ShapeDtypeStruct((bsz, length, D_MODEL), F32)]
                 + [jax.ShapeDtypeStruct((bsz,) + s, F32) for s in L1_STATE_SHAPES] + stream_outs)
    out_specs = ([x_spec] + [_state_spec(s, rows) for s in L1_STATE_SHAPES]
                 + _stream_specs(stream_outs, n_steps, nc))
    row_scratch = ((SUBLANE, LRU_WIDTH), (1, LRU_WIDTH), (q_len, 2 * LRU_WIDTH))
    scratch = [pltpu.VMEM(s, F32) for _ in range(rows) for s in row_scratch]
    return pl.pallas_call(
        functools.partial(_l1_prompt_kernel, front_pad=front_pad, rows=rows, stream=bool(stream_in)),
        grid=grid, in_specs=in_specs, out_specs=out_specs, out_shape=out_shape, scratch_shapes=scratch,
        compiler_params=pltpu.CompilerParams(dimension_semantics=("arbitrary", "arbitrary"),
                                             vmem_limit_bytes=VMEM_LIMIT),
        name="l1_prompt",
    )(h1, h1, *weights, *init, *stream_in)


def _conv_step(buf_ref, x, w_ref, b_ref, newbuf_ref):
    y = b_ref[...] + w_ref[3:4, :] * x
    for tap in range(TAIL):
        y = y + w_ref[tap:tap + 1, :] * buf_ref[tap]
    for tap in range(TAIL - 1):
        newbuf_ref[tap] = buf_ref[tap + 1]
    newbuf_ref[TAIL - 1] = x
    return y


def _l0_sample_pre_kernel(x_ref, nw_ref, win_ref, scw_ref, scb_ref, dtb_ref, alog_ref,
                          mcw_ref, mcb_ref, wqk_ref, wv_ref, wg_ref, bg_ref,
                          sbuf_ref, mbuf_ref, m0_ref, n0_ref,
                          nsb_ref, nmb_ref, zs_ref, xs_ref, bm_ref, cm_ref, xdt_t_ref, dec_t_ref,
                          zm_ref, xc_ref, q_ref, isv_t_ref, fs_t_ref, k_ref, mnew_ref, nnew_ref, den_ref):
    x = x_ref[...]
    hn = _rms(x, nw_ref[...])
    proj = _bdot(hn, _wload(win_ref[...]))
    zs_ref[...] = proj[:, OFF_ZS:OFF_ZS + SSD_WIDTH]
    zm_ref[...] = proj[:, OFF_ZM:OFF_ZM + ML_WIDTH]
    xbc = proj[:, OFF_XBC:OFF_XBC + SSD_CONV_CH]
    dt_raw = proj[:, OFF_DT:OFF_DT + LANE]
    xm = proj[:, OFF_XM:OFF_XM + ML_WIDTH]
    expand = _expand_matrix()

    xbc = _silu(_conv_step(sbuf_ref, xbc, scw_ref, scb_ref, nsb_ref))
    xs = xbc[:, :SSD_WIDTH]
    xs_ref[...] = xs
    bm_ref[...] = xbc[:, SSD_WIDTH:SSD_WIDTH + SSD_GROUPS * SSD_STATE]
    cm_ref[...] = xbc[:, SSD_WIDTH + SSD_GROUPS * SSD_STATE:]
    dt = _softplus(dt_raw + dtb_ref[...])
    log_a = -dt * jnp.exp(alog_ref[...])
    xdt_t_ref[...] = xs * _expand_heads(dt, expand)
    dec_t_ref[...] = jnp.exp(log_a)

    xc = _silu(_conv_step(mbuf_ref, xm, mcw_ref, mcb_ref, nmb_ref))
    xc_ref[...] = xc
    q, k, v, ig, logf = _mlstm_qkv_gates(xm, xc, wqk_ref, wv_ref, wg_ref, bg_ref)
    m0 = m0_ref[...]
    m_new = jnp.maximum(logf + m0, ig)
    fs = jnp.exp(logf + m0 - m_new)
    is_ = jnp.exp(ig - m_new)
    mnew_ref[...] = m_new
    r = lax.broadcasted_iota(jnp.int32, (LANE, ML_WIDTH), 0)
    cidx = lax.broadcasted_iota(jnp.int32, (LANE, ML_WIDTH), 1)
    expand_m = jnp.where(lax.shift_right_logical(cidx, 8) == r, 1.0, 0.0).astype(BF16)
    fs_e = _expand_heads(fs, expand_m)
    is_e = _expand_heads(is_, expand_m)
    n_new = fs_e * n0_ref[...] + is_e * k
    nnew_ref[...] = n_new
    q_ref[...] = q
    k_ref[...] = k
    isv_t_ref[...] = is_e * v
    fs_t_ref[...] = fs
    nq = n_new * q
    floor = jnp.exp(-m_new)
    for hd in range(ML_HEADS):
        den = jnp.sum(nq[:, hd * ML_HEAD_DIM:(hd + 1) * ML_HEAD_DIM], axis=-1, keepdims=True)
        den_ref[:, hd:hd + 1] = jnp.maximum(jnp.abs(den), floor[:, hd:hd + 1])


BT_S = 8
BT_C = 8


def _ssd_state_update(dec_ref, s_ref, xdt_ref, bm_ref, cm_ref, snew_ref, y_ref, *, base):
    n = s_ref.shape[0]
    half = SSD_WIDTH // SSD_GROUPS
    lane = lax.broadcasted_iota(jnp.int32, (half, 2 * SUBLANE), 1)
    accs = [jnp.zeros((half, 2 * SUBLANE), F32) for _ in range(SSD_GROUPS)]
    for i in range(n):
        x_col = xdt_ref[0, :, i:i + 1].reshape(SSD_HEADS, SSD_HEAD_DIM, 1)
        for g in range(SSD_GROUPS):
            hs = slice(g * SSD_HPG, (g + 1) * SSD_HPG)
            gs = slice(g * SSD_STATE, (g + 1) * SSD_STATE)
            b_row = bm_ref[0, i:i + 1, gs].reshape(1, 1, SSD_STATE)
            decay = jnp.stack([jnp.full((1, 1), dec_ref[base + i, hd], F32)
                               for hd in range(hs.start, hs.stop)], axis=0)
            s_new = decay * s_ref[i, hs] + x_col[hs] * b_row
            snew_ref[i, hs] = s_new
            prod = _bdot_nt(s_new.reshape(half, SSD_STATE), cm_ref[0, :, gs])
            accs[g] = jnp.where(lane == i, prod, accs[g])
            yield
    for g in range(SSD_GROUPS):
        y_ref[0, g * half:(g + 1) * half, :] = accs[g][:, :n]


def _mlstm_state_update(fs_ref, c_ref, isv_ref, k_ref, q_ref, cnew_ref, num_ref, *, base):
    n = c_ref.shape[0]
    lane = lax.broadcasted_iota(jnp.int32, (ML_HEAD_DIM, 2 * SUBLANE), 1)
    for hd in range(ML_HEADS):
        sl = slice(hd * ML_HEAD_DIM, (hd + 1) * ML_HEAD_DIM)
        q_rows = q_ref[0, :, sl]
        acc = jnp.zeros((ML_HEAD_DIM, 2 * SUBLANE), F32)
        for i in range(n):
            v_col = isv_ref[0, sl, i:i + 1]
            c_new = fs_ref[base + i, hd] * c_ref[i, hd] + v_col * k_ref[0, i:i + 1, sl]
            cnew_ref[i, hd] = c_new
            acc = jnp.where(lane == i, _bdot_nt(c_new, q_rows), acc)
            yield
        num_ref[0, sl, :] = acc[:, :n]


def _sample_post_kernel(x_ref, ys_t_ref, num_t_ref, den_ref, zs_ref, xs_ref, zm_ref, xc_ref,
                        dsk_ref, snw_ref, msk_ref, mnw_ref, wout_ref,
                        nw1_ref, fnw_ref, win1_ref, wout1_ref, cw_ref, cb_ref,
                        wax_ref, ba_ref, bx_ref, lam_ref, lbuf_ref, h0_ref,
                        y_ref, nlb_ref, hnew_ref):
    xs = xs_ref[...]
    y_s = ys_t_ref[...] + dsk_ref[...] * xs
    y_s = _group_rmsnorm(y_s * _silu(zs_ref[...]), snw_ref[...])
    num = num_t_ref[...]
    den = den_ref[...]
    h_m = jnp.concatenate(
        [num[:, hd * ML_HEAD_DIM:(hd + 1) * ML_HEAD_DIM] / den[:, hd:hd + 1] for hd in range(ML_HEADS)], axis=-1)
    h_m = _head_layernorm(h_m) * mnw_ref[...]
    y_m = (h_m + msk_ref[...] * xc_ref[...]) * _silu(zm_ref[...])
    h1 = x_ref[...] + _bdot(jnp.concatenate([y_s, y_m], axis=-1), _wload(wout_ref[...]))

    hn = _rms(h1, nw1_ref[...])
    proj = _bdot(hn, _wload(win1_ref[...]))
    gate = proj[:, :LRU_WIDTH]
    xr = proj[:, LRU_WIDTH:]
    xc = _conv_step(lbuf_ref, xr, cw_ref, cb_ref, nlb_ref)
    ra, ix = _blockdiag_tiles(xc, wax_ref)
    a, u = _rglru_gates(xc, ra, ix, ba_ref, bx_ref, lam_ref)
    h = a * h0_ref[...] + u
    hnew_ref[...] = h
    h2 = h1 + _bdot(h * _silu(gate), _wload(wout1_ref[...]))
    y_ref[...] = _rms(h2, fnw_ref[...])


def _full_call(kernel_fn, out_shapes, args, name):
    return pl.pallas_call(
        kernel_fn,
        out_shape=[jax.ShapeDtypeStruct(s, F32) for s in out_shapes],
        compiler_params=pltpu.CompilerParams(vmem_limit_bytes=VMEM_LIMIT),
        name=name,
    )(*args)


def _to_cols(a, bt):
    rows, ch = a.shape
    return a.reshape(rows // bt, bt, ch).transpose(0, 2, 1)


def _from_cols(a):
    tiles, ch, bt = a.shape
    return a.transpose(0, 2, 1).reshape(tiles * bt, ch)


def _row(v, width=None):
    v = v.reshape(1, -1).astype(F32)
    if width is not None and v.shape[1] < width:
        v = jnp.pad(v, ((0, 0), (0, width - v.shape[1])))
    return v


PACK_STEPS = 8


def _pack_all(w_in, weights):
    flats = [w_in] + [w.reshape(-1, w.shape[-1]) for w in weights]
    widths = [IN_MIX_PAD] + [f.shape[1] for f in flats[1:]]
    for f in flats:
        assert f.shape[0] % (2 * SUBLANE * PACK_STEPS) == 0
    packed = pl.pallas_call(
        _pack_kernel,
        grid=(PACK_STEPS,),
        in_specs=[pl.BlockSpec((f.shape[0] // PACK_STEPS, f.shape[1]), lambda i: (i, 0)) for f in flats],
        out_specs=[pl.BlockSpec((f.shape[0] // PACK_STEPS // 2, n), lambda i: (i, 0)) for f, n in zip(flats, widths)],
        out_shape=[jax.ShapeDtypeStruct((f.shape[0] // 2, n), jnp.uint32) for f, n in zip(flats, widths)],
        compiler_params=pltpu.CompilerParams(vmem_limit_bytes=VMEM_LIMIT),
        name="pack_weights",
    )(*flats)
    return [packed[0]] + [p.reshape(w.shape[:-2] + (w.shape[-2] // 2, w.shape[-1]))
                          for p, w in zip(packed[1:], weights)]


def _pack_rows(x):
    return pltpu.bitcast(x.astype(BF16), jnp.uint32)


def _pack_kernel(*refs):
    n = len(refs) // 2
    win_ref, wino_ref = refs[0], refs[n]
    wino_ref[:, :OFF_DT] = _pack_rows(win_ref[:, :OFF_DT])
    dt_tile = win_ref[:, OFF_DT:OFF_DT + LANE]
    lane = lax.broadcasted_iota(jnp.int32, dt_tile.shape, 1)
    wino_ref[:, OFF_DT:OFF_ZM] = _pack_rows(jnp.where(lane < SSD_HEADS, dt_tile, 0.0))
    wino_ref[:, OFF_ZM:] = _pack_rows(win_ref[:, OFF_DT + SSD_HEADS:])
    for w_ref, o_ref in zip(refs[1:n], refs[n + 1:]):
        o_ref[...] = _pack_rows(w_ref[...])


def _dense_block_tiles(w):
    nb, bi, bo = w.shape
    per = LANE // bi
    rows = w.reshape(nb // per, per * bi, bo)
    col = jnp.arange(per * bo)
    spread = (col[None, :] % bo == jnp.arange(bo)[:, None]).astype(w.dtype)
    rep = jnp.einsum('tro,oc->trc', rows, spread)
    same_block = (jnp.arange(per * bi)[:, None] // bi) == (col[None, :] // bo)
    return jnp.where(same_block, rep, 0.0)


def kernel(x_prompt, x_sample, state_ssd_conv, state_ssd, state_mlstm_conv, state_mlstm_C, state_mlstm_n,
           state_mlstm_m, state_lru_conv, state_lru_h, meta_tokens, norm_w, final_norm_w, w_in_mix, w_out_mix,
           ssd_conv_w, ssd_conv_b, ssd_dt_bias, ssd_a_log, ssd_d, ssd_norm_w, ml_conv_w, ml_conv_b, ml_wq, ml_wk,
           ml_wv, ml_w_gate, ml_b_gate, ml_skip, ml_norm_w, lru_w_in, lru_w_out, lru_conv_w, lru_conv_b, lru_wa,
           lru_ba, lru_wx, lru_bx, lru_lambda):
    bsz = x_prompt.shape[0]
    dec = x_sample.shape[0]

    wout = w_out_mix[0]
    nw0 = _row(norm_w[0])
    nw1 = _row(norm_w[1])
    fnw = _row(final_norm_w)
    scw = ssd_conv_w[0]
    scb = _row(ssd_conv_b[0])
    dtb = _row(ssd_dt_bias[0], LANE)
    alog = _row(ssd_a_log[0], LANE)
    dsk = _row(jnp.repeat(ssd_d[0], SSD_HEAD_DIM))
    snw = _row(ssd_norm_w[0])
    mcw = ml_conv_w[0]
    mcb = _row(ml_conv_b[0])
    wqk = jnp.concatenate([_dense_block_tiles(ml_wq[0]), _dense_block_tiles(ml_wk[0])], axis=2)
    wv = _dense_block_tiles(ml_wv[0])
    wg_raw = ml_w_gate[0]
    wg = jnp.concatenate([jnp.pad(wg_raw[:, :ML_HEADS], ((0, 0), (0, LANE - ML_HEADS))),
                          jnp.pad(wg_raw[:, ML_HEADS:], ((0, 0), (0, LANE - ML_HEADS)))], axis=1)
    bg = jnp.concatenate([_row(ml_b_gate[0, :ML_HEADS], LANE), _row(ml_b_gate[0, ML_HEADS:], LANE)], axis=1)
    msk = _row(ml_skip[0])
    mnw = _row(ml_norm_w[0])
    win1 = lru_w_in[0]
    wout1 = lru_w_out[0]
    lcw = lru_conv_w[0]
    lcb = _row(lru_conv_b[0])
    wax = jnp.concatenate([lru_wa[0], lru_wx[0]], axis=2)
    r_idx = lax.broadcasted_iota(jnp.int32, (LANE, SSD_WIDTH), 0)
    c_idx = lax.broadcasted_iota(jnp.int32, (LANE, SSD_WIDTH), 1)
    expand = (c_idx // SSD_HEAD_DIM == r_idx).astype(F32)
    ba = _row(lru_ba[0])
    bx = _row(lru_bx[0])
    lam = _row(lru_lambda[0])

    win, wout, wqk, wv, wg, expand, win1, wout1, wax = _pack_all(
        w_in_mix[0], [wout, wqk, wv, wg, expand, win1, wout1, wax])
    l0_w = (nw0, win, wout, scw, scb, dtb, alog, dsk, snw, mcw, mcb, wqk, wv, wg, bg, msk, mnw, expand)
    l1_w = (nw1, fnw, win1, wout1, lcw, lcb, wax, ba, bx, lam)

    xs2 = x_sample[:, 0]
    sbuf = jnp.moveaxis(state_ssd_conv[0], 1, 0)
    mbuf = jnp.moveaxis(state_mlstm_conv[0], 1, 0)
    lbuf = jnp.moveaxis(state_lru_conv[0], 1, 0)
    m0 = jnp.pad(state_mlstm_m[0], ((0, 0), (0, LANE - ML_HEADS)))
    n0 = state_mlstm_n[0].reshape(dec, ML_WIDTH)
    pre_shapes = ((TAIL, dec, SSD_CONV_CH), (TAIL, dec, ML_WIDTH), (dec, SSD_WIDTH), (dec, SSD_WIDTH),
                  (dec, SSD_GROUPS * SSD_STATE), (dec, SSD_GROUPS * SSD_STATE), (dec, SSD_WIDTH), (dec, LANE),
                  (dec, ML_WIDTH), (dec, ML_WIDTH), (dec, ML_WIDTH), (dec, ML_WIDTH), (dec, LANE),
                  (dec, ML_WIDTH), (dec, LANE), (dec, ML_WIDTH), (dec, ML_HEADS))
    (nsb, nmb, zs, xs_c, bm, cm, xdt_t, dec_t, zm, xc_m, q, isv_t, fs_t, k, m_new, n_new, den) = _full_call(
        _l0_sample_pre_kernel, pre_shapes,
        (xs2, nw0, win, scw, scb, dtb, alog, mcw, mcb, wqk, wv, wg, bg, sbuf, mbuf, m0, n0), "l0_sample_pre")

    zero0 = tuple(jnp.zeros((1,) + s, F32) for s in L0_STATE_SHAPES)
    zero1 = tuple(jnp.zeros((1,) + s, F32) for s in L1_STATE_SHAPES)
    meta = jnp.pad(meta_tokens.astype(F32), ((CHUNK - N_META, 0), (0, 0)))[None]
    meta_out = _l0_prompt(meta, l0_w, zero0, CHUNK - N_META)
    meta1_out = _l1_prompt(meta_out[0], l1_w, zero1, CHUNK - N_META)

    n_steps = (bsz // _rows_per_step(bsz, L0_ROWS)) * (x_prompt.shape[1] // CHUNK)
    assert n_steps == (bsz // _rows_per_step(bsz, L1_ROWS)) * (x_prompt.shape[1] // CHUNK)
    per_step = dec // n_steps
    assert per_step * n_steps == dec

    def step_rows(a, pad_rows=False):
        a = a.reshape(n_steps, per_step, a.shape[-1])
        return jnp.pad(a, ((0, 0), (0, 2 * SUBLANE - per_step), (0, 0))) if pad_rows else a

    l0_out = _l0_prompt(
        x_prompt, l0_w, tuple(meta_out[1:]), 0,
        stream_in=(dec_t[:, :SSD_HEADS], state_ssd[0], _to_cols(xdt_t, per_step), step_rows(bm), step_rows(cm, True)),
        stream_out_shapes=((dec, SSD_HEADS, SSD_HEAD_DIM, SSD_STATE), (n_steps, SSD_WIDTH, per_step)))
    h1_p, p_sc, p_s, p_mc, p_c, p_n, p_m, s_new, ys_c = l0_out
    q_rows = step_rows(q, True)
    y_prompt, p_lc, p_lh, c_new, num_c = _l1_prompt(
        h1_p, l1_w, tuple(meta1_out[1:]), 0,
        stream_in=(fs_t[:, :ML_HEADS], state_mlstm_C[0], _to_cols(isv_t, per_step),
                   step_rows(k), q_rows),
        stream_out_shapes=((dec, ML_HEADS, ML_HEAD_DIM, ML_HEAD_DIM), (n_steps, ML_WIDTH, per_step)))

    p_m = p_m[:, 0, :ML_HEADS]
    p_lh = p_lh[:, 0]

    post_shapes = ((dec, D_MODEL), (TAIL, dec, LRU_WIDTH), (dec, LRU_WIDTH))
    y_s2, nlb, h_new = _full_call(
        _sample_post_kernel, post_shapes,
        (xs2, _from_cols(ys_c), _from_cols(num_c), den, zs, xs_c, zm, xc_m, dsk, snw, msk, mnw, wout,
         nw1, fnw, win1, wout1, lcw, lcb, wax, ba, bx, lam, lbuf, state_lru_h[0]), "sample_post")

    s_sc = jnp.moveaxis(nsb, 0, 1)[None]
    s_mc = jnp.moveaxis(nmb, 0, 1)[None]
    s_lc = jnp.moveaxis(nlb, 0, 1)[None]
    return (y_prompt, y_s2[:, None, :],
            p_sc[None], p_s[None], p_mc[None], p_c[None], p_n[None], p_m[None], p_lc[None], p_lh[None],
            s_sc, s_new[None], s_mc, c_new[None], n_new.reshape(dec, ML_HEADS, ML_HEAD_DIM)[None],
            m_new[:, :ML_HEADS][None], s_lc, h_new[None])
```

```python
import functools

import jax
import jax.numpy as jnp
from jax import lax
from jax.experimental import pallas as pl
from jax.experimental.pallas import tpu as pltpu

F32 = jnp.float32
BF16 = jnp.bfloat16

D_MODEL = 1024
N_META = 16
CONV_W = 4
EPS = 1e-6
NEG = -1e30
SSD_WIDTH = 1024
SSD_HEAD_DIM = 64
SSD_HEADS = 16
SSD_GROUPS = 2
SSD_HPG = 8
SSD_STATE = 128
SSD_CONV_CH = 1536
ML_WIDTH = 1024
ML_HEADS = 4
ML_HEAD_DIM = 256
ML_QKV_BLOCK = 4
LRU_WIDTH = 2048
LRU_BLOCKS = 16
LRU_BLOCK = 128
LRU_C = 8.0

LANE = 128
SUBLANE = 8
CHUNK = 128
L0_ROWS = 2
L1_ROWS = 2
L1_GROUPS = 4
TAIL = CONV_W - 1

OFF_ZS = 0
OFF_XBC = OFF_ZS + SSD_WIDTH
OFF_DT = OFF_XBC + SSD_CONV_CH
OFF_ZM = OFF_DT + LANE
OFF_XM = OFF_ZM + ML_WIDTH
IN_MIX_PAD = OFF_XM + ML_WIDTH

VMEM_LIMIT = 56 * 1024 * 1024
L0_VMEM_LIMIT = 61 * 1024 * 1024


def _sigmoid(x):
    return 1.0 / (1.0 + jnp.exp(-x))


def _silu(x):
    return x * _sigmoid(x)


def _softplus(x):
    return jnp.maximum(x, 0.0) + jnp.log1p(jnp.exp(-jnp.abs(x)))


def _rms(x, w):
    return x * lax.rsqrt(jnp.mean(x * x, axis=-1, keepdims=True) + EPS) * w


def _bdot(a, b):
    return jnp.dot(a.astype(BF16), b.astype(BF16), preferred_element_type=F32)


def _bdot_nt(a, b):
    return lax.dot_general(a.astype(BF16), b.astype(BF16), (((1,), (1,)), ((), ())), preferred_element_type=F32)


def _wload(w):
    return pltpu.bitcast(w, BF16)


def _split3(x):
    hi = x.astype(BF16)
    r = x - hi.astype(F32)
    mid = r.astype(BF16)
    lo = (r - mid.astype(F32)).astype(BF16)
    return hi, mid, lo


def _cumsum_rows(x, tril):
    hi, mid, lo = _split3(x)
    d = functools.partial(jnp.dot, preferred_element_type=F32)
    return d(tril, hi) + d(tril, mid) + d(tril, lo)


def _expand_heads(x, expand):
    hi, mid, _ = _split3(x)
    d = functools.partial(jnp.dot, preferred_element_type=F32)
    return d(hi, expand) + d(mid, expand)


def _expand_matrix():
    r = lax.broadcasted_iota(jnp.int32, (LANE, SSD_WIDTH), 0)
    c = lax.broadcasted_iota(jnp.int32, (LANE, SSD_WIDTH), 1)
    return jnp.where(lax.shift_right_logical(c, 6) == r, 1.0, 0.0).astype(BF16)


def _blockdiag_tiles(x, w_ref):
    k = w_ref.shape[0]
    m = w_ref.shape[2] // LANE
    prods = [_bdot(x[:, t * LANE:(t + 1) * LANE], _wload(w_ref[t])) for t in range(k)]
    return [jnp.concatenate([p[:, j * LANE:(j + 1) * LANE] for p in prods], axis=-1) for j in range(m)]


def _group_rmsnorm(y, w):
    half = SSD_WIDTH // SSD_GROUPS
    parts = []
    for g in range(SSD_GROUPS):
        yg = y[:, g * half:(g + 1) * half]
        parts.append(yg * lax.rsqrt(jnp.mean(yg * yg, axis=-1, keepdims=True) + EPS))
    return jnp.concatenate(parts, axis=-1) * w


def _head_layernorm(h):
    parts = []
    for k in range(ML_HEADS):
        hk = h[:, k * ML_HEAD_DIM:(k + 1) * ML_HEAD_DIM]
        mu = jnp.mean(hk, axis=-1, keepdims=True)
        d = hk - mu
        var = jnp.mean(d * d, axis=-1, keepdims=True)
        parts.append(d * lax.rsqrt(var + EPS))
    return jnp.concatenate(parts, axis=-1)


def _mlstm_qkv_gates(xm, xc, wqk_ref, wv_ref, wg_ref, bg_ref):
    q, k = _blockdiag_tiles(xc, wqk_ref)
    v, = _blockdiag_tiles(xm, wv_ref)
    gates = _bdot(jnp.concatenate([q, k, v], axis=-1), _wload(wg_ref[...])) + bg_ref[...]
    ig = gates[:, :LANE]
    logf = -_softplus(-gates[:, LANE:])
    return q, k * (ML_HEAD_DIM ** -0.5), v, ig, logf


N_L0_W = 18
N_L0_S = 6


def _l0_prompt_kernel(x_ref, xnext_ref, *refs, front_pad, rows, stream):
    w_refs = refs[:N_L0_W]
    init_refs = refs[N_L0_W:N_L0_W + N_L0_S]
    pos = N_L0_W + N_L0_S
    stream_in = refs[pos:pos + N_STREAM_IN] if stream else ()
    pos += len(stream_in)
    out_refs = refs[pos:pos + N_L0_S + 1]
    pos += N_L0_S + 1
    stream_out = refs[pos:pos + N_STREAM_OUT] if stream else ()
    pos += len(stream_out)
    scratch = refs[pos:]
    per_row = len(scratch) // rows
    c = pl.program_id(1)
    win_ref, wout_ref = w_refs[1], w_refs[2]
    q_len = x_ref.shape[1]
    pieces = {}

    def each_row(phase):
        return [_l0_prompt_row(x_ref.at[r], xnext_ref.at[r], *w_refs, *init_refs, *(o.at[r] for o in out_refs),
                               *scratch[r * per_row:(r + 1) * per_row], front_pad=front_pad, phase=phase,
                               emit=lambda k0, y, r=r: pieces.setdefault(k0, {}).__setitem__(r, y))
                for r in range(rows)]

    @pl.when(c == 0)
    def _():
        each_row("init")

    bodies = each_row("body")
    proj_refs = [scratch[r * per_row + per_row - 1] for r in range(rows)]
    _, to_time = _perm_matrices(q_len)
    partials = []

    def in_proj():
        lhs = jnp.concatenate([hn_next for _, hn_next, _ in bodies], axis=0)
        for lo, hi in L0_PROJ_PIECES:
            res = _bdot(lhs, _wload(win_ref[:, lo:hi]))
            for r in range(rows):
                proj_refs[r][:, lo:hi] = res[r * q_len:(r + 1) * q_len]
            yield

    def out_proj():
        pending = [(0, SSD_WIDTH)] + [(SSD_WIDTH + hd * ML_HEAD_DIM, ML_HEAD_DIM) for hd in range(ML_HEADS)]
        while pending:
            for k0, width in list(pending):
                if len(pieces.get(k0, ())) == rows:
                    y_t = jnp.concatenate([_move_rows(to_time, pieces[k0][r].astype(BF16)) for r in range(rows)],
                                          axis=0)
                    partials.append(_bdot(y_t, _wload(wout_ref[k0 // 2:(k0 + width) // 2, :])))
                    pending.remove((k0, width))
            yield

    chains = [gen for gens, _, _ in bodies for gen in gens] + [in_proj(), out_proj()]
    if stream:
        step = pl.program_id(0) * pl.num_programs(1) + c
        chains.append(_ssd_state_update(*stream_in, *stream_out, base=step * stream_in[1].shape[0]))
    _run_round_robin(chains)
    total = partials[0]
    for part in partials[1:]:
        total = total + part
    for r, (_, _, x) in enumerate(bodies):
        h1 = x + total[r * q_len:(r + 1) * q_len]
        if front_pad:
            h1 = jnp.where(lax.broadcasted_iota(jnp.int32, (q_len, 1), 0) >= front_pad, h1, 0.0)
        out_refs[0][r] = h1

    @pl.when(c == pl.num_programs(1) - 1)
    def _():
        each_row("final")


L0_PROJ_PIECES = ((OFF_XBC, OFF_ZM), (OFF_XM, IN_MIX_PAD), (OFF_ZM, OFF_XM), (OFF_ZS, OFF_XBC))


def _run_round_robin(gens):
    live = list(gens)
    while live:
        for gen in list(live):
            if next(gen, "done") == "done":
                live.remove(gen)


def _l0_prompt_row(x_ref, xnext_ref, nw_ref, win_ref, wout_ref,
                   scw_ref, scb_ref, dtb_ref, alog_ref, dsk_ref, snw_ref,
                   mcw_ref, mcb_ref, wqk_ref, wv_ref, wg_ref, bg_ref, msk_ref, mnw_ref, expand_ref,
                   isc_ref, iss_ref, imc_ref, ict_ref, inn_ref, imm_ref,
                   h1_ref, osc_ref, oss_ref, omc_ref, oct_ref, onn_ref, omm_ref,
                   sbuf, mbuf, s_st, ct_st, n_st, m_st, proj_s, *, front_pad, phase, emit):
    q_len = x_ref.shape[0]

    if phase == "init":
        sbuf[...] = jnp.zeros(sbuf.shape, F32)
        mbuf[...] = jnp.zeros(mbuf.shape, F32)
        sbuf[SUBLANE - TAIL:SUBLANE, :] = isc_ref[0]
        mbuf[SUBLANE - TAIL:SUBLANE, :] = imc_ref[0]
        for g in range(SSD_GROUPS):
            heads = iss_ref[0, g * SSD_HPG:(g + 1) * SSD_HPG]
            s_st[g] = heads.reshape(SSD_HPG * SSD_HEAD_DIM, SSD_STATE).T
        for hd in range(ML_HEADS):
            ct_st[hd] = ict_ref[0, hd].T
        n_st[...] = inn_ref[0]
        m_st[...] = imm_ref[0]
        hn0 = _move_rows(_perm_matrices(q_len)[0], _rms(x_ref[...], nw_ref[...]).astype(BF16))
        proj_s[...] = _bdot(hn0, _wload(win_ref[...]))
        return None
    if phase == "final":
        osc_ref[...] = sbuf[SUBLANE - TAIL:SUBLANE, :]
        omc_ref[...] = mbuf[SUBLANE - TAIL:SUBLANE, :]
        for g in range(SSD_GROUPS):
            oss_ref[g * SSD_HPG:(g + 1) * SSD_HPG] = s_st[g].T.reshape(SSD_HPG, SSD_HEAD_DIM, SSD_STATE)
        for hd in range(ML_HEADS):
            oct_ref[hd] = ct_st[hd].T
        onn_ref[...] = n_st[...]
        omm_ref[...] = m_st[...]
        return None

    x = x_ref[...]
    to_perm, to_time = _perm_matrices(q_len)
    t_col = _perm_time(q_len)
    t_row = _perm_time(q_len, row=True)
    causal = t_col >= t_row
    tril = jnp.where(causal, 1.0, 0.0).astype(BF16)
    valid = (t_col >= front_pad) if front_pad else None
    hn_next = _move_rows(to_perm, _rms(xnext_ref[...], nw_ref[...]).astype(BF16))
    xbc_raw = proj_s[:, OFF_XBC:OFF_XBC + SSD_CONV_CH]
    dt_raw = proj_s[:, OFF_DT:OFF_DT + LANE]
    xm = proj_s[:, OFF_XM:OFF_XM + ML_WIDTH]
    z_s = proj_s[:, OFF_ZS:OFF_ZS + SSD_WIDTH]
    z_m = proj_s[:, OFF_ZM:OFF_ZM + ML_WIDTH]

    def ssd():
        xbc = _silu(_conv_perm(sbuf, xbc_raw, scw_ref, scb_ref))
        yield
        xs = xbc[:, :SSD_WIDTH]
        bm = xbc[:, SSD_WIDTH:SSD_WIDTH + SSD_GROUPS * SSD_STATE]
        cm = xbc[:, SSD_WIDTH + SSD_GROUPS * SSD_STATE:]
        dt = _softplus(dt_raw + dtb_ref[...])
        if front_pad:
            dt = jnp.where(valid, dt, 0.0)
        log_a = -dt * jnp.exp(alog_ref[...])
        a_cs = _cumsum_rows(log_a, tril)
        yield
        a_last = a_cs[q_len - 1:q_len, :]
        expand = _wload(expand_ref[...])
        w_state = _expand_heads(dt * jnp.exp(a_last - a_cs), expand)
        e_acs = _expand_heads(jnp.exp(a_cs), expand)
        a_cs_t = a_cs.T
        dt_t = dt.T
        yield
        pair_lo = lax.broadcasted_iota(jnp.int32, (q_len, LANE), 1) < SSD_HEAD_DIM
        half = SSD_WIDTH // SSD_GROUPS
        y_groups = []
        for g in range(SSD_GROUPS):
            bg = bm[:, g * SSD_STATE:(g + 1) * SSD_STATE]
            cg = cm[:, g * SSD_STATE:(g + 1) * SSD_STATE]
            bg_t = bg.T
            xg = xs[:, g * half:(g + 1) * half]
            eg = e_acs[:, g * half:(g + 1) * half]
            s_old = s_st[g]
            cb = _bdot(cg, bg_t)
            y_off = _bdot(cg, s_old) * eg
            s_st[g] = eg[q_len - 1:q_len, :] * s_old + _bdot(bg_t, xg * w_state[:, g * half:(g + 1) * half])
            yield
            y_pairs = []
            for pr in range(SSD_HPG // 2):
                ms = []
                for e in (2 * pr, 2 * pr + 1):
                    hd = g * SSD_HPG + e
                    seg = jnp.exp(jnp.where(causal, a_cs[:, hd:hd + 1] - a_cs_t[hd:hd + 1, :], -jnp.inf))
                    ms.append(cb * seg * dt_t[hd:hd + 1, :])
                xp = xg[:, pr * LANE:(pr + 1) * LANE]
                rhs = jnp.concatenate([jnp.where(pair_lo, xp, 0.0), jnp.where(pair_lo, 0.0, xp)], axis=0)
                y_pairs.append(_bdot(jnp.concatenate(ms, axis=-1), rhs))
                yield
            y_groups.append(jnp.concatenate(y_pairs, axis=-1) + y_off)
        y_s = jnp.concatenate(y_groups, axis=-1) + dsk_ref[...] * xs
        emit(0, _group_rmsnorm(y_s * _silu(z_s), snw_ref[...]))

    def mlstm():
        xc = _silu(_conv_perm(mbuf, xm, mcw_ref, mcb_ref))
        yield
        q, k = _blockdiag_tiles(xc, wqk_ref)
        v, = _blockdiag_tiles(xm, wv_ref)
        yield
        gates = _bdot(jnp.concatenate([q, k, v], axis=-1), _wload(wg_ref[...])) + bg_ref[...]
        k = k * (ML_HEAD_DIM ** -0.5)
        yield
        ig = gates[:, :LANE]
        logf = -_softplus(-gates[:, LANE:])
        if front_pad:
            ig = jnp.where(valid, ig, NEG)
            logf = jnp.where(valid, logf, 0.0)
        bcum = _cumsum_rows(logf, tril)
        yield
        ftot = bcum[q_len - 1:q_len, :]
        m_prev = m_st[...]
        w_end = ftot - bcum + ig
        m_new = jnp.maximum(ftot + m_prev, jnp.max(w_end, axis=0, keepdims=True))
        sc = jnp.exp(ftot + m_prev - m_new)
        wexp = jnp.exp(w_end - m_new)
        inter = bcum + m_prev
        bcum_t = bcum.T
        ig_t = ig.T
        m_st[...] = m_new
        yield
        for hd in range(ML_HEADS):
            sl = slice(hd * ML_HEAD_DIM, (hd + 1) * ML_HEAD_DIM)
            q_h, k_h, v_h = q[:, sl], k[:, sl], v[:, sl]
            k_t = k_h.T
            dmat = jnp.where(causal, bcum[:, hd:hd + 1] - bcum_t[hd:hd + 1, :] + ig_t[hd:hd + 1, :], -jnp.inf)
            inter_h = inter[:, hd:hd + 1]
            m_t = jnp.maximum(inter_h, jnp.max(dmat, axis=-1, keepdims=True))
            dexp = jnp.exp(dmat - m_t)
            inter_sc = jnp.exp(inter_h - m_t)
            s = _bdot(q_h, k_t) * dexp
            yield
            ct_old = ct_st[hd]
            n_old = n_st[hd:hd + 1, :]
            num = _bdot(s, v_h) + inter_sc * _bdot(q_h, ct_old)
            den = jnp.sum(s, axis=-1, keepdims=True) + inter_sc * jnp.sum(q_h * n_old, axis=-1, keepdims=True)
            h_h = num / jnp.maximum(jnp.abs(den), jnp.exp(-m_t))
            w_col = wexp[:, hd:hd + 1]
            sc_h = sc[:, hd:hd + 1]
            ct_st[hd] = sc_h * ct_old + _bdot(k_t, v_h * w_col)
            n_st[hd:hd + 1, :] = sc_h * n_old + jnp.sum(k_h * w_col, axis=0, keepdims=True)
            yield
            mu = jnp.mean(h_h, axis=-1, keepdims=True)
            dev = h_h - mu
            var = jnp.mean(dev * dev, axis=-1, keepdims=True)
            h_h = dev * lax.rsqrt(var + EPS) * mnw_ref[:, sl]
            emit(SSD_WIDTH + hd * ML_HEAD_DIM, (h_h + msk_ref[:, sl] * xc[:, sl]) * _silu(z_m[:, sl]))
            yield

    return [ssd(), mlstm()], hn_next, x


def _const_spec(shape):
    nd = len(shape)
    return pl.BlockSpec(shape, lambda b, c: (0,) * nd)


def _state_spec(shape, rows):
    nd = len(shape)
    if rows:
        return pl.BlockSpec((rows,) + shape, lambda b, c: (b,) + (0,) * nd)
    return pl.BlockSpec((1,) + shape, lambda b, c: (0,) * (nd + 1))


def _rows_per_step(bsz, want):
    return want if bsz % want == 0 else 1


L0_STATE_SHAPES = ((TAIL, SSD_CONV_CH), (SSD_HEADS, SSD_HEAD_DIM, SSD_STATE), (TAIL, ML_WIDTH),
                   (ML_HEADS, ML_HEAD_DIM, ML_HEAD_DIM), (ML_HEADS, ML_HEAD_DIM), (1, LANE))
L0_CARRY_SHAPES = ((SUBLANE, SSD_CONV_CH), (SUBLANE, ML_WIDTH), (SSD_GROUPS, SSD_STATE, SSD_WIDTH // SSD_GROUPS),
                   (ML_HEADS, ML_HEAD_DIM, ML_HEAD_DIM), (ML_HEADS, ML_HEAD_DIM), (1, LANE))


def _stream_specs(arrays, n_steps, nc):
    specs = []
    for a in arrays:
        if a.ndim == 2:
            specs.append(pl.BlockSpec(memory_space=pltpu.SMEM))
            continue
        assert a.shape[0] % n_steps == 0
        block = (a.shape[0] // n_steps,) + a.shape[1:]
        specs.append(pl.BlockSpec(block, lambda b, c, nd=a.ndim: (b * nc + c,) + (0,) * (nd - 1)))
    return specs


def _l0_prompt(x, weights, init, front_pad, stream_in=(), stream_out_shapes=()):
    bsz, length, _ = x.shape
    q_len = min(CHUNK, length)
    assert length % q_len == 0
    rows = _rows_per_step(bsz, L0_ROWS)
    assert len(weights) == N_L0_W and len(init) == N_L0_S
    nc = length // q_len
    grid = (bsz // rows, nc)
    n_steps = grid[0] * nc
    last = nc - 1
    x_spec = pl.BlockSpec((rows, q_len, D_MODEL), lambda b, c: (b, c, 0))
    next_spec = pl.BlockSpec((rows, q_len, D_MODEL), lambda b, c: (b, jnp.minimum(c + 1, last), 0))
    stream_outs = [jax.ShapeDtypeStruct(s, F32) for s in stream_out_shapes]
    in_specs = ([x_spec, next_spec] + [_const_spec(w.shape) for w in weights]
                + [_state_spec(s, 0) for s in L0_STATE_SHAPES] + _stream_specs(stream_in, n_steps, nc))
    out_shape = ([jax.ShapeDtypeStruct((bsz, length, D_MODEL), F32)]
                 + [jax.ShapeDtypeStruct((bsz,) + s, F32) for s in L0_STATE_SHAPES] + stream_outs)
    out_specs = ([x_spec] + [_state_spec(s, rows) for s in L0_STATE_SHAPES]
                 + _stream_specs(stream_outs, n_steps, nc))
    row_scratch = L0_CARRY_SHAPES + ((q_len, IN_MIX_PAD),)
    scratch = [pltpu.VMEM(s, F32) for _ in range(rows) for s in row_scratch]
    return pl.pallas_call(
        functools.partial(_l0_prompt_kernel, front_pad=front_pad, rows=rows, stream=bool(stream_in)),
        grid=grid, in_specs=in_specs, out_specs=out_specs, out_shape=out_shape, scratch_shapes=scratch,
        compiler_params=pltpu.CompilerParams(dimension_semantics=("arbitrary", "arbitrary"),
                                             vmem_limit_bytes=L0_VMEM_LIMIT),
        name="l0_prompt",
    )(x, x, *weights, *init, *stream_in)


def _rglru_gates(xc, ra, ix, ba_ref, bx_ref, lam_ref):
    r = _sigmoid(ra + ba_ref[...])
    i = _sigmoid(ix + bx_ref[...])
    log_a = r * (-LRU_C * _softplus(-lam_ref[...]))
    a = jnp.exp(log_a)
    u = jnp.sqrt(1.0 - a * a) * (i * xc)
    return a, u


def _perm_time(n, row=False):
    p = lax.broadcasted_iota(jnp.int32, (1, n) if row else (n, 1), 1 if row else 0)
    return (n // SUBLANE) * (p & (SUBLANE - 1)) + lax.shift_right_logical(p, 3)


def _perm_matrices(n):
    nb = n // SUBLANE
    r = lax.broadcasted_iota(jnp.int32, (n, n), 0)
    c = lax.broadcasted_iota(jnp.int32, (n, n), 1)
    to_perm = jnp.where(c == nb * (r & (SUBLANE - 1)) + lax.shift_right_logical(r, 3), 1.0, 0.0)
    to_time = jnp.where(r == nb * (c & (SUBLANE - 1)) + lax.shift_right_logical(c, 3), 1.0, 0.0)
    return to_perm.astype(BF16), to_time.astype(BF16)


def _move_rows(sel, x_bf16):
    return jnp.dot(sel, x_bf16, preferred_element_type=F32).astype(BF16)


def _conv_perm(tail_ref, x, w_ref, b_ref):
    n, ch = x.shape
    nb = n // SUBLANE
    x3 = x.reshape(nb, SUBLANE, ch)
    tail8 = tail_ref[...]
    sub = lax.broadcasted_iota(jnp.int32, (SUBLANE, ch), 0)
    y = b_ref[...].reshape(1, 1, ch) + w_ref[TAIL:TAIL + 1, :].reshape(1, 1, ch) * x3
    wrapped = [jnp.where(sub >= 1, pltpu.roll(x3[nb - d], 1, 0), tail8[SUBLANE - d:SUBLANE - d + 1, :])
               for d in range(1, CONV_W)]
    for back in range(1, CONV_W):
        head = jnp.stack([wrapped[back - j - 1] for j in range(back)], axis=0)
        shifted = jnp.concatenate([head, x3[:nb - back]], axis=0)
        y = y + w_ref[TAIL - back:TAIL - back + 1, :].reshape(1, 1, ch) * shifted
    for d in range(1, CONV_W):
        tail_ref[SUBLANE - d:SUBLANE - d + 1, :] = x3[nb - d][SUBLANE - 1:SUBLANE, :]
    return y.reshape(n, ch)


def _scan_perm(a, u, h_prev):
    n, ch = a.shape
    nb = n // SUBLANE
    a3 = a.reshape(nb, SUBLANE, ch)
    u3 = u.reshape(nb, SUBLANE, ch)
    local = [u3[0]]
    decay = [a3[0]]
    for j in range(1, nb):
        local.append(a3[j] * local[-1] + u3[j])
        decay.append(a3[j] * decay[-1])
    seg_u, seg_a = local[-1], decay[-1]
    sub = lax.broadcasted_iota(jnp.int32, (SUBLANE, ch), 0)
    shift = 1
    while shift < SUBLANE:
        keep = sub >= shift
        seg_u = seg_u + seg_a * jnp.where(keep, pltpu.roll(seg_u, shift, 0), 0.0)
        seg_a = seg_a * jnp.where(keep, pltpu.roll(seg_a, shift, 0), 1.0)
        shift *= 2
    seg_end = seg_a * h_prev + seg_u
    carry = jnp.where(sub >= 1, pltpu.roll(seg_end, 1, 0), h_prev)
    h3 = jnp.stack([local[j] + decay[j] * carry for j in range(nb)], axis=0)
    return h3.reshape(n, ch), seg_end[SUBLANE - 1:SUBLANE, :]


N_L1_W = 10
N_L1_S = 2


N_STREAM_IN = 5
N_STREAM_OUT = 2


def _l1_prompt_kernel(h_ref, hnext_ref, *refs, front_pad, rows, stream):
    w_refs = refs[:N_L1_W]
    init_refs = refs[N_L1_W:N_L1_W + N_L1_S]
    pos = N_L1_W + N_L1_S
    stream_in = refs[pos:pos + N_STREAM_IN] if stream else ()
    pos += len(stream_in)
    out_refs = refs[pos:pos + N_L1_S + 1]
    pos += N_L1_S + 1
    stream_out = refs[pos:pos + N_STREAM_OUT] if stream else ()
    pos += len(stream_out)
    scratch = refs[pos:]
    per_row = len(scratch) // rows
    c = pl.program_id(1)
    fnw_ref, win_ref, wout_ref = w_refs[1], w_refs[2], w_refs[3]
    q_len = h_ref.shape[1]
    gw = LRU_WIDTH // L1_GROUPS
    pieces = {}

    def each_row(phase):
        return [_l1_prompt_row(h_ref.at[r], hnext_ref.at[r], *w_refs, *init_refs, *(o.at[r] for o in out_refs),
                               *scratch[r * per_row:(r + 1) * per_row], front_pad=front_pad, phase=phase,
                               emit=lambda g, y, r=r: pieces.setdefault(g, {}).__setitem__(r, y))
                for r in range(rows)]

    @pl.when(c == 0)
    def _():
        each_row("init")

    bodies = each_row("body")
    proj_refs = [scratch[r * per_row + per_row - 1] for r in range(rows)]
    _, to_time = _perm_matrices(q_len)
    partials = []

    def in_proj():
        lhs = jnp.concatenate([hn_next for _, hn_next, _ in bodies], axis=0)
        for g in range(L1_GROUPS):
            for lo in (g * gw, LRU_WIDTH + g * gw):
                res = _bdot(lhs, _wload(win_ref[:, lo:lo + gw]))
                for r in range(rows):
                    proj_refs[r][:, lo:lo + gw] = res[r * q_len:(r + 1) * q_len]
            yield

    def out_proj():
        pending = list(range(L1_GROUPS))
        while pending:
            for g in list(pending):
                if len(pieces.get(g, ())) == rows:
                    y_t = jnp.concatenate([_move_rows(to_time, pieces[g][r].astype(BF16)) for r in range(rows)],
                                          axis=0)
                    partials.append(_bdot(y_t, _wload(wout_ref[g * gw // 2:(g + 1) * gw // 2, :])))
                    pending.remove(g)
            yield

    chains = [bodies[r][0][g] for g in range(L1_GROUPS) for r in range(rows)]
    if stream:
        step = pl.program_id(0) * pl.num_programs(1) + c
        chains.append(_mlstm_state_update(*stream_in, *stream_out, base=step * stream_in[1].shape[0]))
    _run_staggered([in_proj()] + chains + [out_proj()])
    total = partials[0]
    for part in partials[1:]:
        total = total + part
    for r, (_, _, h_in) in enumerate(bodies):
        out_refs[0][r] = _rms(h_in + total[r * q_len:(r + 1) * q_len], fnw_ref[...])

    @pl.when(c == pl.num_programs(1) - 1)
    def _():
        each_row("final")


def _l1_in_proj(h_val, nw_ref, win_ref, to_perm):
    hn = _move_rows(to_perm, _rms(h_val, nw_ref[...]).astype(BF16))
    return _bdot(hn, _wload(win_ref[...]))


def _l1_prompt_row(h_ref, hnext_ref, nw_ref, fnw_ref, win_ref, wout_ref, cw_ref, cb_ref,
                   wax_ref, ba_ref, bx_ref, lam_ref, ilc_ref, ilh_ref,
                   y_ref, olc_ref, olh_ref, lbuf, h_st, proj_s, *, front_pad, phase, emit):
    if phase == "init":
        lbuf[...] = jnp.zeros(lbuf.shape, F32)
        lbuf[SUBLANE - TAIL:SUBLANE, :] = ilc_ref[0]
        h_st[...] = ilh_ref[0]
        proj_s[...] = _l1_in_proj(h_ref[...], nw_ref, win_ref, _perm_matrices(h_ref.shape[0])[0])
        return None
    if phase == "final":
        olc_ref[...] = lbuf[SUBLANE - TAIL:SUBLANE, :]
        olh_ref[...] = h_st[...]
        return None

    q_len = h_ref.shape[0]
    h_in = h_ref[...]
    to_perm, _ = _perm_matrices(q_len)
    hn_next = _move_rows(to_perm, _rms(hnext_ref[...], nw_ref[...]).astype(BF16))
    if front_pad:
        valid = _perm_time(q_len) >= front_pad
    gw = LRU_WIDTH // L1_GROUPS
    tiles = gw // LANE
    gates = [proj_s[:, g * gw:(g + 1) * gw] for g in range(L1_GROUPS)]
    xrs = [proj_s[:, LRU_WIDTH + g * gw:LRU_WIDTH + (g + 1) * gw] for g in range(L1_GROUPS)]

    def group(g):
        cg = slice(g * gw, (g + 1) * gw)
        xc = _conv_perm(lbuf.at[:, cg], xrs[g], cw_ref.at[:, cg], cb_ref.at[:, cg])
        ra, ix = _blockdiag_tiles(xc, wax_ref.at[g * tiles:(g + 1) * tiles])
        yield
        a, u = _rglru_gates(xc, ra, ix, ba_ref.at[:, cg], bx_ref.at[:, cg], lam_ref.at[:, cg])
        if front_pad:
            a = jnp.where(valid, a, 1.0)
            u = jnp.where(valid, u, 0.0)
        yield
        h, h_last = _scan_perm(a, u, h_st[:, cg])
        h_st[:, cg] = h_last
        yield
        emit(g, h * _silu(gates[g]))

    return [group(g) for g in range(L1_GROUPS)], hn_next, h_in


def _run_staggered(gens):
    live = []
    pending = list(gens)
    while pending or live:
        if pending:
            live.append(pending.pop(0))
        for gen in list(live):
            if next(gen, "done") == "done":
                live.remove(gen)


L1_STATE_SHAPES = ((TAIL, LRU_WIDTH), (1, LRU_WIDTH))


def _l1_prompt(h1, weights, init, front_pad, stream_in=(), stream_out_shapes=()):
    bsz, length, _ = h1.shape
    q_len = min(CHUNK, length)
    assert length % q_len == 0
    rows = _rows_per_step(bsz, L1_ROWS)
    assert len(weights) == N_L1_W and len(init) == N_L1_S
    nc = length // q_len
    grid = (bsz // rows, nc)
    n_steps = grid[0] * nc
    last = nc - 1
    x_spec = pl.BlockSpec((rows, q_len, D_MODEL), lambda b, c: (b, c, 0))
    next_spec = pl.BlockSpec((rows, q_len, D_MODEL), lambda b, c: (b, jnp.minimum(c + 1, last), 0))
    stream_outs = [jax.ShapeDtypeStruct(s, F32) for s in stream_out_shapes]
    in_specs = ([x_spec, next_spec] + [_const_spec(w.shape) for w in weights]
                + [_state_spec(s, 0) for s in L1_STATE_SHAPES] + _stream_specs(stream_in, n_steps, nc))
    out_shape = ([jax.ShapeDtypeStruct((bsz, length, D_MODEL), F32)]
                 + [jax.ShapeDtypeStruct((bsz,) + s, F32) for s in L1_STATE_SHAPES] + stream_outs)
    out_specs = ([x_spec] + [_state_spec(s, rows) for s in L1_STATE_SHAPES]
                 + _stream_specs(stream_outs, n_steps, nc))
    row_scratch = ((SUBLANE, LRU_WIDTH), (1, LRU_WIDTH), (q_len, 2 * LRU_WIDTH))
    scratch = [pltpu.VMEM(s, F32) for _ in range(rows) for s in row_scratch]
    return pl.pallas_call(
        functools.partial(_l1_prompt_kernel, front_pad=front_pad, rows=rows, stream=bool(stream_in)),
        grid=grid, in_specs=in_specs, out_specs=out_specs, out_shape=out_shape, scratch_shapes=scratch,
        compiler_params=pltpu.CompilerParams(dimension_semantics=("arbitrary", "arbitrary"),
                                             vmem_limit_bytes=VMEM_LIMIT),
        name="l1_prompt",
    )(h1, h1, *weights, *init, *stream_in)


def _conv_step(buf_ref, x, w_ref, b_ref, newbuf_ref):
    y = b_ref[...] + w_ref[3:4, :] * x
    for tap in range(TAIL):
        y = y + w_ref[tap:tap + 1, :] * buf_ref[tap]
    for tap in range(TAIL - 1):
        newbuf_ref[tap] = buf_ref[tap + 1]
    newbuf_ref[TAIL - 1] = x
    return y


def _l0_sample_pre_kernel(x_ref, nw_ref, win_ref, scw_ref, scb_ref, dtb_ref, alog_ref,
                          mcw_ref, mcb_ref, wqk_ref, wv_ref, wg_ref, bg_ref,
                          sbuf_ref, mbuf_ref, m0_ref, n0_ref,
                          nsb_ref, nmb_ref, zs_ref, xs_ref, bm_ref, cm_ref, xdt_t_ref, dec_t_ref,
                          zm_ref, xc_ref, q_ref, isv_t_ref, fs_t_ref, k_ref, mnew_ref, nnew_ref, den_ref):
    x = x_ref[...]
    hn = _rms(x, nw_ref[...])
    proj = _bdot(hn, _wload(win_ref[...]))
    zs_ref[...] = proj[:, OFF_ZS:OFF_ZS + SSD_WIDTH]
    zm_ref[...] = proj[:, OFF_ZM:OFF_ZM + ML_WIDTH]
    xbc = proj[:, OFF_XBC:OFF_XBC + SSD_CONV_CH]
    dt_raw = proj[:, OFF_DT:OFF_DT + LANE]
    xm = proj[:, OFF_XM:OFF_XM + ML_WIDTH]
    expand = _expand_matrix()

    xbc = _silu(_conv_step(sbuf_ref, xbc, scw_ref, scb_ref, nsb_ref))
    xs = xbc[:, :SSD_WIDTH]
    xs_ref[...] = xs
    bm_ref[...] = xbc[:, SSD_WIDTH:SSD_WIDTH + SSD_GROUPS * SSD_STATE]
    cm_ref[...] = xbc[:, SSD_WIDTH + SSD_GROUPS * SSD_STATE:]
    dt = _softplus(dt_raw + dtb_ref[...])
    log_a = -dt * jnp.exp(alog_ref[...])
    xdt_t_ref[...] = xs * _expand_heads(dt, expand)
    dec_t_ref[...] = jnp.exp(log_a)

    xc = _silu(_conv_step(mbuf_ref, xm, mcw_ref, mcb_ref, nmb_ref))
    xc_ref[...] = xc
    q, k, v, ig, logf = _mlstm_qkv_gates(xm, xc, wqk_ref, wv_ref, wg_ref, bg_ref)
    m0 = m0_ref[...]
    m_new = jnp.maximum(logf + m0, ig)
    fs = jnp.exp(logf + m0 - m_new)
    is_ = jnp.exp(ig - m_new)
    mnew_ref[...] = m_new
    r = lax.broadcasted_iota(jnp.int32, (LANE, ML_WIDTH), 0)
    cidx = lax.broadcasted_iota(jnp.int32, (LANE, ML_WIDTH), 1)
    expand_m = jnp.where(lax.shift_right_logical(cidx, 8) == r, 1.0, 0.0).astype(BF16)
    fs_e = _expand_heads(fs, expand_m)
    is_e = _expand_heads(is_, expand_m)
    n_new = fs_e * n0_ref[...] + is_e * k
    nnew_ref[...] = n_new
    q_ref[...] = q
    k_ref[...] = k
    isv_t_ref[...] = is_e * v
    fs_t_ref[...] = fs
    nq = n_new * q
    floor = jnp.exp(-m_new)
    for hd in range(ML_HEADS):
        den = jnp.sum(nq[:, hd * ML_HEAD_DIM:(hd + 1) * ML_HEAD_DIM], axis=-1, keepdims=True)
        den_ref[:, hd:hd + 1] = jnp.maximum(jnp.abs(den), floor[:, hd:hd + 1])


def _rows_to_tile(rows8):
    return jnp.concatenate([rows8] + [jnp.zeros_like(rows8)] * (LANE // SUBLANE - 1), axis=0)


def _ssd_state_update(dec_ref, s_ref, xdt_ref, bm_ref, cm_ref, snew_ref, y_ref, *, base):
    n = s_ref.shape[0]
    half = SSD_WIDTH // SSD_GROUPS
    x_cols = _rows_to_tile(xdt_ref[0]).T
    lane = lax.broadcasted_iota(jnp.int32, (half, LANE), 1)
    accs = [jnp.zeros((half, LANE), F32) for _ in range(SSD_GROUPS)]
    for i in range(n):
        x_col = x_cols[:, i:i + 1].reshape(SSD_HEADS, SSD_HEAD_DIM, 1)
        for g in range(SSD_GROUPS):
            hs = slice(g * SSD_HPG, (g + 1) * SSD_HPG)
            gs = slice(g * SSD_STATE, (g + 1) * SSD_STATE)
            b_row = bm_ref[0, i:i + 1, gs].reshape(1, 1, SSD_STATE)
            decay = jnp.stack([jnp.full((1, 1), dec_ref[base + i, hd], F32)
                               for hd in range(hs.start, hs.stop)], axis=0)
            s_new = decay * s_ref[i, hs] + x_col[hs] * b_row
            snew_ref[i, hs] = s_new
            prod = _bdot_nt(s_new.reshape(half, SSD_STATE), _rows_to_tile(cm_ref[0, :, gs]))
            accs[g] = jnp.where(lane == i, prod, accs[g])
            yield
    y_ref[0] = jnp.concatenate(accs, axis=0).T[:SUBLANE]


def _mlstm_state_update(fs_ref, c_ref, isv_ref, k_ref, q_ref, cnew_ref, num_ref, *, base):
    n = c_ref.shape[0]
    v_cols = _rows_to_tile(isv_ref[0]).T
    lane = lax.broadcasted_iota(jnp.int32, (ML_HEAD_DIM, LANE), 1)
    for hd in range(ML_HEADS):
        sl = slice(hd * ML_HEAD_DIM, (hd + 1) * ML_HEAD_DIM)
        q_rows = _rows_to_tile(q_ref[0, :, sl])
        acc = jnp.zeros((ML_HEAD_DIM, LANE), F32)
        for i in range(n):
            c_new = fs_ref[base + i, hd] * c_ref[i, hd] + v_cols[sl, i:i + 1] * k_ref[0, i:i + 1, sl]
            cnew_ref[i, hd] = c_new
            acc = jnp.where(lane == i, _bdot_nt(c_new, q_rows), acc)
            yield
        num_ref[0, :, sl] = acc.T[:SUBLANE]


def _sample_post_kernel(x_ref, ys_t_ref, num_t_ref, den_ref, zs_ref, xs_ref, zm_ref, xc_ref,
                        dsk_ref, snw_ref, msk_ref, mnw_ref, wout_ref,
                        nw1_ref, fnw_ref, win1_ref, wout1_ref, cw_ref, cb_ref,
                        wax_ref, ba_ref, bx_ref, lam_ref, lbuf_ref, h0_ref,
                        y_ref, nlb_ref, hnew_ref):
    xs = xs_ref[...]
    y_s = ys_t_ref[...] + dsk_ref[...] * xs
    y_s = _group_rmsnorm(y_s * _silu(zs_ref[...]), snw_ref[...])
    num = num_t_ref[...]
    den = den_ref[...]
    h_m = jnp.concatenate(
        [num[:, hd * ML_HEAD_DIM:(hd + 1) * ML_HEAD_DIM] / den[:, hd:hd + 1] for hd in range(ML_HEADS)], axis=-1)
    h_m = _head_layernorm(h_m) * mnw_ref[...]
    y_m = (h_m + msk_ref[...] * xc_ref[...]) * _silu(zm_ref[...])
    h1 = x_ref[...] + _bdot(jnp.concatenate([y_s, y_m], axis=-1), _wload(wout_ref[...]))

    hn = _rms(h1, nw1_ref[...])
    proj = _bdot(hn, _wload(win1_ref[...]))
    gate = proj[:, :LRU_WIDTH]
    xr = proj[:, LRU_WIDTH:]
    xc = _conv_step(lbuf_ref, xr, cw_ref, cb_ref, nlb_ref)
    ra, ix = _blockdiag_tiles(xc, wax_ref)
    a, u = _rglru_gates(xc, ra, ix, ba_ref, bx_ref, lam_ref)
    h = a * h0_ref[...] + u
    hnew_ref[...] = h
    h2 = h1 + _bdot(h * _silu(gate), _wload(wout1_ref[...]))
    y_ref[...] = _rms(h2, fnw_ref[...])


def _full_call(kernel_fn, out_shapes, args, name):
    return pl.pallas_call(
        kernel_fn,
        out_shape=[jax.ShapeDtypeStruct(s, F32) for s in out_shapes],
        compiler_params=pltpu.CompilerParams(vmem_limit_bytes=VMEM_LIMIT),
        name=name,
    )(*args)


def _row(v, width=None):
    v = v.reshape(1, -1).astype(F32)
    if width is not None and v.shape[1] < width:
        v = jnp.pad(v, ((0, 0), (0, width - v.shape[1])))
    return v


PACK_STEPS = 8


def _pack_all(w_in, weights):
    flats = [w_in] + [w.reshape(-1, w.shape[-1]) for w in weights]
    widths = [IN_MIX_PAD] + [f.shape[1] for f in flats[1:]]
    for f in flats:
        assert f.shape[0] % (2 * SUBLANE * PACK_STEPS) == 0
    packed = pl.pallas_call(
        _pack_kernel,
        grid=(PACK_STEPS,),
        in_specs=[pl.BlockSpec((f.shape[0] // PACK_STEPS, f.shape[1]), lambda i: (i, 0)) for f in flats],
        out_specs=[pl.BlockSpec((f.shape[0] // PACK_STEPS // 2, n), lambda i: (i, 0)) for f, n in zip(flats, widths)],
        out_shape=[jax.ShapeDtypeStruct((f.shape[0] // 2, n), jnp.uint32) for f, n in zip(flats, widths)],
        compiler_params=pltpu.CompilerParams(vmem_limit_bytes=VMEM_LIMIT),
        name="pack_weights",
    )(*flats)
    return [packed[0]] + [p.reshape(w.shape[:-2] + (w.shape[-2] // 2, w.shape[-1]))
                          for p, w in zip(packed[1:], weights)]


def _pack_rows(x):
    return pltpu.bitcast(x.astype(BF16), jnp.uint32)


def _pack_kernel(*refs):
    n = len(refs) // 2
    win_ref, wino_ref = refs[0], refs[n]
    wino_ref[:, :OFF_DT] = _pack_rows(win_ref[:, :OFF_DT])
    dt_tile = win_ref[:, OFF_DT:OFF_DT + LANE]
    lane = lax.broadcasted_iota(jnp.int32, dt_tile.shape, 1)
    wino_ref[:, OFF_DT:OFF_ZM] = _pack_rows(jnp.where(lane < SSD_HEADS, dt_tile, 0.0))
    wino_ref[:, OFF_ZM:] = _pack_rows(win_ref[:, OFF_DT + SSD_HEADS:])
    for w_ref, o_ref in zip(refs[1:n], refs[n + 1:]):
        o_ref[...] = _pack_rows(w_ref[...])


def _dense_block_tiles(w):
    nb, bi, bo = w.shape
    per = LANE // bi
    rows = w.reshape(nb // per, per * bi, bo)
    col = jnp.arange(per * bo)
    spread = (col[None, :] % bo == jnp.arange(bo)[:, None]).astype(w.dtype)
    rep = jnp.einsum('tro,oc->trc', rows, spread)
    same_block = (jnp.arange(per * bi)[:, None] // bi) == (col[None, :] // bo)
    return jnp.where(same_block, rep, 0.0)


def kernel(x_prompt, x_sample, state_ssd_conv, state_ssd, state_mlstm_conv, state_mlstm_C, state_mlstm_n,
           state_mlstm_m, state_lru_conv, state_lru_h, meta_tokens, norm_w, final_norm_w, w_in_mix, w_out_mix,
           ssd_conv_w, ssd_conv_b, ssd_dt_bias, ssd_a_log, ssd_d, ssd_norm_w, ml_conv_w, ml_conv_b, ml_wq, ml_wk,
           ml_wv, ml_w_gate, ml_b_gate, ml_skip, ml_norm_w, lru_w_in, lru_w_out, lru_conv_w, lru_conv_b, lru_wa,
           lru_ba, lru_wx, lru_bx, lru_lambda):
    bsz = x_prompt.shape[0]
    dec = x_sample.shape[0]

    wout = w_out_mix[0]
    nw0 = _row(norm_w[0])
    nw1 = _row(norm_w[1])
    fnw = _row(final_norm_w)
    scw = ssd_conv_w[0]
    scb = _row(ssd_conv_b[0])
    dtb = _row(ssd_dt_bias[0], LANE)
    alog = _row(ssd_a_log[0], LANE)
    dsk = _row(jnp.repeat(ssd_d[0], SSD_HEAD_DIM))
    snw = _row(ssd_norm_w[0])
    mcw = ml_conv_w[0]
    mcb = _row(ml_conv_b[0])
    wqk = jnp.concatenate([_dense_block_tiles(ml_wq[0]), _dense_block_tiles(ml_wk[0])], axis=2)
    wv = _dense_block_tiles(ml_wv[0])
    wg_raw = ml_w_gate[0]
    wg = jnp.concatenate([jnp.pad(wg_raw[:, :ML_HEADS], ((0, 0), (0, LANE - ML_HEADS))),
                          jnp.pad(wg_raw[:, ML_HEADS:], ((0, 0), (0, LANE - ML_HEADS)))], axis=1)
    bg = jnp.concatenate([_row(ml_b_gate[0, :ML_HEADS], LANE), _row(ml_b_gate[0, ML_HEADS:], LANE)], axis=1)
    msk = _row(ml_skip[0])
    mnw = _row(ml_norm_w[0])
    win1 = lru_w_in[0]
    wout1 = lru_w_out[0]
    lcw = lru_conv_w[0]
    lcb = _row(lru_conv_b[0])
    wax = jnp.concatenate([lru_wa[0], lru_wx[0]], axis=2)
    r_idx = lax.broadcasted_iota(jnp.int32, (LANE, SSD_WIDTH), 0)
    c_idx = lax.broadcasted_iota(jnp.int32, (LANE, SSD_WIDTH), 1)
    expand = (c_idx // SSD_HEAD_DIM == r_idx).astype(F32)
    ba = _row(lru_ba[0])
    bx = _row(lru_bx[0])
    lam = _row(lru_lambda[0])

    win, wout, wqk, wv, wg, expand, win1, wout1, wax = _pack_all(
        w_in_mix[0], [wout, wqk, wv, wg, expand, win1, wout1, wax])
    l0_w = (nw0, win, wout, scw, scb, dtb, alog, dsk, snw, mcw, mcb, wqk, wv, wg, bg, msk, mnw, expand)
    l1_w = (nw1, fnw, win1, wout1, lcw, lcb, wax, ba, bx, lam)

    xs2 = x_sample[:, 0]
    sbuf = jnp.moveaxis(state_ssd_conv[0], 1, 0)
    mbuf = jnp.moveaxis(state_mlstm_conv[0], 1, 0)
    lbuf = jnp.moveaxis(state_lru_conv[0], 1, 0)
    m0 = jnp.pad(state_mlstm_m[0], ((0, 0), (0, LANE - ML_HEADS)))
    n0 = state_mlstm_n[0].reshape(dec, ML_WIDTH)
    pre_shapes = ((TAIL, dec, SSD_CONV_CH), (TAIL, dec, ML_WIDTH), (dec, SSD_WIDTH), (dec, SSD_WIDTH),
                  (dec, SSD_GROUPS * SSD_STATE), (dec, SSD_GROUPS * SSD_STATE), (dec, SSD_WIDTH), (dec, LANE),
                  (dec, ML_WIDTH), (dec, ML_WIDTH), (dec, ML_WIDTH), (dec, ML_WIDTH), (dec, LANE),
                  (dec, ML_WIDTH), (dec, LANE), (dec, ML_WIDTH), (dec, ML_HEADS))
    (nsb, nmb, zs, xs_c, bm, cm, xdt_t, dec_t, zm, xc_m, q, isv_t, fs_t, k, m_new, n_new, den) = _full_call(
        _l0_sample_pre_kernel, pre_shapes,
        (xs2, nw0, win, scw, scb, dtb, alog, mcw, mcb, wqk, wv, wg, bg, sbuf, mbuf, m0, n0), "l0_sample_pre")

    zero0 = tuple(jnp.zeros((1,) + s, F32) for s in L0_STATE_SHAPES)
    zero1 = tuple(jnp.zeros((1,) + s, F32) for s in L1_STATE_SHAPES)
    meta = jnp.pad(meta_tokens.astype(F32), ((CHUNK - N_META, 0), (0, 0)))[None]
    meta_out = _l0_prompt(meta, l0_w, zero0, CHUNK - N_META)
    meta1_out = _l1_prompt(meta_out[0], l1_w, zero1, CHUNK - N_META)

    n_steps = (bsz // _rows_per_step(bsz, L0_ROWS)) * (x_prompt.shape[1] // CHUNK)
    assert n_steps == (bsz // _rows_per_step(bsz, L1_ROWS)) * (x_prompt.shape[1] // CHUNK)
    per_step = dec // n_steps
    assert per_step * n_steps == dec

    def step_rows(a):
        a = a.reshape(n_steps, per_step, a.shape[-1])
        return jnp.pad(a, ((0, 0), (0, SUBLANE - per_step), (0, 0)))

    def from_step_rows(a):
        return a[:, :per_step].reshape(dec, a.shape[-1])

    l0_out = _l0_prompt(
        x_prompt, l0_w, tuple(meta_out[1:]), 0,
        stream_in=(dec_t[:, :SSD_HEADS], state_ssd[0], step_rows(xdt_t), step_rows(bm), step_rows(cm)),
        stream_out_shapes=((dec, SSD_HEADS, SSD_HEAD_DIM, SSD_STATE), (n_steps, SUBLANE, SSD_WIDTH)))
    h1_p, p_sc, p_s, p_mc, p_c, p_n, p_m, s_new, ys_r = l0_out
    y_prompt, p_lc, p_lh, c_new, num_r = _l1_prompt(
        h1_p, l1_w, tuple(meta1_out[1:]), 0,
        stream_in=(fs_t[:, :ML_HEADS], state_mlstm_C[0], step_rows(isv_t), step_rows(k), step_rows(q)),
        stream_out_shapes=((dec, ML_HEADS, ML_HEAD_DIM, ML_HEAD_DIM), (n_steps, SUBLANE, ML_WIDTH)))

    p_m = p_m[:, 0, :ML_HEADS]
    p_lh = p_lh[:, 0]

    post_shapes = ((dec, D_MODEL), (TAIL, dec, LRU_WIDTH), (dec, LRU_WIDTH))
    y_s2, nlb, h_new = _full_call(
        _sample_post_kernel, post_shapes,
        (xs2, from_step_rows(ys_r), from_step_rows(num_r), den, zs, xs_c, zm, xc_m, dsk, snw, msk, mnw, wout,
         nw1, fnw, win1, wout1, lcw, lcb, wax, ba, bx, lam, lbuf, state_lru_h[0]), "sample_post")

    s_sc = jnp.moveaxis(nsb, 0, 1)[None]
    s_mc = jnp.moveaxis(nmb, 0, 1)[None]
    s_lc = jnp.moveaxis(nlb, 0, 1)[None]
    return (y_prompt, y_s2[:, None, :],
            p_sc[None], p_s[None], p_mc[None], p_c[None], p_n[None], p_m[None], p_lc[None], p_lh[None],
            s_sc, s_new[None], s_mc, c_new[None], n_new.reshape(dec, ML_HEADS, ML_HEAD_DIM)[None],
            m_new[:, :ML_HEADS][None], s_lc, h_new[None])
```

```python
import functools

import jax
import jax.numpy as jnp
from jax import lax
from jax.experimental import pallas as pl
from jax.experimental.pallas import tpu as pltpu

F32 = jnp.float32
BF16 = jnp.bfloat16

D_MODEL = 1024
N_META = 16
CONV_W = 4
EPS = 1e-6
NEG = -1e30
SSD_WIDTH = 1024
SSD_HEAD_DIM = 64
SSD_HEADS = 16
SSD_GROUPS = 2
SSD_HPG = 8
SSD_STATE = 128
SSD_CONV_CH = 1536
ML_WIDTH = 1024
ML_HEADS = 4
ML_HEAD_DIM = 256
ML_QKV_BLOCK = 4
LRU_WIDTH = 2048
LRU_BLOCKS = 16
LRU_BLOCK = 128
LRU_C = 8.0

LANE = 128
SUBLANE = 8
CHUNK = 128
L0_ROWS = 2
L1_ROWS = 2
L1_GROUPS = 8
TAIL = CONV_W - 1

OFF_ZS = 0
OFF_XBC = OFF_ZS + SSD_WIDTH
OFF_DT = OFF_XBC + SSD_CONV_CH
OFF_ZM = OFF_DT + LANE
OFF_XM = OFF_ZM + ML_WIDTH
IN_MIX_PAD = OFF_XM + ML_WIDTH

VMEM_LIMIT = 56 * 1024 * 1024
L0_VMEM_LIMIT = 61 * 1024 * 1024


def _sigmoid(x):
    return 1.0 / (1.0 + jnp.exp(-x))


def _silu(x):
    return x * _sigmoid(x)


def _softplus(x):
    return jnp.maximum(x, 0.0) + jnp.log1p(jnp.exp(-jnp.abs(x)))


def _rms(x, w):
    return x * lax.rsqrt(jnp.mean(x * x, axis=-1, keepdims=True) + EPS) * w


def _bdot(a, b):
    return jnp.dot(a.astype(BF16), b.astype(BF16), preferred_element_type=F32)


def _bdot_nt(a, b):
    return lax.dot_general(a.astype(BF16), b.astype(BF16), (((1,), (1,)), ((), ())), preferred_element_type=F32)


def _wload(w):
    return pltpu.bitcast(w, BF16)


def _split3(x):
    hi = x.astype(BF16)
    r = x - hi.astype(F32)
    mid = r.astype(BF16)
    lo = (r - mid.astype(F32)).astype(BF16)
    return hi, mid, lo


def _cumsum_rows(x, tril):
    hi, mid, lo = _split3(x)
    d = functools.partial(jnp.dot, preferred_element_type=F32)
    return d(tril, hi) + d(tril, mid) + d(tril, lo)


def _expand_heads(x, expand):
    hi, mid, _ = _split3(x)
    d = functools.partial(jnp.dot, preferred_element_type=F32)
    return d(hi, expand) + d(mid, expand)


def _expand_matrix():
    r = lax.broadcasted_iota(jnp.int32, (LANE, SSD_WIDTH), 0)
    c = lax.broadcasted_iota(jnp.int32, (LANE, SSD_WIDTH), 1)
    return jnp.where(lax.shift_right_logical(c, 6) == r, 1.0, 0.0).astype(BF16)


def _blockdiag_tiles(x, w_ref):
    k = w_ref.shape[0]
    m = w_ref.shape[2] // LANE
    prods = [_bdot(x[:, t * LANE:(t + 1) * LANE], _wload(w_ref[t])) for t in range(k)]
    return [jnp.concatenate([p[:, j * LANE:(j + 1) * LANE] for p in prods], axis=-1) for j in range(m)]


def _group_rmsnorm(y, w):
    half = SSD_WIDTH // SSD_GROUPS
    parts = []
    for g in range(SSD_GROUPS):
        yg = y[:, g * half:(g + 1) * half]
        parts.append(yg * lax.rsqrt(jnp.mean(yg * yg, axis=-1, keepdims=True) + EPS))
    return jnp.concatenate(parts, axis=-1) * w


def _head_layernorm(h):
    parts = []
    for k in range(ML_HEADS):
        hk = h[:, k * ML_HEAD_DIM:(k + 1) * ML_HEAD_DIM]
        mu = jnp.mean(hk, axis=-1, keepdims=True)
        d = hk - mu
        var = jnp.mean(d * d, axis=-1, keepdims=True)
        parts.append(d * lax.rsqrt(var + EPS))
    return jnp.concatenate(parts, axis=-1)


def _mlstm_qkv_gates(xm, xc, wqk_ref, wv_ref, wg_ref, bg_ref):
    q, k = _blockdiag_tiles(xc, wqk_ref)
    v, = _blockdiag_tiles(xm, wv_ref)
    gates = _bdot(jnp.concatenate([q, k, v], axis=-1), _wload(wg_ref[...])) + bg_ref[...]
    ig = gates[:, :LANE]
    logf = -_softplus(-gates[:, LANE:])
    return q, k * (ML_HEAD_DIM ** -0.5), v, ig, logf


N_L0_W = 18
N_L0_S = 6


def _l0_prompt_kernel(x_ref, xnext_ref, *refs, front_pad, rows, stream):
    w_refs = refs[:N_L0_W]
    init_refs = refs[N_L0_W:N_L0_W + N_L0_S]
    pos = N_L0_W + N_L0_S
    stream_in = refs[pos:pos + N_STREAM_IN] if stream else ()
    pos += len(stream_in)
    out_refs = refs[pos:pos + N_L0_S + 1]
    pos += N_L0_S + 1
    stream_out = refs[pos:pos + N_STREAM_OUT] if stream else ()
    pos += len(stream_out)
    scratch = refs[pos:]
    per_row = len(scratch) // rows
    c = pl.program_id(1)
    win_ref, wout_ref = w_refs[1], w_refs[2]
    q_len = x_ref.shape[1]
    pieces = {}

    def each_row(phase):
        return [_l0_prompt_row(x_ref.at[r], xnext_ref.at[r], *w_refs, *init_refs, *(o.at[r] for o in out_refs),
                               *scratch[r * per_row:(r + 1) * per_row], front_pad=front_pad, phase=phase,
                               emit=lambda k0, y, r=r: pieces.setdefault(k0, {}).__setitem__(r, y))
                for r in range(rows)]

    @pl.when(c == 0)
    def _():
        each_row("init")

    bodies = each_row("body")
    proj_refs = [scratch[r * per_row + per_row - 1] for r in range(rows)]
    _, to_time = _perm_matrices(q_len)
    partials = []

    def in_proj():
        lhs = jnp.concatenate([hn_next for _, hn_next, _ in bodies], axis=0)
        for lo, hi in L0_PROJ_PIECES:
            res = _bdot(lhs, _wload(win_ref[:, lo:hi]))
            for r in range(rows):
                proj_refs[r][:, lo:hi] = res[r * q_len:(r + 1) * q_len]
            yield

    def out_proj():
        half = SSD_WIDTH // SSD_GROUPS
        pending = ([(g * half, half) for g in range(SSD_GROUPS)]
                   + [(SSD_WIDTH + hd * ML_HEAD_DIM, ML_HEAD_DIM) for hd in range(ML_HEADS)])
        while pending:
            for k0, width in list(pending):
                if len(pieces.get(k0, ())) == rows:
                    y_t = jnp.concatenate([_move_rows(to_time, pieces[k0][r].astype(BF16)) for r in range(rows)],
                                          axis=0)
                    partials.append(_bdot(y_t, _wload(wout_ref[k0 // 2:(k0 + width) // 2, :])))
                    pending.remove((k0, width))
            yield

    chains = [gen for gens, _, _ in bodies for gen in gens] + [in_proj(), out_proj()]
    if stream:
        step = pl.program_id(0) * pl.num_programs(1) + c
        chains.append(_ssd_state_update(*stream_in, *stream_out, base=step * stream_in[1].shape[0]))
    _run_round_robin(chains)
    total = partials[0]
    for part in partials[1:]:
        total = total + part
    for r, (_, _, x) in enumerate(bodies):
        h1 = x + total[r * q_len:(r + 1) * q_len]
        if front_pad:
            h1 = jnp.where(lax.broadcasted_iota(jnp.int32, (q_len, 1), 0) >= front_pad, h1, 0.0)
        out_refs[0][r] = h1

    @pl.when(c == pl.num_programs(1) - 1)
    def _():
        each_row("final")


L0_PROJ_PIECES = ((OFF_XBC, OFF_ZM), (OFF_XM, IN_MIX_PAD), (OFF_ZM, OFF_XM), (OFF_ZS, OFF_XBC))


def _run_round_robin(gens):
    live = list(gens)
    while live:
        for gen in list(live):
            step = next(gen, "done")
            if step == "done":
                live.remove(gen)
            elif step is not None:
                live.extend(step)


def _l0_prompt_row(x_ref, xnext_ref, nw_ref, win_ref, wout_ref,
                   scw_ref, scb_ref, dtb_ref, alog_ref, dsk_ref, snw_ref,
                   mcw_ref, mcb_ref, wqk_ref, wv_ref, wg_ref, bg_ref, msk_ref, mnw_ref, expand_ref,
                   isc_ref, iss_ref, imc_ref, ict_ref, inn_ref, imm_ref,
                   h1_ref, osc_ref, oss_ref, omc_ref, oct_ref, onn_ref, omm_ref,
                   sbuf, mbuf, s_st, ct_st, n_st, m_st, proj_s, *, front_pad, phase, emit):
    q_len = x_ref.shape[0]

    if phase == "init":
        sbuf[...] = jnp.zeros(sbuf.shape, F32)
        mbuf[...] = jnp.zeros(mbuf.shape, F32)
        sbuf[SUBLANE - TAIL:SUBLANE, :] = isc_ref[0]
        mbuf[SUBLANE - TAIL:SUBLANE, :] = imc_ref[0]
        for g in range(SSD_GROUPS):
            heads = iss_ref[0, g * SSD_HPG:(g + 1) * SSD_HPG]
            s_st[g] = heads.reshape(SSD_HPG * SSD_HEAD_DIM, SSD_STATE).T
        for hd in range(ML_HEADS):
            ct_st[hd] = ict_ref[0, hd].T
        n_st[...] = inn_ref[0]
        m_st[...] = imm_ref[0]
        hn0 = _move_rows(_perm_matrices(q_len)[0], _rms(x_ref[...], nw_ref[...]).astype(BF16))
        proj_s[...] = _bdot(hn0, _wload(win_ref[...]))
        return None
    if phase == "final":
        osc_ref[...] = sbuf[SUBLANE - TAIL:SUBLANE, :]
        omc_ref[...] = mbuf[SUBLANE - TAIL:SUBLANE, :]
        for g in range(SSD_GROUPS):
            oss_ref[g * SSD_HPG:(g + 1) * SSD_HPG] = s_st[g].T.reshape(SSD_HPG, SSD_HEAD_DIM, SSD_STATE)
        for hd in range(ML_HEADS):
            oct_ref[hd] = ct_st[hd].T
        onn_ref[...] = n_st[...]
        omm_ref[...] = m_st[...]
        return None

    x = x_ref[...]
    to_perm, to_time = _perm_matrices(q_len)
    t_col = _perm_time(q_len)
    t_row = _perm_time(q_len, row=True)
    causal = t_col >= t_row
    tril = jnp.where(causal, 1.0, 0.0).astype(BF16)
    valid = (t_col >= front_pad) if front_pad else None
    hn_next = _move_rows(to_perm, _rms(xnext_ref[...], nw_ref[...]).astype(BF16))
    xbc_raw = proj_s[:, OFF_XBC:OFF_XBC + SSD_CONV_CH]
    dt_raw = proj_s[:, OFF_DT:OFF_DT + LANE]
    xm = proj_s[:, OFF_XM:OFF_XM + ML_WIDTH]
    z_s = proj_s[:, OFF_ZS:OFF_ZS + SSD_WIDTH]
    z_m = proj_s[:, OFF_ZM:OFF_ZM + ML_WIDTH]

    def ssd():
        xbc = _silu(_conv_perm(sbuf, xbc_raw, scw_ref, scb_ref))
        yield
        xs = xbc[:, :SSD_WIDTH]
        bm = xbc[:, SSD_WIDTH:SSD_WIDTH + SSD_GROUPS * SSD_STATE]
        cm = xbc[:, SSD_WIDTH + SSD_GROUPS * SSD_STATE:]
        dt = _softplus(dt_raw + dtb_ref[...])
        if front_pad:
            dt = jnp.where(valid, dt, 0.0)
        log_a = -dt * jnp.exp(alog_ref[...])
        a_cs = _cumsum_rows(log_a, tril)
        yield
        a_last = a_cs[q_len - 1:q_len, :]
        expand = _wload(expand_ref[...])
        w_state = _expand_heads(dt * jnp.exp(a_last - a_cs), expand)
        e_acs = _expand_heads(jnp.exp(a_cs), expand)
        a_cs_t = a_cs.T
        dt_t = dt.T
        yield
        pair_lo = lax.broadcasted_iota(jnp.int32, (q_len, LANE), 1) < SSD_HEAD_DIM
        half = SSD_WIDTH // SSD_GROUPS

        def group(g):
            cols = slice(g * half, (g + 1) * half)
            bg = bm[:, g * SSD_STATE:(g + 1) * SSD_STATE]
            cg = cm[:, g * SSD_STATE:(g + 1) * SSD_STATE]
            bg_t = bg.T
            xg = xs[:, cols]
            eg = e_acs[:, cols]
            s_old = s_st[g]
            cb = _bdot(cg, bg_t)
            y_off = _bdot(cg, s_old) * eg
            s_st[g] = eg[q_len - 1:q_len, :] * s_old + _bdot(bg_t, xg * w_state[:, cols])
            yield
            y_pairs = []
            for pr in range(SSD_HPG // 2):
                ms = []
                for e in (2 * pr, 2 * pr + 1):
                    hd = g * SSD_HPG + e
                    seg = jnp.exp(jnp.where(causal, a_cs[:, hd:hd + 1] - a_cs_t[hd:hd + 1, :], -jnp.inf))
                    ms.append(cb * seg * dt_t[hd:hd + 1, :])
                xp = xg[:, pr * LANE:(pr + 1) * LANE]
                rhs = jnp.concatenate([jnp.where(pair_lo, xp, 0.0), jnp.where(pair_lo, 0.0, xp)], axis=0)
                y_pairs.append(_bdot(jnp.concatenate(ms, axis=-1), rhs))
                yield
            y_g = (jnp.concatenate(y_pairs, axis=-1) + y_off + dsk_ref[:, cols] * xg) * _silu(z_s[:, cols])
            y_g = y_g * lax.rsqrt(jnp.mean(y_g * y_g, axis=-1, keepdims=True) + EPS)
            emit(g * half, y_g * snw_ref[:, cols])

        yield [group(g) for g in range(SSD_GROUPS)]

    def mlstm():
        xc = _silu(_conv_perm(mbuf, xm, mcw_ref, mcb_ref))
        yield
        q, k = _blockdiag_tiles(xc, wqk_ref)
        v, = _blockdiag_tiles(xm, wv_ref)
        yield
        gates = _bdot(jnp.concatenate([q, k, v], axis=-1), _wload(wg_ref[...])) + bg_ref[...]
        k = k * (ML_HEAD_DIM ** -0.5)
        yield
        ig = gates[:, :LANE]
        logf = -_softplus(-gates[:, LANE:])
        if front_pad:
            ig = jnp.where(valid, ig, NEG)
            logf = jnp.where(valid, logf, 0.0)
        bcum = _cumsum_rows(logf, tril)
        yield
        ftot = bcum[q_len - 1:q_len, :]
        m_prev = m_st[...]
        w_end = ftot - bcum + ig
        m_new = jnp.maximum(ftot + m_prev, jnp.max(w_end, axis=0, keepdims=True))
        sc = jnp.exp(ftot + m_prev - m_new)
        wexp = jnp.exp(w_end - m_new)
        inter = bcum + m_prev
        bcum_t = bcum.T
        ig_t = ig.T
        m_st[...] = m_new

        def head(hd):
            sl = slice(hd * ML_HEAD_DIM, (hd + 1) * ML_HEAD_DIM)
            q_h, k_h, v_h = q[:, sl], k[:, sl], v[:, sl]
            k_t = k_h.T
            dmat = jnp.where(causal, bcum[:, hd:hd + 1] - bcum_t[hd:hd + 1, :] + ig_t[hd:hd + 1, :], -jnp.inf)
            inter_h = inter[:, hd:hd + 1]
            m_t = jnp.maximum(inter_h, jnp.max(dmat, axis=-1, keepdims=True))
            dexp = jnp.exp(dmat - m_t)
            inter_sc = jnp.exp(inter_h - m_t)
            s = _bdot(q_h, k_t) * dexp
            yield
            ct_old = ct_st[hd]
            n_old = n_st[hd:hd + 1, :]
            num = _bdot(s, v_h) + inter_sc * _bdot(q_h, ct_old)
            den = jnp.sum(s, axis=-1, keepdims=True) + inter_sc * jnp.sum(q_h * n_old, axis=-1, keepdims=True)
            h_h = num / jnp.maximum(jnp.abs(den), jnp.exp(-m_t))
            w_col = wexp[:, hd:hd + 1]
            sc_h = sc[:, hd:hd + 1]
            ct_st[hd] = sc_h * ct_old + _bdot(k_t, v_h * w_col)
            n_st[hd:hd + 1, :] = sc_h * n_old + jnp.sum(k_h * w_col, axis=0, keepdims=True)
            yield
            mu = jnp.mean(h_h, axis=-1, keepdims=True)
            dev = h_h - mu
            var = jnp.mean(dev * dev, axis=-1, keepdims=True)
            h_h = dev * lax.rsqrt(var + EPS) * mnw_ref[:, sl]
            emit(SSD_WIDTH + hd * ML_HEAD_DIM, (h_h + msk_ref[:, sl] * xc[:, sl]) * _silu(z_m[:, sl]))

        yield [head(hd) for hd in range(ML_HEADS)]

    return [ssd(), mlstm()], hn_next, x


def _const_spec(shape):
    nd = len(shape)
    return pl.BlockSpec(shape, lambda b, c: (0,) * nd)


def _state_spec(shape, rows):
    nd = len(shape)
    if rows:
        return pl.BlockSpec((rows,) + shape, lambda b, c: (b,) + (0,) * nd)
    return pl.BlockSpec((1,) + shape, lambda b, c: (0,) * (nd + 1))


def _rows_per_step(bsz, want):
    return want if bsz % want == 0 else 1


L0_STATE_SHAPES = ((TAIL, SSD_CONV_CH), (SSD_HEADS, SSD_HEAD_DIM, SSD_STATE), (TAIL, ML_WIDTH),
                   (ML_HEADS, ML_HEAD_DIM, ML_HEAD_DIM), (ML_HEADS, ML_HEAD_DIM), (1, LANE))
L0_CARRY_SHAPES = ((SUBLANE, SSD_CONV_CH), (SUBLANE, ML_WIDTH), (SSD_GROUPS, SSD_STATE, SSD_WIDTH // SSD_GROUPS),
                   (ML_HEADS, ML_HEAD_DIM, ML_HEAD_DIM), (ML_HEADS, ML_HEAD_DIM), (1, LANE))


def _stream_specs(arrays, n_steps, nc):
    specs = []
    for a in arrays:
        if a.ndim == 2:
            specs.append(pl.BlockSpec(memory_space=pltpu.SMEM))
            continue
        assert a.shape[0] % n_steps == 0
        block = (a.shape[0] // n_steps,) + a.shape[1:]
        specs.append(pl.BlockSpec(block, lambda b, c, nd=a.ndim: (b * nc + c,) + (0,) * (nd - 1)))
    return specs


def _l0_prompt(x, weights, init, front_pad, stream_in=(), stream_out_shapes=()):
    bsz, length, _ = x.shape
    q_len = min(CHUNK, length)
    assert length % q_len == 0
    rows = _rows_per_step(bsz, L0_ROWS)
    assert len(weights) == N_L0_W and len(init) == N_L0_S
    nc = length // q_len
    grid = (bsz // rows, nc)
    n_steps = grid[0] * nc
    last = nc - 1
    x_spec = pl.BlockSpec((rows, q_len, D_MODEL), lambda b, c: (b, c, 0))
    next_spec = pl.BlockSpec((rows, q_len, D_MODEL), lambda b, c: (b, jnp.minimum(c + 1, last), 0))
    stream_outs = [jax.ShapeDtypeStruct(s, F32) for s in stream_out_shapes]
    in_specs = ([x_spec, next_spec] + [_const_spec(w.shape) for w in weights]
                + [_state_spec(s, 0) for s in L0_STATE_SHAPES] + _stream_specs(stream_in, n_steps, nc))
    out_shape = ([jax.ShapeDtypeStruct((bsz, length, D_MODEL), F32)]
                 + [jax.ShapeDtypeStruct((bsz,) + s, F32) for s in L0_STATE_SHAPES] + stream_outs)
    out_specs = ([x_spec] + [_state_spec(s, rows) for s in L0_STATE_SHAPES]
                 + _stream_specs(stream_outs, n_steps, nc))
    row_scratch = L0_CARRY_SHAPES + ((q_len, IN_MIX_PAD),)
    scratch = [pltpu.VMEM(s, F32) for _ in range(rows) for s in row_scratch]
    return pl.pallas_call(
        functools.partial(_l0_prompt_kernel, front_pad=front_pad, rows=rows, stream=bool(stream_in)),
        grid=grid, in_specs=in_specs, out_specs=out_specs, out_shape=out_shape, scratch_shapes=scratch,
        compiler_params=pltpu.CompilerParams(dimension_semantics=("arbitrary", "arbitrary"),
                                             vmem_limit_bytes=L0_VMEM_LIMIT),
        name="l0_prompt",
    )(x, x, *weights, *init, *stream_in)


def _rglru_gates(xc, ra, ix, ba_ref, bx_ref, lam_ref):
    r = _sigmoid(ra + ba_ref[...])
    i = _sigmoid(ix + bx_ref[...])
    log_a = r * (-LRU_C * _softplus(-lam_ref[...]))
    a = jnp.exp(log_a)
    u = jnp.sqrt(1.0 - a * a) * (i * xc)
    return a, u


def _perm_time(n, row=False):
    p = lax.broadcasted_iota(jnp.int32, (1, n) if row else (n, 1), 1 if row else 0)
    return (n // SUBLANE) * (p & (SUBLANE - 1)) + lax.shift_right_logical(p, 3)


def _perm_matrices(n):
    nb = n // SUBLANE
    r = lax.broadcasted_iota(jnp.int32, (n, n), 0)
    c = lax.broadcasted_iota(jnp.int32, (n, n), 1)
    to_perm = jnp.where(c == nb * (r & (SUBLANE - 1)) + lax.shift_right_logical(r, 3), 1.0, 0.0)
    to_time = jnp.where(r == nb * (c & (SUBLANE - 1)) + lax.shift_right_logical(c, 3), 1.0, 0.0)
    return to_perm.astype(BF16), to_time.astype(BF16)


def _move_rows(sel, x_bf16):
    return jnp.dot(sel, x_bf16, preferred_element_type=F32).astype(BF16)


def _conv_perm(tail_ref, x, w_ref, b_ref):
    n, ch = x.shape
    nb = n // SUBLANE
    x3 = x.reshape(nb, SUBLANE, ch)
    tail8 = tail_ref[...]
    sub = lax.broadcasted_iota(jnp.int32, (SUBLANE, ch), 0)
    y = b_ref[...].reshape(1, 1, ch) + w_ref[TAIL:TAIL + 1, :].reshape(1, 1, ch) * x3
    wrapped = [jnp.where(sub >= 1, pltpu.roll(x3[nb - d], 1, 0), tail8[SUBLANE - d:SUBLANE - d + 1, :])
               for d in range(1, CONV_W)]
    for back in range(1, CONV_W):
        head = jnp.stack([wrapped[back - j - 1] for j in range(back)], axis=0)
        shifted = jnp.concatenate([head, x3[:nb - back]], axis=0)
        y = y + w_ref[TAIL - back:TAIL - back + 1, :].reshape(1, 1, ch) * shifted
    for d in range(1, CONV_W):
        tail_ref[SUBLANE - d:SUBLANE - d + 1, :] = x3[nb - d][SUBLANE - 1:SUBLANE, :]
    return y.reshape(n, ch)


def _scan_perm(a, u, h_prev):
    n, ch = a.shape
    nb = n // SUBLANE
    a3 = a.reshape(nb, SUBLANE, ch)
    u3 = u.reshape(nb, SUBLANE, ch)
    local = [u3[0]]
    decay = [a3[0]]
    for j in range(1, nb):
        local.append(a3[j] * local[-1] + u3[j])
        decay.append(a3[j] * decay[-1])
    seg_u, seg_a = local[-1], decay[-1]
    sub = lax.broadcasted_iota(jnp.int32, (SUBLANE, ch), 0)
    shift = 1
    while shift < SUBLANE:
        keep = sub >= shift
        seg_u = seg_u + seg_a * jnp.where(keep, pltpu.roll(seg_u, shift, 0), 0.0)
        seg_a = seg_a * jnp.where(keep, pltpu.roll(seg_a, shift, 0), 1.0)
        shift *= 2
    seg_end = seg_a * h_prev + seg_u
    carry = jnp.where(sub >= 1, pltpu.roll(seg_end, 1, 0), h_prev)
    h3 = jnp.stack([local[j] + decay[j] * carry for j in range(nb)], axis=0)
    return h3.reshape(n, ch), seg_end[SUBLANE - 1:SUBLANE, :]


N_L1_W = 10
N_L1_S = 2


N_STREAM_IN = 5
N_STREAM_OUT = 2


def _l1_prompt_kernel(h_ref, hnext_ref, *refs, front_pad, rows, stream):
    w_refs = refs[:N_L1_W]
    init_refs = refs[N_L1_W:N_L1_W + N_L1_S]
    pos = N_L1_W + N_L1_S
    stream_in = refs[pos:pos + N_STREAM_IN] if stream else ()
    pos += len(stream_in)
    out_refs = refs[pos:pos + N_L1_S + 1]
    pos += N_L1_S + 1
    stream_out = refs[pos:pos + N_STREAM_OUT] if stream else ()
    pos += len(stream_out)
    scratch = refs[pos:]
    per_row = len(scratch) // rows
    c = pl.program_id(1)
    fnw_ref, win_ref, wout_ref = w_refs[1], w_refs[2], w_refs[3]
    q_len = h_ref.shape[1]
    gw = LRU_WIDTH // L1_GROUPS
    pieces = {}

    def each_row(phase):
        return [_l1_prompt_row(h_ref.at[r], hnext_ref.at[r], *w_refs, *init_refs, *(o.at[r] for o in out_refs),
                               *scratch[r * per_row:(r + 1) * per_row], front_pad=front_pad, phase=phase,
                               emit=lambda g, y, r=r: pieces.setdefault(g, {}).__setitem__(r, y))
                for r in range(rows)]

    @pl.when(c == 0)
    def _():
        each_row("init")

    bodies = each_row("body")
    proj_refs = [scratch[r * per_row + per_row - 1] for r in range(rows)]
    _, to_time = _perm_matrices(q_len)
    partials = []

    def in_proj():
        lhs = jnp.concatenate([hn_next for _, hn_next, _ in bodies], axis=0)
        for g in range(L1_GROUPS):
            for lo in (g * gw, LRU_WIDTH + g * gw):
                res = _bdot(lhs, _wload(win_ref[:, lo:lo + gw]))
                for r in range(rows):
                    proj_refs[r][:, lo:lo + gw] = res[r * q_len:(r + 1) * q_len]
            yield

    def out_proj():
        pending = list(range(L1_GROUPS))
        while pending:
            for g in list(pending):
                if len(pieces.get(g, ())) == rows:
                    y_t = jnp.concatenate([_move_rows(to_time, pieces[g][r].astype(BF16)) for r in range(rows)],
                                          axis=0)
                    partials.append(_bdot(y_t, _wload(wout_ref[g * gw // 2:(g + 1) * gw // 2, :])))
                    pending.remove(g)
            yield

    chains = [bodies[r][0][g] for g in range(L1_GROUPS) for r in range(rows)]
    if stream:
        step = pl.program_id(0) * pl.num_programs(1) + c
        chains.append(_mlstm_state_update(*stream_in, *stream_out, base=step * stream_in[1].shape[0]))
    _run_staggered([in_proj()] + chains + [out_proj()])
    total = partials[0]
    for part in partials[1:]:
        total = total + part
    for r, (_, _, h_in) in enumerate(bodies):
        out_refs[0][r] = _rms(h_in + total[r * q_len:(r + 1) * q_len], fnw_ref[...])

    @pl.when(c == pl.num_programs(1) - 1)
    def _():
        each_row("final")


def _l1_in_proj(h_val, nw_ref, win_ref, to_perm):
    hn = _move_rows(to_perm, _rms(h_val, nw_ref[...]).astype(BF16))
    return _bdot(hn, _wload(win_ref[...]))


def _l1_prompt_row(h_ref, hnext_ref, nw_ref, fnw_ref, win_ref, wout_ref, cw_ref, cb_ref,
                   wax_ref, ba_ref, bx_ref, lam_ref, ilc_ref, ilh_ref,
                   y_ref, olc_ref, olh_ref, lbuf, h_st, proj_s, *, front_pad, phase, emit):
    if phase == "init":
        lbuf[...] = jnp.zeros(lbuf.shape, F32)
        lbuf[SUBLANE - TAIL:SUBLANE, :] = ilc_ref[0]
        h_st[...] = ilh_ref[0]
        proj_s[...] = _l1_in_proj(h_ref[...], nw_ref, win_ref, _perm_matrices(h_ref.shape[0])[0])
        return None
    if phase == "final":
        olc_ref[...] = lbuf[SUBLANE - TAIL:SUBLANE, :]
        olh_ref[...] = h_st[...]
        return None

    q_len = h_ref.shape[0]
    h_in = h_ref[...]
    to_perm, _ = _perm_matrices(q_len)
    hn_next = _move_rows(to_perm, _rms(hnext_ref[...], nw_ref[...]).astype(BF16))
    if front_pad:
        valid = _perm_time(q_len) >= front_pad
    gw = LRU_WIDTH // L1_GROUPS
    tiles = gw // LANE
    gates = [proj_s[:, g * gw:(g + 1) * gw] for g in range(L1_GROUPS)]
    xrs = [proj_s[:, LRU_WIDTH + g * gw:LRU_WIDTH + (g + 1) * gw] for g in range(L1_GROUPS)]

    def group(g):
        cg = slice(g * gw, (g + 1) * gw)
        xc = _conv_perm(lbuf.at[:, cg], xrs[g], cw_ref.at[:, cg], cb_ref.at[:, cg])
        ra, ix = _blockdiag_tiles(xc, wax_ref.at[g * tiles:(g + 1) * tiles])
        yield
        a, u = _rglru_gates(xc, ra, ix, ba_ref.at[:, cg], bx_ref.at[:, cg], lam_ref.at[:, cg])
        if front_pad:
            a = jnp.where(valid, a, 1.0)
            u = jnp.where(valid, u, 0.0)
        yield
        h, h_last = _scan_perm(a, u, h_st[:, cg])
        h_st[:, cg] = h_last
        yield
        emit(g, h * _silu(gates[g]))

    return [group(g) for g in range(L1_GROUPS)], hn_next, h_in


def _run_staggered(gens):
    live = []
    pending = list(gens)
    while pending or live:
        if pending:
            live.append(pending.pop(0))
        for gen in list(live):
            if next(gen, "done") == "done":
                live.remove(gen)


L1_STATE_SHAPES = ((TAIL, LRU_WIDTH), (1, LRU_WIDTH))


def _l1_prompt(h1, weights, init, front_pad, stream_in=(), stream_out_shapes=()):
    bsz, length, _ = h1.shape
    q_len = min(CHUNK, length)
    assert length % q_len == 0
    rows = _rows_per_step(bsz, L1_ROWS)
    assert len(weights) == N_L1_W and len(init) == N_L1_S
    nc = length // q_len
    grid = (bsz // rows, nc)
    n_steps = grid[0] * nc
    last = nc - 1
    x_spec = pl.BlockSpec((rows, q_len, D_MODEL), lambda b, c: (b, c, 0))
    next_spec = pl.BlockSpec((rows, q_len, D_MODEL), lambda b, c: (b, jnp.minimum(c + 1, last), 0))
    stream_outs = [jax.ShapeDtypeStruct(s, F32) for s in stream_out_shapes]
    in_specs = ([x_spec, next_spec] + [_const_spec(w.shape) for w in weights]
                + [_state_spec(s, 0) for s in L1_STATE_SHAPES] + _stream_specs(stream_in, n_steps, nc))
    out_shape = ([jax.ShapeDtypeStruct((bsz, length, D_MODEL), F32)]
                 + [jax.ShapeDtypeStruct((bsz,) + s, F32) for s in L1_STATE_SHAPES] + stream_outs)
    out_specs = ([x_spec] + [_state_spec(s, rows) for s in L1_STATE_SHAPES]
                 + _stream_specs(stream_outs, n_steps, nc))
    row_scratch = ((SUBLANE, LRU_WIDTH), (1, LRU_WIDTH), (q_len, 2 * LRU_WIDTH))
    scratch = [pltpu.VMEM(s, F32) for _ in range(rows) for s in row_scratch]
    return pl.pallas_call(
        functools.partial(_l1_prompt_kernel, front_pad=front_pad, rows=rows, stream=bool(stream_in)),
        grid=grid, in_specs=in_specs, out_specs=out_specs, out_shape=out_shape, scratch_shapes=scratch,
        compiler_params=pltpu.CompilerParams(dimension_semantics=("arbitrary", "arbitrary"),
                                             vmem_limit_bytes=VMEM_LIMIT),
        name="l1_prompt",
    )(h1, h1, *weights, *init, *stream_in)


def _conv_step(buf_ref, x, w_ref, b_ref, newbuf_ref):
    y = b_ref[...] + w_ref[3:4, :] * x
    for tap in range(TAIL):
        y = y + w_ref[tap:tap + 1, :] * buf_ref[tap]
    for tap in range(TAIL - 1):
        newbuf_ref[tap] = buf_ref[tap + 1]
    newbuf_ref[TAIL - 1] = x
    return y


def _l0_sample_pre_kernel(x_ref, nw_ref, win_ref, scw_ref, scb_ref, dtb_ref, alog_ref,
                          mcw_ref, mcb_ref, wqk_ref, wv_ref, wg_ref, bg_ref,
                          sbuf_ref, mbuf_ref, m0_ref, n0_ref,
                          nsb_ref, nmb_ref, zs_ref, xs_ref, bm_ref, cm_ref, xdt_t_ref, dec_t_ref,
                          zm_ref, xc_ref, q_ref, isv_t_ref, fs_t_ref, k_ref, mnew_ref, nnew_ref, den_ref):
    x = x_ref[...]
    hn = _rms(x, nw_ref[...])
    proj = _bdot(hn, _wload(win_ref[...]))
    zs_ref[...] = proj[:, OFF_ZS:OFF_ZS + SSD_WIDTH]
    zm_ref[...] = proj[:, OFF_ZM:OFF_ZM + ML_WIDTH]
    xbc = proj[:, OFF_XBC:OFF_XBC + SSD_CONV_CH]
    dt_raw = proj[:, OFF_DT:OFF_DT + LANE]
    xm = proj[:, OFF_XM:OFF_XM + ML_WIDTH]
    expand = _expand_matrix()

    xbc = _silu(_conv_step(sbuf_ref, xbc, scw_ref, scb_ref, nsb_ref))
    xs = xbc[:, :SSD_WIDTH]
    xs_ref[...] = xs
    bm_ref[...] = xbc[:, SSD_WIDTH:SSD_WIDTH + SSD_GROUPS * SSD_STATE]
    cm_ref[...] = xbc[:, SSD_WIDTH + SSD_GROUPS * SSD_STATE:]
    dt = _softplus(dt_raw + dtb_ref[...])
    log_a = -dt * jnp.exp(alog_ref[...])
    xdt_t_ref[...] = xs * _expand_heads(dt, expand)
    dec_t_ref[...] = jnp.exp(log_a)

    xc = _silu(_conv_step(mbuf_ref, xm, mcw_ref, mcb_ref, nmb_ref))
    xc_ref[...] = xc
    q, k, v, ig, logf = _mlstm_qkv_gates(xm, xc, wqk_ref, wv_ref, wg_ref, bg_ref)
    m0 = m0_ref[...]
    m_new = jnp.maximum(logf + m0, ig)
    fs = jnp.exp(logf + m0 - m_new)
    is_ = jnp.exp(ig - m_new)
    mnew_ref[...] = m_new
    r = lax.broadcasted_iota(jnp.int32, (LANE, ML_WIDTH), 0)
    cidx = lax.broadcasted_iota(jnp.int32, (LANE, ML_WIDTH), 1)
    expand_m = jnp.where(lax.shift_right_logical(cidx, 8) == r, 1.0, 0.0).astype(BF16)
    fs_e = _expand_heads(fs, expand_m)
    is_e = _expand_heads(is_, expand_m)
    n_new = fs_e * n0_ref[...] + is_e * k
    nnew_ref[...] = n_new
    q_ref[...] = q
    k_ref[...] = k
    isv_t_ref[...] = is_e * v
    fs_t_ref[...] = fs
    nq = n_new * q
    floor = jnp.exp(-m_new)
    for hd in range(ML_HEADS):
        den = jnp.sum(nq[:, hd * ML_HEAD_DIM:(hd + 1) * ML_HEAD_DIM], axis=-1, keepdims=True)
        den_ref[:, hd:hd + 1] = jnp.maximum(jnp.abs(den), floor[:, hd:hd + 1])


def _rows_to_tile(rows8):
    return jnp.concatenate([rows8] + [jnp.zeros_like(rows8)] * (LANE // SUBLANE - 1), axis=0)


def _ssd_state_update(dec_ref, s_ref, xdt_ref, bm_ref, cm_ref, snew_ref, y_ref, *, base):
    n = s_ref.shape[0]
    half = SSD_WIDTH // SSD_GROUPS
    x_cols = _rows_to_tile(xdt_ref[0]).T
    lane = lax.broadcasted_iota(jnp.int32, (half, LANE), 1)
    accs = [jnp.zeros((half, LANE), F32) for _ in range(SSD_GROUPS)]
    for i in range(n):
        x_col = x_cols[:, i:i + 1].reshape(SSD_HEADS, SSD_HEAD_DIM, 1)
        for g in range(SSD_GROUPS):
            hs = slice(g * SSD_HPG, (g + 1) * SSD_HPG)
            gs = slice(g * SSD_STATE, (g + 1) * SSD_STATE)
            b_row = bm_ref[0, i:i + 1, gs].reshape(1, 1, SSD_STATE)
            decay = jnp.stack([jnp.full((1, 1), dec_ref[base + i, hd], F32)
                               for hd in range(hs.start, hs.stop)], axis=0)
            s_new = decay * s_ref[i, hs] + x_col[hs] * b_row
            snew_ref[i, hs] = s_new
            prod = _bdot_nt(s_new.reshape(half, SSD_STATE), _rows_to_tile(cm_ref[0, :, gs]))
            accs[g] = jnp.where(lane == i, prod, accs[g])
            yield
    y_ref[0] = jnp.concatenate(accs, axis=0).T[:SUBLANE]


def _mlstm_state_update(fs_ref, c_ref, isv_ref, k_ref, q_ref, cnew_ref, num_ref, *, base):
    n = c_ref.shape[0]
    v_cols = _rows_to_tile(isv_ref[0]).T
    lane = lax.broadcasted_iota(jnp.int32, (ML_HEAD_DIM, LANE), 1)
    for hd in range(ML_HEADS):
        sl = slice(hd * ML_HEAD_DIM, (hd + 1) * ML_HEAD_DIM)
        q_rows = _rows_to_tile(q_ref[0, :, sl])
        acc = jnp.zeros((ML_HEAD_DIM, LANE), F32)
        for i in range(n):
            c_new = fs_ref[base + i, hd] * c_ref[i, hd] + v_cols[sl, i:i + 1] * k_ref[0, i:i + 1, sl]
            cnew_ref[i, hd] = c_new
            acc = jnp.where(lane == i, _bdot_nt(c_new, q_rows), acc)
            yield
        num_ref[0, :, sl] = acc.T[:SUBLANE]


def _sample_post_kernel(x_ref, ys_t_ref, num_t_ref, den_ref, zs_ref, xs_ref, zm_ref, xc_ref,
                        dsk_ref, snw_ref, msk_ref, mnw_ref, wout_ref,
                        nw1_ref, fnw_ref, win1_ref, wout1_ref, cw_ref, cb_ref,
                        wax_ref, ba_ref, bx_ref, lam_ref, lbuf_ref, h0_ref,
                        y_ref, nlb_ref, hnew_ref):
    xs = xs_ref[...]
    y_s = ys_t_ref[...] + dsk_ref[...] * xs
    y_s = _group_rmsnorm(y_s * _silu(zs_ref[...]), snw_ref[...])
    num = num_t_ref[...]
    den = den_ref[...]
    h_m = jnp.concatenate(
        [num[:, hd * ML_HEAD_DIM:(hd + 1) * ML_HEAD_DIM] / den[:, hd:hd + 1] for hd in range(ML_HEADS)], axis=-1)
    h_m = _head_layernorm(h_m) * mnw_ref[...]
    y_m = (h_m + msk_ref[...] * xc_ref[...]) * _silu(zm_ref[...])
    h1 = x_ref[...] + _bdot(jnp.concatenate([y_s, y_m], axis=-1), _wload(wout_ref[...]))

    hn = _rms(h1, nw1_ref[...])
    proj = _bdot(hn, _wload(win1_ref[...]))
    gate = proj[:, :LRU_WIDTH]
    xr = proj[:, LRU_WIDTH:]
    xc = _conv_step(lbuf_ref, xr, cw_ref, cb_ref, nlb_ref)
    ra, ix = _blockdiag_tiles(xc, wax_ref)
    a, u = _rglru_gates(xc, ra, ix, ba_ref, bx_ref, lam_ref)
    h = a * h0_ref[...] + u
    hnew_ref[...] = h
    h2 = h1 + _bdot(h * _silu(gate), _wload(wout1_ref[...]))
    y_ref[...] = _rms(h2, fnw_ref[...])


def _full_call(kernel_fn, out_shapes, args, name):
    return pl.pallas_call(
        kernel_fn,
        out_shape=[jax.ShapeDtypeStruct(s, F32) for s in out_shapes],
        compiler_params=pltpu.CompilerParams(vmem_limit_bytes=VMEM_LIMIT),
        name=name,
    )(*args)


def _row(v, width=None):
    v = v.reshape(1, -1).astype(F32)
    if width is not None and v.shape[1] < width:
        v = jnp.pad(v, ((0, 0), (0, width - v.shape[1])))
    return v


PACK_STEPS = 8


def _pack_all(w_in, weights):
    flats = [w_in] + [w.reshape(-1, w.shape[-1]) for w in weights]
    widths = [IN_MIX_PAD] + [f.shape[1] for f in flats[1:]]
    for f in flats:
        assert f.shape[0] % (2 * SUBLANE * PACK_STEPS) == 0
    packed = pl.pallas_call(
        _pack_kernel,
        grid=(PACK_STEPS,),
        in_specs=[pl.BlockSpec((f.shape[0] // PACK_STEPS, f.shape[1]), lambda i: (i, 0)) for f in flats],
        out_specs=[pl.BlockSpec((f.shape[0] // PACK_STEPS // 2, n), lambda i: (i, 0)) for f, n in zip(flats, widths)],
        out_shape=[jax.ShapeDtypeStruct((f.shape[0] // 2, n), jnp.uint32) for f, n in zip(flats, widths)],
        compiler_params=pltpu.CompilerParams(vmem_limit_bytes=VMEM_LIMIT),
        name="pack_weights",
    )(*flats)
    return [packed[0]] + [p.reshape(w.shape[:-2] + (w.shape[-2] // 2, w.shape[-1]))
                          for p, w in zip(packed[1:], weights)]


def _pack_rows(x):
    return pltpu.bitcast(x.astype(BF16), jnp.uint32)


def _pack_kernel(*refs):
    n = len(refs) // 2
    win_ref, wino_ref = refs[0], refs[n]
    wino_ref[:, :OFF_DT] = _pack_rows(win_ref[:, :OFF_DT])
    dt_tile = win_ref[:, OFF_DT:OFF_DT + LANE]
    lane = lax.broadcasted_iota(jnp.int32, dt_tile.shape, 1)
    wino_ref[:, OFF_DT:OFF_ZM] = _pack_rows(jnp.where(lane < SSD_HEADS, dt_tile, 0.0))
    wino_ref[:, OFF_ZM:] = _pack_rows(win_ref[:, OFF_DT + SSD_HEADS:])
    for w_ref, o_ref in zip(refs[1:n], refs[n + 1:]):
        o_ref[...] = _pack_rows(w_ref[...])


def _dense_block_tiles(w):
    nb, bi, bo = w.shape
    per = LANE // bi
    rows = w.reshape(nb // per, per * bi, bo)
    col = jnp.arange(per * bo)
    spread = (col[None, :] % bo == jnp.arange(bo)[:, None]).astype(w.dtype)
    rep = jnp.einsum('tro,oc->trc', rows, spread)
    same_block = (jnp.arange(per * bi)[:, None] // bi) == (col[None, :] // bo)
    return jnp.where(same_block, rep, 0.0)


def kernel(x_prompt, x_sample, state_ssd_conv, state_ssd, state_mlstm_conv, state_mlstm_C, state_mlstm_n,
           state_mlstm_m, state_lru_conv, state_lru_h, meta_tokens, norm_w, final_norm_w, w_in_mix, w_out_mix,
           ssd_conv_w, ssd_conv_b, ssd_dt_bias, ssd_a_log, ssd_d, ssd_norm_w, ml_conv_w, ml_conv_b, ml_wq, ml_wk,
           ml_wv, ml_w_gate, ml_b_gate, ml_skip, ml_norm_w, lru_w_in, lru_w_out, lru_conv_w, lru_conv_b, lru_wa,
           lru_ba, lru_wx, lru_bx, lru_lambda):
    bsz = x_prompt.shape[0]
    dec = x_sample.shape[0]

    wout = w_out_mix[0]
    nw0 = _row(norm_w[0])
    nw1 = _row(norm_w[1])
    fnw = _row(final_norm_w)
    scw = ssd_conv_w[0]
    scb = _row(ssd_conv_b[0])
    dtb = _row(ssd_dt_bias[0], LANE)
    alog = _row(ssd_a_log[0], LANE)
    dsk = _row(jnp.repeat(ssd_d[0], SSD_HEAD_DIM))
    snw = _row(ssd_norm_w[0])
    mcw = ml_conv_w[0]
    mcb = _row(ml_conv_b[0])
    wqk = jnp.concatenate([_dense_block_tiles(ml_wq[0]), _dense_block_tiles(ml_wk[0])], axis=2)
    wv = _dense_block_tiles(ml_wv[0])
    wg_raw = ml_w_gate[0]
    wg = jnp.concatenate([jnp.pad(wg_raw[:, :ML_HEADS], ((0, 0), (0, LANE - ML_HEADS))),
                          jnp.pad(wg_raw[:, ML_HEADS:], ((0, 0), (0, LANE - ML_HEADS)))], axis=1)
    bg = jnp.concatenate([_row(ml_b_gate[0, :ML_HEADS], LANE), _row(ml_b_gate[0, ML_HEADS:], LANE)], axis=1)
    msk = _row(ml_skip[0])
    mnw = _row(ml_norm_w[0])
    win1 = lru_w_in[0]
    wout1 = lru_w_out[0]
    lcw = lru_conv_w[0]
    lcb = _row(lru_conv_b[0])
    wax = jnp.concatenate([lru_wa[0], lru_wx[0]], axis=2)
    r_idx = lax.broadcasted_iota(jnp.int32, (LANE, SSD_WIDTH), 0)
    c_idx = lax.broadcasted_iota(jnp.int32, (LANE, SSD_WIDTH), 1)
    expand = (c_idx // SSD_HEAD_DIM == r_idx).astype(F32)
    ba = _row(lru_ba[0])
    bx = _row(lru_bx[0])
    lam = _row(lru_lambda[0])

    win, wout, wqk, wv, wg, expand, win1, wout1, wax = _pack_all(
        w_in_mix[0], [wout, wqk, wv, wg, expand, win1, wout1, wax])
    l0_w = (nw0, win, wout, scw, scb, dtb, alog, dsk, snw, mcw, mcb, wqk, wv, wg, bg, msk, mnw, expand)
    l1_w = (nw1, fnw, win1, wout1, lcw, lcb, wax, ba, bx, lam)

    xs2 = x_sample[:, 0]
    sbuf = jnp.moveaxis(state_ssd_conv[0], 1, 0)
    mbuf = jnp.moveaxis(state_mlstm_conv[0], 1, 0)
    lbuf = jnp.moveaxis(state_lru_conv[0], 1, 0)
    m0 = jnp.pad(state_mlstm_m[0], ((0, 0), (0, LANE - ML_HEADS)))
    n0 = state_mlstm_n[0].reshape(dec, ML_WIDTH)
    pre_shapes = ((TAIL, dec, SSD_CONV_CH), (TAIL, dec, ML_WIDTH), (dec, SSD_WIDTH), (dec, SSD_WIDTH),
                  (dec, SSD_GROUPS * SSD_STATE), (dec, SSD_GROUPS * SSD_STATE), (dec, SSD_WIDTH), (dec, LANE),
                  (dec, ML_WIDTH), (dec, ML_WIDTH), (dec, ML_WIDTH), (dec, ML_WIDTH), (dec, LANE),
                  (dec, ML_WIDTH), (dec, LANE), (dec, ML_WIDTH), (dec, ML_HEADS))
    (nsb, nmb, zs, xs_c, bm, cm, xdt_t, dec_t, zm, xc_m, q, isv_t, fs_t, k, m_new, n_new, den) = _full_call(
        _l0_sample_pre_kernel, pre_shapes,
        (xs2, nw0, win, scw, scb, dtb, alog, mcw, mcb, wqk, wv, wg, bg, sbuf, mbuf, m0, n0), "l0_sample_pre")

    zero0 = tuple(jnp.zeros((1,) + s, F32) for s in L0_STATE_SHAPES)
    zero1 = tuple(jnp.zeros((1,) + s, F32) for s in L1_STATE_SHAPES)
    meta = jnp.pad(meta_tokens.astype(F32), ((CHUNK - N_META, 0), (0, 0)))[None]
    meta_out = _l0_prompt(meta, l0_w, zero0, CHUNK - N_META)
    meta1_out = _l1_prompt(meta_out[0], l1_w, zero1, CHUNK - N_META)

    n_steps = (bsz // _rows_per_step(bsz, L0_ROWS)) * (x_prompt.shape[1] // CHUNK)
    assert n_steps == (bsz // _rows_per_step(bsz, L1_ROWS)) * (x_prompt.shape[1] // CHUNK)
    per_step = dec // n_steps
    assert per_step * n_steps == dec

    def step_rows(a):
        a = a.reshape(n_steps, per_step, a.shape[-1])
        return jnp.pad(a, ((0, 0), (0, SUBLANE - per_step), (0, 0)))

    def from_step_rows(a):
        return a[:, :per_step].reshape(dec, a.shape[-1])

    l0_out = _l0_prompt(
        x_prompt, l0_w, tuple(meta_out[1:]), 0,
        stream_in=(dec_t[:, :SSD_HEADS], state_ssd[0], step_rows(xdt_t), step_rows(bm), step_rows(cm)),
        stream_out_shapes=((dec, SSD_HEADS, SSD_HEAD_DIM, SSD_STATE), (n_steps, SUBLANE, SSD_WIDTH)))
    h1_p, p_sc, p_s, p_mc, p_c, p_n, p_m, s_new, ys_r = l0_out
    y_prompt, p_lc, p_lh, c_new, num_r = _l1_prompt(
        h1_p, l1_w, tuple(meta1_out[1:]), 0,
        stream_in=(fs_t[:, :ML_HEADS], state_mlstm_C[0], step_rows(isv_t), step_rows(k), step_rows(q)),
        stream_out_shapes=((dec, ML_HEADS, ML_HEAD_DIM, ML_HEAD_DIM), (n_steps, SUBLANE, ML_WIDTH)))

    p_m = p_m[:, 0, :ML_HEADS]
    p_lh = p_lh[:, 0]

    post_shapes = ((dec, D_MODEL), (TAIL, dec, LRU_WIDTH), (dec, LRU_WIDTH))
    y_s2, nlb, h_new = _full_call(
        _sample_post_kernel, post_shapes,
        (xs2, from_step_rows(ys_r), from_step_rows(num_r), den, zs, xs_c, zm, xc_m, dsk, snw, msk, mnw, wout,
         nw1, fnw, win1, wout1, lcw, lcb, wax, ba, bx, lam, lbuf, state_lru_h[0]), "sample_post")

    s_sc = jnp.moveaxis(nsb, 0, 1)[None]
    s_mc = jnp.moveaxis(nmb, 0, 1)[None]
    s_lc = jnp.moveaxis(nlb, 0, 1)[None]
    return (y_prompt, y_s2[:, None, :],
            p_sc[None], p_s[None], p_mc[None], p_c[None], p_n[None], p_m[None], p_lc[None], p_lh[None],
            s_sc, s_new[None], s_mc, c_new[None], n_new.reshape(dec, ML_HEADS, ML_HEAD_DIM)[None],
            m_new[:, :ML_HEADS][None], s_lc, h_new[None])
```

```python
import functools

import jax
import jax.numpy as jnp
from jax import lax
from jax.experimental import pallas as pl
from jax.experimental.pallas import tpu as pltpu

F32 = jnp.float32
BF16 = jnp.bfloat16

D_MODEL = 1024
N_META = 16
CONV_W = 4
EPS = 1e-6
NEG = -1e30
SSD_WIDTH = 1024
SSD_HEAD_DIM = 64
SSD_HEADS = 16
SSD_GROUPS = 2
SSD_HPG = 8
SSD_STATE = 128
SSD_CONV_CH = 1536
ML_WIDTH = 1024
ML_HEADS = 4
ML_HEAD_DIM = 256
ML_QKV_BLOCK = 4
LRU_WIDTH = 2048
LRU_BLOCKS = 16
LRU_BLOCK = 128
LRU_C = 8.0

LANE = 128
SUBLANE = 8
CHUNK = 128
L0_ROWS = 2
L1_ROWS = 2
L1_GROUPS = 8
TAIL = CONV_W - 1

OFF_ZS = 0
OFF_XBC = OFF_ZS + SSD_WIDTH
OFF_DT = OFF_XBC + SSD_CONV_CH
OFF_ZM = OFF_DT + LANE
OFF_XM = OFF_ZM + ML_WIDTH
IN_MIX_PAD = OFF_XM + ML_WIDTH

VMEM_LIMIT = 56 * 1024 * 1024
L0_VMEM_LIMIT = 61 * 1024 * 1024


def _sigmoid(x):
    return 1.0 / (1.0 + jnp.exp(-x))


def _silu(x):
    return x * _sigmoid(x)


def _softplus(x):
    return jnp.maximum(x, 0.0) + jnp.log1p(jnp.exp(-jnp.abs(x)))


def _rms(x, w):
    return x * lax.rsqrt(jnp.mean(x * x, axis=-1, keepdims=True) + EPS) * w


def _bdot(a, b):
    return jnp.dot(a.astype(BF16), b.astype(BF16), preferred_element_type=F32)


def _bdot_nt(a, b):
    return lax.dot_general(a.astype(BF16), b.astype(BF16), (((1,), (1,)), ((), ())), preferred_element_type=F32)


def _wload(w):
    return pltpu.bitcast(w, BF16)


def _split3(x):
    hi = x.astype(BF16)
    r = x - hi.astype(F32)
    mid = r.astype(BF16)
    lo = (r - mid.astype(F32)).astype(BF16)
    return hi, mid, lo


def _cumsum_rows(x, tril):
    hi, mid, lo = _split3(x)
    d = functools.partial(jnp.dot, preferred_element_type=F32)
    return d(tril, hi) + d(tril, mid) + d(tril, lo)


def _expand_heads(x, expand):
    hi, mid, _ = _split3(x)
    d = functools.partial(jnp.dot, preferred_element_type=F32)
    return d(hi, expand) + d(mid, expand)


def _expand_matrix():
    r = lax.broadcasted_iota(jnp.int32, (LANE, SSD_WIDTH), 0)
    c = lax.broadcasted_iota(jnp.int32, (LANE, SSD_WIDTH), 1)
    return jnp.where(lax.shift_right_logical(c, 6) == r, 1.0, 0.0).astype(BF16)


def _blockdiag_tiles(x, w_ref):
    k = w_ref.shape[0]
    m = w_ref.shape[2] // LANE
    prods = [_bdot(x[:, t * LANE:(t + 1) * LANE], _wload(w_ref[t])) for t in range(k)]
    return [jnp.concatenate([p[:, j * LANE:(j + 1) * LANE] for p in prods], axis=-1) for j in range(m)]


def _group_rmsnorm(y, w):
    half = SSD_WIDTH // SSD_GROUPS
    parts = []
    for g in range(SSD_GROUPS):
        yg = y[:, g * half:(g + 1) * half]
        parts.append(yg * lax.rsqrt(jnp.mean(yg * yg, axis=-1, keepdims=True) + EPS))
    return jnp.concatenate(parts, axis=-1) * w


def _head_layernorm(h):
    parts = []
    for k in range(ML_HEADS):
        hk = h[:, k * ML_HEAD_DIM:(k + 1) * ML_HEAD_DIM]
        mu = jnp.mean(hk, axis=-1, keepdims=True)
        d = hk - mu
        var = jnp.mean(d * d, axis=-1, keepdims=True)
        parts.append(d * lax.rsqrt(var + EPS))
    return jnp.concatenate(parts, axis=-1)


def _mlstm_qkv_gates(xm, xc, wqk_ref, wv_ref, wg_ref, bg_ref):
    q, k = _blockdiag_tiles(xc, wqk_ref)
    v, = _blockdiag_tiles(xm, wv_ref)
    gates = _bdot(jnp.concatenate([q, k, v], axis=-1), _wload(wg_ref[...])) + bg_ref[...]
    ig = gates[:, :LANE]
    logf = -_softplus(-gates[:, LANE:])
    return q, k * (ML_HEAD_DIM ** -0.5), v, ig, logf


N_L0_W = 18
N_L0_S = 6


def _l0_prompt_kernel(x_ref, xnext_ref, *refs, front_pad, rows, stream):
    w_refs = refs[:N_L0_W]
    init_refs = refs[N_L0_W:N_L0_W + N_L0_S]
    pos = N_L0_W + N_L0_S
    stream_in = refs[pos:pos + N_STREAM_IN] if stream else ()
    pos += len(stream_in)
    out_refs = refs[pos:pos + N_L0_S + 1]
    pos += N_L0_S + 1
    stream_out = refs[pos:pos + N_STREAM_OUT] if stream else ()
    pos += len(stream_out)
    scratch = refs[pos:]
    per_row = len(scratch) // rows
    c = pl.program_id(1)
    win_ref, wout_ref = w_refs[1], w_refs[2]
    q_len = x_ref.shape[1]
    pieces = {}

    def each_row(phase):
        return [_l0_prompt_row(x_ref.at[r], xnext_ref.at[r], *w_refs, *init_refs, *(o.at[r] for o in out_refs),
                               *scratch[r * per_row:(r + 1) * per_row], front_pad=front_pad, phase=phase,
                               emit=lambda k0, y, r=r: pieces.setdefault(k0, {}).__setitem__(r, y))
                for r in range(rows)]

    @pl.when(c == 0)
    def _():
        each_row("init")

    bodies = each_row("body")
    proj_refs = [scratch[r * per_row + per_row - 1] for r in range(rows)]
    _, to_time = _perm_matrices(q_len)
    partials = []

    def in_proj():
        lhs = jnp.concatenate([hn_next for _, hn_next, _ in bodies], axis=0)
        for lo, hi in L0_PROJ_PIECES:
            res = _bdot(lhs, _wload(win_ref[:, lo:hi]))
            for r in range(rows):
                proj_refs[r][:, lo:hi] = res[r * q_len:(r + 1) * q_len]
            yield

    def out_proj():
        half = SSD_WIDTH // SSD_GROUPS
        pending = ([(g * half, half) for g in range(SSD_GROUPS)]
                   + [(SSD_WIDTH + hd * ML_HEAD_DIM, ML_HEAD_DIM) for hd in range(ML_HEADS)])
        while pending:
            for k0, width in list(pending):
                if len(pieces.get(k0, ())) == rows:
                    y_t = jnp.concatenate([_move_rows(to_time, pieces[k0][r].astype(BF16)) for r in range(rows)],
                                          axis=0)
                    partials.append(_bdot(y_t, _wload(wout_ref[k0 // 2:(k0 + width) // 2, :])))
                    pending.remove((k0, width))
            yield

    chains = [gen for gens, _, _ in bodies for gen in gens] + [in_proj(), out_proj()]
    if stream:
        step = pl.program_id(0) * pl.num_programs(1) + c
        chains.append(_ssd_state_update(*stream_in, *stream_out, base=step * stream_in[1].shape[0]))
    _run_round_robin(chains)
    total = partials[0]
    for part in partials[1:]:
        total = total + part
    for r, (_, _, x) in enumerate(bodies):
        h1 = x + total[r * q_len:(r + 1) * q_len]
        if front_pad:
            h1 = jnp.where(lax.broadcasted_iota(jnp.int32, (q_len, 1), 0) >= front_pad, h1, 0.0)
        out_refs[0][r] = h1

    @pl.when(c == pl.num_programs(1) - 1)
    def _():
        each_row("final")


L0_PROJ_PIECES = ((OFF_XBC, OFF_ZM), (OFF_XM, IN_MIX_PAD), (OFF_ZM, OFF_XM), (OFF_ZS, OFF_XBC))


def _run_round_robin(gens):
    live = list(gens)
    while live:
        for gen in list(live):
            step = next(gen, "done")
            if step == "done":
                live.remove(gen)
            elif step is not None:
                live.extend(step)


def _l0_prompt_row(x_ref, xnext_ref, nw_ref, win_ref, wout_ref,
                   scw_ref, scb_ref, dtb_ref, alog_ref, dsk_ref, snw_ref,
                   mcw_ref, mcb_ref, wqk_ref, wv_ref, wg_ref, bg_ref, msk_ref, mnw_ref, expand_ref,
                   isc_ref, iss_ref, imc_ref, ict_ref, inn_ref, imm_ref,
                   h1_ref, osc_ref, oss_ref, omc_ref, oct_ref, onn_ref, omm_ref,
                   sbuf, mbuf, s_st, ct_st, n_st, m_st, proj_s, *, front_pad, phase, emit):
    q_len = x_ref.shape[0]

    if phase == "init":
        sbuf[...] = jnp.zeros(sbuf.shape, F32)
        mbuf[...] = jnp.zeros(mbuf.shape, F32)
        sbuf[SUBLANE - TAIL:SUBLANE, :] = isc_ref[0]
        mbuf[SUBLANE - TAIL:SUBLANE, :] = imc_ref[0]
        for g in range(SSD_GROUPS):
            heads = iss_ref[0, g * SSD_HPG:(g + 1) * SSD_HPG]
            s_st[g] = heads.reshape(SSD_HPG * SSD_HEAD_DIM, SSD_STATE).T
        for hd in range(ML_HEADS):
            ct_st[hd] = ict_ref[0, hd].T
        n_st[...] = inn_ref[0]
        m_st[...] = imm_ref[0]
        hn0 = _move_rows(_perm_matrices(q_len)[0], _rms(x_ref[...], nw_ref[...]).astype(BF16))
        proj_s[...] = _bdot(hn0, _wload(win_ref[...]))
        return None
    if phase == "final":
        osc_ref[...] = sbuf[SUBLANE - TAIL:SUBLANE, :]
        omc_ref[...] = mbuf[SUBLANE - TAIL:SUBLANE, :]
        for g in range(SSD_GROUPS):
            oss_ref[g * SSD_HPG:(g + 1) * SSD_HPG] = s_st[g].T.reshape(SSD_HPG, SSD_HEAD_DIM, SSD_STATE)
        for hd in range(ML_HEADS):
            oct_ref[hd] = ct_st[hd].T
        onn_ref[...] = n_st[...]
        omm_ref[...] = m_st[...]
        return None

    x = x_ref[...]
    to_perm, to_time = _perm_matrices(q_len)
    t_col = _perm_time(q_len)
    t_row = _perm_time(q_len, row=True)
    causal = t_col >= t_row
    tril = jnp.where(causal, 1.0, 0.0).astype(BF16)
    valid = (t_col >= front_pad) if front_pad else None
    hn_next = _move_rows(to_perm, _rms(xnext_ref[...], nw_ref[...]).astype(BF16))
    xbc_raw = proj_s[:, OFF_XBC:OFF_XBC + SSD_CONV_CH]
    dt_raw = proj_s[:, OFF_DT:OFF_DT + LANE]
    xm = proj_s[:, OFF_XM:OFF_XM + ML_WIDTH]
    z_s = proj_s[:, OFF_ZS:OFF_ZS + SSD_WIDTH]
    z_m = proj_s[:, OFF_ZM:OFF_ZM + ML_WIDTH]

    def ssd():
        xbc = _silu(_conv_perm(sbuf, xbc_raw, scw_ref, scb_ref))
        yield
        xs = xbc[:, :SSD_WIDTH]
        bm = xbc[:, SSD_WIDTH:SSD_WIDTH + SSD_GROUPS * SSD_STATE]
        cm = xbc[:, SSD_WIDTH + SSD_GROUPS * SSD_STATE:]
        dt = _softplus(dt_raw + dtb_ref[...])
        if front_pad:
            dt = jnp.where(valid, dt, 0.0)
        log_a = -dt * jnp.exp(alog_ref[...])
        a_cs = _cumsum_rows(log_a, tril)
        yield
        a_last = a_cs[q_len - 1:q_len, :]
        expand = _wload(expand_ref[...])
        w_state = _expand_heads(dt * jnp.exp(a_last - a_cs), expand)
        e_acs = _expand_heads(jnp.exp(a_cs), expand)
        a_cs_t = a_cs.T
        dt_t = dt.T
        yield
        pair_lo = lax.broadcasted_iota(jnp.int32, (q_len, LANE), 1) < SSD_HEAD_DIM
        half = SSD_WIDTH // SSD_GROUPS

        def group(g):
            cols = slice(g * half, (g + 1) * half)
            bg = bm[:, g * SSD_STATE:(g + 1) * SSD_STATE]
            cg = cm[:, g * SSD_STATE:(g + 1) * SSD_STATE]
            bg_t = bg.T
            xg = xs[:, cols]
            eg = e_acs[:, cols]
            s_old = s_st[g]
            cb = _bdot(cg, bg_t)
            y_off = _bdot(cg, s_old) * eg
            s_st[g] = eg[q_len - 1:q_len, :] * s_old + _bdot(bg_t, xg * w_state[:, cols])
            yield
            y_pairs = []
            for pr in range(SSD_HPG // 2):
                ms = []
                for e in (2 * pr, 2 * pr + 1):
                    hd = g * SSD_HPG + e
                    seg = jnp.exp(jnp.where(causal, a_cs[:, hd:hd + 1] - a_cs_t[hd:hd + 1, :], -jnp.inf))
                    ms.append(cb * seg * dt_t[hd:hd + 1, :])
                xp = xg[:, pr * LANE:(pr + 1) * LANE]
                rhs = jnp.concatenate([jnp.where(pair_lo, xp, 0.0), jnp.where(pair_lo, 0.0, xp)], axis=0)
                y_pairs.append(_bdot(jnp.concatenate(ms, axis=-1), rhs))
                yield
            y_g = (jnp.concatenate(y_pairs, axis=-1) + y_off + dsk_ref[:, cols] * xg) * _silu(z_s[:, cols])
            y_g = y_g * lax.rsqrt(jnp.mean(y_g * y_g, axis=-1, keepdims=True) + EPS)
            emit(g * half, y_g * snw_ref[:, cols])

        yield [group(g) for g in range(SSD_GROUPS)]

    def mlstm():
        xc = _silu(_conv_perm(mbuf, xm, mcw_ref, mcb_ref))
        yield
        q, k = _blockdiag_tiles(xc, wqk_ref)
        v, = _blockdiag_tiles(xm, wv_ref)
        yield
        gates = _bdot(jnp.concatenate([q, k, v], axis=-1), _wload(wg_ref[...])) + bg_ref[...]
        k = k * (ML_HEAD_DIM ** -0.5)
        yield
        ig = gates[:, :LANE]
        logf = -_softplus(-gates[:, LANE:])
        if front_pad:
            ig = jnp.where(valid, ig, NEG)
            logf = jnp.where(valid, logf, 0.0)
        bcum = _cumsum_rows(logf, tril)
        yield
        ftot = bcum[q_len - 1:q_len, :]
        m_prev = m_st[...]
        w_end = ftot - bcum + ig
        m_new = jnp.maximum(ftot + m_prev, jnp.max(w_end, axis=0, keepdims=True))
        sc = jnp.exp(ftot + m_prev - m_new)
        wexp = jnp.exp(w_end - m_new)
        inter = bcum + m_prev
        bcum_t = bcum.T
        ig_t = ig.T
        m_st[...] = m_new

        def head(hd):
            sl = slice(hd * ML_HEAD_DIM, (hd + 1) * ML_HEAD_DIM)
            q_h, k_h, v_h = q[:, sl], k[:, sl], v[:, sl]
            k_t = k_h.T
            dmat = jnp.where(causal, bcum[:, hd:hd + 1] - bcum_t[hd:hd + 1, :] + ig_t[hd:hd + 1, :], -jnp.inf)
            inter_h = inter[:, hd:hd + 1]
            m_t = jnp.maximum(inter_h, jnp.max(dmat, axis=-1, keepdims=True))
            dexp = jnp.exp(dmat - m_t)
            inter_sc = jnp.exp(inter_h - m_t)
            s = _bdot(q_h, k_t) * dexp
            yield
            ct_old = ct_st[hd]
            n_old = n_st[hd:hd + 1, :]
            num = _bdot(s, v_h) + inter_sc * _bdot(q_h, ct_old)
            den = jnp.sum(s, axis=-1, keepdims=True) + inter_sc * jnp.sum(q_h * n_old, axis=-1, keepdims=True)
            h_h = num / jnp.maximum(jnp.abs(den), jnp.exp(-m_t))
            w_col = wexp[:, hd:hd + 1]
            sc_h = sc[:, hd:hd + 1]
            ct_st[hd] = sc_h * ct_old + _bdot(k_t, v_h * w_col)
            n_st[hd:hd + 1, :] = sc_h * n_old + jnp.sum(k_h * w_col, axis=0, keepdims=True)
            yield
            mu = jnp.mean(h_h, axis=-1, keepdims=True)
            dev = h_h - mu
            var = jnp.mean(dev * dev, axis=-1, keepdims=True)
            h_h = dev * lax.rsqrt(var + EPS) * mnw_ref[:, sl]
            emit(SSD_WIDTH + hd * ML_HEAD_DIM, (h_h + msk_ref[:, sl] * xc[:, sl]) * _silu(z_m[:, sl]))

        yield [head(hd) for hd in range(ML_HEADS)]

    return [ssd(), mlstm()], hn_next, x


def _const_spec(shape):
    nd = len(shape)
    return pl.BlockSpec(shape, lambda b, c: (0,) * nd)


def _state_spec(shape, rows):
    nd = len(shape)
    if rows:
        return pl.BlockSpec((rows,) + shape, lambda b, c: (b,) + (0,) * nd)
    return pl.BlockSpec((1,) + shape, lambda b, c: (0,) * (nd + 1))


def _rows_per_step(bsz, want):
    return want if bsz % want == 0 else 1


L0_STATE_SHAPES = ((TAIL, SSD_CONV_CH), (SSD_HEADS, SSD_HEAD_DIM, SSD_STATE), (TAIL, ML_WIDTH),
                   (ML_HEADS, ML_HEAD_DIM, ML_HEAD_DIM), (ML_HEADS, ML_HEAD_DIM), (1, LANE))
L0_CARRY_SHAPES = ((SUBLANE, SSD_CONV_CH), (SUBLANE, ML_WIDTH), (SSD_GROUPS, SSD_STATE, SSD_WIDTH // SSD_GROUPS),
                   (ML_HEADS, ML_HEAD_DIM, ML_HEAD_DIM), (ML_HEADS, ML_HEAD_DIM), (1, LANE))


def _stream_specs(arrays, n_steps, nc):
    specs = []
    for a in arrays:
        if a.ndim == 2:
            specs.append(pl.BlockSpec(memory_space=pltpu.SMEM))
            continue
        assert a.shape[0] % n_steps == 0
        block = (a.shape[0] // n_steps,) + a.shape[1:]
        specs.append(pl.BlockSpec(block, lambda b, c, nd=a.ndim: (b * nc + c,) + (0,) * (nd - 1)))
    return specs


def _l0_prompt(x, weights, init, front_pad, stream_in=(), stream_out_shapes=()):
    bsz, length, _ = x.shape
    q_len = min(CHUNK, length)
    assert length % q_len == 0
    rows = _rows_per_step(bsz, L0_ROWS)
    assert len(weights) == N_L0_W and len(init) == N_L0_S
    nc = length // q_len
    grid = (bsz // rows, nc)
    n_steps = grid[0] * nc
    last = nc - 1
    x_spec = pl.BlockSpec((rows, q_len, D_MODEL), lambda b, c: (b, c, 0))
    next_spec = pl.BlockSpec((rows, q_len, D_MODEL), lambda b, c: (b, jnp.minimum(c + 1, last), 0))
    stream_outs = [jax.ShapeDtypeStruct(s, F32) for s in stream_out_shapes]
    in_specs = ([x_spec, next_spec] + [_const_spec(w.shape) for w in weights]
                + [_state_spec(s, 0) for s in L0_STATE_SHAPES] + _stream_specs(stream_in, n_steps, nc))
    out_shape = ([jax.ShapeDtypeStruct((bsz, length, D_MODEL), F32)]
                 + [jax.ShapeDtypeStruct((bsz,) + s, F32) for s in L0_STATE_SHAPES] + stream_outs)
    out_specs = ([x_spec] + [_state_spec(s, rows) for s in L0_STATE_SHAPES]
                 + _stream_specs(stream_outs, n_steps, nc))
    row_scratch = L0_CARRY_SHAPES + ((q_len, IN_MIX_PAD),)
    scratch = [pltpu.VMEM(s, F32) for _ in range(rows) for s in row_scratch]
    return pl.pallas_call(
        functools.partial(_l0_prompt_kernel, front_pad=front_pad, rows=rows, stream=bool(stream_in)),
        grid=grid, in_specs=in_specs, out_specs=out_specs, out_shape=out_shape, scratch_shapes=scratch,
        compiler_params=pltpu.CompilerParams(dimension_semantics=("arbitrary", "arbitrary"),
                                             vmem_limit_bytes=L0_VMEM_LIMIT),
        name="l0_prompt",
    )(x, x, *weights, *init, *stream_in)


def _rglru_gates(xc, ra, ix, ba_ref, bx_ref, lam_ref):
    r = _sigmoid(ra + ba_ref[...])
    i = _sigmoid(ix + bx_ref[...])
    log_a = r * (-LRU_C * _softplus(-lam_ref[...]))
    a = jnp.exp(log_a)
    u = jnp.sqrt(1.0 - a * a) * (i * xc)
    return a, u


def _perm_time(n, row=False):
    p = lax.broadcasted_iota(jnp.int32, (1, n) if row else (n, 1), 1 if row else 0)
    return (n // SUBLANE) * (p & (SUBLANE - 1)) + lax.shift_right_logical(p, 3)


def _perm_matrices(n):
    nb = n // SUBLANE
    r = lax.broadcasted_iota(jnp.int32, (n, n), 0)
    c = lax.broadcasted_iota(jnp.int32, (n, n), 1)
    to_perm = jnp.where(c == nb * (r & (SUBLANE - 1)) + lax.shift_right_logical(r, 3), 1.0, 0.0)
    to_time = jnp.where(r == nb * (c & (SUBLANE - 1)) + lax.shift_right_logical(c, 3), 1.0, 0.0)
    return to_perm.astype(BF16), to_time.astype(BF16)


def _move_rows(sel, x_bf16):
    return jnp.dot(sel, x_bf16, preferred_element_type=F32).astype(BF16)


def _conv_perm(tail_ref, x, w_ref, b_ref):
    n, ch = x.shape
    nb = n // SUBLANE
    x3 = x.reshape(nb, SUBLANE, ch)
    tail8 = tail_ref[...]
    sub = lax.broadcasted_iota(jnp.int32, (SUBLANE, ch), 0)
    y = b_ref[...].reshape(1, 1, ch) + w_ref[TAIL:TAIL + 1, :].reshape(1, 1, ch) * x3
    wrapped = [jnp.where(sub >= 1, pltpu.roll(x3[nb - d], 1, 0), tail8[SUBLANE - d:SUBLANE - d + 1, :])
               for d in range(1, CONV_W)]
    for back in range(1, CONV_W):
        head = jnp.stack([wrapped[back - j - 1] for j in range(back)], axis=0)
        shifted = jnp.concatenate([head, x3[:nb - back]], axis=0)
        y = y + w_ref[TAIL - back:TAIL - back + 1, :].reshape(1, 1, ch) * shifted
    for d in range(1, CONV_W):
        tail_ref[SUBLANE - d:SUBLANE - d + 1, :] = x3[nb - d][SUBLANE - 1:SUBLANE, :]
    return y.reshape(n, ch)


def _scan_perm(a, u, h_prev):
    n, ch = a.shape
    nb = n // SUBLANE
    a3 = a.reshape(nb, SUBLANE, ch)
    u3 = u.reshape(nb, SUBLANE, ch)
    local = [u3[0]]
    decay = [a3[0]]
    for j in range(1, nb):
        local.append(a3[j] * local[-1] + u3[j])
        decay.append(a3[j] * decay[-1])
    seg_u, seg_a = local[-1], decay[-1]
    sub = lax.broadcasted_iota(jnp.int32, (SUBLANE, ch), 0)
    shift = 1
    while shift < SUBLANE:
        keep = sub >= shift
        seg_u = seg_u + seg_a * jnp.where(keep, pltpu.roll(seg_u, shift, 0), 0.0)
        seg_a = seg_a * jnp.where(keep, pltpu.roll(seg_a, shift, 0), 1.0)
        shift *= 2
    seg_end = seg_a * h_prev + seg_u
    carry = jnp.where(sub >= 1, pltpu.roll(seg_end, 1, 0), h_prev)
    h3 = jnp.stack([local[j] + decay[j] * carry for j in range(nb)], axis=0)
    return h3.reshape(n, ch), seg_end[SUBLANE - 1:SUBLANE, :]


N_L1_W = 10
N_L1_S = 2


N_STREAM_IN = 5
N_STREAM_OUT = 2


def _l1_prompt_kernel(h_ref, hnext_ref, *refs, front_pad, rows, stream):
    w_refs = refs[:N_L1_W]
    init_refs = refs[N_L1_W:N_L1_W + N_L1_S]
    pos = N_L1_W + N_L1_S
    stream_in = refs[pos:pos + N_STREAM_IN] if stream else ()
    pos += len(stream_in)
    out_refs = refs[pos:pos + N_L1_S + 1]
    pos += N_L1_S + 1
    stream_out = refs[pos:pos + N_STREAM_OUT] if stream else ()
    pos += len(stream_out)
    scratch = refs[pos:]
    per_row = len(scratch) // rows
    c = pl.program_id(1)
    fnw_ref, win_ref, wout_ref = w_refs[1], w_refs[2], w_refs[3]
    q_len = h_ref.shape[1]
    gw = LRU_WIDTH // L1_GROUPS
    pieces = {}

    def each_row(phase):
        return [_l1_prompt_row(h_ref.at[r], hnext_ref.at[r], *w_refs, *init_refs, *(o.at[r] for o in out_refs),
                               *scratch[r * per_row:(r + 1) * per_row], front_pad=front_pad, phase=phase,
                               emit=lambda g, y, r=r: pieces.setdefault(g, {}).__setitem__(r, y))
                for r in range(rows)]

    @pl.when(c == 0)
    def _():
        each_row("init")

    bodies = each_row("body")
    proj_refs = [scratch[r * per_row + per_row - 1] for r in range(rows)]
    _, to_time = _perm_matrices(q_len)
    partials = []

    def in_proj():
        lhs = jnp.concatenate([hn_next for _, hn_next, _ in bodies], axis=0)
        for g in range(L1_GROUPS):
            for lo in (g * gw, LRU_WIDTH + g * gw):
                res = _bdot(lhs, _wload(win_ref[:, lo:lo + gw]))
                for r in range(rows):
                    proj_refs[r][:, lo:lo + gw] = res[r * q_len:(r + 1) * q_len]
            yield

    def out_proj():
        pending = list(range(L1_GROUPS))
        while pending:
            for g in list(pending):
                if len(pieces.get(g, ())) == rows:
                    y_t = jnp.concatenate([_move_rows(to_time, pieces[g][r].astype(BF16)) for r in range(rows)],
                                          axis=0)
                    partials.append(_bdot(y_t, _wload(wout_ref[g * gw // 2:(g + 1) * gw // 2, :])))
                    pending.remove(g)
            yield

    chains = [bodies[r][0][g] for g in range(L1_GROUPS) for r in range(rows)]
    if stream:
        step = pl.program_id(0) * pl.num_programs(1) + c
        chains.append(_mlstm_state_update(*stream_in, *stream_out, base=step * stream_in[1].shape[0]))
    _run_staggered([in_proj()] + chains + [out_proj()])
    total = partials[0]
    for part in partials[1:]:
        total = total + part
    for r, (_, _, h_in) in enumerate(bodies):
        out_refs[0][r] = _rms(h_in + total[r * q_len:(r + 1) * q_len], fnw_ref[...])

    @pl.when(c == pl.num_programs(1) - 1)
    def _():
        each_row("final")


def _l1_in_proj(h_val, nw_ref, win_ref, to_perm):
    hn = _move_rows(to_perm, _rms(h_val, nw_ref[...]).astype(BF16))
    return _bdot(hn, _wload(win_ref[...]))


def _l1_prompt_row(h_ref, hnext_ref, nw_ref, fnw_ref, win_ref, wout_ref, cw_ref, cb_ref,
                   wax_ref, ba_ref, bx_ref, lam_ref, ilc_ref, ilh_ref,
                   y_ref, olc_ref, olh_ref, lbuf, h_st, proj_s, *, front_pad, phase, emit):
    if phase == "init":
        lbuf[...] = jnp.zeros(lbuf.shape, F32)
        lbuf[SUBLANE - TAIL:SUBLANE, :] = ilc_ref[0]
        h_st[...] = ilh_ref[0]
        proj_s[...] = _l1_in_proj(h_ref[...], nw_ref, win_ref, _perm_matrices(h_ref.shape[0])[0])
        return None
    if phase == "final":
        olc_ref[...] = lbuf[SUBLANE - TAIL:SUBLANE, :]
        olh_ref[...] = h_st[...]
        return None

    q_len = h_ref.shape[0]
    h_in = h_ref[...]
    to_perm, _ = _perm_matrices(q_len)
    hn_next = _move_rows(to_perm, _rms(hnext_ref[...], nw_ref[...]).astype(BF16))
    if front_pad:
        valid = _perm_time(q_len) >= front_pad
    gw = LRU_WIDTH // L1_GROUPS
    tiles = gw // LANE
    gates = [proj_s[:, g * gw:(g + 1) * gw] for g in range(L1_GROUPS)]
    xrs = [proj_s[:, LRU_WIDTH + g * gw:LRU_WIDTH + (g + 1) * gw] for g in range(L1_GROUPS)]

    def group(g):
        cg = slice(g * gw, (g + 1) * gw)
        xc = _conv_perm(lbuf.at[:, cg], xrs[g], cw_ref.at[:, cg], cb_ref.at[:, cg])
        ra, ix = _blockdiag_tiles(xc, wax_ref.at[g * tiles:(g + 1) * tiles])
        yield
        a, u = _rglru_gates(xc, ra, ix, ba_ref.at[:, cg], bx_ref.at[:, cg], lam_ref.at[:, cg])
        if front_pad:
            a = jnp.where(valid, a, 1.0)
            u = jnp.where(valid, u, 0.0)
        yield
        h, h_last = _scan_perm(a, u, h_st[:, cg])
        h_st[:, cg] = h_last
        yield
        emit(g, h * _silu(gates[g]))

    return [group(g) for g in range(L1_GROUPS)], hn_next, h_in


def _run_staggered(gens):
    live = []
    pending = list(gens)
    while pending or live:
        if pending:
            live.append(pending.pop(0))
        for gen in list(live):
            if next(gen, "done") == "done":
                live.remove(gen)


L1_STATE_SHAPES = ((TAIL, LRU_WIDTH), (1, LRU_WIDTH))


def _l1_prompt(h1, weights, init, front_pad, stream_in=(), stream_out_shapes=()):
    bsz, length, _ = h1.shape
    q_len = min(CHUNK, length)
    assert length % q_len == 0
    rows = _rows_per_step(bsz, L1_ROWS)
    assert len(weights) == N_L1_W and len(init) == N_L1_S
    nc = length // q_len
    grid = (bsz // rows, nc)
    n_steps = grid[0] * nc
    last = nc - 1
    x_spec = pl.BlockSpec((rows, q_len, D_MODEL), lambda b, c: (b, c, 0))
    next_spec = pl.BlockSpec((rows, q_len, D_MODEL), lambda b, c: (b, jnp.minimum(c + 1, last), 0))
    stream_outs = [jax.ShapeDtypeStruct(s, F32) for s in stream_out_shapes]
    in_specs = ([x_spec, next_spec] + [_const_spec(w.shape) for w in weights]
                + [_state_spec(s, 0) for s in L1_STATE_SHAPES] + _stream_specs(stream_in, n_steps, nc))
    out_shape = ([jax.ShapeDtypeStruct((bsz, length, D_MODEL), F32)]
                 + [jax.ShapeDtypeStruct((bsz,) + s, F32) for s in L1_STATE_SHAPES] + stream_outs)
    out_specs = ([x_spec] + [_state_spec(s, rows) for s in L1_STATE_SHAPES]
                 + _stream_specs(stream_outs, n_steps, nc))
    row_scratch = ((SUBLANE, LRU_WIDTH), (1, LRU_WIDTH), (q_len, 2 * LRU_WIDTH))
    scratch = [pltpu.VMEM(s, F32) for _ in range(rows) for s in row_scratch]
    return pl.pallas_call(
        functools.partial(_l1_prompt_kernel, front_pad=front_pad, rows=rows, stream=bool(stream_in)),
        grid=grid, in_specs=in_specs, out_specs=out_specs, out_shape=out_shape, scratch_shapes=scratch,
        compiler_params=pltpu.CompilerParams(dimension_semantics=("arbitrary", "arbitrary"),
                                             vmem_limit_bytes=VMEM_LIMIT),
        name="l1_prompt",
    )(h1, h1, *weights, *init, *stream_in)


def _conv_step(buf_ref, x, w_ref, b_ref, newbuf_ref):
    y = b_ref[...] + w_ref[3:4, :] * x
    for tap in range(TAIL):
        y = y + w_ref[tap:tap + 1, :] * buf_ref[tap]
    for tap in range(TAIL - 1):
        newbuf_ref[tap] = buf_ref[tap + 1]
    newbuf_ref[TAIL - 1] = x
    return y


def _l0_sample_pre_kernel(x_ref, nw_ref, win_ref, scw_ref, scb_ref, dtb_ref, alog_ref,
                          mcw_ref, mcb_ref, wqk_ref, wv_ref, wg_ref, bg_ref,
                          sbuf_ref, mbuf_ref, m0_ref, n0_ref,
                          nsb_ref, nmb_ref, zs_ref, xs_ref, bm_ref, cm_ref, xdt_t_ref, dec_t_ref,
                          zm_ref, xc_ref, q_ref, isv_t_ref, fs_t_ref, k_ref, mnew_ref, nnew_ref, den_ref):
    x = x_ref[...]
    hn = _rms(x, nw_ref[...])
    proj = _bdot(hn, _wload(win_ref[...]))
    zs_ref[...] = proj[:, OFF_ZS:OFF_ZS + SSD_WIDTH]
    zm_ref[...] = proj[:, OFF_ZM:OFF_ZM + ML_WIDTH]
    xbc = proj[:, OFF_XBC:OFF_XBC + SSD_CONV_CH]
    dt_raw = proj[:, OFF_DT:OFF_DT + LANE]
    xm = proj[:, OFF_XM:OFF_XM + ML_WIDTH]
    expand = _expand_matrix()

    xbc = _silu(_conv_step(sbuf_ref, xbc, scw_ref, scb_ref, nsb_ref))
    xs = xbc[:, :SSD_WIDTH]
    xs_ref[...] = xs
    bm_ref[...] = xbc[:, SSD_WIDTH:SSD_WIDTH + SSD_GROUPS * SSD_STATE]
    cm_ref[...] = xbc[:, SSD_WIDTH + SSD_GROUPS * SSD_STATE:]
    dt = _softplus(dt_raw + dtb_ref[...])
    log_a = -dt * jnp.exp(alog_ref[...])
    xdt_t_ref[...] = xs * _expand_heads(dt, expand)
    dec_t_ref[...] = jnp.exp(log_a)

    xc = _silu(_conv_step(mbuf_ref, xm, mcw_ref, mcb_ref, nmb_ref))
    xc_ref[...] = xc
    q, k, v, ig, logf = _mlstm_qkv_gates(xm, xc, wqk_ref, wv_ref, wg_ref, bg_ref)
    m0 = m0_ref[...]
    m_new = jnp.maximum(logf + m0, ig)
    fs = jnp.exp(logf + m0 - m_new)
    is_ = jnp.exp(ig - m_new)
    mnew_ref[...] = m_new
    r = lax.broadcasted_iota(jnp.int32, (LANE, ML_WIDTH), 0)
    cidx = lax.broadcasted_iota(jnp.int32, (LANE, ML_WIDTH), 1)
    expand_m = jnp.where(lax.shift_right_logical(cidx, 8) == r, 1.0, 0.0).astype(BF16)
    fs_e = _expand_heads(fs, expand_m)
    is_e = _expand_heads(is_, expand_m)
    n_new = fs_e * n0_ref[...] + is_e * k
    nnew_ref[...] = n_new
    q_ref[...] = q
    k_ref[...] = k
    isv_t_ref[...] = is_e * v
    fs_t_ref[...] = fs
    nq = n_new * q
    floor = jnp.exp(-m_new)
    for hd in range(ML_HEADS):
        den = jnp.sum(nq[:, hd * ML_HEAD_DIM:(hd + 1) * ML_HEAD_DIM], axis=-1, keepdims=True)
        den_ref[:, hd:hd + 1] = jnp.maximum(jnp.abs(den), floor[:, hd:hd + 1])


def _rows_to_tile(rows8):
    return jnp.concatenate([rows8] + [jnp.zeros_like(rows8)] * (LANE // SUBLANE - 1), axis=0)


def _ssd_state_update(dec_ref, s_ref, xdt_ref, bm_ref, cm_ref, snew_ref, y_ref, *, base):
    n = s_ref.shape[0]
    half = SSD_WIDTH // SSD_GROUPS
    x_cols = _rows_to_tile(xdt_ref[0]).T
    lane = lax.broadcasted_iota(jnp.int32, (half, LANE), 1)
    accs = [jnp.zeros((half, LANE), F32) for _ in range(SSD_GROUPS)]
    for i in range(n):
        x_col = x_cols[:, i:i + 1].reshape(SSD_HEADS, SSD_HEAD_DIM, 1)
        for g in range(SSD_GROUPS):
            hs = slice(g * SSD_HPG, (g + 1) * SSD_HPG)
            gs = slice(g * SSD_STATE, (g + 1) * SSD_STATE)
            b_row = bm_ref[0, i:i + 1, gs].reshape(1, 1, SSD_STATE)
            decay = jnp.stack([jnp.full((1, 1), dec_ref[base + i, hd], F32)
                               for hd in range(hs.start, hs.stop)], axis=0)
            s_new = decay * s_ref[i, hs] + x_col[hs] * b_row
            snew_ref[i, hs] = s_new
            prod = _bdot_nt(s_new.reshape(half, SSD_STATE), _rows_to_tile(cm_ref[0, :, gs]))
            accs[g] = jnp.where(lane == i, prod, accs[g])
            yield
    y_ref[0] = jnp.concatenate(accs, axis=0).T[:SUBLANE]


def _mlstm_state_update(fs_ref, c_ref, isv_ref, k_ref, q_ref, cnew_ref, num_ref, *, base):
    n = c_ref.shape[0]
    v_cols = _rows_to_tile(isv_ref[0]).T
    lane = lax.broadcasted_iota(jnp.int32, (ML_HEAD_DIM, LANE), 1)
    for hd in range(ML_HEADS):
        sl = slice(hd * ML_HEAD_DIM, (hd + 1) * ML_HEAD_DIM)
        q_rows = _rows_to_tile(q_ref[0, :, sl])
        acc = jnp.zeros((ML_HEAD_DIM, LANE), F32)
        for i in range(n):
            c_new = fs_ref[base + i, hd] * c_ref[i, hd] + v_cols[sl, i:i + 1] * k_ref[0, i:i + 1, sl]
            cnew_ref[i, hd] = c_new
            acc = jnp.where(lane == i, _bdot_nt(c_new, q_rows), acc)
            yield
        num_ref[0, :, sl] = acc.T[:SUBLANE]


def _sample_post_kernel(x_ref, ys_t_ref, num_t_ref, den_ref, zs_ref, xs_ref, zm_ref, xc_ref,
                        dsk_ref, snw_ref, msk_ref, mnw_ref, wout_ref,
                        nw1_ref, fnw_ref, win1_ref, wout1_ref, cw_ref, cb_ref,
                        wax_ref, ba_ref, bx_ref, lam_ref, lbuf_ref, h0_ref,
                        y_ref, nlb_ref, hnew_ref):
    xs = xs_ref[...]
    y_s = ys_t_ref[...] + dsk_ref[...] * xs
    y_s = _group_rmsnorm(y_s * _silu(zs_ref[...]), snw_ref[...])
    num = num_t_ref[...]
    den = den_ref[...]
    h_m = jnp.concatenate(
        [num[:, hd * ML_HEAD_DIM:(hd + 1) * ML_HEAD_DIM] / den[:, hd:hd + 1] for hd in range(ML_HEADS)], axis=-1)
    h_m = _head_layernorm(h_m) * mnw_ref[...]
    y_m = (h_m + msk_ref[...] * xc_ref[...]) * _silu(zm_ref[...])
    h1 = x_ref[...] + _bdot(jnp.concatenate([y_s, y_m], axis=-1), _wload(wout_ref[...]))

    hn = _rms(h1, nw1_ref[...])
    proj = _bdot(hn, _wload(win1_ref[...]))
    gate = proj[:, :LRU_WIDTH]
    xr = proj[:, LRU_WIDTH:]
    xc = _conv_step(lbuf_ref, xr, cw_ref, cb_ref, nlb_ref)
    ra, ix = _blockdiag_tiles(xc, wax_ref)
    a, u = _rglru_gates(xc, ra, ix, ba_ref, bx_ref, lam_ref)
    h = a * h0_ref[...] + u
    hnew_ref[...] = h
    h2 = h1 + _bdot(h * _silu(gate), _wload(wout1_ref[...]))
    y_ref[...] = _rms(h2, fnw_ref[...])


def _full_call(kernel_fn, out_shapes, args, name):
    return pl.pallas_call(
        kernel_fn,
        out_shape=[jax.ShapeDtypeStruct(s, F32) for s in out_shapes],
        compiler_params=pltpu.CompilerParams(vmem_limit_bytes=VMEM_LIMIT),
        name=name,
    )(*args)


def _row(v, width=None):
    v = v.reshape(1, -1).astype(F32)
    if width is not None and v.shape[1] < width:
        v = jnp.pad(v, ((0, 0), (0, width - v.shape[1])))
    return v


PACK_STEPS = 8


def _pack_all(w_in_t, weights):
    flats = [w.reshape(-1, w.shape[-1]) for w in weights]
    k_in = w_in_t.shape[1]
    assert k_in == PACK_STEPS * LANE
    for f in flats:
        assert f.shape[0] % (2 * SUBLANE * PACK_STEPS) == 0
    packed = pl.pallas_call(
        _pack_kernel,
        grid=(PACK_STEPS,),
        in_specs=([pl.BlockSpec((w_in_t.shape[0], LANE), lambda i: (0, i))]
                  + [pl.BlockSpec((f.shape[0] // PACK_STEPS, f.shape[1]), lambda i: (i, 0)) for f in flats]),
        out_specs=([pl.BlockSpec((LANE // 2, IN_MIX_PAD), lambda i: (i, 0))]
                   + [pl.BlockSpec((f.shape[0] // PACK_STEPS // 2, f.shape[1]), lambda i: (i, 0)) for f in flats]),
        out_shape=([jax.ShapeDtypeStruct((k_in // 2, IN_MIX_PAD), jnp.uint32)]
                   + [jax.ShapeDtypeStruct((f.shape[0] // 2, f.shape[1]), jnp.uint32) for f in flats]),
        compiler_params=pltpu.CompilerParams(vmem_limit_bytes=VMEM_LIMIT),
        name="pack_weights",
    )(w_in_t, *flats)
    return [packed[0]] + [p.reshape(w.shape[:-2] + (w.shape[-2] // 2, w.shape[-1]))
                          for p, w in zip(packed[1:], weights)]


def _pack_rows(x):
    return pltpu.bitcast(x.astype(BF16), jnp.uint32)


def _pack_kernel(*refs):
    n = len(refs) // 2
    win_ref, wino_ref = refs[0], refs[n]
    wino_ref[:, :OFF_DT] = _pack_rows(win_ref[:OFF_DT, :].T)
    dt_tile = win_ref[OFF_DT:OFF_DT + LANE, :].T
    lane = lax.broadcasted_iota(jnp.int32, dt_tile.shape, 1)
    wino_ref[:, OFF_DT:OFF_ZM] = _pack_rows(jnp.where(lane < SSD_HEADS, dt_tile, 0.0))
    wino_ref[:, OFF_ZM:] = _pack_rows(win_ref[OFF_DT + SSD_HEADS:, :].T)
    for w_ref, o_ref in zip(refs[1:n], refs[n + 1:]):
        o_ref[...] = _pack_rows(w_ref[...])


def _dense_block_tiles(w):
    nb, bi, bo = w.shape
    per = LANE // bi
    rows = w.reshape(nb // per, per * bi, bo)
    col = jnp.arange(per * bo)
    spread = (col[None, :] % bo == jnp.arange(bo)[:, None]).astype(w.dtype)
    rep = jnp.einsum('tro,oc->trc', rows, spread)
    same_block = (jnp.arange(per * bi)[:, None] // bi) == (col[None, :] // bo)
    return jnp.where(same_block, rep, 0.0)


def kernel(x_prompt, x_sample, state_ssd_conv, state_ssd, state_mlstm_conv, state_mlstm_C, state_mlstm_n,
           state_mlstm_m, state_lru_conv, state_lru_h, meta_tokens, norm_w, final_norm_w, w_in_mix, w_out_mix,
           ssd_conv_w, ssd_conv_b, ssd_dt_bias, ssd_a_log, ssd_d, ssd_norm_w, ml_conv_w, ml_conv_b, ml_wq, ml_wk,
           ml_wv, ml_w_gate, ml_b_gate, ml_skip, ml_norm_w, lru_w_in, lru_w_out, lru_conv_w, lru_conv_b, lru_wa,
           lru_ba, lru_wx, lru_bx, lru_lambda):
    bsz = x_prompt.shape[0]
    dec = x_sample.shape[0]

    wout = w_out_mix[0]
    nw0 = _row(norm_w[0])
    nw1 = _row(norm_w[1])
    fnw = _row(final_norm_w)
    scw = ssd_conv_w[0]
    scb = _row(ssd_conv_b[0])
    dtb = _row(ssd_dt_bias[0], LANE)
    alog = _row(ssd_a_log[0], LANE)
    dsk = _row(jnp.repeat(ssd_d[0], SSD_HEAD_DIM))
    snw = _row(ssd_norm_w[0])
    mcw = ml_conv_w[0]
    mcb = _row(ml_conv_b[0])
    wqk = jnp.concatenate([_dense_block_tiles(ml_wq[0]), _dense_block_tiles(ml_wk[0])], axis=2)
    wv = _dense_block_tiles(ml_wv[0])
    wg_raw = ml_w_gate[0]
    wg = jnp.concatenate([jnp.pad(wg_raw[:, :ML_HEADS], ((0, 0), (0, LANE - ML_HEADS))),
                          jnp.pad(wg_raw[:, ML_HEADS:], ((0, 0), (0, LANE - ML_HEADS)))], axis=1)
    bg = jnp.concatenate([_row(ml_b_gate[0, :ML_HEADS], LANE), _row(ml_b_gate[0, ML_HEADS:], LANE)], axis=1)
    msk = _row(ml_skip[0])
    mnw = _row(ml_norm_w[0])
    win1 = lru_w_in[0]
    wout1 = lru_w_out[0]
    lcw = lru_conv_w[0]
    lcb = _row(lru_conv_b[0])
    wax = jnp.concatenate([lru_wa[0], lru_wx[0]], axis=2)
    r_idx = lax.broadcasted_iota(jnp.int32, (LANE, SSD_WIDTH), 0)
    c_idx = lax.broadcasted_iota(jnp.int32, (LANE, SSD_WIDTH), 1)
    expand = (c_idx // SSD_HEAD_DIM == r_idx).astype(F32)
    ba = _row(lru_ba[0])
    bx = _row(lru_bx[0])
    lam = _row(lru_lambda[0])

    win, wout, wqk, wv, wg, expand, win1, wout1, wax = _pack_all(
        jnp.swapaxes(w_in_mix[0], 0, 1), [wout, wqk, wv, wg, expand, win1, wout1, wax])
    l0_w = (nw0, win, wout, scw, scb, dtb, alog, dsk, snw, mcw, mcb, wqk, wv, wg, bg, msk, mnw, expand)
    l1_w = (nw1, fnw, win1, wout1, lcw, lcb, wax, ba, bx, lam)

    xs2 = x_sample[:, 0]
    sbuf = jnp.moveaxis(state_ssd_conv[0], 1, 0)
    mbuf = jnp.moveaxis(state_mlstm_conv[0], 1, 0)
    lbuf = jnp.moveaxis(state_lru_conv[0], 1, 0)
    m0 = jnp.pad(state_mlstm_m[0], ((0, 0), (0, LANE - ML_HEADS)))
    n0 = state_mlstm_n[0].reshape(dec, ML_WIDTH)
    pre_shapes = ((TAIL, dec, SSD_CONV_CH), (TAIL, dec, ML_WIDTH), (dec, SSD_WIDTH), (dec, SSD_WIDTH),
                  (dec, SSD_GROUPS * SSD_STATE), (dec, SSD_GROUPS * SSD_STATE), (dec, SSD_WIDTH), (dec, LANE),
                  (dec, ML_WIDTH), (dec, ML_WIDTH), (dec, ML_WIDTH), (dec, ML_WIDTH), (dec, LANE),
                  (dec, ML_WIDTH), (dec, LANE), (dec, ML_WIDTH), (dec, ML_HEADS))
    (nsb, nmb, zs, xs_c, bm, cm, xdt_t, dec_t, zm, xc_m, q, isv_t, fs_t, k, m_new, n_new, den) = _full_call(
        _l0_sample_pre_kernel, pre_shapes,
        (xs2, nw0, win, scw, scb, dtb, alog, mcw, mcb, wqk, wv, wg, bg, sbuf, mbuf, m0, n0), "l0_sample_pre")

    zero0 = tuple(jnp.zeros((1,) + s, F32) for s in L0_STATE_SHAPES)
    zero1 = tuple(jnp.zeros((1,) + s, F32) for s in L1_STATE_SHAPES)
    meta = jnp.pad(meta_tokens.astype(F32), ((CHUNK - N_META, 0), (0, 0)))[None]
    meta_out = _l0_prompt(meta, l0_w, zero0, CHUNK - N_META)
    meta1_out = _l1_prompt(meta_out[0], l1_w, zero1, CHUNK - N_META)

    n_steps = (bsz // _rows_per_step(bsz, L0_ROWS)) * (x_prompt.shape[1] // CHUNK)
    assert n_steps == (bsz // _rows_per_step(bsz, L1_ROWS)) * (x_prompt.shape[1] // CHUNK)
    per_step = dec // n_steps
    assert per_step * n_steps == dec

    def step_rows(a):
        a = a.reshape(n_steps, per_step, a.shape[-1])
        return jnp.pad(a, ((0, 0), (0, SUBLANE - per_step), (0, 0)))

    def from_step_rows(a):
        return a[:, :per_step].reshape(dec, a.shape[-1])

    l0_out = _l0_prompt(
        x_prompt, l0_w, tuple(meta_out[1:]), 0,
        stream_in=(dec_t[:, :SSD_HEADS], state_ssd[0], step_rows(xdt_t), step_rows(bm), step_rows(cm)),
        stream_out_shapes=((dec, SSD_HEADS, SSD_HEAD_DIM, SSD_STATE), (n_steps, SUBLANE, SSD_WIDTH)))
    h1_p, p_sc, p_s, p_mc, p_c, p_n, p_m, s_new, ys_r = l0_out
    y_prompt, p_lc, p_lh, c_new, num_r = _l1_prompt(
        h1_p, l1_w, tuple(meta1_out[1:]), 0,
        stream_in=(fs_t[:, :ML_HEADS], state_mlstm_C[0], step_rows(isv_t), step_rows(k), step_rows(q)),
        stream_out_shapes=((dec, ML_HEADS, ML_HEAD_DIM, ML_HEAD_DIM), (n_steps, SUBLANE, ML_WIDTH)))

    p_m = p_m[:, 0, :ML_HEADS]
    p_lh = p_lh[:, 0]

    post_shapes = ((dec, D_MODEL), (TAIL, dec, LRU_WIDTH), (dec, LRU_WIDTH))
    y_s2, nlb, h_new = _full_call(
        _sample_post_kernel, post_shapes,
        (xs2, from_step_rows(ys_r), from_step_rows(num_r), den, zs, xs_c, zm, xc_m, dsk, snw, msk, mnw, wout,
         nw1, fnw, win1, wout1, lcw, lcb, wax, ba, bx, lam, lbuf, state_lru_h[0]), "sample_post")

    s_sc = jnp.moveaxis(nsb, 0, 1)[None]
    s_mc = jnp.moveaxis(nmb, 0, 1)[None]
    s_lc = jnp.moveaxis(nlb, 0, 1)[None]
    return (y_prompt, y_s2[:, None, :],
            p_sc[None], p_s[None], p_mc[None], p_c[None], p_n[None], p_m[None], p_lc[None], p_lh[None],
            s_sc, s_new[None], s_mc, c_new[None], n_new.reshape(dec, ML_HEADS, ML_HEAD_DIM)[None],
            m_new[:, :ML_HEADS][None], s_lc, h_new[None])
```

```python
import functools

import jax
import jax.numpy as jnp
from jax import lax
from jax.experimental import pallas as pl
from jax.experimental.pallas import tpu as pltpu

F32 = jnp.float32
BF16 = jnp.bfloat16

D_MODEL = 1024
N_META = 16
CONV_W = 4
EPS = 1e-6
NEG = -1e30
SSD_WIDTH = 1024
SSD_HEAD_DIM = 64
SSD_HEADS = 16
SSD_GROUPS = 2
SSD_HPG = 8
SSD_STATE = 128
SSD_CONV_CH = 1536
ML_WIDTH = 1024
ML_HEADS = 4
ML_HEAD_DIM = 256
ML_QKV_BLOCK = 4
LRU_WIDTH = 2048
LRU_BLOCKS = 16
LRU_BLOCK = 128
LRU_C = 8.0

LANE = 128
SUBLANE = 8
CHUNK = 128
L0_ROWS = 2
L1_ROWS = 2
L1_GROUPS = 8
SUBCHAINS_PER_ROUND = 2
TAIL = CONV_W - 1

OFF_ZS = 0
OFF_XBC = OFF_ZS + SSD_WIDTH
OFF_DT = OFF_XBC + SSD_CONV_CH
OFF_ZM = OFF_DT + LANE
OFF_XM = OFF_ZM + ML_WIDTH
IN_MIX_PAD = OFF_XM + ML_WIDTH

VMEM_LIMIT = 56 * 1024 * 1024
L0_VMEM_LIMIT = 61 * 1024 * 1024


def _sigmoid(x):
    return 1.0 / (1.0 + jnp.exp(-x))


def _silu(x):
    return x * _sigmoid(x)


def _softplus(x):
    return jnp.maximum(x, 0.0) + jnp.log1p(jnp.exp(-jnp.abs(x)))


def _rms(x, w):
    return x * lax.rsqrt(jnp.mean(x * x, axis=-1, keepdims=True) + EPS) * w


def _bdot(a, b):
    return jnp.dot(a.astype(BF16), b.astype(BF16), preferred_element_type=F32)


def _bdot_nt(a, b):
    return lax.dot_general(a.astype(BF16), b.astype(BF16), (((1,), (1,)), ((), ())), preferred_element_type=F32)


def _wload(w):
    return pltpu.bitcast(w, BF16)


def _split3(x):
    hi = x.astype(BF16)
    r = x - hi.astype(F32)
    mid = r.astype(BF16)
    lo = (r - mid.astype(F32)).astype(BF16)
    return hi, mid, lo


def _cumsum_rows(x, tril):
    hi, mid, lo = _split3(x)
    d = functools.partial(jnp.dot, preferred_element_type=F32)
    return d(tril, hi) + d(tril, mid) + d(tril, lo)


def _expand_heads(x, expand):
    hi, mid, _ = _split3(x)
    d = functools.partial(jnp.dot, preferred_element_type=F32)
    return d(hi, expand) + d(mid, expand)


def _expand_matrix():
    r = lax.broadcasted_iota(jnp.int32, (LANE, SSD_WIDTH), 0)
    c = lax.broadcasted_iota(jnp.int32, (LANE, SSD_WIDTH), 1)
    return jnp.where(lax.shift_right_logical(c, 6) == r, 1.0, 0.0).astype(BF16)


def _blockdiag_tiles(x, w_ref):
    k = w_ref.shape[0]
    m = w_ref.shape[2] // LANE
    prods = [_bdot(x[:, t * LANE:(t + 1) * LANE], _wload(w_ref[t])) for t in range(k)]
    return [jnp.concatenate([p[:, j * LANE:(j + 1) * LANE] for p in prods], axis=-1) for j in range(m)]


def _group_rmsnorm(y, w):
    half = SSD_WIDTH // SSD_GROUPS
    parts = []
    for g in range(SSD_GROUPS):
        yg = y[:, g * half:(g + 1) * half]
        parts.append(yg * lax.rsqrt(jnp.mean(yg * yg, axis=-1, keepdims=True) + EPS))
    return jnp.concatenate(parts, axis=-1) * w


def _head_layernorm(h):
    parts = []
    for k in range(ML_HEADS):
        hk = h[:, k * ML_HEAD_DIM:(k + 1) * ML_HEAD_DIM]
        mu = jnp.mean(hk, axis=-1, keepdims=True)
        d = hk - mu
        var = jnp.mean(d * d, axis=-1, keepdims=True)
        parts.append(d * lax.rsqrt(var + EPS))
    return jnp.concatenate(parts, axis=-1)


def _mlstm_qkv_gates(xm, xc, wqk_ref, wv_ref, wg_ref, bg_ref):
    q, k = _blockdiag_tiles(xc, wqk_ref)
    v, = _blockdiag_tiles(xm, wv_ref)
    gates = _bdot(jnp.concatenate([q, k, v], axis=-1), _wload(wg_ref[...])) + bg_ref[...]
    ig = gates[:, :LANE]
    logf = -_softplus(-gates[:, LANE:])
    return q, k * (ML_HEAD_DIM ** -0.5), v, ig, logf


N_L0_W = 18
N_L0_S = 6


def _l0_prompt_kernel(x_ref, xnext_ref, *refs, front_pad, rows, stream):
    w_refs = refs[:N_L0_W]
    init_refs = refs[N_L0_W:N_L0_W + N_L0_S]
    pos = N_L0_W + N_L0_S
    stream_in = refs[pos:pos + N_STREAM_IN] if stream else ()
    pos += len(stream_in)
    out_refs = refs[pos:pos + N_L0_S + 1]
    pos += N_L0_S + 1
    stream_out = refs[pos:pos + N_STREAM_OUT] if stream else ()
    pos += len(stream_out)
    scratch = refs[pos:]
    per_row = len(scratch) // rows
    c = pl.program_id(1)
    win_ref, wout_ref = w_refs[1], w_refs[2]
    q_len = x_ref.shape[1]
    pieces = {}

    def each_row(phase):
        return [_l0_prompt_row(x_ref.at[r], xnext_ref.at[r], *w_refs, *init_refs, *(o.at[r] for o in out_refs),
                               *scratch[r * per_row:(r + 1) * per_row], front_pad=front_pad, phase=phase,
                               emit=lambda k0, y, r=r: pieces.setdefault(k0, {}).__setitem__(r, y))
                for r in range(rows)]

    @pl.when(c == 0)
    def _():
        each_row("init")

    bodies = each_row("body")
    proj_refs = [scratch[r * per_row + per_row - 1] for r in range(rows)]
    _, to_time = _perm_matrices(q_len)
    partials = []

    def in_proj():
        lhs = jnp.concatenate([hn_next for _, hn_next, _ in bodies], axis=0)
        for lo, hi in L0_PROJ_PIECES:
            res = _bdot(lhs, _wload(win_ref[:, lo:hi]))
            for r in range(rows):
                proj_refs[r][:, lo:hi] = res[r * q_len:(r + 1) * q_len]
            yield

    def out_proj():
        half = SSD_WIDTH // SSD_GROUPS
        pending = ([(g * half, half) for g in range(SSD_GROUPS)]
                   + [(SSD_WIDTH + hd * ML_HEAD_DIM, ML_HEAD_DIM) for hd in range(ML_HEADS)])
        while pending:
            for k0, width in list(pending):
                if len(pieces.get(k0, ())) == rows:
                    y_t = jnp.concatenate([_move_rows(to_time, pieces[k0][r].astype(BF16)) for r in range(rows)],
                                          axis=0)
                    partials.append(_bdot(y_t, _wload(wout_ref[k0 // 2:(k0 + width) // 2, :])))
                    pending.remove((k0, width))
            yield

    chains = [gen for gens, _, _ in bodies for gen in gens] + [in_proj(), out_proj()]
    if stream:
        step = pl.program_id(0) * pl.num_programs(1) + c
        chains.append(_ssd_state_update(*stream_in, *stream_out, base=step * stream_in[1].shape[0]))
    _run_round_robin(chains)
    total = partials[0]
    for part in partials[1:]:
        total = total + part
    for r, (_, _, x) in enumerate(bodies):
        h1 = x + total[r * q_len:(r + 1) * q_len]
        if front_pad:
            h1 = jnp.where(lax.broadcasted_iota(jnp.int32, (q_len, 1), 0) >= front_pad, h1, 0.0)
        out_refs[0][r] = h1

    @pl.when(c == pl.num_programs(1) - 1)
    def _():
        each_row("final")


L0_PROJ_PIECES = ((OFF_XBC, OFF_ZM), (OFF_XM, IN_MIX_PAD), (OFF_ZM, OFF_XM), (OFF_ZS, OFF_XBC))


def _run_round_robin(gens):
    live = list(gens)
    waiting = []
    while live or waiting:
        for _ in range(min(SUBCHAINS_PER_ROUND, len(waiting))):
            live.append(waiting.pop(0))
        for gen in list(live):
            step = next(gen, "done")
            if step == "done":
                live.remove(gen)
            elif step is not None:
                waiting.extend(step)


def _l0_prompt_row(x_ref, xnext_ref, nw_ref, win_ref, wout_ref,
                   scw_ref, scb_ref, dtb_ref, alog_ref, dsk_ref, snw_ref,
                   mcw_ref, mcb_ref, wqk_ref, wv_ref, wg_ref, bg_ref, msk_ref, mnw_ref, expand_ref,
                   isc_ref, iss_ref, imc_ref, ict_ref, inn_ref, imm_ref,
                   h1_ref, osc_ref, oss_ref, omc_ref, oct_ref, onn_ref, omm_ref,
                   sbuf, mbuf, s_st, ct_st, n_st, m_st, proj_s, *, front_pad, phase, emit):
    q_len = x_ref.shape[0]

    if phase == "init":
        sbuf[...] = jnp.zeros(sbuf.shape, F32)
        mbuf[...] = jnp.zeros(mbuf.shape, F32)
        sbuf[SUBLANE - TAIL:SUBLANE, :] = isc_ref[0]
        mbuf[SUBLANE - TAIL:SUBLANE, :] = imc_ref[0]
        for g in range(SSD_GROUPS):
            heads = iss_ref[0, g * SSD_HPG:(g + 1) * SSD_HPG]
            s_st[g] = heads.reshape(SSD_HPG * SSD_HEAD_DIM, SSD_STATE).T
        for hd in range(ML_HEADS):
            ct_st[hd] = ict_ref[0, hd].T
        n_st[...] = inn_ref[0]
        m_st[...] = imm_ref[0]
        hn0 = _move_rows(_perm_matrices(q_len)[0], _rms(x_ref[...], nw_ref[...]).astype(BF16))
        proj_s[...] = _bdot(hn0, _wload(win_ref[...]))
        return None
    if phase == "final":
        osc_ref[...] = sbuf[SUBLANE - TAIL:SUBLANE, :]
        omc_ref[...] = mbuf[SUBLANE - TAIL:SUBLANE, :]
        for g in range(SSD_GROUPS):
            oss_ref[g * SSD_HPG:(g + 1) * SSD_HPG] = s_st[g].T.reshape(SSD_HPG, SSD_HEAD_DIM, SSD_STATE)
        for hd in range(ML_HEADS):
            oct_ref[hd] = ct_st[hd].T
        onn_ref[...] = n_st[...]
        omm_ref[...] = m_st[...]
        return None

    x = x_ref[...]
    to_perm, to_time = _perm_matrices(q_len)
    t_col = _perm_time(q_len)
    t_row = _perm_time(q_len, row=True)
    causal = t_col >= t_row
    tril = jnp.where(causal, 1.0, 0.0).astype(BF16)
    valid = (t_col >= front_pad) if front_pad else None
    hn_next = _move_rows(to_perm, _rms(xnext_ref[...], nw_ref[...]).astype(BF16))
    xbc_raw = proj_s[:, OFF_XBC:OFF_XBC + SSD_CONV_CH]
    dt_raw = proj_s[:, OFF_DT:OFF_DT + LANE]
    xm = proj_s[:, OFF_XM:OFF_XM + ML_WIDTH]
    z_s = proj_s[:, OFF_ZS:OFF_ZS + SSD_WIDTH]
    z_m = proj_s[:, OFF_ZM:OFF_ZM + ML_WIDTH]

    def ssd():
        xbc = _silu(_conv_perm(sbuf, xbc_raw, scw_ref, scb_ref))
        yield
        xs = xbc[:, :SSD_WIDTH]
        bm = xbc[:, SSD_WIDTH:SSD_WIDTH + SSD_GROUPS * SSD_STATE]
        cm = xbc[:, SSD_WIDTH + SSD_GROUPS * SSD_STATE:]
        dt = _softplus(dt_raw + dtb_ref[...])
        if front_pad:
            dt = jnp.where(valid, dt, 0.0)
        log_a = -dt * jnp.exp(alog_ref[...])
        a_cs = _cumsum_rows(log_a, tril)
        yield
        a_last = a_cs[q_len - 1:q_len, :]
        expand = _wload(expand_ref[...])
        w_state = _expand_heads(dt * jnp.exp(a_last - a_cs), expand)
        e_acs = _expand_heads(jnp.exp(a_cs), expand)
        a_cs_t = a_cs.T
        dt_t = dt.T
        yield
        pair_lo = lax.broadcasted_iota(jnp.int32, (q_len, LANE), 1) < SSD_HEAD_DIM
        half = SSD_WIDTH // SSD_GROUPS

        def group(g):
            cols = slice(g * half, (g + 1) * half)
            bg = bm[:, g * SSD_STATE:(g + 1) * SSD_STATE]
            cg = cm[:, g * SSD_STATE:(g + 1) * SSD_STATE]
            bg_t = bg.T
            xg = xs[:, cols]
            eg = e_acs[:, cols]
            s_old = s_st[g]
            cb = _bdot(cg, bg_t)
            y_off = _bdot(cg, s_old) * eg
            s_st[g] = eg[q_len - 1:q_len, :] * s_old + _bdot(bg_t, xg * w_state[:, cols])
            yield
            y_pairs = []
            for pr in range(SSD_HPG // 2):
                ms = []
                for e in (2 * pr, 2 * pr + 1):
                    hd = g * SSD_HPG + e
                    seg = jnp.exp(jnp.where(causal, a_cs[:, hd:hd + 1] - a_cs_t[hd:hd + 1, :], -jnp.inf))
                    ms.append(cb * seg * dt_t[hd:hd + 1, :])
                xp = xg[:, pr * LANE:(pr + 1) * LANE]
                rhs = jnp.concatenate([jnp.where(pair_lo, xp, 0.0), jnp.where(pair_lo, 0.0, xp)], axis=0)
                y_pairs.append(_bdot(jnp.concatenate(ms, axis=-1), rhs))
                yield
            y_g = (jnp.concatenate(y_pairs, axis=-1) + y_off + dsk_ref[:, cols] * xg) * _silu(z_s[:, cols])
            y_g = y_g * lax.rsqrt(jnp.mean(y_g * y_g, axis=-1, keepdims=True) + EPS)
            emit(g * half, y_g * snw_ref[:, cols])

        yield [group(g) for g in range(SSD_GROUPS)]

    def mlstm():
        xc = _silu(_conv_perm(mbuf, xm, mcw_ref, mcb_ref))
        yield
        q, k = _blockdiag_tiles(xc, wqk_ref)
        v, = _blockdiag_tiles(xm, wv_ref)
        yield
        gates = _bdot(jnp.concatenate([q, k, v], axis=-1), _wload(wg_ref[...])) + bg_ref[...]
        k = k * (ML_HEAD_DIM ** -0.5)
        yield
        ig = gates[:, :LANE]
        logf = -_softplus(-gates[:, LANE:])
        if front_pad:
            ig = jnp.where(valid, ig, NEG)
            logf = jnp.where(valid, logf, 0.0)
        bcum = _cumsum_rows(logf, tril)
        yield
        ftot = bcum[q_len - 1:q_len, :]
        m_prev = m_st[...]
        w_end = ftot - bcum + ig
        m_new = jnp.maximum(ftot + m_prev, jnp.max(w_end, axis=0, keepdims=True))
        sc = jnp.exp(ftot + m_prev - m_new)
        wexp = jnp.exp(w_end - m_new)
        inter = bcum + m_prev
        bcum_t = bcum.T
        ig_t = ig.T
        m_st[...] = m_new

        def head(hd):
            sl = slice(hd * ML_HEAD_DIM, (hd + 1) * ML_HEAD_DIM)
            q_h, k_h, v_h = q[:, sl], k[:, sl], v[:, sl]
            k_t = k_h.T
            dmat = jnp.where(causal, bcum[:, hd:hd + 1] - bcum_t[hd:hd + 1, :] + ig_t[hd:hd + 1, :], -jnp.inf)
            inter_h = inter[:, hd:hd + 1]
            m_t = jnp.maximum(inter_h, jnp.max(dmat, axis=-1, keepdims=True))
            dexp = jnp.exp(dmat - m_t)
            inter_sc = jnp.exp(inter_h - m_t)
            s = _bdot(q_h, k_t) * dexp
            yield
            ct_old = ct_st[hd]
            n_old = n_st[hd:hd + 1, :]
            num = _bdot(s, v_h) + inter_sc * _bdot(q_h, ct_old)
            den = jnp.sum(s, axis=-1, keepdims=True) + inter_sc * jnp.sum(q_h * n_old, axis=-1, keepdims=True)
            h_h = num / jnp.maximum(jnp.abs(den), jnp.exp(-m_t))
            w_col = wexp[:, hd:hd + 1]
            sc_h = sc[:, hd:hd + 1]
            ct_st[hd] = sc_h * ct_old + _bdot(k_t, v_h * w_col)
            n_st[hd:hd + 1, :] = sc_h * n_old + jnp.sum(k_h * w_col, axis=0, keepdims=True)
            yield
            mu = jnp.mean(h_h, axis=-1, keepdims=True)
            dev = h_h - mu
            var = jnp.mean(dev * dev, axis=-1, keepdims=True)
            h_h = dev * lax.rsqrt(var + EPS) * mnw_ref[:, sl]
            emit(SSD_WIDTH + hd * ML_HEAD_DIM, (h_h + msk_ref[:, sl] * xc[:, sl]) * _silu(z_m[:, sl]))

        yield [head(hd) for hd in range(ML_HEADS)]

    return [ssd(), mlstm()], hn_next, x


def _const_spec(shape):
    nd = len(shape)
    return pl.BlockSpec(shape, lambda b, c: (0,) * nd)


def _state_spec(shape, rows):
    nd = len(shape)
    if rows:
        return pl.BlockSpec((rows,) + shape, lambda b, c: (b,) + (0,) * nd)
    return pl.BlockSpec((1,) + shape, lambda b, c: (0,) * (nd + 1))


def _rows_per_step(bsz, want):
    return want if bsz % want == 0 else 1


L0_STATE_SHAPES = ((TAIL, SSD_CONV_CH), (SSD_HEADS, SSD_HEAD_DIM, SSD_STATE), (TAIL, ML_WIDTH),
                   (ML_HEADS, ML_HEAD_DIM, ML_HEAD_DIM), (ML_HEADS, ML_HEAD_DIM), (1, LANE))
L0_CARRY_SHAPES = ((SUBLANE, SSD_CONV_CH), (SUBLANE, ML_WIDTH), (SSD_GROUPS, SSD_STATE, SSD_WIDTH // SSD_GROUPS),
                   (ML_HEADS, ML_HEAD_DIM, ML_HEAD_DIM), (ML_HEADS, ML_HEAD_DIM), (1, LANE))


def _stream_specs(arrays, n_steps, nc):
    specs = []
    for a in arrays:
        if a.ndim == 2:
            specs.append(pl.BlockSpec(memory_space=pltpu.SMEM))
            continue
        assert a.shape[0] % n_steps == 0
        block = (a.shape[0] // n_steps,) + a.shape[1:]
        specs.append(pl.BlockSpec(block, lambda b, c, nd=a.ndim: (b * nc + c,) + (0,) * (nd - 1)))
    return specs


def _l0_prompt(x, weights, init, front_pad, stream_in=(), stream_out_shapes=()):
    bsz, length, _ = x.shape
    q_len = min(CHUNK, length)
    assert length % q_len == 0
    rows = _rows_per_step(bsz, L0_ROWS)
    assert len(weights) == N_L0_W and len(init) == N_L0_S
    nc = length // q_len
    grid = (bsz // rows, nc)
    n_steps = grid[0] * nc
    last = nc - 1
    x_spec = pl.BlockSpec((rows, q_len, D_MODEL), lambda b, c: (b, c, 0))
    next_spec = pl.BlockSpec((rows, q_len, D_MODEL), lambda b, c: (b, jnp.minimum(c + 1, last), 0))
    stream_outs = [jax.ShapeDtypeStruct(s, F32) for s in stream_out_shapes]
    in_specs = ([x_spec, next_spec] + [_const_spec(w.shape) for w in weights]
                + [_state_spec(s, 0) for s in L0_STATE_SHAPES] + _stream_specs(stream_in, n_steps, nc))
    out_shape = ([jax.ShapeDtypeStruct((bsz, length, D_MODEL), F32)]
                 + [jax.ShapeDtypeStruct((bsz,) + s, F32) for s in L0_STATE_SHAPES] + stream_outs)
    out_specs = ([x_spec] + [_state_spec(s, rows) for s in L0_STATE_SHAPES]
                 + _stream_specs(stream_outs, n_steps, nc))
    row_scratch = L0_CARRY_SHAPES + ((q_len, IN_MIX_PAD),)
    scratch = [pltpu.VMEM(s, F32) for _ in range(rows) for s in row_scratch]
    return pl.pallas_call(
        functools.partial(_l0_prompt_kernel, front_pad=front_pad, rows=rows, stream=bool(stream_in)),
        grid=grid, in_specs=in_specs, out_specs=out_specs, out_shape=out_shape, scratch_shapes=scratch,
        compiler_params=pltpu.CompilerParams(dimension_semantics=("arbitrary", "arbitrary"),
                                             vmem_limit_bytes=L0_VMEM_LIMIT),
        name="l0_prompt",
    )(x, x, *weights, *init, *stream_in)


def _rglru_gates(xc, ra, ix, ba_ref, bx_ref, lam_ref):
    r = _sigmoid(ra + ba_ref[...])
    i = _sigmoid(ix + bx_ref[...])
    log_a = r * (-LRU_C * _softplus(-lam_ref[...]))
    a = jnp.exp(log_a)
    u = jnp.sqrt(1.0 - a * a) * (i * xc)
    return a, u


def _perm_time(n, row=False):
    p = lax.broadcasted_iota(jnp.int32, (1, n) if row else (n, 1), 1 if row else 0)
    return (n // SUBLANE) * (p & (SUBLANE - 1)) + lax.shift_right_logical(p, 3)


def _perm_matrices(n):
    nb = n // SUBLANE
    r = lax.broadcasted_iota(jnp.int32, (n, n), 0)
    c = lax.broadcasted_iota(jnp.int32, (n, n), 1)
    to_perm = jnp.where(c == nb * (r & (SUBLANE - 1)) + lax.shift_right_logical(r, 3), 1.0, 0.0)
    to_time = jnp.where(r == nb * (c & (SUBLANE - 1)) + lax.shift_right_logical(c, 3), 1.0, 0.0)
    return to_perm.astype(BF16), to_time.astype(BF16)


def _move_rows(sel, x_bf16):
    return jnp.dot(sel, x_bf16, preferred_element_type=F32).astype(BF16)


def _conv_perm(tail_ref, x, w_ref, b_ref):
    n, ch = x.shape
    nb = n // SUBLANE
    x3 = x.reshape(nb, SUBLANE, ch)
    tail8 = tail_ref[...]
    sub = lax.broadcasted_iota(jnp.int32, (SUBLANE, ch), 0)
    y = b_ref[...].reshape(1, 1, ch) + w_ref[TAIL:TAIL + 1, :].reshape(1, 1, ch) * x3
    wrapped = [jnp.where(sub >= 1, pltpu.roll(x3[nb - d], 1, 0), tail8[SUBLANE - d:SUBLANE - d + 1, :])
               for d in range(1, CONV_W)]
    for back in range(1, CONV_W):
        head = jnp.stack([wrapped[back - j - 1] for j in range(back)], axis=0)
        shifted = jnp.concatenate([head, x3[:nb - back]], axis=0)
        y = y + w_ref[TAIL - back:TAIL - back + 1, :].reshape(1, 1, ch) * shifted
    for d in range(1, CONV_W):
        tail_ref[SUBLANE - d:SUBLANE - d + 1, :] = x3[nb - d][SUBLANE - 1:SUBLANE, :]
    return y.reshape(n, ch)


def _scan_perm(a, u, h_prev):
    n, ch = a.shape
    nb = n // SUBLANE
    a3 = a.reshape(nb, SUBLANE, ch)
    u3 = u.reshape(nb, SUBLANE, ch)
    local = [u3[0]]
    decay = [a3[0]]
    for j in range(1, nb):
        local.append(a3[j] * local[-1] + u3[j])
        decay.append(a3[j] * decay[-1])
    seg_u, seg_a = local[-1], decay[-1]
    sub = lax.broadcasted_iota(jnp.int32, (SUBLANE, ch), 0)
    shift = 1
    while shift < SUBLANE:
        keep = sub >= shift
        seg_u = seg_u + seg_a * jnp.where(keep, pltpu.roll(seg_u, shift, 0), 0.0)
        seg_a = seg_a * jnp.where(keep, pltpu.roll(seg_a, shift, 0), 1.0)
        shift *= 2
    seg_end = seg_a * h_prev + seg_u
    carry = jnp.where(sub >= 1, pltpu.roll(seg_end, 1, 0), h_prev)
    h3 = jnp.stack([local[j] + decay[j] * carry for j in range(nb)], axis=0)
    return h3.reshape(n, ch), seg_end[SUBLANE - 1:SUBLANE, :]


N_L1_W = 10
N_L1_S = 2


N_STREAM_IN = 5
N_STREAM_OUT = 2


def _l1_prompt_kernel(h_ref, hnext_ref, *refs, front_pad, rows, stream):
    w_refs = refs[:N_L1_W]
    init_refs = refs[N_L1_W:N_L1_W + N_L1_S]
    pos = N_L1_W + N_L1_S
    stream_in = refs[pos:pos + N_STREAM_IN] if stream else ()
    pos += len(stream_in)
    out_refs = refs[pos:pos + N_L1_S + 1]
    pos += N_L1_S + 1
    stream_out = refs[pos:pos + N_STREAM_OUT] if stream else ()
    pos += len(stream_out)
    scratch = refs[pos:]
    per_row = len(scratch) // rows
    c = pl.program_id(1)
    fnw_ref, win_ref, wout_ref = w_refs[1], w_refs[2], w_refs[3]
    q_len = h_ref.shape[1]
    gw = LRU_WIDTH // L1_GROUPS
    pieces = {}

    def each_row(phase):
        return [_l1_prompt_row(h_ref.at[r], hnext_ref.at[r], *w_refs, *init_refs, *(o.at[r] for o in out_refs),
                               *scratch[r * per_row:(r + 1) * per_row], front_pad=front_pad, phase=phase,
                               emit=lambda g, y, r=r: pieces.setdefault(g, {}).__setitem__(r, y))
                for r in range(rows)]

    @pl.when(c == 0)
    def _():
        each_row("init")

    bodies = each_row("body")
    proj_refs = [scratch[r * per_row + per_row - 1] for r in range(rows)]
    _, to_time = _perm_matrices(q_len)
    partials = []

    def in_proj():
        lhs = jnp.concatenate([hn_next for _, hn_next, _ in bodies], axis=0)
        for g in range(L1_GROUPS):
            for lo in (g * gw, LRU_WIDTH + g * gw):
                res = _bdot(lhs, _wload(win_ref[:, lo:lo + gw]))
                for r in range(rows):
                    proj_refs[r][:, lo:lo + gw] = res[r * q_len:(r + 1) * q_len]
            yield

    def out_proj():
        pending = list(range(L1_GROUPS))
        while pending:
            for g in list(pending):
                if len(pieces.get(g, ())) == rows:
                    y_t = jnp.concatenate([_move_rows(to_time, pieces[g][r].astype(BF16)) for r in range(rows)],
                                          axis=0)
                    partials.append(_bdot(y_t, _wload(wout_ref[g * gw // 2:(g + 1) * gw // 2, :])))
                    pending.remove(g)
            yield

    chains = [bodies[r][0][g] for g in range(L1_GROUPS) for r in range(rows)]
    if stream:
        step = pl.program_id(0) * pl.num_programs(1) + c
        chains.append(_mlstm_state_update(*stream_in, *stream_out, base=step * stream_in[1].shape[0]))
    _run_staggered([in_proj()] + chains + [out_proj()], per_round=rows)
    total = partials[0]
    for part in partials[1:]:
        total = total + part
    for r, (_, _, h_in) in enumerate(bodies):
        out_refs[0][r] = _rms(h_in + total[r * q_len:(r + 1) * q_len], fnw_ref[...])

    @pl.when(c == pl.num_programs(1) - 1)
    def _():
        each_row("final")


def _l1_in_proj(h_val, nw_ref, win_ref, to_perm):
    hn = _move_rows(to_perm, _rms(h_val, nw_ref[...]).astype(BF16))
    return _bdot(hn, _wload(win_ref[...]))


def _l1_prompt_row(h_ref, hnext_ref, nw_ref, fnw_ref, win_ref, wout_ref, cw_ref, cb_ref,
                   wax_ref, ba_ref, bx_ref, lam_ref, ilc_ref, ilh_ref,
                   y_ref, olc_ref, olh_ref, lbuf, h_st, proj_s, *, front_pad, phase, emit):
    if phase == "init":
        lbuf[...] = jnp.zeros(lbuf.shape, F32)
        lbuf[SUBLANE - TAIL:SUBLANE, :] = ilc_ref[0]
        h_st[...] = ilh_ref[0]
        proj_s[...] = _l1_in_proj(h_ref[...], nw_ref, win_ref, _perm_matrices(h_ref.shape[0])[0])
        return None
    if phase == "final":
        olc_ref[...] = lbuf[SUBLANE - TAIL:SUBLANE, :]
        olh_ref[...] = h_st[...]
        return None

    q_len = h_ref.shape[0]
    h_in = h_ref[...]
    to_perm, _ = _perm_matrices(q_len)
    hn_next = _move_rows(to_perm, _rms(hnext_ref[...], nw_ref[...]).astype(BF16))
    if front_pad:
        valid = _perm_time(q_len) >= front_pad
    gw = LRU_WIDTH // L1_GROUPS
    tiles = gw // LANE
    gates = [proj_s[:, g * gw:(g + 1) * gw] for g in range(L1_GROUPS)]
    xrs = [proj_s[:, LRU_WIDTH + g * gw:LRU_WIDTH + (g + 1) * gw] for g in range(L1_GROUPS)]

    def group(g):
        cg = slice(g * gw, (g + 1) * gw)
        xc = _conv_perm(lbuf.at[:, cg], xrs[g], cw_ref.at[:, cg], cb_ref.at[:, cg])
        ra, ix = _blockdiag_tiles(xc, wax_ref.at[g * tiles:(g + 1) * tiles])
        yield
        a, u = _rglru_gates(xc, ra, ix, ba_ref.at[:, cg], bx_ref.at[:, cg], lam_ref.at[:, cg])
        if front_pad:
            a = jnp.where(valid, a, 1.0)
            u = jnp.where(valid, u, 0.0)
        yield
        h, h_last = _scan_perm(a, u, h_st[:, cg])
        h_st[:, cg] = h_last
        yield
        emit(g, h * _silu(gates[g]))

    return [group(g) for g in range(L1_GROUPS)], hn_next, h_in


def _run_staggered(gens, per_round=1):
    live = []
    pending = list(gens)
    while pending or live:
        for _ in range(min(per_round, len(pending))):
            live.append(pending.pop(0))
        for gen in list(live):
            if next(gen, "done") == "done":
                live.remove(gen)


L1_STATE_SHAPES = ((TAIL, LRU_WIDTH), (1, LRU_WIDTH))


def _l1_prompt(h1, weights, init, front_pad, stream_in=(), stream_out_shapes=()):
    bsz, length, _ = h1.shape
    q_len = min(CHUNK, length)
    assert length % q_len == 0
    rows = _rows_per_step(bsz, L1_ROWS)
    assert len(weights) == N_L1_W and len(init) == N_L1_S
    nc = length // q_len
    grid = (bsz // rows, nc)
    n_steps = grid[0] * nc
    last = nc - 1
    x_spec = pl.BlockSpec((rows, q_len, D_MODEL), lambda b, c: (b, c, 0))
    next_spec = pl.BlockSpec((rows, q_len, D_MODEL), lambda b, c: (b, jnp.minimum(c + 1, last), 0))
    stream_outs = [jax.ShapeDtypeStruct(s, F32) for s in stream_out_shapes]
    in_specs = ([x_spec, next_spec] + [_const_spec(w.shape) for w in weights]
                + [_state_spec(s, 0) for s in L1_STATE_SHAPES] + _stream_specs(stream_in, n_steps, nc))
    out_shape = ([jax.ShapeDtypeStruct((bsz, length, D_MODEL), F32)]
                 + [jax.ShapeDtypeStruct((bsz,) + s, F32) for s in L1_STATE_SHAPES] + stream_outs)
    out_specs = ([x_spec] + [_state_spec(s, rows) for s in L1_STATE_SHAPES]
                 + _stream_specs(stream_outs, n_steps, nc))
    row_scratch = ((SUBLANE, LRU_WIDTH), (1, LRU_WIDTH), (q_len, 2 * LRU_WIDTH))
    scratch = [pltpu.VMEM(s, F32) for _ in range(rows) for s in row_scratch]
    return pl.pallas_call(
        functools.partial(_l1_prompt_kernel, front_pad=front_pad, rows=rows, stream=bool(stream_in)),
        grid=grid, in_specs=in_specs, out_specs=out_specs, out_shape=out_shape, scratch_shapes=scratch,
        compiler_params=pltpu.CompilerParams(dimension_semantics=("arbitrary", "arbitrary"),
                                             vmem_limit_bytes=VMEM_LIMIT),
        name="l1_prompt",
    )(h1, h1, *weights, *init, *stream_in)


def _conv_step(buf_ref, x, w_ref, b_ref, newbuf_ref):
    y = b_ref[...] + w_ref[3:4, :] * x
    for tap in range(TAIL):
        y = y + w_ref[tap:tap + 1, :] * buf_ref[tap]
    for tap in range(TAIL - 1):
        newbuf_ref[tap] = buf_ref[tap + 1]
    newbuf_ref[TAIL - 1] = x
    return y


def _l0_sample_pre_kernel(x_ref, nw_ref, win_ref, scw_ref, scb_ref, dtb_ref, alog_ref,
                          mcw_ref, mcb_ref, wqk_ref, wv_ref, wg_ref, bg_ref,
                          sbuf_ref, mbuf_ref, m0_ref, n0_ref,
                          nsb_ref, nmb_ref, zs_ref, xs_ref, bm_ref, cm_ref, xdt_t_ref, dec_t_ref,
                          zm_ref, xc_ref, q_ref, isv_t_ref, fs_t_ref, k_ref, mnew_ref, nnew_ref, den_ref):
    x = x_ref[...]
    hn = _rms(x, nw_ref[...])
    proj = _bdot(hn, _wload(win_ref[...]))
    zs_ref[...] = proj[:, OFF_ZS:OFF_ZS + SSD_WIDTH]
    zm_ref[...] = proj[:, OFF_ZM:OFF_ZM + ML_WIDTH]
    xbc = proj[:, OFF_XBC:OFF_XBC + SSD_CONV_CH]
    dt_raw = proj[:, OFF_DT:OFF_DT + LANE]
    xm = proj[:, OFF_XM:OFF_XM + ML_WIDTH]
    expand = _expand_matrix()

    xbc = _silu(_conv_step(sbuf_ref, xbc, scw_ref, scb_ref, nsb_ref))
    xs = xbc[:, :SSD_WIDTH]
    xs_ref[...] = xs
    bm_ref[...] = xbc[:, SSD_WIDTH:SSD_WIDTH + SSD_GROUPS * SSD_STATE]
    cm_ref[...] = xbc[:, SSD_WIDTH + SSD_GROUPS * SSD_STATE:]
    dt = _softplus(dt_raw + dtb_ref[...])
    log_a = -dt * jnp.exp(alog_ref[...])
    xdt_t_ref[...] = xs * _expand_heads(dt, expand)
    dec_t_ref[...] = jnp.exp(log_a)

    xc = _silu(_conv_step(mbuf_ref, xm, mcw_ref, mcb_ref, nmb_ref))
    xc_ref[...] = xc
    q, k, v, ig, logf = _mlstm_qkv_gates(xm, xc, wqk_ref, wv_ref, wg_ref, bg_ref)
    m0 = m0_ref[...]
    m_new = jnp.maximum(logf + m0, ig)
    fs = jnp.exp(logf + m0 - m_new)
    is_ = jnp.exp(ig - m_new)
    mnew_ref[...] = m_new
    r = lax.broadcasted_iota(jnp.int32, (LANE, ML_WIDTH), 0)
    cidx = lax.broadcasted_iota(jnp.int32, (LANE, ML_WIDTH), 1)
    expand_m = jnp.where(lax.shift_right_logical(cidx, 8) == r, 1.0, 0.0).astype(BF16)
    fs_e = _expand_heads(fs, expand_m)
    is_e = _expand_heads(is_, expand_m)
    n_new = fs_e * n0_ref[...] + is_e * k
    nnew_ref[...] = n_new
    q_ref[...] = q
    k_ref[...] = k
    isv_t_ref[...] = is_e * v
    fs_t_ref[...] = fs
    nq = n_new * q
    floor = jnp.exp(-m_new)
    for hd in range(ML_HEADS):
        den = jnp.sum(nq[:, hd * ML_HEAD_DIM:(hd + 1) * ML_HEAD_DIM], axis=-1, keepdims=True)
        den_ref[:, hd:hd + 1] = jnp.maximum(jnp.abs(den), floor[:, hd:hd + 1])


def _rows_to_tile(rows8):
    return jnp.concatenate([rows8] + [jnp.zeros_like(rows8)] * (LANE // SUBLANE - 1), axis=0)


def _ssd_state_update(dec_ref, s_ref, xdt_ref, bm_ref, cm_ref, snew_ref, y_ref, *, base):
    n = s_ref.shape[0]
    half = SSD_WIDTH // SSD_GROUPS
    x_cols = _rows_to_tile(xdt_ref[0]).T
    lane = lax.broadcasted_iota(jnp.int32, (half, LANE), 1)
    accs = [jnp.zeros((half, LANE), F32) for _ in range(SSD_GROUPS)]
    for i in range(n):
        x_col = x_cols[:, i:i + 1].reshape(SSD_HEADS, SSD_HEAD_DIM, 1)
        for g in range(SSD_GROUPS):
            hs = slice(g * SSD_HPG, (g + 1) * SSD_HPG)
            gs = slice(g * SSD_STATE, (g + 1) * SSD_STATE)
            b_row = bm_ref[0, i:i + 1, gs].reshape(1, 1, SSD_STATE)
            decay = jnp.stack([jnp.full((1, 1), dec_ref[base + i, hd], F32)
                               for hd in range(hs.start, hs.stop)], axis=0)
            s_new = decay * s_ref[i, hs] + x_col[hs] * b_row
            snew_ref[i, hs] = s_new
            prod = _bdot_nt(s_new.reshape(half, SSD_STATE), _rows_to_tile(cm_ref[0, :, gs]))
            accs[g] = jnp.where(lane == i, prod, accs[g])
            yield
    y_ref[0] = jnp.concatenate(accs, axis=0).T[:SUBLANE]


def _mlstm_state_update(fs_ref, c_ref, isv_ref, k_ref, q_ref, cnew_ref, num_ref, *, base):
    n = c_ref.shape[0]
    v_cols = _rows_to_tile(isv_ref[0]).T
    lane = lax.broadcasted_iota(jnp.int32, (ML_HEAD_DIM, LANE), 1)
    for hd in range(ML_HEADS):
        sl = slice(hd * ML_HEAD_DIM, (hd + 1) * ML_HEAD_DIM)
        q_rows = _rows_to_tile(q_ref[0, :, sl])
        acc = jnp.zeros((ML_HEAD_DIM, LANE), F32)
        for i in range(n):
            c_new = fs_ref[base + i, hd] * c_ref[i, hd] + v_cols[sl, i:i + 1] * k_ref[0, i:i + 1, sl]
            cnew_ref[i, hd] = c_new
            acc = jnp.where(lane == i, _bdot_nt(c_new, q_rows), acc)
            yield
        num_ref[0, :, sl] = acc.T[:SUBLANE]


def _sample_post_kernel(x_ref, ys_t_ref, num_t_ref, den_ref, zs_ref, xs_ref, zm_ref, xc_ref,
                        dsk_ref, snw_ref, msk_ref, mnw_ref, wout_ref,
                        nw1_ref, fnw_ref, win1_ref, wout1_ref, cw_ref, cb_ref,
                        wax_ref, ba_ref, bx_ref, lam_ref, lbuf_ref, h0_ref,
                        y_ref, nlb_ref, hnew_ref):
    xs = xs_ref[...]
    y_s = ys_t_ref[...] + dsk_ref[...] * xs
    y_s = _group_rmsnorm(y_s * _silu(zs_ref[...]), snw_ref[...])
    num = num_t_ref[...]
    den = den_ref[...]
    h_m = jnp.concatenate(
        [num[:, hd * ML_HEAD_DIM:(hd + 1) * ML_HEAD_DIM] / den[:, hd:hd + 1] for hd in range(ML_HEADS)], axis=-1)
    h_m = _head_layernorm(h_m) * mnw_ref[...]
    y_m = (h_m + msk_ref[...] * xc_ref[...]) * _silu(zm_ref[...])
    h1 = x_ref[...] + _bdot(jnp.concatenate([y_s, y_m], axis=-1), _wload(wout_ref[...]))

    hn = _rms(h1, nw1_ref[...])
    proj = _bdot(hn, _wload(win1_ref[...]))
    gate = proj[:, :LRU_WIDTH]
    xr = proj[:, LRU_WIDTH:]
    xc = _conv_step(lbuf_ref, xr, cw_ref, cb_ref, nlb_ref)
    ra, ix = _blockdiag_tiles(xc, wax_ref)
    a, u = _rglru_gates(xc, ra, ix, ba_ref, bx_ref, lam_ref)
    h = a * h0_ref[...] + u
    hnew_ref[...] = h
    h2 = h1 + _bdot(h * _silu(gate), _wload(wout1_ref[...]))
    y_ref[...] = _rms(h2, fnw_ref[...])


def _full_call(kernel_fn, out_shapes, args, name):
    return pl.pallas_call(
        kernel_fn,
        out_shape=[jax.ShapeDtypeStruct(s, F32) for s in out_shapes],
        compiler_params=pltpu.CompilerParams(vmem_limit_bytes=VMEM_LIMIT),
        name=name,
    )(*args)


def _row(v, width=None):
    v = v.reshape(1, -1).astype(F32)
    if width is not None and v.shape[1] < width:
        v = jnp.pad(v, ((0, 0), (0, width - v.shape[1])))
    return v


PACK_STEPS = 8


def _pack_all(w_in_t, weights):
    flats = [w.reshape(-1, w.shape[-1]) for w in weights]
    k_in = w_in_t.shape[1]
    assert k_in == PACK_STEPS * LANE
    for f in flats:
        assert f.shape[0] % (2 * SUBLANE * PACK_STEPS) == 0
    packed = pl.pallas_call(
        _pack_kernel,
        grid=(PACK_STEPS,),
        in_specs=([pl.BlockSpec((w_in_t.shape[0], LANE), lambda i: (0, i))]
                  + [pl.BlockSpec((f.shape[0] // PACK_STEPS, f.shape[1]), lambda i: (i, 0)) for f in flats]),
        out_specs=([pl.BlockSpec((LANE // 2, IN_MIX_PAD), lambda i: (i, 0))]
                   + [pl.BlockSpec((f.shape[0] // PACK_STEPS // 2, f.shape[1]), lambda i: (i, 0)) for f in flats]),
        out_shape=([jax.ShapeDtypeStruct((k_in // 2, IN_MIX_PAD), jnp.uint32)]
                   + [jax.ShapeDtypeStruct((f.shape[0] // 2, f.shape[1]), jnp.uint32) for f in flats]),
        compiler_params=pltpu.CompilerParams(vmem_limit_bytes=VMEM_LIMIT),
        name="pack_weights",
    )(w_in_t, *flats)
    return [packed[0]] + [p.reshape(w.shape[:-2] + (w.shape[-2] // 2, w.shape[-1]))
                          for p, w in zip(packed[1:], weights)]


def _pack_rows(x):
    return pltpu.bitcast(x.astype(BF16), jnp.uint32)


def _pack_kernel(*refs):
    n = len(refs) // 2
    win_ref, wino_ref = refs[0], refs[n]
    wino_ref[:, :OFF_DT] = _pack_rows(win_ref[:OFF_DT, :].T)
    dt_tile = win_ref[OFF_DT:OFF_DT + LANE, :].T
    lane = lax.broadcasted_iota(jnp.int32, dt_tile.shape, 1)
    wino_ref[:, OFF_DT:OFF_ZM] = _pack_rows(jnp.where(lane < SSD_HEADS, dt_tile, 0.0))
    wino_ref[:, OFF_ZM:] = _pack_rows(win_ref[OFF_DT + SSD_HEADS:, :].T)
    for w_ref, o_ref in zip(refs[1:n], refs[n + 1:]):
        o_ref[...] = _pack_rows(w_ref[...])


def _dense_block_tiles(w):
    nb, bi, bo = w.shape
    per = LANE // bi
    rows = w.reshape(nb // per, per * bi, bo)
    col = jnp.arange(per * bo)
    spread = (col[None, :] % bo == jnp.arange(bo)[:, None]).astype(w.dtype)
    rep = jnp.einsum('tro,oc->trc', rows, spread)
    same_block = (jnp.arange(per * bi)[:, None] // bi) == (col[None, :] // bo)
    return jnp.where(same_block, rep, 0.0)


def kernel(x_prompt, x_sample, state_ssd_conv, state_ssd, state_mlstm_conv, state_mlstm_C, state_mlstm_n,
           state_mlstm_m, state_lru_conv, state_lru_h, meta_tokens, norm_w, final_norm_w, w_in_mix, w_out_mix,
           ssd_conv_w, ssd_conv_b, ssd_dt_bias, ssd_a_log, ssd_d, ssd_norm_w, ml_conv_w, ml_conv_b, ml_wq, ml_wk,
           ml_wv, ml_w_gate, ml_b_gate, ml_skip, ml_norm_w, lru_w_in, lru_w_out, lru_conv_w, lru_conv_b, lru_wa,
           lru_ba, lru_wx, lru_bx, lru_lambda):
    bsz = x_prompt.shape[0]
    dec = x_sample.shape[0]

    wout = w_out_mix[0]
    nw0 = _row(norm_w[0])
    nw1 = _row(norm_w[1])
    fnw = _row(final_norm_w)
    scw = ssd_conv_w[0]
    scb = _row(ssd_conv_b[0])
    dtb = _row(ssd_dt_bias[0], LANE)
    alog = _row(ssd_a_log[0], LANE)
    dsk = _row(jnp.repeat(ssd_d[0], SSD_HEAD_DIM))
    snw = _row(ssd_norm_w[0])
    mcw = ml_conv_w[0]
    mcb = _row(ml_conv_b[0])
    wqk = jnp.concatenate([_dense_block_tiles(ml_wq[0]), _dense_block_tiles(ml_wk[0])], axis=2)
    wv = _dense_block_tiles(ml_wv[0])
    wg_raw = ml_w_gate[0]
    wg = jnp.concatenate([jnp.pad(wg_raw[:, :ML_HEADS], ((0, 0), (0, LANE - ML_HEADS))),
                          jnp.pad(wg_raw[:, ML_HEADS:], ((0, 0), (0, LANE - ML_HEADS)))], axis=1)
    bg = jnp.concatenate([_row(ml_b_gate[0, :ML_HEADS], LANE), _row(ml_b_gate[0, ML_HEADS:], LANE)], axis=1)
    msk = _row(ml_skip[0])
    mnw = _row(ml_norm_w[0])
    win1 = lru_w_in[0]
    wout1 = lru_w_out[0]
    lcw = lru_conv_w[0]
    lcb = _row(lru_conv_b[0])
    wax = jnp.concatenate([lru_wa[0], lru_wx[0]], axis=2)
    r_idx = lax.broadcasted_iota(jnp.int32, (LANE, SSD_WIDTH), 0)
    c_idx = lax.broadcasted_iota(jnp.int32, (LANE, SSD_WIDTH), 1)
    expand = (c_idx // SSD_HEAD_DIM == r_idx).astype(F32)
    ba = _row(lru_ba[0])
    bx = _row(lru_bx[0])
    lam = _row(lru_lambda[0])

    win, wout, wqk, wv, wg, expand, win1, wout1, wax = _pack_all(
        jnp.swapaxes(w_in_mix[0], 0, 1), [wout, wqk, wv, wg, expand, win1, wout1, wax])
    l0_w = (nw0, win, wout, scw, scb, dtb, alog, dsk, snw, mcw, mcb, wqk, wv, wg, bg, msk, mnw, expand)
    l1_w = (nw1, fnw, win1, wout1, lcw, lcb, wax, ba, bx, lam)

    xs2 = x_sample[:, 0]
    sbuf = jnp.moveaxis(state_ssd_conv[0], 1, 0)
    mbuf = jnp.moveaxis(state_mlstm_conv[0], 1, 0)
    lbuf = jnp.moveaxis(state_lru_conv[0], 1, 0)
    m0 = jnp.pad(state_mlstm_m[0], ((0, 0), (0, LANE - ML_HEADS)))
    n0 = state_mlstm_n[0].reshape(dec, ML_WIDTH)
    pre_shapes = ((TAIL, dec, SSD_CONV_CH), (TAIL, dec, ML_WIDTH), (dec, SSD_WIDTH), (dec, SSD_WIDTH),
                  (dec, SSD_GROUPS * SSD_STATE), (dec, SSD_GROUPS * SSD_STATE), (dec, SSD_WIDTH), (dec, LANE),
                  (dec, ML_WIDTH), (dec, ML_WIDTH), (dec, ML_WIDTH), (dec, ML_WIDTH), (dec, LANE),
                  (dec, ML_WIDTH), (dec, LANE), (dec, ML_WIDTH), (dec, ML_HEADS))
    (nsb, nmb, zs, xs_c, bm, cm, xdt_t, dec_t, zm, xc_m, q, isv_t, fs_t, k, m_new, n_new, den) = _full_call(
        _l0_sample_pre_kernel, pre_shapes,
        (xs2, nw0, win, scw, scb, dtb, alog, mcw, mcb, wqk, wv, wg, bg, sbuf, mbuf, m0, n0), "l0_sample_pre")

    zero0 = tuple(jnp.zeros((1,) + s, F32) for s in L0_STATE_SHAPES)
    zero1 = tuple(jnp.zeros((1,) + s, F32) for s in L1_STATE_SHAPES)
    meta = jnp.pad(meta_tokens.astype(F32), ((CHUNK - N_META, 0), (0, 0)))[None]
    meta_out = _l0_prompt(meta, l0_w, zero0, CHUNK - N_META)
    meta1_out = _l1_prompt(meta_out[0], l1_w, zero1, CHUNK - N_META)

    n_steps = (bsz // _rows_per_step(bsz, L0_ROWS)) * (x_prompt.shape[1] // CHUNK)
    assert n_steps == (bsz // _rows_per_step(bsz, L1_ROWS)) * (x_prompt.shape[1] // CHUNK)
    per_step = dec // n_steps
    assert per_step * n_steps == dec

    def step_rows(a):
        a = a.reshape(n_steps, per_step, a.shape[-1])
        return jnp.pad(a, ((0, 0), (0, SUBLANE - per_step), (0, 0)))

    def from_step_rows(a):
        return a[:, :per_step].reshape(dec, a.shape[-1])

    l0_out = _l0_prompt(
        x_prompt, l0_w, tuple(meta_out[1:]), 0,
        stream_in=(dec_t[:, :SSD_HEADS], state_ssd[0], step_rows(xdt_t), step_rows(bm), step_rows(cm)),
        stream_out_shapes=((dec, SSD_HEADS, SSD_HEAD_DIM, SSD_STATE), (n_steps, SUBLANE, SSD_WIDTH)))
    h1_p, p_sc, p_s, p_mc, p_c, p_n, p_m, s_new, ys_r = l0_out
    y_prompt, p_lc, p_lh, c_new, num_r = _l1_prompt(
        h1_p, l1_w, tuple(meta1_out[1:]), 0,
        stream_in=(fs_t[:, :ML_HEADS], state_mlstm_C[0], step_rows(isv_t), step_rows(k), step_rows(q)),
        stream_out_shapes=((dec, ML_HEADS, ML_HEAD_DIM, ML_HEAD_DIM), (n_steps, SUBLANE, ML_WIDTH)))

    p_m = p_m[:, 0, :ML_HEADS]
    p_lh = p_lh[:, 0]

    post_shapes = ((dec, D_MODEL), (TAIL, dec, LRU_WIDTH), (dec, LRU_WIDTH))
    y_s2, nlb, h_new = _full_call(
        _sample_post_kernel, post_shapes,
        (xs2, from_step_rows(ys_r), from_step_rows(num_r), den, zs, xs_c, zm, xc_m, dsk, snw, msk, mnw, wout,
         nw1, fnw, win1, wout1, lcw, lcb, wax, ba, bx, lam, lbuf, state_lru_h[0]), "sample_post")

    s_sc = jnp.moveaxis(nsb, 0, 1)[None]
    s_mc = jnp.moveaxis(nmb, 0, 1)[None]
    s_lc = jnp.moveaxis(nlb, 0, 1)[None]
    return (y_prompt, y_s2[:, None, :],
            p_sc[None], p_s[None], p_mc[None], p_c[None], p_n[None], p_m[None], p_lc[None], p_lh[None],
            s_sc, s_new[None], s_mc, c_new[None], n_new.reshape(dec, ML_HEADS, ML_HEAD_DIM)[None],
            m_new[:, :ML_HEADS][None], s_lc, h_new[None])
```

```python
import functools

import jax
import jax.numpy as jnp
from jax import lax
from jax.experimental import pallas as pl
from jax.experimental.pallas import tpu as pltpu

F32 = jnp.float32
BF16 = jnp.bfloat16

D_MODEL = 1024
N_META = 16
CONV_W = 4
EPS = 1e-6
NEG = -1e30
SSD_WIDTH = 1024
SSD_HEAD_DIM = 64
SSD_HEADS = 16
SSD_GROUPS = 2
SSD_HPG = 8
SSD_STATE = 128
SSD_CONV_CH = 1536
ML_WIDTH = 1024
ML_HEADS = 4
ML_HEAD_DIM = 256
ML_QKV_BLOCK = 4
LRU_WIDTH = 2048
LRU_BLOCKS = 16
LRU_BLOCK = 128
LRU_C = 8.0

LANE = 128
SUBLANE = 8
CHUNK = 128
L0_ROWS = 2
L1_ROWS = 2
L1_GROUPS = 8
SUBCHAINS_PER_ROUND = 2
TAIL = CONV_W - 1

OFF_ZS = 0
OFF_XBC = OFF_ZS + SSD_WIDTH
OFF_DT = OFF_XBC + SSD_CONV_CH
OFF_ZM = OFF_DT + LANE
OFF_XM = OFF_ZM + ML_WIDTH
IN_MIX_PAD = OFF_XM + ML_WIDTH

VMEM_LIMIT = 56 * 1024 * 1024
L0_VMEM_LIMIT = 61 * 1024 * 1024


def _sigmoid(x):
    return 1.0 / (1.0 + jnp.exp(-x))


def _silu(x):
    return x * _sigmoid(x)


def _softplus(x):
    return jnp.maximum(x, 0.0) + jnp.log1p(jnp.exp(-jnp.abs(x)))


def _rms(x, w):
    return x * lax.rsqrt(jnp.mean(x * x, axis=-1, keepdims=True) + EPS) * w


def _bdot(a, b):
    return jnp.dot(a.astype(BF16), b.astype(BF16), preferred_element_type=F32)


def _bdot_nt(a, b):
    return lax.dot_general(a.astype(BF16), b.astype(BF16), (((1,), (1,)), ((), ())), preferred_element_type=F32)


def _wload(w):
    return pltpu.bitcast(w, BF16)


def _split3(x):
    hi = x.astype(BF16)
    r = x - hi.astype(F32)
    mid = r.astype(BF16)
    lo = (r - mid.astype(F32)).astype(BF16)
    return hi, mid, lo


def _cumsum_rows(x, tril):
    hi, mid, lo = _split3(x)
    d = functools.partial(jnp.dot, preferred_element_type=F32)
    return d(tril, hi) + d(tril, mid) + d(tril, lo)


def _expand_heads(x, expand):
    hi, mid, _ = _split3(x)
    d = functools.partial(jnp.dot, preferred_element_type=F32)
    return d(hi, expand) + d(mid, expand)


def _expand_matrix():
    r = lax.broadcasted_iota(jnp.int32, (LANE, SSD_WIDTH), 0)
    c = lax.broadcasted_iota(jnp.int32, (LANE, SSD_WIDTH), 1)
    return jnp.where(lax.shift_right_logical(c, 6) == r, 1.0, 0.0).astype(BF16)


def _blockdiag_tiles(x, w_ref):
    k = w_ref.shape[0]
    m = w_ref.shape[2] // LANE
    prods = [_bdot(x[:, t * LANE:(t + 1) * LANE], _wload(w_ref[t])) for t in range(k)]
    return [jnp.concatenate([p[:, j * LANE:(j + 1) * LANE] for p in prods], axis=-1) for j in range(m)]


def _group_rmsnorm(y, w):
    half = SSD_WIDTH // SSD_GROUPS
    parts = []
    for g in range(SSD_GROUPS):
        yg = y[:, g * half:(g + 1) * half]
        parts.append(yg * lax.rsqrt(jnp.mean(yg * yg, axis=-1, keepdims=True) + EPS))
    return jnp.concatenate(parts, axis=-1) * w


def _head_layernorm(h):
    parts = []
    for k in range(ML_HEADS):
        hk = h[:, k * ML_HEAD_DIM:(k + 1) * ML_HEAD_DIM]
        mu = jnp.mean(hk, axis=-1, keepdims=True)
        d = hk - mu
        var = jnp.mean(d * d, axis=-1, keepdims=True)
        parts.append(d * lax.rsqrt(var + EPS))
    return jnp.concatenate(parts, axis=-1)


def _mlstm_qkv_gates(xm, xc, wqk_ref, wv_ref, wg_ref, bg_ref):
    q, k = _blockdiag_tiles(xc, wqk_ref)
    v, = _blockdiag_tiles(xm, wv_ref)
    gates = _bdot(jnp.concatenate([q, k, v], axis=-1), _wload(wg_ref[...])) + bg_ref[...]
    ig = gates[:, :LANE]
    logf = -_softplus(-gates[:, LANE:])
    return q, k * (ML_HEAD_DIM ** -0.5), v, ig, logf


N_L0_W = 18
N_L0_S = 6


def _l0_prompt_kernel(x_ref, xnext_ref, *refs, front_pad, rows, stream):
    w_refs = refs[:N_L0_W]
    init_refs = refs[N_L0_W:N_L0_W + N_L0_S]
    pos = N_L0_W + N_L0_S
    stream_in = refs[pos:pos + N_STREAM_IN] if stream else ()
    pos += len(stream_in)
    out_refs = refs[pos:pos + N_L0_S + 1]
    pos += N_L0_S + 1
    stream_out = refs[pos:pos + N_STREAM_OUT] if stream else ()
    pos += len(stream_out)
    scratch = refs[pos:]
    per_row = len(scratch) // rows
    c = pl.program_id(1)
    win_ref, wout_ref = w_refs[1], w_refs[2]
    q_len = x_ref.shape[1]
    pieces = {}

    def each_row(phase):
        return [_l0_prompt_row(x_ref.at[r], xnext_ref.at[r], *w_refs, *init_refs, *(o.at[r] for o in out_refs),
                               *scratch[r * per_row:(r + 1) * per_row], front_pad=front_pad, phase=phase,
                               emit=lambda k0, y, r=r: pieces.setdefault(k0, {}).__setitem__(r, y))
                for r in range(rows)]

    @pl.when(c == 0)
    def _():
        each_row("init")

    bodies = each_row("body")
    proj_refs = [scratch[r * per_row + per_row - 1] for r in range(rows)]
    _, to_time = _perm_matrices(q_len)
    partials = []

    def in_proj():
        lhs = jnp.concatenate([hn_next for _, hn_next, _ in bodies], axis=0)
        for lo, hi in L0_PROJ_PIECES:
            res = _bdot(lhs, _wload(win_ref[:, lo:hi]))
            for r in range(rows):
                proj_refs[r][:, lo:hi] = res[r * q_len:(r + 1) * q_len]
            yield

    def out_proj():
        half = SSD_WIDTH // SSD_GROUPS
        pending = ([(g * half, half) for g in range(SSD_GROUPS)]
                   + [(SSD_WIDTH + hd * ML_HEAD_DIM, ML_HEAD_DIM) for hd in range(ML_HEADS)])
        while pending:
            for k0, width in list(pending):
                if len(pieces.get(k0, ())) == rows:
                    y_t = jnp.concatenate([_move_rows(to_time, pieces[k0][r].astype(BF16)) for r in range(rows)],
                                          axis=0)
                    partials.append(_bdot(y_t, _wload(wout_ref[k0 // 2:(k0 + width) // 2, :])))
                    pending.remove((k0, width))
            yield

    chains = [gen for gens, _, _ in bodies for gen in gens] + [in_proj(), out_proj()]
    if stream:
        step = pl.program_id(0) * pl.num_programs(1) + c
        chains.append(_ssd_state_update(*stream_in, *stream_out, base=step * stream_in[1].shape[0]))
    _run_round_robin(chains)
    total = partials[0]
    for part in partials[1:]:
        total = total + part
    for r, (_, _, x) in enumerate(bodies):
        h1 = x + total[r * q_len:(r + 1) * q_len]
        if front_pad:
            h1 = jnp.where(lax.broadcasted_iota(jnp.int32, (q_len, 1), 0) >= front_pad, h1, 0.0)
        out_refs[0][r] = h1

    @pl.when(c == pl.num_programs(1) - 1)
    def _():
        each_row("final")


L0_PROJ_PIECES = ((OFF_XBC, OFF_ZM), (OFF_XM, IN_MIX_PAD), (OFF_ZM, OFF_XM), (OFF_ZS, OFF_XBC))


def _run_round_robin(gens):
    live = list(gens)
    waiting = []
    while live or waiting:
        for _ in range(min(SUBCHAINS_PER_ROUND, len(waiting))):
            live.append(waiting.pop(0))
        for gen in list(live):
            step = next(gen, "done")
            if step == "done":
                live.remove(gen)
            elif step is not None:
                waiting.extend(step)


def _l0_prompt_row(x_ref, xnext_ref, nw_ref, win_ref, wout_ref,
                   scw_ref, scb_ref, dtb_ref, alog_ref, dsk_ref, snw_ref,
                   mcw_ref, mcb_ref, wqk_ref, wv_ref, wg_ref, bg_ref, msk_ref, mnw_ref, expand_ref,
                   isc_ref, iss_ref, imc_ref, ict_ref, inn_ref, imm_ref,
                   h1_ref, osc_ref, oss_ref, omc_ref, oct_ref, onn_ref, omm_ref,
                   sbuf, mbuf, s_st, ct_st, n_st, m_st, proj_s, *, front_pad, phase, emit):
    q_len = x_ref.shape[0]

    if phase == "init":
        sbuf[...] = jnp.zeros(sbuf.shape, F32)
        mbuf[...] = jnp.zeros(mbuf.shape, F32)
        sbuf[SUBLANE - TAIL:SUBLANE, :] = isc_ref[0]
        mbuf[SUBLANE - TAIL:SUBLANE, :] = imc_ref[0]
        for g in range(SSD_GROUPS):
            heads = iss_ref[0, g * SSD_HPG:(g + 1) * SSD_HPG]
            s_st[g] = heads.reshape(SSD_HPG * SSD_HEAD_DIM, SSD_STATE).T
        for hd in range(ML_HEADS):
            ct_st[hd] = ict_ref[0, hd].T
        n_st[...] = inn_ref[0]
        m_st[...] = imm_ref[0]
        hn0 = _move_rows(_perm_matrices(q_len)[0], _rms(x_ref[...], nw_ref[...]).astype(BF16))
        proj_s[...] = _bdot(hn0, _wload(win_ref[...]))
        return None
    if phase == "final":
        osc_ref[...] = sbuf[SUBLANE - TAIL:SUBLANE, :]
        omc_ref[...] = mbuf[SUBLANE - TAIL:SUBLANE, :]
        for g in range(SSD_GROUPS):
            oss_ref[g * SSD_HPG:(g + 1) * SSD_HPG] = s_st[g].T.reshape(SSD_HPG, SSD_HEAD_DIM, SSD_STATE)
        for hd in range(ML_HEADS):
            oct_ref[hd] = ct_st[hd].T
        onn_ref[...] = n_st[...]
        omm_ref[...] = m_st[...]
        return None

    x = x_ref[...]
    to_perm, to_time = _perm_matrices(q_len)
    t_col = _perm_time(q_len)
    t_row = _perm_time(q_len, row=True)
    causal = t_col >= t_row
    tril = jnp.where(causal, 1.0, 0.0).astype(BF16)
    valid = (t_col >= front_pad) if front_pad else None
    hn_next = _move_rows(to_perm, _rms(xnext_ref[...], nw_ref[...]).astype(BF16))
    xbc_raw = proj_s[:, OFF_XBC:OFF_XBC + SSD_CONV_CH]
    dt_raw = proj_s[:, OFF_DT:OFF_DT + LANE]
    xm = proj_s[:, OFF_XM:OFF_XM + ML_WIDTH]
    z_s = proj_s[:, OFF_ZS:OFF_ZS + SSD_WIDTH]
    z_m = proj_s[:, OFF_ZM:OFF_ZM + ML_WIDTH]

    def ssd():
        xbc = _silu(_conv_perm(sbuf, xbc_raw, scw_ref, scb_ref))
        yield
        xs = xbc[:, :SSD_WIDTH]
        bm = xbc[:, SSD_WIDTH:SSD_WIDTH + SSD_GROUPS * SSD_STATE]
        cm = xbc[:, SSD_WIDTH + SSD_GROUPS * SSD_STATE:]
        dt = _softplus(dt_raw + dtb_ref[...])
        if front_pad:
            dt = jnp.where(valid, dt, 0.0)
        log_a = -dt * jnp.exp(alog_ref[...])
        a_cs = _cumsum_rows(log_a, tril)
        yield
        a_last = a_cs[q_len - 1:q_len, :]
        expand = _wload(expand_ref[...])
        w_state = _expand_heads(dt * jnp.exp(a_last - a_cs), expand)
        e_acs = _expand_heads(jnp.exp(a_cs), expand)
        a_cs_t = a_cs.T
        dt_t = dt.T
        yield
        pair_lo = lax.broadcasted_iota(jnp.int32, (q_len, LANE), 1) < SSD_HEAD_DIM
        half = SSD_WIDTH // SSD_GROUPS

        def group(g):
            cols = slice(g * half, (g + 1) * half)
            bg = bm[:, g * SSD_STATE:(g + 1) * SSD_STATE]
            cg = cm[:, g * SSD_STATE:(g + 1) * SSD_STATE]
            bg_t = bg.T
            xg = xs[:, cols]
            eg = e_acs[:, cols]
            s_old = s_st[g]
            cb = _bdot(cg, bg_t)
            y_off = _bdot(cg, s_old) * eg
            s_st[g] = eg[q_len - 1:q_len, :] * s_old + _bdot(bg_t, xg * w_state[:, cols])
            yield
            y_pairs = []
            for pr in range(SSD_HPG // 2):
                ms = []
                for e in (2 * pr, 2 * pr + 1):
                    hd = g * SSD_HPG + e
                    seg = jnp.exp(jnp.where(causal, a_cs[:, hd:hd + 1] - a_cs_t[hd:hd + 1, :], -jnp.inf))
                    ms.append(cb * seg * dt_t[hd:hd + 1, :])
                xp = xg[:, pr * LANE:(pr + 1) * LANE]
                rhs = jnp.concatenate([jnp.where(pair_lo, xp, 0.0), jnp.where(pair_lo, 0.0, xp)], axis=0)
                y_pairs.append(_bdot(jnp.concatenate(ms, axis=-1), rhs))
                yield
            y_g = (jnp.concatenate(y_pairs, axis=-1) + y_off + dsk_ref[:, cols] * xg) * _silu(z_s[:, cols])
            y_g = y_g * lax.rsqrt(jnp.mean(y_g * y_g, axis=-1, keepdims=True) + EPS)
            emit(g * half, y_g * snw_ref[:, cols])

        yield [group(g) for g in range(SSD_GROUPS)]

    def mlstm():
        xc = _silu(_conv_perm(mbuf, xm, mcw_ref, mcb_ref))
        yield
        q, k = _blockdiag_tiles(xc, wqk_ref)
        v, = _blockdiag_tiles(xm, wv_ref)
        yield
        gates = _bdot(jnp.concatenate([q, k, v], axis=-1), _wload(wg_ref[...])) + bg_ref[...]
        k = k * (ML_HEAD_DIM ** -0.5)
        yield
        ig = gates[:, :LANE]
        logf = -_softplus(-gates[:, LANE:])
        if front_pad:
            ig = jnp.where(valid, ig, NEG)
            logf = jnp.where(valid, logf, 0.0)
        bcum = _cumsum_rows(logf, tril)
        yield
        ftot = bcum[q_len - 1:q_len, :]
        m_prev = m_st[...]
        w_end = ftot - bcum + ig
        m_new = jnp.maximum(ftot + m_prev, jnp.max(w_end, axis=0, keepdims=True))
        sc = jnp.exp(ftot + m_prev - m_new)
        wexp = jnp.exp(w_end - m_new)
        inter = bcum + m_prev
        bcum_t = bcum.T
        ig_t = ig.T
        m_st[...] = m_new

        def head(hd):
            sl = slice(hd * ML_HEAD_DIM, (hd + 1) * ML_HEAD_DIM)
            q_h, k_h, v_h = q[:, sl], k[:, sl], v[:, sl]
            k_t = k_h.T
            dmat = jnp.where(causal, bcum[:, hd:hd + 1] - bcum_t[hd:hd + 1, :] + ig_t[hd:hd + 1, :], -jnp.inf)
            inter_h = inter[:, hd:hd + 1]
            m_t = jnp.maximum(inter_h, jnp.max(dmat, axis=-1, keepdims=True))
            dexp = jnp.exp(dmat - m_t)
            inter_sc = jnp.exp(inter_h - m_t)
            s = _bdot(q_h, k_t) * dexp
            yield
            ct_old = ct_st[hd]
            n_old = n_st[hd:hd + 1, :]
            num = _bdot(s, v_h) + inter_sc * _bdot(q_h, ct_old)
            den = jnp.sum(s, axis=-1, keepdims=True) + inter_sc * jnp.sum(q_h * n_old, axis=-1, keepdims=True)
            h_h = num / jnp.maximum(jnp.abs(den), jnp.exp(-m_t))
            w_col = wexp[:, hd:hd + 1]
            sc_h = sc[:, hd:hd + 1]
            ct_st[hd] = sc_h * ct_old + _bdot(k_t, v_h * w_col)
            n_st[hd:hd + 1, :] = sc_h * n_old + jnp.sum(k_h * w_col, axis=0, keepdims=True)
            yield
            mu = jnp.mean(h_h, axis=-1, keepdims=True)
            dev = h_h - mu
            var = jnp.mean(dev * dev, axis=-1, keepdims=True)
            h_h = dev * lax.rsqrt(var + EPS) * mnw_ref[:, sl]
            emit(SSD_WIDTH + hd * ML_HEAD_DIM, (h_h + msk_ref[:, sl] * xc[:, sl]) * _silu(z_m[:, sl]))

        yield [head(hd) for hd in range(ML_HEADS)]

    return [ssd(), mlstm()], hn_next, x


def _const_spec(shape):
    nd = len(shape)
    return pl.BlockSpec(shape, lambda b, c: (0,) * nd)


def _state_spec(shape, rows):
    nd = len(shape)
    if rows:
        return pl.BlockSpec((rows,) + shape, lambda b, c: (b,) + (0,) * nd)
    return pl.BlockSpec((1,) + shape, lambda b, c: (0,) * (nd + 1))


def _rows_per_step(bsz, want):
    return want if bsz % want == 0 else 1


L0_STATE_SHAPES = ((TAIL, SSD_CONV_CH), (SSD_HEADS, SSD_HEAD_DIM, SSD_STATE), (TAIL, ML_WIDTH),
                   (ML_HEADS, ML_HEAD_DIM, ML_HEAD_DIM), (ML_HEADS, ML_HEAD_DIM), (1, LANE))
L0_CARRY_SHAPES = ((SUBLANE, SSD_CONV_CH), (SUBLANE, ML_WIDTH), (SSD_GROUPS, SSD_STATE, SSD_WIDTH // SSD_GROUPS),
                   (ML_HEADS, ML_HEAD_DIM, ML_HEAD_DIM), (ML_HEADS, ML_HEAD_DIM), (1, LANE))


def _stream_specs(arrays, n_steps, nc):
    specs = []
    for a in arrays:
        if a.ndim == 2:
            specs.append(pl.BlockSpec(memory_space=pltpu.SMEM))
            continue
        assert a.shape[0] % n_steps == 0
        block = (a.shape[0] // n_steps,) + a.shape[1:]
        specs.append(pl.BlockSpec(block, lambda b, c, nd=a.ndim: (b * nc + c,) + (0,) * (nd - 1)))
    return specs


def _l0_prompt(x, weights, init, front_pad, stream_in=(), stream_out_shapes=()):
    bsz, length, _ = x.shape
    q_len = min(CHUNK, length)
    assert length % q_len == 0
    rows = _rows_per_step(bsz, L0_ROWS)
    assert len(weights) == N_L0_W and len(init) == N_L0_S
    nc = length // q_len
    grid = (bsz // rows, nc)
    n_steps = grid[0] * nc
    last = nc - 1
    x_spec = pl.BlockSpec((rows, q_len, D_MODEL), lambda b, c: (b, c, 0))
    next_spec = pl.BlockSpec((rows, q_len, D_MODEL), lambda b, c: (b, jnp.minimum(c + 1, last), 0))
    stream_outs = [jax.ShapeDtypeStruct(s, F32) for s in stream_out_shapes]
    in_specs = ([x_spec, next_spec] + [_const_spec(w.shape) for w in weights]
                + [_state_spec(s, 0) for s in L0_STATE_SHAPES] + _stream_specs(stream_in, n_steps, nc))
    out_shape = ([jax.ShapeDtypeStruct((bsz, length, D_MODEL), F32)]
                 + [jax.ShapeDtypeStruct((bsz,) + s, F32) for s in L0_STATE_SHAPES] + stream_outs)
    out_specs = ([x_spec] + [_state_spec(s, rows) for s in L0_STATE_SHAPES]
                 + _stream_specs(stream_outs, n_steps, nc))
    row_scratch = L0_CARRY_SHAPES + ((q_len, IN_MIX_PAD),)
    scratch = [pltpu.VMEM(s, F32) for _ in range(rows) for s in row_scratch]
    return pl.pallas_call(
        functools.partial(_l0_prompt_kernel, front_pad=front_pad, rows=rows, stream=bool(stream_in)),
        grid=grid, in_specs=in_specs, out_specs=out_specs, out_shape=out_shape, scratch_shapes=scratch,
        compiler_params=pltpu.CompilerParams(dimension_semantics=("arbitrary", "arbitrary"),
                                             vmem_limit_bytes=L0_VMEM_LIMIT),
        name="l0_prompt",
    )(x, x, *weights, *init, *stream_in)


def _rglru_gates(xc, ra, ix, ba_ref, bx_ref, lam_ref):
    r = _sigmoid(ra + ba_ref[...])
    i = _sigmoid(ix + bx_ref[...])
    log_a = r * (-LRU_C * _softplus(-lam_ref[...]))
    a = jnp.exp(log_a)
    u = jnp.sqrt(1.0 - a * a) * (i * xc)
    return a, u


def _perm_time(n, row=False):
    p = lax.broadcasted_iota(jnp.int32, (1, n) if row else (n, 1), 1 if row else 0)
    return (n // SUBLANE) * (p & (SUBLANE - 1)) + lax.shift_right_logical(p, 3)


def _perm_matrices(n):
    nb = n // SUBLANE
    r = lax.broadcasted_iota(jnp.int32, (n, n), 0)
    c = lax.broadcasted_iota(jnp.int32, (n, n), 1)
    to_perm = jnp.where(c == nb * (r & (SUBLANE - 1)) + lax.shift_right_logical(r, 3), 1.0, 0.0)
    to_time = jnp.where(r == nb * (c & (SUBLANE - 1)) + lax.shift_right_logical(c, 3), 1.0, 0.0)
    return to_perm.astype(BF16), to_time.astype(BF16)


def _move_rows(sel, x_bf16):
    return jnp.dot(sel, x_bf16, preferred_element_type=F32).astype(BF16)


def _conv_perm(tail_ref, x, w_ref, b_ref):
    n, ch = x.shape
    nb = n // SUBLANE
    x3 = x.reshape(nb, SUBLANE, ch)
    tail8 = tail_ref[...]
    sub = lax.broadcasted_iota(jnp.int32, (SUBLANE, ch), 0)
    y = b_ref[...].reshape(1, 1, ch) + w_ref[TAIL:TAIL + 1, :].reshape(1, 1, ch) * x3
    wrapped = [jnp.where(sub >= 1, pltpu.roll(x3[nb - d], 1, 0), tail8[SUBLANE - d:SUBLANE - d + 1, :])
               for d in range(1, CONV_W)]
    for back in range(1, CONV_W):
        head = jnp.stack([wrapped[back - j - 1] for j in range(back)], axis=0)
        shifted = jnp.concatenate([head, x3[:nb - back]], axis=0)
        y = y + w_ref[TAIL - back:TAIL - back + 1, :].reshape(1, 1, ch) * shifted
    for d in range(1, CONV_W):
        tail_ref[SUBLANE - d:SUBLANE - d + 1, :] = x3[nb - d][SUBLANE - 1:SUBLANE, :]
    return y.reshape(n, ch)


def _scan_perm(a, u, h_prev):
    n, ch = a.shape
    nb = n // SUBLANE
    a3 = a.reshape(nb, SUBLANE, ch)
    u3 = u.reshape(nb, SUBLANE, ch)
    local = [u3[0]]
    decay = [a3[0]]
    for j in range(1, nb):
        local.append(a3[j] * local[-1] + u3[j])
        decay.append(a3[j] * decay[-1])
    seg_u, seg_a = local[-1], decay[-1]
    sub = lax.broadcasted_iota(jnp.int32, (SUBLANE, ch), 0)
    shift = 1
    while shift < SUBLANE:
        keep = sub >= shift
        seg_u = seg_u + seg_a * jnp.where(keep, pltpu.roll(seg_u, shift, 0), 0.0)
        seg_a = seg_a * jnp.where(keep, pltpu.roll(seg_a, shift, 0), 1.0)
        shift *= 2
    seg_end = seg_a * h_prev + seg_u
    carry = jnp.where(sub >= 1, pltpu.roll(seg_end, 1, 0), h_prev)
    h3 = jnp.stack([local[j] + decay[j] * carry for j in range(nb)], axis=0)
    return h3.reshape(n, ch), seg_end[SUBLANE - 1:SUBLANE, :]


N_L1_W = 10
N_L1_S = 2


N_STREAM_IN = 5
N_STREAM_OUT = 2


def _l1_prompt_kernel(h_ref, hnext_ref, *refs, front_pad, rows, stream):
    w_refs = refs[:N_L1_W]
    init_refs = refs[N_L1_W:N_L1_W + N_L1_S]
    pos = N_L1_W + N_L1_S
    stream_in = refs[pos:pos + N_STREAM_IN] if stream else ()
    pos += len(stream_in)
    out_refs = refs[pos:pos + N_L1_S + 1]
    pos += N_L1_S + 1
    stream_out = refs[pos:pos + N_STREAM_OUT] if stream else ()
    pos += len(stream_out)
    scratch = refs[pos:]
    per_row = len(scratch) // rows
    c = pl.program_id(1)
    fnw_ref, win_ref, wout_ref = w_refs[1], w_refs[2], w_refs[3]
    q_len = h_ref.shape[1]
    gw = LRU_WIDTH // L1_GROUPS
    pieces = {}

    def each_row(phase):
        return [_l1_prompt_row(h_ref.at[r], hnext_ref.at[r], *w_refs, *init_refs, *(o.at[r] for o in out_refs),
                               *scratch[r * per_row:(r + 1) * per_row], front_pad=front_pad, phase=phase,
                               emit=lambda g, y, r=r: pieces.setdefault(g, {}).__setitem__(r, y))
                for r in range(rows)]

    @pl.when(c == 0)
    def _():
        each_row("init")

    bodies = each_row("body")
    proj_refs = [scratch[r * per_row + per_row - 1] for r in range(rows)]
    _, to_time = _perm_matrices(q_len)
    partials = []

    def in_proj():
        lhs = jnp.concatenate([hn_next for _, hn_next, _ in bodies], axis=0)
        for g in range(L1_GROUPS):
            for lo in (g * gw, LRU_WIDTH + g * gw):
                res = _bdot(lhs, _wload(win_ref[:, lo:lo + gw]))
                for r in range(rows):
                    proj_refs[r][:, lo:lo + gw] = res[r * q_len:(r + 1) * q_len]
            yield

    def out_proj():
        pending = list(range(L1_GROUPS))
        while pending:
            for g in list(pending):
                if len(pieces.get(g, ())) == rows:
                    y_t = jnp.concatenate([_move_rows(to_time, pieces[g][r].astype(BF16)) for r in range(rows)],
                                          axis=0)
                    partials.append(_bdot(y_t, _wload(wout_ref[g * gw // 2:(g + 1) * gw // 2, :])))
                    pending.remove(g)
            yield

    chains = [bodies[r][0][g] for g in range(L1_GROUPS) for r in range(rows)]
    if stream:
        step = pl.program_id(0) * pl.num_programs(1) + c
        chains.append(_mlstm_state_update(*stream_in, *stream_out, base=step * stream_in[1].shape[0]))
    _run_staggered([in_proj()] + chains + [out_proj()])
    total = partials[0]
    for part in partials[1:]:
        total = total + part
    for r, (_, _, h_in) in enumerate(bodies):
        out_refs[0][r] = _rms(h_in + total[r * q_len:(r + 1) * q_len], fnw_ref[...])

    @pl.when(c == pl.num_programs(1) - 1)
    def _():
        each_row("final")


def _l1_in_proj(h_val, nw_ref, win_ref, to_perm):
    hn = _move_rows(to_perm, _rms(h_val, nw_ref[...]).astype(BF16))
    return _bdot(hn, _wload(win_ref[...]))


def _l1_prompt_row(h_ref, hnext_ref, nw_ref, fnw_ref, win_ref, wout_ref, cw_ref, cb_ref,
                   wax_ref, ba_ref, bx_ref, lam_ref, ilc_ref, ilh_ref,
                   y_ref, olc_ref, olh_ref, lbuf, h_st, proj_s, *, front_pad, phase, emit):
    if phase == "init":
        lbuf[...] = jnp.zeros(lbuf.shape, F32)
        lbuf[SUBLANE - TAIL:SUBLANE, :] = ilc_ref[0]
        h_st[...] = ilh_ref[0]
        proj_s[...] = _l1_in_proj(h_ref[...], nw_ref, win_ref, _perm_matrices(h_ref.shape[0])[0])
        return None
    if phase == "final":
        olc_ref[...] = lbuf[SUBLANE - TAIL:SUBLANE, :]
        olh_ref[...] = h_st[...]
        return None

    q_len = h_ref.shape[0]
    h_in = h_ref[...]
    to_perm, _ = _perm_matrices(q_len)
    hn_next = _move_rows(to_perm, _rms(hnext_ref[...], nw_ref[...]).astype(BF16))
    if front_pad:
        valid = _perm_time(q_len) >= front_pad
    gw = LRU_WIDTH // L1_GROUPS
    tiles = gw // LANE
    gates = [proj_s[:, g * gw:(g + 1) * gw] for g in range(L1_GROUPS)]
    xrs = [proj_s[:, LRU_WIDTH + g * gw:LRU_WIDTH + (g + 1) * gw] for g in range(L1_GROUPS)]

    def group(g):
        cg = slice(g * gw, (g + 1) * gw)
        xc = _conv_perm(lbuf.at[:, cg], xrs[g], cw_ref.at[:, cg], cb_ref.at[:, cg])
        ra, ix = _blockdiag_tiles(xc, wax_ref.at[g * tiles:(g + 1) * tiles])
        yield
        a, u = _rglru_gates(xc, ra, ix, ba_ref.at[:, cg], bx_ref.at[:, cg], lam_ref.at[:, cg])
        if front_pad:
            a = jnp.where(valid, a, 1.0)
            u = jnp.where(valid, u, 0.0)
        yield
        h, h_last = _scan_perm(a, u, h_st[:, cg])
        h_st[:, cg] = h_last
        yield
        emit(g, h * _silu(gates[g]))

    return [group(g) for g in range(L1_GROUPS)], hn_next, h_in


def _run_staggered(gens):
    live = []
    pending = list(gens)
    while pending or live:
        if pending:
            live.append(pending.pop(0))
        for gen in list(live):
            if next(gen, "done") == "done":
                live.remove(gen)


L1_STATE_SHAPES = ((TAIL, LRU_WIDTH), (1, LRU_WIDTH))


def _l1_prompt(h1, weights, init, front_pad, stream_in=(), stream_out_shapes=()):
    bsz, length, _ = h1.shape
    q_len = min(CHUNK, length)
    assert length % q_len == 0
    rows = _rows_per_step(bsz, L1_ROWS)
    assert len(weights) == N_L1_W and len(init) == N_L1_S
    nc = length // q_len
    grid = (bsz // rows, nc)
    n_steps = grid[0] * nc
    last = nc - 1
    x_spec = pl.BlockSpec((rows, q_len, D_MODEL), lambda b, c: (b, c, 0))
    next_spec = pl.BlockSpec((rows, q_len, D_MODEL), lambda b, c: (b, jnp.minimum(c + 1, last), 0))
    stream_outs = [jax.ShapeDtypeStruct(s, F32) for s in stream_out_shapes]
    in_specs = ([x_spec, next_spec] + [_const_spec(w.shape) for w in weights]
                + [_state_spec(s, 0) for s in L1_STATE_SHAPES] + _stream_specs(stream_in, n_steps, nc))
    out_shape = ([jax.ShapeDtypeStruct((bsz, length, D_MODEL), F32)]
                 + [jax.ShapeDtypeStruct((bsz,) + s, F32) for s in L1_STATE_SHAPES] + stream_outs)
    out_specs = ([x_spec] + [_state_spec(s, rows) for s in L1_STATE_SHAPES]
                 + _stream_specs(stream_outs, n_steps, nc))
    row_scratch = ((SUBLANE, LRU_WIDTH), (1, LRU_WIDTH), (q_len, 2 * LRU_WIDTH))
    scratch = [pltpu.VMEM(s, F32) for _ in range(rows) for s in row_scratch]
    return pl.pallas_call(
        functools.partial(_l1_prompt_kernel, front_pad=front_pad, rows=rows, stream=bool(stream_in)),
        grid=grid, in_specs=in_specs, out_specs=out_specs, out_shape=out_shape, scratch_shapes=scratch,
        compiler_params=pltpu.CompilerParams(dimension_semantics=("arbitrary", "arbitrary"),
                                             vmem_limit_bytes=VMEM_LIMIT),
        name="l1_prompt",
    )(h1, h1, *weights, *init, *stream_in)


def _conv_step(buf_ref, x, w_ref, b_ref, newbuf_ref):
    y = b_ref[...] + w_ref[3:4, :] * x
    for tap in range(TAIL):
        y = y + w_ref[tap:tap + 1, :] * buf_ref[tap]
    for tap in range(TAIL - 1):
        newbuf_ref[tap] = buf_ref[tap + 1]
    newbuf_ref[TAIL - 1] = x
    return y


def _l0_sample_pre_kernel(x_ref, nw_ref, win_ref, scw_ref, scb_ref, dtb_ref, alog_ref,
                          mcw_ref, mcb_ref, wqk_ref, wv_ref, wg_ref, bg_ref,
                          sbuf_ref, mbuf_ref, m0_ref, n0_ref,
                          nsb_ref, nmb_ref, zs_ref, xs_ref, bm_ref, cm_ref, xdt_t_ref, dec_t_ref,
                          zm_ref, xc_ref, q_ref, isv_t_ref, fs_t_ref, k_ref, mnew_ref, nnew_ref, den_ref):
    x = x_ref[...]
    hn = _rms(x, nw_ref[...])
    proj = _bdot(hn, _wload(win_ref[...]))
    zs_ref[...] = proj[:, OFF_ZS:OFF_ZS + SSD_WIDTH]
    zm_ref[...] = proj[:, OFF_ZM:OFF_ZM + ML_WIDTH]
    xbc = proj[:, OFF_XBC:OFF_XBC + SSD_CONV_CH]
    dt_raw = proj[:, OFF_DT:OFF_DT + LANE]
    xm = proj[:, OFF_XM:OFF_XM + ML_WIDTH]
    expand = _expand_matrix()

    xbc = _silu(_conv_step(sbuf_ref, xbc, scw_ref, scb_ref, nsb_ref))
    xs = xbc[:, :SSD_WIDTH]
    xs_ref[...] = xs
    bm_ref[...] = xbc[:, SSD_WIDTH:SSD_WIDTH + SSD_GROUPS * SSD_STATE]
    cm_ref[...] = xbc[:, SSD_WIDTH + SSD_GROUPS * SSD_STATE:]
    dt = _softplus(dt_raw + dtb_ref[...])
    log_a = -dt * jnp.exp(alog_ref[...])
    xdt_t_ref[...] = xs * _expand_heads(dt, expand)
    dec_t_ref[...] = jnp.exp(log_a)

    xc = _silu(_conv_step(mbuf_ref, xm, mcw_ref, mcb_ref, nmb_ref))
    xc_ref[...] = xc
    q, k, v, ig, logf = _mlstm_qkv_gates(xm, xc, wqk_ref, wv_ref, wg_ref, bg_ref)
    m0 = m0_ref[...]
    m_new = jnp.maximum(logf + m0, ig)
    fs = jnp.exp(logf + m0 - m_new)
    is_ = jnp.exp(ig - m_new)
    mnew_ref[...] = m_new
    r = lax.broadcasted_iota(jnp.int32, (LANE, ML_WIDTH), 0)
    cidx = lax.broadcasted_iota(jnp.int32, (LANE, ML_WIDTH), 1)
    expand_m = jnp.where(lax.shift_right_logical(cidx, 8) == r, 1.0, 0.0).astype(BF16)
    fs_e = _expand_heads(fs, expand_m)
    is_e = _expand_heads(is_, expand_m)
    n_new = fs_e * n0_ref[...] + is_e * k
    nnew_ref[...] = n_new
    q_ref[...] = q
    k_ref[...] = k
    isv_t_ref[...] = is_e * v
    fs_t_ref[...] = fs
    nq = n_new * q
    floor = jnp.exp(-m_new)
    for hd in range(ML_HEADS):
        den = jnp.sum(nq[:, hd * ML_HEAD_DIM:(hd + 1) * ML_HEAD_DIM], axis=-1, keepdims=True)
        den_ref[:, hd:hd + 1] = jnp.maximum(jnp.abs(den), floor[:, hd:hd + 1])


def _rows_to_tile(rows8):
    return jnp.concatenate([rows8] + [jnp.zeros_like(rows8)] * (LANE // SUBLANE - 1), axis=0)


def _ssd_state_update(dec_ref, s_ref, xdt_ref, bm_ref, cm_ref, snew_ref, y_ref, *, base):
    n = s_ref.shape[0]
    half = SSD_WIDTH // SSD_GROUPS
    x_cols = _rows_to_tile(xdt_ref[0]).T
    lane = lax.broadcasted_iota(jnp.int32, (half, LANE), 1)
    accs = [jnp.zeros((half, LANE), F32) for _ in range(SSD_GROUPS)]
    for i in range(n):
        x_col = x_cols[:, i:i + 1].reshape(SSD_HEADS, SSD_HEAD_DIM, 1)
        for g in range(SSD_GROUPS):
            hs = slice(g * SSD_HPG, (g + 1) * SSD_HPG)
            gs = slice(g * SSD_STATE, (g + 1) * SSD_STATE)
            b_row = bm_ref[0, i:i + 1, gs].reshape(1, 1, SSD_STATE)
            decay = jnp.stack([jnp.full((1, 1), dec_ref[base + i, hd], F32)
                               for hd in range(hs.start, hs.stop)], axis=0)
            s_new = decay * s_ref[i, hs] + x_col[hs] * b_row
            snew_ref[i, hs] = s_new
            prod = _bdot_nt(s_new.reshape(half, SSD_STATE), _rows_to_tile(cm_ref[0, :, gs]))
            accs[g] = jnp.where(lane == i, prod, accs[g])
            yield
    y_ref[0] = jnp.concatenate(accs, axis=0).T[:SUBLANE]


def _mlstm_state_update(fs_ref, c_ref, isv_ref, k_ref, q_ref, cnew_ref, num_ref, *, base):
    n = c_ref.shape[0]
    v_cols = _rows_to_tile(isv_ref[0]).T
    lane = lax.broadcasted_iota(jnp.int32, (ML_HEAD_DIM, LANE), 1)
    for hd in range(ML_HEADS):
        sl = slice(hd * ML_HEAD_DIM, (hd + 1) * ML_HEAD_DIM)
        q_rows = _rows_to_tile(q_ref[0, :, sl])
        acc = jnp.zeros((ML_HEAD_DIM, LANE), F32)
        for i in range(n):
            c_new = fs_ref[base + i, hd] * c_ref[i, hd] + v_cols[sl, i:i + 1] * k_ref[0, i:i + 1, sl]
            cnew_ref[i, hd] = c_new
            acc = jnp.where(lane == i, _bdot_nt(c_new, q_rows), acc)
            yield
        num_ref[0, :, sl] = acc.T[:SUBLANE]


def _sample_post_kernel(x_ref, ys_t_ref, num_t_ref, den_ref, zs_ref, xs_ref, zm_ref, xc_ref,
                        dsk_ref, snw_ref, msk_ref, mnw_ref, wout_ref,
                        nw1_ref, fnw_ref, win1_ref, wout1_ref, cw_ref, cb_ref,
                        wax_ref, ba_ref, bx_ref, lam_ref, lbuf_ref, h0_ref,
                        y_ref, nlb_ref, hnew_ref):
    xs = xs_ref[...]
    y_s = ys_t_ref[...] + dsk_ref[...] * xs
    y_s = _group_rmsnorm(y_s * _silu(zs_ref[...]), snw_ref[...])
    num = num_t_ref[...]
    den = den_ref[...]
    h_m = jnp.concatenate(
        [num[:, hd * ML_HEAD_DIM:(hd + 1) * ML_HEAD_DIM] / den[:, hd:hd + 1] for hd in range(ML_HEADS)], axis=-1)
    h_m = _head_layernorm(h_m) * mnw_ref[...]
    y_m = (h_m + msk_ref[...] * xc_ref[...]) * _silu(zm_ref[...])
    h1 = x_ref[...] + _bdot(jnp.concatenate([y_s, y_m], axis=-1), _wload(wout_ref[...]))

    hn = _rms(h1, nw1_ref[...])
    proj = _bdot(hn, _wload(win1_ref[...]))
    gate = proj[:, :LRU_WIDTH]
    xr = proj[:, LRU_WIDTH:]
    xc = _conv_step(lbuf_ref, xr, cw_ref, cb_ref, nlb_ref)
    ra, ix = _blockdiag_tiles(xc, wax_ref)
    a, u = _rglru_gates(xc, ra, ix, ba_ref, bx_ref, lam_ref)
    h = a * h0_ref[...] + u
    hnew_ref[...] = h
    h2 = h1 + _bdot(h * _silu(gate), _wload(wout1_ref[...]))
    y_ref[...] = _rms(h2, fnw_ref[...])


def _full_call(kernel_fn, out_shapes, args, name):
    return pl.pallas_call(
        kernel_fn,
        out_shape=[jax.ShapeDtypeStruct(s, F32) for s in out_shapes],
        compiler_params=pltpu.CompilerParams(vmem_limit_bytes=VMEM_LIMIT),
        name=name,
    )(*args)


def _row(v, width=None):
    v = v.reshape(1, -1).astype(F32)
    if width is not None and v.shape[1] < width:
        v = jnp.pad(v, ((0, 0), (0, width - v.shape[1])))
    return v


PACK_STEPS = 8


def _pack_all(w_in_t, weights):
    flats = [w.reshape(-1, w.shape[-1]) for w in weights]
    k_in = w_in_t.shape[1]
    assert k_in == PACK_STEPS * LANE
    for f in flats:
        assert f.shape[0] % (2 * SUBLANE * PACK_STEPS) == 0
    packed = pl.pallas_call(
        _pack_kernel,
        grid=(PACK_STEPS,),
        in_specs=([pl.BlockSpec((w_in_t.shape[0], LANE), lambda i: (0, i))]
                  + [pl.BlockSpec((f.shape[0] // PACK_STEPS, f.shape[1]), lambda i: (i, 0)) for f in flats]),
        out_specs=([pl.BlockSpec((LANE // 2, IN_MIX_PAD), lambda i: (i, 0))]
                   + [pl.BlockSpec((f.shape[0] // PACK_STEPS // 2, f.shape[1]), lambda i: (i, 0)) for f in flats]),
        out_shape=([jax.ShapeDtypeStruct((k_in // 2, IN_MIX_PAD), jnp.uint32)]
                   + [jax.ShapeDtypeStruct((f.shape[0] // 2, f.shape[1]), jnp.uint32) for f in flats]),
        compiler_params=pltpu.CompilerParams(vmem_limit_bytes=VMEM_LIMIT),
        name="pack_weights",
    )(w_in_t, *flats)
    return [packed[0]] + [p.reshape(w.shape[:-2] + (w.shape[-2] // 2, w.shape[-1]))
                          for p, w in zip(packed[1:], weights)]


def _pack_rows(x):
    return pltpu.bitcast(x.astype(BF16), jnp.uint32)


def _pack_kernel(*refs):
    n = len(refs) // 2
    win_ref, wino_ref = refs[0], refs[n]
    wino_ref[:, :OFF_DT] = _pack_rows(win_ref[:OFF_DT, :].T)
    dt_tile = win_ref[OFF_DT:OFF_DT + LANE, :].T
    lane = lax.broadcasted_iota(jnp.int32, dt_tile.shape, 1)
    wino_ref[:, OFF_DT:OFF_ZM] = _pack_rows(jnp.where(lane < SSD_HEADS, dt_tile, 0.0))
    wino_ref[:, OFF_ZM:] = _pack_rows(win_ref[OFF_DT + SSD_HEADS:, :].T)
    for w_ref, o_ref in zip(refs[1:n], refs[n + 1:]):
        o_ref[...] = _pack_rows(w_ref[...])


def _dense_block_tiles(w):
    nb, bi, bo = w.shape
    per = LANE // bi
    rows = w.reshape(nb // per, per * bi, bo)
    col = jnp.arange(per * bo)
    spread = (col[None, :] % bo == jnp.arange(bo)[:, None]).astype(w.dtype)
    rep = jnp.einsum('tro,oc->trc', rows, spread)
    same_block = (jnp.arange(per * bi)[:, None] // bi) == (col[None, :] // bo)
    return jnp.where(same_block, rep, 0.0)


def kernel(x_prompt, x_sample, state_ssd_conv, state_ssd, state_mlstm_conv, state_mlstm_C, state_mlstm_n,
           state_mlstm_m, state_lru_conv, state_lru_h, meta_tokens, norm_w, final_norm_w, w_in_mix, w_out_mix,
           ssd_conv_w, ssd_conv_b, ssd_dt_bias, ssd_a_log, ssd_d, ssd_norm_w, ml_conv_w, ml_conv_b, ml_wq, ml_wk,
           ml_wv, ml_w_gate, ml_b_gate, ml_skip, ml_norm_w, lru_w_in, lru_w_out, lru_conv_w, lru_conv_b, lru_wa,
           lru_ba, lru_wx, lru_bx, lru_lambda):
    bsz = x_prompt.shape[0]
    dec = x_sample.shape[0]

    wout = w_out_mix[0]
    nw0 = _row(norm_w[0])
    nw1 = _row(norm_w[1])
    fnw = _row(final_norm_w)
    scw = ssd_conv_w[0]
    scb = _row(ssd_conv_b[0])
    dtb = _row(ssd_dt_bias[0], LANE)
    alog = _row(ssd_a_log[0], LANE)
    dsk = _row(jnp.repeat(ssd_d[0], SSD_HEAD_DIM))
    snw = _row(ssd_norm_w[0])
    mcw = ml_conv_w[0]
    mcb = _row(ml_conv_b[0])
    wqk = jnp.concatenate([_dense_block_tiles(ml_wq[0]), _dense_block_tiles(ml_wk[0])], axis=2)
    wv = _dense_block_tiles(ml_wv[0])
    wg_raw = ml_w_gate[0]
    wg = jnp.concatenate([jnp.pad(wg_raw[:, :ML_HEADS], ((0, 0), (0, LANE - ML_HEADS))),
                          jnp.pad(wg_raw[:, ML_HEADS:], ((0, 0), (0, LANE - ML_HEADS)))], axis=1)
    bg = jnp.concatenate([_row(ml_b_gate[0, :ML_HEADS], LANE), _row(ml_b_gate[0, ML_HEADS:], LANE)], axis=1)
    msk = _row(ml_skip[0])
    mnw = _row(ml_norm_w[0])
    win1 = lru_w_in[0]
    wout1 = lru_w_out[0]
    lcw = lru_conv_w[0]
    lcb = _row(lru_conv_b[0])
    wax = jnp.concatenate([lru_wa[0], lru_wx[0]], axis=2)
    r_idx = lax.broadcasted_iota(jnp.int32, (LANE, SSD_WIDTH), 0)
    c_idx = lax.broadcasted_iota(jnp.int32, (LANE, SSD_WIDTH), 1)
    expand = (c_idx // SSD_HEAD_DIM == r_idx).astype(F32)
    ba = _row(lru_ba[0])
    bx = _row(lru_bx[0])
    lam = _row(lru_lambda[0])

    win, wout, wqk, wv, wg, expand, win1, wout1, wax = _pack_all(
        jnp.swapaxes(w_in_mix[0], 0, 1), [wout, wqk, wv, wg, expand, win1, wout1, wax])
    l0_w = (nw0, win, wout, scw, scb, dtb, alog, dsk, snw, mcw, mcb, wqk, wv, wg, bg, msk, mnw, expand)
    l1_w = (nw1, fnw, win1, wout1, lcw, lcb, wax, ba, bx, lam)

    xs2 = x_sample[:, 0]
    sbuf = jnp.moveaxis(state_ssd_conv[0], 1, 0)
    mbuf = jnp.moveaxis(state_mlstm_conv[0], 1, 0)
    lbuf = jnp.moveaxis(state_lru_conv[0], 1, 0)
    m0 = jnp.pad(state_mlstm_m[0], ((0, 0), (0, LANE - ML_HEADS)))
    n0 = state_mlstm_n[0].reshape(dec, ML_WIDTH)
    pre_shapes = ((TAIL, dec, SSD_CONV_CH), (TAIL, dec, ML_WIDTH), (dec, SSD_WIDTH), (dec, SSD_WIDTH),
                  (dec, SSD_GROUPS * SSD_STATE), (dec, SSD_GROUPS * SSD_STATE), (dec, SSD_WIDTH), (dec, LANE),
                  (dec, ML_WIDTH), (dec, ML_WIDTH), (dec, ML_WIDTH), (dec, ML_WIDTH), (dec, LANE),
                  (dec, ML_WIDTH), (dec, LANE), (dec, ML_WIDTH), (dec, ML_HEADS))
    (nsb, nmb, zs, xs_c, bm, cm, xdt_t, dec_t, zm, xc_m, q, isv_t, fs_t, k, m_new, n_new, den) = _full_call(
        _l0_sample_pre_kernel, pre_shapes,
        (xs2, nw0, win, scw, scb, dtb, alog, mcw, mcb, wqk, wv, wg, bg, sbuf, mbuf, m0, n0), "l0_sample_pre")

    zero0 = tuple(jnp.zeros((1,) + s, F32) for s in L0_STATE_SHAPES)
    zero1 = tuple(jnp.zeros((1,) + s, F32) for s in L1_STATE_SHAPES)
    meta = jnp.pad(meta_tokens.astype(F32), ((CHUNK - N_META, 0), (0, 0)))[None]
    meta_out = _l0_prompt(meta, l0_w, zero0, CHUNK - N_META)
    meta1_out = _l1_prompt(meta_out[0], l1_w, zero1, CHUNK - N_META)

    n_steps = (bsz // _rows_per_step(bsz, L0_ROWS)) * (x_prompt.shape[1] // CHUNK)
    assert n_steps == (bsz // _rows_per_step(bsz, L1_ROWS)) * (x_prompt.shape[1] // CHUNK)
    per_step = dec // n_steps
    assert per_step * n_steps == dec

    def step_rows(a):
        a = a.reshape(n_steps, per_step, a.shape[-1])
        return jnp.pad(a, ((0, 0), (0, SUBLANE - per_step), (0, 0)))

    def from_step_rows(a):
        return a[:, :per_step].reshape(dec, a.shape[-1])

    l0_out = _l0_prompt(
        x_prompt, l0_w, tuple(meta_out[1:]), 0,
        stream_in=(dec_t[:, :SSD_HEADS], state_ssd[0], step_rows(xdt_t), step_rows(bm), step_rows(cm)),
        stream_out_shapes=((dec, SSD_HEADS, SSD_HEAD_DIM, SSD_STATE), (n_steps, SUBLANE, SSD_WIDTH)))
    h1_p, p_sc, p_s, p_mc, p_c, p_n, p_m, s_new, ys_r = l0_out
    y_prompt, p_lc, p_lh, c_new, num_r = _l1_prompt(
        h1_p, l1_w, tuple(meta1_out[1:]), 0,
        stream_in=(fs_t[:, :ML_HEADS], state_mlstm_C[0], step_rows(isv_t), step_rows(k), step_rows(q)),
        stream_out_shapes=((dec, ML_HEADS, ML_HEAD_DIM, ML_HEAD_DIM), (n_steps, SUBLANE, ML_WIDTH)))

    p_m = p_m[:, 0, :ML_HEADS]
    p_lh = p_lh[:, 0]

    post_shapes = ((dec, D_MODEL), (TAIL, dec, LRU_WIDTH), (dec, LRU_WIDTH))
    y_s2, nlb, h_new = _full_call(
        _sample_post_kernel, post_shapes,
        (xs2, from_step_rows(ys_r), from_step_rows(num_r), den, zs, xs_c, zm, xc_m, dsk, snw, msk, mnw, wout,
         nw1, fnw, win1, wout1, lcw, lcb, wax, ba, bx, lam, lbuf, state_lru_h[0]), "sample_post")

    s_sc = jnp.moveaxis(nsb, 0, 1)[None]
    s_mc = jnp.moveaxis(nmb, 0, 1)[None]
    s_lc = jnp.moveaxis(nlb, 0, 1)[None]
    return (y_prompt, y_s2[:, None, :],
            p_sc[None], p_s[None], p_mc[None], p_c[None], p_n[None], p_m[None], p_lc[None], p_lh[None],
            s_sc, s_new[None], s_mc, c_new[None], n_new.reshape(dec, ML_HEADS, ML_HEAD_DIM)[None],
            m_new[:, :ML_HEADS][None], s_lc, h_new[None])
```

```python
import functools

import jax
import jax.numpy as jnp
from jax import lax
from jax.experimental import pallas as pl
from jax.experimental.pallas import tpu as pltpu

F32 = jnp.float32
BF16 = jnp.bfloat16

D_MODEL = 1024
N_META = 16
CONV_W = 4
EPS = 1e-6
NEG = -1e30
SSD_WIDTH = 1024
SSD_HEAD_DIM = 64
SSD_HEADS = 16
SSD_GROUPS = 2
SSD_HPG = 8
SSD_STATE = 128
SSD_CONV_CH = 1536
ML_WIDTH = 1024
ML_HEADS = 4
ML_HEAD_DIM = 256
ML_QKV_BLOCK = 4
LRU_WIDTH = 2048
LRU_BLOCKS = 16
LRU_BLOCK = 128
LRU_C = 8.0

LANE = 128
SUBLANE = 8
CHUNK = 128
L0_ROWS = 2
L1_ROWS = 2
L1_GROUPS = 8
SUBCHAINS_PER_ROUND = 2
TAIL = CONV_W - 1

OFF_ZS = 0
OFF_XBC = OFF_ZS + SSD_WIDTH
OFF_DT = OFF_XBC + SSD_CONV_CH
OFF_ZM = OFF_DT + LANE
OFF_XM = OFF_ZM + ML_WIDTH
IN_MIX_PAD = OFF_XM + ML_WIDTH

VMEM_LIMIT = 56 * 1024 * 1024
L0_VMEM_LIMIT = 61 * 1024 * 1024


def _sigmoid(x):
    return 1.0 / (1.0 + jnp.exp(-x))


def _silu(x):
    return x * _sigmoid(x)


def _softplus(x):
    return jnp.maximum(x, 0.0) + jnp.log1p(jnp.exp(-jnp.abs(x)))


def _rms(x, w):
    return x * lax.rsqrt(jnp.mean(x * x, axis=-1, keepdims=True) + EPS) * w


def _bdot(a, b):
    return jnp.dot(a.astype(BF16), b.astype(BF16), preferred_element_type=F32)


def _bdot_nt(a, b):
    return lax.dot_general(a.astype(BF16), b.astype(BF16), (((1,), (1,)), ((), ())), preferred_element_type=F32)


def _wload(w):
    return pltpu.bitcast(w, BF16)


def _split3(x):
    hi = x.astype(BF16)
    r = x - hi.astype(F32)
    mid = r.astype(BF16)
    lo = (r - mid.astype(F32)).astype(BF16)
    return hi, mid, lo


def _cumsum_rows(x, tril):
    hi, mid, lo = _split3(x)
    d = functools.partial(jnp.dot, preferred_element_type=F32)
    return d(tril, hi) + d(tril, mid) + d(tril, lo)


def _expand_heads(x, expand):
    hi, mid, _ = _split3(x)
    d = functools.partial(jnp.dot, preferred_element_type=F32)
    return d(hi, expand) + d(mid, expand)


def _expand_matrix():
    r = lax.broadcasted_iota(jnp.int32, (LANE, SSD_WIDTH), 0)
    c = lax.broadcasted_iota(jnp.int32, (LANE, SSD_WIDTH), 1)
    return jnp.where(lax.shift_right_logical(c, 6) == r, 1.0, 0.0).astype(BF16)


def _blockdiag_tiles(x, w_ref):
    k = w_ref.shape[0]
    w = 2 * w_ref.shape[1]
    m = w_ref.shape[2] // w
    prods = [_bdot(x[:, t * w:(t + 1) * w], _wload(w_ref[t])) for t in range(k)]
    return [jnp.concatenate([p[:, j * w:(j + 1) * w] for p in prods], axis=-1) for j in range(m)]


def _group_rmsnorm(y, w):
    half = SSD_WIDTH // SSD_GROUPS
    parts = []
    for g in range(SSD_GROUPS):
        yg = y[:, g * half:(g + 1) * half]
        parts.append(yg * lax.rsqrt(jnp.mean(yg * yg, axis=-1, keepdims=True) + EPS))
    return jnp.concatenate(parts, axis=-1) * w


def _head_layernorm(h):
    parts = []
    for k in range(ML_HEADS):
        hk = h[:, k * ML_HEAD_DIM:(k + 1) * ML_HEAD_DIM]
        mu = jnp.mean(hk, axis=-1, keepdims=True)
        d = hk - mu
        var = jnp.mean(d * d, axis=-1, keepdims=True)
        parts.append(d * lax.rsqrt(var + EPS))
    return jnp.concatenate(parts, axis=-1)


def _mlstm_qkv_gates(xm, xc, wqk_ref, wv_ref, wg_ref, bg_ref):
    q, k = _blockdiag_tiles(xc, wqk_ref)
    v, = _blockdiag_tiles(xm, wv_ref)
    gates = _bdot(jnp.concatenate([q, k, v], axis=-1), _wload(wg_ref[...])) + bg_ref[...]
    ig = gates[:, :LANE]
    logf = -_softplus(-gates[:, LANE:])
    return q, k * (ML_HEAD_DIM ** -0.5), v, ig, logf


N_L0_W = 18
N_L0_S = 6


def _l0_prompt_kernel(x_ref, xnext_ref, *refs, front_pad, rows, stream):
    w_refs = refs[:N_L0_W]
    init_refs = refs[N_L0_W:N_L0_W + N_L0_S]
    pos = N_L0_W + N_L0_S
    stream_in = refs[pos:pos + N_STREAM_IN] if stream else ()
    pos += len(stream_in)
    out_refs = refs[pos:pos + N_L0_S + 1]
    pos += N_L0_S + 1
    stream_out = refs[pos:pos + N_STREAM_OUT] if stream else ()
    pos += len(stream_out)
    scratch = refs[pos:]
    per_row = len(scratch) // rows
    c = pl.program_id(1)
    win_ref, wout_ref = w_refs[1], w_refs[2]
    q_len = x_ref.shape[1]
    pieces = {}

    def each_row(phase):
        return [_l0_prompt_row(x_ref.at[r], xnext_ref.at[r], *w_refs, *init_refs, *(o.at[r] for o in out_refs),
                               *scratch[r * per_row:(r + 1) * per_row], front_pad=front_pad, phase=phase,
                               emit=lambda k0, y, r=r: pieces.setdefault(k0, {}).__setitem__(r, y))
                for r in range(rows)]

    @pl.when(c == 0)
    def _():
        each_row("init")

    bodies = each_row("body")
    proj_refs = [scratch[r * per_row + per_row - 1] for r in range(rows)]
    _, to_time = _perm_matrices(q_len)
    partials = []

    def in_proj():
        lhs = jnp.concatenate([hn_next for _, hn_next, _ in bodies], axis=0)
        for lo, hi in L0_PROJ_PIECES:
            res = _bdot(lhs, _wload(win_ref[:, lo:hi]))
            for r in range(rows):
                proj_refs[r][:, lo:hi] = res[r * q_len:(r + 1) * q_len]
            yield

    def out_proj():
        half = SSD_WIDTH // SSD_GROUPS
        pending = ([(g * half, half) for g in range(SSD_GROUPS)]
                   + [(SSD_WIDTH + hd * ML_HEAD_DIM, ML_HEAD_DIM) for hd in range(ML_HEADS)])
        while pending:
            for k0, width in list(pending):
                if len(pieces.get(k0, ())) == rows:
                    y_t = jnp.concatenate([_move_rows(to_time, pieces[k0][r].astype(BF16)) for r in range(rows)],
                                          axis=0)
                    partials.append(_bdot(y_t, _wload(wout_ref[k0 // 2:(k0 + width) // 2, :])))
                    pending.remove((k0, width))
            yield

    chains = [gen for gens, _, _ in bodies for gen in gens] + [in_proj(), out_proj()]
    if stream:
        step = pl.program_id(0) * pl.num_programs(1) + c
        chains.append(_ssd_state_update(*stream_in, *stream_out, base=step * stream_in[1].shape[0]))
    _run_round_robin(chains)
    total = partials[0]
    for part in partials[1:]:
        total = total + part
    for r, (_, _, x) in enumerate(bodies):
        h1 = x + total[r * q_len:(r + 1) * q_len]
        if front_pad:
            h1 = jnp.where(lax.broadcasted_iota(jnp.int32, (q_len, 1), 0) >= front_pad, h1, 0.0)
        out_refs[0][r] = h1

    @pl.when(c == pl.num_programs(1) - 1)
    def _():
        each_row("final")


L0_PROJ_PIECES = ((OFF_XBC, OFF_ZM), (OFF_XM, IN_MIX_PAD), (OFF_ZM, OFF_XM), (OFF_ZS, OFF_XBC))


def _run_round_robin(gens):
    live = list(gens)
    waiting = []
    while live or waiting:
        for _ in range(min(SUBCHAINS_PER_ROUND, len(waiting))):
            live.append(waiting.pop(0))
        for gen in list(live):
            step = next(gen, "done")
            if step == "done":
                live.remove(gen)
            elif step is not None:
                waiting.extend(step)


def _l0_prompt_row(x_ref, xnext_ref, nw_ref, win_ref, wout_ref,
                   scw_ref, scb_ref, dtb_ref, alog_ref, dsk_ref, snw_ref,
                   mcw_ref, mcb_ref, wqk_ref, wv_ref, wg_ref, bg_ref, msk_ref, mnw_ref, expand_ref,
                   isc_ref, iss_ref, imc_ref, ict_ref, inn_ref, imm_ref,
                   h1_ref, osc_ref, oss_ref, omc_ref, oct_ref, onn_ref, omm_ref,
                   sbuf, mbuf, s_st, ct_st, n_st, m_st, proj_s, *, front_pad, phase, emit):
    q_len = x_ref.shape[0]

    if phase == "init":
        sbuf[...] = jnp.zeros(sbuf.shape, F32)
        mbuf[...] = jnp.zeros(mbuf.shape, F32)
        sbuf[SUBLANE - TAIL:SUBLANE, :] = isc_ref[0]
        mbuf[SUBLANE - TAIL:SUBLANE, :] = imc_ref[0]
        for g in range(SSD_GROUPS):
            heads = iss_ref[0, g * SSD_HPG:(g + 1) * SSD_HPG]
            s_st[g] = heads.reshape(SSD_HPG * SSD_HEAD_DIM, SSD_STATE).T
        for hd in range(ML_HEADS):
            ct_st[hd] = ict_ref[0, hd].T
        n_st[...] = inn_ref[0]
        m_st[...] = imm_ref[0]
        hn0 = _move_rows(_perm_matrices(q_len)[0], _rms(x_ref[...], nw_ref[...]).astype(BF16))
        proj_s[...] = _bdot(hn0, _wload(win_ref[...]))
        return None
    if phase == "final":
        osc_ref[...] = sbuf[SUBLANE - TAIL:SUBLANE, :]
        omc_ref[...] = mbuf[SUBLANE - TAIL:SUBLANE, :]
        for g in range(SSD_GROUPS):
            oss_ref[g * SSD_HPG:(g + 1) * SSD_HPG] = s_st[g].T.reshape(SSD_HPG, SSD_HEAD_DIM, SSD_STATE)
        for hd in range(ML_HEADS):
            oct_ref[hd] = ct_st[hd].T
        onn_ref[...] = n_st[...]
        omm_ref[...] = m_st[...]
        return None

    x = x_ref[...]
    to_perm, to_time = _perm_matrices(q_len)
    t_col = _perm_time(q_len)
    t_row = _perm_time(q_len, row=True)
    causal = t_col >= t_row
    tril = jnp.where(causal, 1.0, 0.0).astype(BF16)
    valid = (t_col >= front_pad) if front_pad else None
    hn_next = _move_rows(to_perm, _rms(xnext_ref[...], nw_ref[...]).astype(BF16))
    xbc_raw = proj_s[:, OFF_XBC:OFF_XBC + SSD_CONV_CH]
    dt_raw = proj_s[:, OFF_DT:OFF_DT + LANE]
    xm = proj_s[:, OFF_XM:OFF_XM + ML_WIDTH]
    z_s = proj_s[:, OFF_ZS:OFF_ZS + SSD_WIDTH]
    z_m = proj_s[:, OFF_ZM:OFF_ZM + ML_WIDTH]

    def ssd():
        xbc = _silu(_conv_perm(sbuf, xbc_raw, scw_ref, scb_ref))
        yield
        xs = xbc[:, :SSD_WIDTH]
        bm = xbc[:, SSD_WIDTH:SSD_WIDTH + SSD_GROUPS * SSD_STATE]
        cm = xbc[:, SSD_WIDTH + SSD_GROUPS * SSD_STATE:]
        dt = _softplus(dt_raw + dtb_ref[...])
        if front_pad:
            dt = jnp.where(valid, dt, 0.0)
        log_a = -dt * jnp.exp(alog_ref[...])
        a_cs = _cumsum_rows(log_a, tril)
        yield
        a_last = a_cs[q_len - 1:q_len, :]
        expand = _wload(expand_ref[...])
        w_state = _expand_heads(dt * jnp.exp(a_last - a_cs), expand)
        e_acs = _expand_heads(jnp.exp(a_cs), expand)
        a_cs_t = a_cs.T
        dt_t = dt.T
        yield
        pair_lo = lax.broadcasted_iota(jnp.int32, (q_len, LANE), 1) < SSD_HEAD_DIM
        half = SSD_WIDTH // SSD_GROUPS

        def group(g):
            cols = slice(g * half, (g + 1) * half)
            bg = bm[:, g * SSD_STATE:(g + 1) * SSD_STATE]
            cg = cm[:, g * SSD_STATE:(g + 1) * SSD_STATE]
            bg_t = bg.T
            xg = xs[:, cols]
            eg = e_acs[:, cols]
            s_old = s_st[g]
            cb = _bdot(cg, bg_t)
            y_off = _bdot(cg, s_old) * eg
            s_st[g] = eg[q_len - 1:q_len, :] * s_old + _bdot(bg_t, xg * w_state[:, cols])
            yield
            y_pairs = []
            for pr in range(SSD_HPG // 2):
                ms = []
                for e in (2 * pr, 2 * pr + 1):
                    hd = g * SSD_HPG + e
                    seg = jnp.exp(jnp.where(causal, a_cs[:, hd:hd + 1] - a_cs_t[hd:hd + 1, :], -jnp.inf))
                    ms.append(cb * seg * dt_t[hd:hd + 1, :])
                xp = xg[:, pr * LANE:(pr + 1) * LANE]
                rhs = jnp.concatenate([jnp.where(pair_lo, xp, 0.0), jnp.where(pair_lo, 0.0, xp)], axis=0)
                y_pairs.append(_bdot(jnp.concatenate(ms, axis=-1), rhs))
                yield
            y_g = (jnp.concatenate(y_pairs, axis=-1) + y_off + dsk_ref[:, cols] * xg) * _silu(z_s[:, cols])
            y_g = y_g * lax.rsqrt(jnp.mean(y_g * y_g, axis=-1, keepdims=True) + EPS)
            emit(g * half, y_g * snw_ref[:, cols])

        yield [group(g) for g in range(SSD_GROUPS)]

    def mlstm():
        xc = _silu(_conv_perm(mbuf, xm, mcw_ref, mcb_ref))
        yield
        q, k = _blockdiag_tiles(xc, wqk_ref)
        v, = _blockdiag_tiles(xm, wv_ref)
        yield
        gates = _bdot(jnp.concatenate([q, k, v], axis=-1), _wload(wg_ref[...])) + bg_ref[...]
        k = k * (ML_HEAD_DIM ** -0.5)
        yield
        ig = gates[:, :LANE]
        logf = -_softplus(-gates[:, LANE:])
        if front_pad:
            ig = jnp.where(valid, ig, NEG)
            logf = jnp.where(valid, logf, 0.0)
        bcum = _cumsum_rows(logf, tril)
        yield
        ftot = bcum[q_len - 1:q_len, :]
        m_prev = m_st[...]
        w_end = ftot - bcum + ig
        m_new = jnp.maximum(ftot + m_prev, jnp.max(w_end, axis=0, keepdims=True))
        sc = jnp.exp(ftot + m_prev - m_new)
        wexp = jnp.exp(w_end - m_new)
        inter = bcum + m_prev
        bcum_t = bcum.T
        ig_t = ig.T
        m_st[...] = m_new

        def head(hd):
            sl = slice(hd * ML_HEAD_DIM, (hd + 1) * ML_HEAD_DIM)
            q_h, k_h, v_h = q[:, sl], k[:, sl], v[:, sl]
            k_t = k_h.T
            dmat = jnp.where(causal, bcum[:, hd:hd + 1] - bcum_t[hd:hd + 1, :] + ig_t[hd:hd + 1, :], -jnp.inf)
            inter_h = inter[:, hd:hd + 1]
            m_t = jnp.maximum(inter_h, jnp.max(dmat, axis=-1, keepdims=True))
            dexp = jnp.exp(dmat - m_t)
            inter_sc = jnp.exp(inter_h - m_t)
            s = _bdot(q_h, k_t) * dexp
            yield
            ct_old = ct_st[hd]
            n_old = n_st[hd:hd + 1, :]
            num = _bdot(s, v_h) + inter_sc * _bdot(q_h, ct_old)
            den = jnp.sum(s, axis=-1, keepdims=True) + inter_sc * jnp.sum(q_h * n_old, axis=-1, keepdims=True)
            h_h = num / jnp.maximum(jnp.abs(den), jnp.exp(-m_t))
            w_col = wexp[:, hd:hd + 1]
            sc_h = sc[:, hd:hd + 1]
            ct_st[hd] = sc_h * ct_old + _bdot(k_t, v_h * w_col)
            n_st[hd:hd + 1, :] = sc_h * n_old + jnp.sum(k_h * w_col, axis=0, keepdims=True)
            yield
            mu = jnp.mean(h_h, axis=-1, keepdims=True)
            dev = h_h - mu
            var = jnp.mean(dev * dev, axis=-1, keepdims=True)
            h_h = dev * lax.rsqrt(var + EPS) * mnw_ref[:, sl]
            emit(SSD_WIDTH + hd * ML_HEAD_DIM, (h_h + msk_ref[:, sl] * xc[:, sl]) * _silu(z_m[:, sl]))

        yield [head(hd) for hd in range(ML_HEADS)]

    return [ssd(), mlstm()], hn_next, x


def _const_spec(shape):
    nd = len(shape)
    return pl.BlockSpec(shape, lambda b, c: (0,) * nd)


def _state_spec(shape, rows):
    nd = len(shape)
    if rows:
        return pl.BlockSpec((rows,) + shape, lambda b, c: (b,) + (0,) * nd)
    return pl.BlockSpec((1,) + shape, lambda b, c: (0,) * (nd + 1))


def _rows_per_step(bsz, want):
    return want if bsz % want == 0 else 1


L0_STATE_SHAPES = ((TAIL, SSD_CONV_CH), (SSD_HEADS, SSD_HEAD_DIM, SSD_STATE), (TAIL, ML_WIDTH),
                   (ML_HEADS, ML_HEAD_DIM, ML_HEAD_DIM), (ML_HEADS, ML_HEAD_DIM), (1, LANE))
L0_CARRY_SHAPES = ((SUBLANE, SSD_CONV_CH), (SUBLANE, ML_WIDTH), (SSD_GROUPS, SSD_STATE, SSD_WIDTH // SSD_GROUPS),
                   (ML_HEADS, ML_HEAD_DIM, ML_HEAD_DIM), (ML_HEADS, ML_HEAD_DIM), (1, LANE))


def _stream_specs(arrays, n_steps, nc):
    specs = []
    for a in arrays:
        if a.ndim == 2:
            specs.append(pl.BlockSpec(memory_space=pltpu.SMEM))
            continue
        assert a.shape[0] % n_steps == 0
        block = (a.shape[0] // n_steps,) + a.shape[1:]
        specs.append(pl.BlockSpec(block, lambda b, c, nd=a.ndim: (b * nc + c,) + (0,) * (nd - 1)))
    return specs


def _l0_prompt(x, weights, init, front_pad, stream_in=(), stream_out_shapes=()):
    bsz, length, _ = x.shape
    q_len = min(CHUNK, length)
    assert length % q_len == 0
    rows = _rows_per_step(bsz, L0_ROWS)
    assert len(weights) == N_L0_W and len(init) == N_L0_S
    nc = length // q_len
    grid = (bsz // rows, nc)
    n_steps = grid[0] * nc
    last = nc - 1
    x_spec = pl.BlockSpec((rows, q_len, D_MODEL), lambda b, c: (b, c, 0))
    next_spec = pl.BlockSpec((rows, q_len, D_MODEL), lambda b, c: (b, jnp.minimum(c + 1, last), 0))
    stream_outs = [jax.ShapeDtypeStruct(s, F32) for s in stream_out_shapes]
    in_specs = ([x_spec, next_spec] + [_const_spec(w.shape) for w in weights]
                + [_state_spec(s, 0) for s in L0_STATE_SHAPES] + _stream_specs(stream_in, n_steps, nc))
    out_shape = ([jax.ShapeDtypeStruct((bsz, length, D_MODEL), F32)]
                 + [jax.ShapeDtypeStruct((bsz,) + s, F32) for s in L0_STATE_SHAPES] + stream_outs)
    out_specs = ([x_spec] + [_state_spec(s, rows) for s in L0_STATE_SHAPES]
                 + _stream_specs(stream_outs, n_steps, nc))
    row_scratch = L0_CARRY_SHAPES + ((q_len, IN_MIX_PAD),)
    scratch = [pltpu.VMEM(s, F32) for _ in range(rows) for s in row_scratch]
    return pl.pallas_call(
        functools.partial(_l0_prompt_kernel, front_pad=front_pad, rows=rows, stream=bool(stream_in)),
        grid=grid, in_specs=in_specs, out_specs=out_specs, out_shape=out_shape, scratch_shapes=scratch,
        compiler_params=pltpu.CompilerParams(dimension_semantics=("arbitrary", "arbitrary"),
                                             vmem_limit_bytes=L0_VMEM_LIMIT),
        name="l0_prompt",
    )(x, x, *weights, *init, *stream_in)


def _rglru_gates(xc, ra, ix, ba_ref, bx_ref, lam_ref):
    r = _sigmoid(ra + ba_ref[...])
    i = _sigmoid(ix + bx_ref[...])
    log_a = r * (-LRU_C * _softplus(-lam_ref[...]))
    a = jnp.exp(log_a)
    var = 1.0 - a * a
    root = jnp.where(var > 0.0, var * lax.rsqrt(var), 0.0)
    u = root * (i * xc)
    return a, u


def _perm_time(n, row=False):
    p = lax.broadcasted_iota(jnp.int32, (1, n) if row else (n, 1), 1 if row else 0)
    return (n // SUBLANE) * (p & (SUBLANE - 1)) + lax.shift_right_logical(p, 3)


def _perm_matrices(n):
    nb = n // SUBLANE
    r = lax.broadcasted_iota(jnp.int32, (n, n), 0)
    c = lax.broadcasted_iota(jnp.int32, (n, n), 1)
    to_perm = jnp.where(c == nb * (r & (SUBLANE - 1)) + lax.shift_right_logical(r, 3), 1.0, 0.0)
    to_time = jnp.where(r == nb * (c & (SUBLANE - 1)) + lax.shift_right_logical(c, 3), 1.0, 0.0)
    return to_perm.astype(BF16), to_time.astype(BF16)


def _move_rows(sel, x_bf16):
    return jnp.dot(sel, x_bf16, preferred_element_type=F32).astype(BF16)


def _conv_perm(tail_ref, x, w_ref, b_ref):
    n, ch = x.shape
    nb = n // SUBLANE
    x3 = x.reshape(nb, SUBLANE, ch)
    tail8 = tail_ref[...]
    sub = lax.broadcasted_iota(jnp.int32, (SUBLANE, ch), 0)
    y = b_ref[...].reshape(1, 1, ch) + w_ref[TAIL:TAIL + 1, :].reshape(1, 1, ch) * x3
    wrapped = [jnp.where(sub >= 1, pltpu.roll(x3[nb - d], 1, 0), tail8[SUBLANE - d:SUBLANE - d + 1, :])
               for d in range(1, CONV_W)]
    for back in range(1, CONV_W):
        head = jnp.stack([wrapped[back - j - 1] for j in range(back)], axis=0)
        shifted = jnp.concatenate([head, x3[:nb - back]], axis=0)
        y = y + w_ref[TAIL - back:TAIL - back + 1, :].reshape(1, 1, ch) * shifted
    for d in range(1, CONV_W):
        tail_ref[SUBLANE - d:SUBLANE - d + 1, :] = x3[nb - d][SUBLANE - 1:SUBLANE, :]
    return y.reshape(n, ch)


def _scan_perm(a, u, h_prev):
    n, ch = a.shape
    nb = n // SUBLANE
    a3 = a.reshape(nb, SUBLANE, ch)
    u3 = u.reshape(nb, SUBLANE, ch)
    local = [u3[0]]
    decay = [a3[0]]
    for j in range(1, nb):
        local.append(a3[j] * local[-1] + u3[j])
        decay.append(a3[j] * decay[-1])
    seg_u, seg_a = local[-1], decay[-1]
    sub = lax.broadcasted_iota(jnp.int32, (SUBLANE, ch), 0)
    shift = 1
    while shift < SUBLANE:
        keep = sub >= shift
        seg_u = seg_u + seg_a * jnp.where(keep, pltpu.roll(seg_u, shift, 0), 0.0)
        seg_a = seg_a * jnp.where(keep, pltpu.roll(seg_a, shift, 0), 1.0)
        shift *= 2
    seg_end = seg_a * h_prev + seg_u
    carry = jnp.where(sub >= 1, pltpu.roll(seg_end, 1, 0), h_prev)
    h3 = jnp.stack([local[j] + decay[j] * carry for j in range(nb)], axis=0)
    return h3.reshape(n, ch), seg_end[SUBLANE - 1:SUBLANE, :]


N_L1_W = 10
N_L1_S = 2


N_STREAM_IN = 5
N_STREAM_OUT = 2


def _l1_prompt_kernel(h_ref, hnext_ref, *refs, front_pad, rows, stream):
    w_refs = refs[:N_L1_W]
    init_refs = refs[N_L1_W:N_L1_W + N_L1_S]
    pos = N_L1_W + N_L1_S
    stream_in = refs[pos:pos + N_STREAM_IN] if stream else ()
    pos += len(stream_in)
    out_refs = refs[pos:pos + N_L1_S + 1]
    pos += N_L1_S + 1
    stream_out = refs[pos:pos + N_STREAM_OUT] if stream else ()
    pos += len(stream_out)
    scratch = refs[pos:]
    per_row = len(scratch) // rows
    c = pl.program_id(1)
    fnw_ref, win_ref, wout_ref = w_refs[1], w_refs[2], w_refs[3]
    q_len = h_ref.shape[1]
    gw = LRU_WIDTH // L1_GROUPS
    pieces = {}

    def each_row(phase):
        return [_l1_prompt_row(h_ref.at[r], hnext_ref.at[r], *w_refs, *init_refs, *(o.at[r] for o in out_refs),
                               *scratch[r * per_row:(r + 1) * per_row], front_pad=front_pad, phase=phase,
                               emit=lambda g, y, r=r: pieces.setdefault(g, {}).__setitem__(r, y))
                for r in range(rows)]

    @pl.when(c == 0)
    def _():
        each_row("init")

    bodies = each_row("body")
    proj_refs = [scratch[r * per_row + per_row - 1] for r in range(rows)]
    _, to_time = _perm_matrices(q_len)
    partials = []

    def in_proj():
        lhs = jnp.concatenate([hn_next for _, hn_next, _ in bodies], axis=0)
        for g in range(L1_GROUPS):
            for lo in (g * gw, LRU_WIDTH + g * gw):
                res = _bdot(lhs, _wload(win_ref[:, lo:lo + gw]))
                for r in range(rows):
                    proj_refs[r][:, lo:lo + gw] = res[r * q_len:(r + 1) * q_len]
            yield

    def out_proj():
        pending = list(range(L1_GROUPS))
        while pending:
            for g in list(pending):
                if len(pieces.get(g, ())) == rows:
                    y_t = jnp.concatenate([_move_rows(to_time, pieces[g][r].astype(BF16)) for r in range(rows)],
                                          axis=0)
                    partials.append(_bdot(y_t, _wload(wout_ref[g * gw // 2:(g + 1) * gw // 2, :])))
                    pending.remove(g)
            yield

    chains = [bodies[r][0][g] for g in range(L1_GROUPS) for r in range(rows)]
    if stream:
        step = pl.program_id(0) * pl.num_programs(1) + c
        chains.append(_mlstm_state_update(*stream_in, *stream_out, base=step * stream_in[1].shape[0]))
    _run_staggered([in_proj()] + chains + [out_proj()])
    total = partials[0]
    for part in partials[1:]:
        total = total + part
    for r, (_, _, h_in) in enumerate(bodies):
        out_refs[0][r] = _rms(h_in + total[r * q_len:(r + 1) * q_len], fnw_ref[...])

    @pl.when(c == pl.num_programs(1) - 1)
    def _():
        each_row("final")


def _l1_in_proj(h_val, nw_ref, win_ref, to_perm):
    hn = _move_rows(to_perm, _rms(h_val, nw_ref[...]).astype(BF16))
    return _bdot(hn, _wload(win_ref[...]))


def _l1_prompt_row(h_ref, hnext_ref, nw_ref, fnw_ref, win_ref, wout_ref, cw_ref, cb_ref,
                   wax_ref, ba_ref, bx_ref, lam_ref, ilc_ref, ilh_ref,
                   y_ref, olc_ref, olh_ref, lbuf, h_st, proj_s, *, front_pad, phase, emit):
    if phase == "init":
        lbuf[...] = jnp.zeros(lbuf.shape, F32)
        lbuf[SUBLANE - TAIL:SUBLANE, :] = ilc_ref[0]
        h_st[...] = ilh_ref[0]
        proj_s[...] = _l1_in_proj(h_ref[...], nw_ref, win_ref, _perm_matrices(h_ref.shape[0])[0])
        return None
    if phase == "final":
        olc_ref[...] = lbuf[SUBLANE - TAIL:SUBLANE, :]
        olh_ref[...] = h_st[...]
        return None

    q_len = h_ref.shape[0]
    h_in = h_ref[...]
    to_perm, _ = _perm_matrices(q_len)
    hn_next = _move_rows(to_perm, _rms(hnext_ref[...], nw_ref[...]).astype(BF16))
    if front_pad:
        valid = _perm_time(q_len) >= front_pad
    gw = LRU_WIDTH // L1_GROUPS
    tiles = gw // LANE
    gates = [proj_s[:, g * gw:(g + 1) * gw] for g in range(L1_GROUPS)]
    xrs = [proj_s[:, LRU_WIDTH + g * gw:LRU_WIDTH + (g + 1) * gw] for g in range(L1_GROUPS)]

    def group(g):
        cg = slice(g * gw, (g + 1) * gw)
        xc = _conv_perm(lbuf.at[:, cg], xrs[g], cw_ref.at[:, cg], cb_ref.at[:, cg])
        ra, ix = _blockdiag_tiles(xc, wax_ref.at[g * tiles:(g + 1) * tiles])
        yield
        a, u = _rglru_gates(xc, ra, ix, ba_ref.at[:, cg], bx_ref.at[:, cg], lam_ref.at[:, cg])
        if front_pad:
            a = jnp.where(valid, a, 1.0)
            u = jnp.where(valid, u, 0.0)
        yield
        h, h_last = _scan_perm(a, u, h_st[:, cg])
        h_st[:, cg] = h_last
        yield
        emit(g, h * _silu(gates[g]))

    return [group(g) for g in range(L1_GROUPS)], hn_next, h_in


def _run_staggered(gens):
    live = []
    pending = list(gens)
    while pending or live:
        if pending:
            live.append(pending.pop(0))
        for gen in list(live):
            if next(gen, "done") == "done":
                live.remove(gen)


L1_STATE_SHAPES = ((TAIL, LRU_WIDTH), (1, LRU_WIDTH))


def _l1_prompt(h1, weights, init, front_pad, stream_in=(), stream_out_shapes=()):
    bsz, length, _ = h1.shape
    q_len = min(CHUNK, length)
    assert length % q_len == 0
    rows = _rows_per_step(bsz, L1_ROWS)
    assert len(weights) == N_L1_W and len(init) == N_L1_S
    nc = length // q_len
    grid = (bsz // rows, nc)
    n_steps = grid[0] * nc
    last = nc - 1
    x_spec = pl.BlockSpec((rows, q_len, D_MODEL), lambda b, c: (b, c, 0))
    next_spec = pl.BlockSpec((rows, q_len, D_MODEL), lambda b, c: (b, jnp.minimum(c + 1, last), 0))
    stream_outs = [jax.ShapeDtypeStruct(s, F32) for s in stream_out_shapes]
    in_specs = ([x_spec, next_spec] + [_const_spec(w.shape) for w in weights]
                + [_state_spec(s, 0) for s in L1_STATE_SHAPES] + _stream_specs(stream_in, n_steps, nc))
    out_shape = ([jax.ShapeDtypeStruct((bsz, length, D_MODEL), F32)]
                 + [jax.ShapeDtypeStruct((bsz,) + s, F32) for s in L1_STATE_SHAPES] + stream_outs)
    out_specs = ([x_spec] + [_state_spec(s, rows) for s in L1_STATE_SHAPES]
                 + _stream_specs(stream_outs, n_steps, nc))
    row_scratch = ((SUBLANE, LRU_WIDTH), (1, LRU_WIDTH), (q_len, 2 * LRU_WIDTH))
    scratch = [pltpu.VMEM(s, F32) for _ in range(rows) for s in row_scratch]
    return pl.pallas_call(
        functools.partial(_l1_prompt_kernel, front_pad=front_pad, rows=rows, stream=bool(stream_in)),
        grid=grid, in_specs=in_specs, out_specs=out_specs, out_shape=out_shape, scratch_shapes=scratch,
        compiler_params=pltpu.CompilerParams(dimension_semantics=("arbitrary", "arbitrary"),
                                             vmem_limit_bytes=VMEM_LIMIT),
        name="l1_prompt",
    )(h1, h1, *weights, *init, *stream_in)


def _conv_step(buf_ref, x, w_ref, b_ref, newbuf_ref):
    y = b_ref[...] + w_ref[3:4, :] * x
    for tap in range(TAIL):
        y = y + w_ref[tap:tap + 1, :] * buf_ref[tap]
    for tap in range(TAIL - 1):
        newbuf_ref[tap] = buf_ref[tap + 1]
    newbuf_ref[TAIL - 1] = x
    return y


def _l0_sample_pre_kernel(x_ref, nw_ref, win_ref, scw_ref, scb_ref, dtb_ref, alog_ref,
                          mcw_ref, mcb_ref, wqk_ref, wv_ref, wg_ref, bg_ref,
                          sbuf_ref, mbuf_ref, m0_ref, n0_ref,
                          nsb_ref, nmb_ref, zs_ref, xs_ref, bm_ref, cm_ref, xdt_t_ref, dec_t_ref,
                          zm_ref, xc_ref, q_ref, isv_t_ref, fs_t_ref, k_ref, mnew_ref, nnew_ref, den_ref):
    x = x_ref[...]
    hn = _rms(x, nw_ref[...])
    proj = _bdot(hn, _wload(win_ref[...]))
    zs_ref[...] = proj[:, OFF_ZS:OFF_ZS + SSD_WIDTH]
    zm_ref[...] = proj[:, OFF_ZM:OFF_ZM + ML_WIDTH]
    xbc = proj[:, OFF_XBC:OFF_XBC + SSD_CONV_CH]
    dt_raw = proj[:, OFF_DT:OFF_DT + LANE]
    xm = proj[:, OFF_XM:OFF_XM + ML_WIDTH]
    expand = _expand_matrix()

    xbc = _silu(_conv_step(sbuf_ref, xbc, scw_ref, scb_ref, nsb_ref))
    xs = xbc[:, :SSD_WIDTH]
    xs_ref[...] = xs
    bm_ref[...] = xbc[:, SSD_WIDTH:SSD_WIDTH + SSD_GROUPS * SSD_STATE]
    cm_ref[...] = xbc[:, SSD_WIDTH + SSD_GROUPS * SSD_STATE:]
    dt = _softplus(dt_raw + dtb_ref[...])
    log_a = -dt * jnp.exp(alog_ref[...])
    xdt_t_ref[...] = xs * _expand_heads(dt, expand)
    dec_t_ref[...] = jnp.exp(log_a)

    xc = _silu(_conv_step(mbuf_ref, xm, mcw_ref, mcb_ref, nmb_ref))
    xc_ref[...] = xc
    q, k, v, ig, logf = _mlstm_qkv_gates(xm, xc, wqk_ref, wv_ref, wg_ref, bg_ref)
    m0 = m0_ref[...]
    m_new = jnp.maximum(logf + m0, ig)
    fs = jnp.exp(logf + m0 - m_new)
    is_ = jnp.exp(ig - m_new)
    mnew_ref[...] = m_new
    r = lax.broadcasted_iota(jnp.int32, (LANE, ML_WIDTH), 0)
    cidx = lax.broadcasted_iota(jnp.int32, (LANE, ML_WIDTH), 1)
    expand_m = jnp.where(lax.shift_right_logical(cidx, 8) == r, 1.0, 0.0).astype(BF16)
    fs_e = _expand_heads(fs, expand_m)
    is_e = _expand_heads(is_, expand_m)
    n_new = fs_e * n0_ref[...] + is_e * k
    nnew_ref[...] = n_new
    q_ref[...] = q
    k_ref[...] = k
    isv_t_ref[...] = is_e * v
    fs_t_ref[...] = fs
    nq = n_new * q
    floor = jnp.exp(-m_new)
    for hd in range(ML_HEADS):
        den = jnp.sum(nq[:, hd * ML_HEAD_DIM:(hd + 1) * ML_HEAD_DIM], axis=-1, keepdims=True)
        den_ref[:, hd:hd + 1] = jnp.maximum(jnp.abs(den), floor[:, hd:hd + 1])


def _rows_to_tile(rows8):
    return jnp.concatenate([rows8] + [jnp.zeros_like(rows8)] * (LANE // SUBLANE - 1), axis=0)


def _ssd_state_update(dec_ref, s_ref, xdt_ref, bm_ref, cm_ref, snew_ref, y_ref, *, base):
    n = s_ref.shape[0]
    half = SSD_WIDTH // SSD_GROUPS
    x_cols = _rows_to_tile(xdt_ref[0]).T
    lane = lax.broadcasted_iota(jnp.int32, (half, LANE), 1)
    accs = [jnp.zeros((half, LANE), F32) for _ in range(SSD_GROUPS)]
    for i in range(n):
        x_col = x_cols[:, i:i + 1].reshape(SSD_HEADS, SSD_HEAD_DIM, 1)
        for g in range(SSD_GROUPS):
            hs = slice(g * SSD_HPG, (g + 1) * SSD_HPG)
            gs = slice(g * SSD_STATE, (g + 1) * SSD_STATE)
            b_row = bm_ref[0, i:i + 1, gs].reshape(1, 1, SSD_STATE)
            decay = jnp.stack([jnp.full((1, 1), dec_ref[base + i, hd], F32)
                               for hd in range(hs.start, hs.stop)], axis=0)
            s_new = decay * s_ref[i, hs] + x_col[hs] * b_row
            snew_ref[i, hs] = s_new
            prod = _bdot_nt(s_new.reshape(half, SSD_STATE), _rows_to_tile(cm_ref[0, :, gs]))
            accs[g] = jnp.where(lane == i, prod, accs[g])
            yield
    y_ref[0] = jnp.concatenate(accs, axis=0).T[:SUBLANE]


def _mlstm_state_update(fs_ref, c_ref, isv_ref, k_ref, q_ref, cnew_ref, num_ref, *, base):
    n = c_ref.shape[0]
    v_cols = _rows_to_tile(isv_ref[0]).T
    lane = lax.broadcasted_iota(jnp.int32, (ML_HEAD_DIM, LANE), 1)
    for hd in range(ML_HEADS):
        sl = slice(hd * ML_HEAD_DIM, (hd + 1) * ML_HEAD_DIM)
        q_rows = _rows_to_tile(q_ref[0, :, sl])
        acc = jnp.zeros((ML_HEAD_DIM, LANE), F32)
        for i in range(n):
            c_new = fs_ref[base + i, hd] * c_ref[i, hd] + v_cols[sl, i:i + 1] * k_ref[0, i:i + 1, sl]
            cnew_ref[i, hd] = c_new
            acc = jnp.where(lane == i, _bdot_nt(c_new, q_rows), acc)
            yield
        num_ref[0, :, sl] = acc.T[:SUBLANE]


def _sample_post_kernel(x_ref, ys_t_ref, num_t_ref, den_ref, zs_ref, xs_ref, zm_ref, xc_ref,
                        dsk_ref, snw_ref, msk_ref, mnw_ref, wout_ref,
                        nw1_ref, fnw_ref, win1_ref, wout1_ref, cw_ref, cb_ref,
                        wax_ref, ba_ref, bx_ref, lam_ref, lbuf_ref, h0_ref,
                        y_ref, nlb_ref, hnew_ref):
    xs = xs_ref[...]
    y_s = ys_t_ref[...] + dsk_ref[...] * xs
    y_s = _group_rmsnorm(y_s * _silu(zs_ref[...]), snw_ref[...])
    num = num_t_ref[...]
    den = den_ref[...]
    h_m = jnp.concatenate(
        [num[:, hd * ML_HEAD_DIM:(hd + 1) * ML_HEAD_DIM] / den[:, hd:hd + 1] for hd in range(ML_HEADS)], axis=-1)
    h_m = _head_layernorm(h_m) * mnw_ref[...]
    y_m = (h_m + msk_ref[...] * xc_ref[...]) * _silu(zm_ref[...])
    h1 = x_ref[...] + _bdot(jnp.concatenate([y_s, y_m], axis=-1), _wload(wout_ref[...]))

    hn = _rms(h1, nw1_ref[...])
    proj = _bdot(hn, _wload(win1_ref[...]))
    gate = proj[:, :LRU_WIDTH]
    xr = proj[:, LRU_WIDTH:]
    xc = _conv_step(lbuf_ref, xr, cw_ref, cb_ref, nlb_ref)
    ra, ix = _blockdiag_tiles(xc, wax_ref)
    a, u = _rglru_gates(xc, ra, ix, ba_ref, bx_ref, lam_ref)
    h = a * h0_ref[...] + u
    hnew_ref[...] = h
    h2 = h1 + _bdot(h * _silu(gate), _wload(wout1_ref[...]))
    y_ref[...] = _rms(h2, fnw_ref[...])


def _full_call(kernel_fn, out_shapes, args, name):
    return pl.pallas_call(
        kernel_fn,
        out_shape=[jax.ShapeDtypeStruct(s, F32) for s in out_shapes],
        compiler_params=pltpu.CompilerParams(vmem_limit_bytes=VMEM_LIMIT),
        name=name,
    )(*args)


def _row(v, width=None):
    v = v.reshape(1, -1).astype(F32)
    if width is not None and v.shape[1] < width:
        v = jnp.pad(v, ((0, 0), (0, width - v.shape[1])))
    return v


PACK_STEPS = 8


def _pack_all(w_in_t, weights):
    flats = [w.reshape(-1, w.shape[-1]) for w in weights]
    k_in = w_in_t.shape[1]
    assert k_in == PACK_STEPS * LANE
    for f in flats:
        assert f.shape[0] % (2 * SUBLANE * PACK_STEPS) == 0
    packed = pl.pallas_call(
        _pack_kernel,
        grid=(PACK_STEPS,),
        in_specs=([pl.BlockSpec((w_in_t.shape[0], LANE), lambda i: (0, i))]
                  + [pl.BlockSpec((f.shape[0] // PACK_STEPS, f.shape[1]), lambda i: (i, 0)) for f in flats]),
        out_specs=([pl.BlockSpec((LANE // 2, IN_MIX_PAD), lambda i: (i, 0))]
                   + [pl.BlockSpec((f.shape[0] // PACK_STEPS // 2, f.shape[1]), lambda i: (i, 0)) for f in flats]),
        out_shape=([jax.ShapeDtypeStruct((k_in // 2, IN_MIX_PAD), jnp.uint32)]
                   + [jax.ShapeDtypeStruct((f.shape[0] // 2, f.shape[1]), jnp.uint32) for f in flats]),
        compiler_params=pltpu.CompilerParams(vmem_limit_bytes=VMEM_LIMIT),
        name="pack_weights",
    )(w_in_t, *flats)
    return [packed[0]] + [p.reshape(w.shape[:-2] + (w.shape[-2] // 2, w.shape[-1]))
                          for p, w in zip(packed[1:], weights)]


def _pack_rows(x):
    return pltpu.bitcast(x.astype(BF16), jnp.uint32)


def _pack_kernel(*refs):
    n = len(refs) // 2
    win_ref, wino_ref = refs[0], refs[n]
    wino_ref[:, :OFF_DT] = _pack_rows(win_ref[:OFF_DT, :].T)
    dt_tile = win_ref[OFF_DT:OFF_DT + LANE, :].T
    lane = lax.broadcasted_iota(jnp.int32, dt_tile.shape, 1)
    wino_ref[:, OFF_DT:OFF_ZM] = _pack_rows(jnp.where(lane < SSD_HEADS, dt_tile, 0.0))
    wino_ref[:, OFF_ZM:] = _pack_rows(win_ref[OFF_DT + SSD_HEADS:, :].T)
    for w_ref, o_ref in zip(refs[1:n], refs[n + 1:]):
        o_ref[...] = _pack_rows(w_ref[...])


def _dense_block_tiles(w, tile=LANE):
    nb, bi, bo = w.shape
    per = tile // bi
    rows = w.reshape(nb // per, per * bi, bo)
    col = jnp.arange(per * bo)
    spread = (col[None, :] % bo == jnp.arange(bo)[:, None]).astype(w.dtype)
    rep = jnp.einsum('tro,oc->trc', rows, spread)
    same_block = (jnp.arange(per * bi)[:, None] // bi) == (col[None, :] // bo)
    return jnp.where(same_block, rep, 0.0)


def kernel(x_prompt, x_sample, state_ssd_conv, state_ssd, state_mlstm_conv, state_mlstm_C, state_mlstm_n,
           state_mlstm_m, state_lru_conv, state_lru_h, meta_tokens, norm_w, final_norm_w, w_in_mix, w_out_mix,
           ssd_conv_w, ssd_conv_b, ssd_dt_bias, ssd_a_log, ssd_d, ssd_norm_w, ml_conv_w, ml_conv_b, ml_wq, ml_wk,
           ml_wv, ml_w_gate, ml_b_gate, ml_skip, ml_norm_w, lru_w_in, lru_w_out, lru_conv_w, lru_conv_b, lru_wa,
           lru_ba, lru_wx, lru_bx, lru_lambda):
    bsz = x_prompt.shape[0]
    dec = x_sample.shape[0]

    wout = w_out_mix[0]
    nw0 = _row(norm_w[0])
    nw1 = _row(norm_w[1])
    fnw = _row(final_norm_w)
    scw = ssd_conv_w[0]
    scb = _row(ssd_conv_b[0])
    dtb = _row(ssd_dt_bias[0], LANE)
    alog = _row(ssd_a_log[0], LANE)
    dsk = _row(jnp.repeat(ssd_d[0], SSD_HEAD_DIM))
    snw = _row(ssd_norm_w[0])
    mcw = ml_conv_w[0]
    mcb = _row(ml_conv_b[0])
    wqk = jnp.concatenate([_dense_block_tiles(ml_wq[0]), _dense_block_tiles(ml_wk[0])], axis=2)
    wv = _dense_block_tiles(ml_wv[0], tile=2 * LANE)
    wg_raw = ml_w_gate[0]
    wg = jnp.concatenate([jnp.pad(wg_raw[:, :ML_HEADS], ((0, 0), (0, LANE - ML_HEADS))),
                          jnp.pad(wg_raw[:, ML_HEADS:], ((0, 0), (0, LANE - ML_HEADS)))], axis=1)
    bg = jnp.concatenate([_row(ml_b_gate[0, :ML_HEADS], LANE), _row(ml_b_gate[0, ML_HEADS:], LANE)], axis=1)
    msk = _row(ml_skip[0])
    mnw = _row(ml_norm_w[0])
    win1 = lru_w_in[0]
    wout1 = lru_w_out[0]
    lcw = lru_conv_w[0]
    lcb = _row(lru_conv_b[0])
    wax = jnp.concatenate([lru_wa[0], lru_wx[0]], axis=2)
    r_idx = lax.broadcasted_iota(jnp.int32, (LANE, SSD_WIDTH), 0)
    c_idx = lax.broadcasted_iota(jnp.int32, (LANE, SSD_WIDTH), 1)
    expand = (c_idx // SSD_HEAD_DIM == r_idx).astype(F32)
    ba = _row(lru_ba[0])
    bx = _row(lru_bx[0])
    lam = _row(lru_lambda[0])

    win, wout, wqk, wv, wg, expand, win1, wout1, wax = _pack_all(
        jnp.swapaxes(w_in_mix[0], 0, 1), [wout, wqk, wv, wg, expand, win1, wout1, wax])
    l0_w = (nw0, win, wout, scw, scb, dtb, alog, dsk, snw, mcw, mcb, wqk, wv, wg, bg, msk, mnw, expand)
    l1_w = (nw1, fnw, win1, wout1, lcw, lcb, wax, ba, bx, lam)

    xs2 = x_sample[:, 0]
    sbuf = jnp.moveaxis(state_ssd_conv[0], 1, 0)
    mbuf = jnp.moveaxis(state_mlstm_conv[0], 1, 0)
    lbuf = jnp.moveaxis(state_lru_conv[0], 1, 0)
    m0 = jnp.pad(state_mlstm_m[0], ((0, 0), (0, LANE - ML_HEADS)))
    n0 = state_mlstm_n[0].reshape(dec, ML_WIDTH)
    pre_shapes = ((TAIL, dec, SSD_CONV_CH), (TAIL, dec, ML_WIDTH), (dec, SSD_WIDTH), (dec, SSD_WIDTH),
                  (dec, SSD_GROUPS * SSD_STATE), (dec, SSD_GROUPS * SSD_STATE), (dec, SSD_WIDTH), (dec, LANE),
                  (dec, ML_WIDTH), (dec, ML_WIDTH), (dec, ML_WIDTH), (dec, ML_WIDTH), (dec, LANE),
                  (dec, ML_WIDTH), (dec, LANE), (dec, ML_WIDTH), (dec, ML_HEADS))
    (nsb, nmb, zs, xs_c, bm, cm, xdt_t, dec_t, zm, xc_m, q, isv_t, fs_t, k, m_new, n_new, den) = _full_call(
        _l0_sample_pre_kernel, pre_shapes,
        (xs2, nw0, win, scw, scb, dtb, alog, mcw, mcb, wqk, wv, wg, bg, sbuf, mbuf, m0, n0), "l0_sample_pre")

    zero0 = tuple(jnp.zeros((1,) + s, F32) for s in L0_STATE_SHAPES)
    zero1 = tuple(jnp.zeros((1,) + s, F32) for s in L1_STATE_SHAPES)
    meta = jnp.pad(meta_tokens.astype(F32), ((CHUNK - N_META, 0), (0, 0)))[None]
    meta_out = _l0_prompt(meta, l0_w, zero0, CHUNK - N_META)
    meta1_out = _l1_prompt(meta_out[0], l1_w, zero1, CHUNK - N_META)

    n_steps = (bsz // _rows_per_step(bsz, L0_ROWS)) * (x_prompt.shape[1] // CHUNK)
    assert n_steps == (bsz // _rows_per_step(bsz, L1_ROWS)) * (x_prompt.shape[1] // CHUNK)
    per_step = dec // n_steps
    assert per_step * n_steps == dec

    def step_rows(a):
        a = a.reshape(n_steps, per_step, a.shape[-1])
        return jnp.pad(a, ((0, 0), (0, SUBLANE - per_step), (0, 0)))

    def from_step_rows(a):
        return a[:, :per_step].reshape(dec, a.shape[-1])

    l0_out = _l0_prompt(
        x_prompt, l0_w, tuple(meta_out[1:]), 0,
        stream_in=(dec_t[:, :SSD_HEADS], state_ssd[0], step_rows(xdt_t), step_rows(bm), step_rows(cm)),
        stream_out_shapes=((dec, SSD_HEADS, SSD_HEAD_DIM, SSD_STATE), (n_steps, SUBLANE, SSD_WIDTH)))
    h1_p, p_sc, p_s, p_mc, p_c, p_n, p_m, s_new, ys_r = l0_out
    y_prompt, p_lc, p_lh, c_new, num_r = _l1_prompt(
        h1_p, l1_w, tuple(meta1_out[1:]), 0,
        stream_in=(fs_t[:, :ML_HEADS], state_mlstm_C[0], step_rows(isv_t), step_rows(k), step_rows(q)),
        stream_out_shapes=((dec, ML_HEADS, ML_HEAD_DIM, ML_HEAD_DIM), (n_steps, SUBLANE, ML_WIDTH)))

    p_m = p_m[:, 0, :ML_HEADS]
    p_lh = p_lh[:, 0]

    post_shapes = ((dec, D_MODEL), (TAIL, dec, LRU_WIDTH), (dec, LRU_WIDTH))
    y_s2, nlb, h_new = _full_call(
        _sample_post_kernel, post_shapes,
        (xs2, from_step_rows(ys_r), from_step_rows(num_r), den, zs, xs_c, zm, xc_m, dsk, snw, msk, mnw, wout,
         nw1, fnw, win1, wout1, lcw, lcb, wax, ba, bx, lam, lbuf, state_lru_h[0]), "sample_post")

    s_sc = jnp.moveaxis(nsb, 0, 1)[None]
    s_mc = jnp.moveaxis(nmb, 0, 1)[None]
    s_lc = jnp.moveaxis(nlb, 0, 1)[None]
    return (y_prompt, y_s2[:, None, :],
            p_sc[None], p_s[None], p_mc[None], p_c[None], p_n[None], p_m[None], p_lc[None], p_lh[None],
            s_sc, s_new[None], s_mc, c_new[None], n_new.reshape(dec, ML_HEADS, ML_HEAD_DIM)[None],
            m_new[:, :ML_HEADS][None], s_lc, h_new[None])
```

```python
import functools

import jax
import jax.numpy as jnp
from jax import lax
from jax.experimental import pallas as pl
from jax.experimental.pallas import tpu as pltpu

F32 = jnp.float32
BF16 = jnp.bfloat16

D_MODEL = 1024
N_META = 16
CONV_W = 4
EPS = 1e-6
NEG = -1e30
SSD_WIDTH = 1024
SSD_HEAD_DIM = 64
SSD_HEADS = 16
SSD_GROUPS = 2
SSD_HPG = 8
SSD_STATE = 128
SSD_CONV_CH = 1536
ML_WIDTH = 1024
ML_HEADS = 4
ML_HEAD_DIM = 256
ML_QKV_BLOCK = 4
LRU_WIDTH = 2048
LRU_BLOCKS = 16
LRU_BLOCK = 128
LRU_C = 8.0

LANE = 128
SUBLANE = 8
CHUNK = 128
L0_ROWS = 2
L1_ROWS = 2
L1_GROUPS = 8
SUBCHAINS_PER_ROUND = 2
TAIL = CONV_W - 1

OFF_ZS = 0
OFF_XBC = OFF_ZS + SSD_WIDTH
OFF_DT = OFF_XBC + SSD_CONV_CH
OFF_ZM = OFF_DT + LANE
OFF_XM = OFF_ZM + ML_WIDTH
IN_MIX_PAD = OFF_XM + ML_WIDTH

VMEM_LIMIT = 56 * 1024 * 1024
L0_VMEM_LIMIT = 61 * 1024 * 1024


def _sigmoid(x):
    return 1.0 / (1.0 + jnp.exp(-x))


def _silu(x):
    return x * _sigmoid(x)


def _softplus(x):
    return jnp.maximum(x, 0.0) + jnp.log1p(jnp.exp(-jnp.abs(x)))


def _rms(x, w):
    return x * lax.rsqrt(jnp.mean(x * x, axis=-1, keepdims=True) + EPS) * w


def _bdot(a, b):
    return jnp.dot(a.astype(BF16), b.astype(BF16), preferred_element_type=F32)


def _bdot_nt(a, b):
    return lax.dot_general(a.astype(BF16), b.astype(BF16), (((1,), (1,)), ((), ())), preferred_element_type=F32)


def _wload(w):
    return pltpu.bitcast(w, BF16)


def _split3(x):
    hi = x.astype(BF16)
    r = x - hi.astype(F32)
    mid = r.astype(BF16)
    lo = (r - mid.astype(F32)).astype(BF16)
    return hi, mid, lo


def _cumsum_rows(x, tril):
    hi, mid, lo = _split3(x)
    d = functools.partial(jnp.dot, preferred_element_type=F32)
    return d(tril, hi) + d(tril, mid) + d(tril, lo)


def _expand_heads(x, expand):
    hi, mid, _ = _split3(x)
    d = functools.partial(jnp.dot, preferred_element_type=F32)
    return d(hi, expand) + d(mid, expand)


def _expand_matrix():
    r = lax.broadcasted_iota(jnp.int32, (LANE, SSD_WIDTH), 0)
    c = lax.broadcasted_iota(jnp.int32, (LANE, SSD_WIDTH), 1)
    return jnp.where(lax.shift_right_logical(c, 6) == r, 1.0, 0.0).astype(BF16)


def _blockdiag_tiles(x, w_ref):
    k = w_ref.shape[0]
    w = 2 * w_ref.shape[1]
    m = w_ref.shape[2] // w
    prods = [_bdot(x[:, t * w:(t + 1) * w], _wload(w_ref[t])) for t in range(k)]
    return [jnp.concatenate([p[:, j * w:(j + 1) * w] for p in prods], axis=-1) for j in range(m)]


def _group_rmsnorm(y, w):
    half = SSD_WIDTH // SSD_GROUPS
    parts = []
    for g in range(SSD_GROUPS):
        yg = y[:, g * half:(g + 1) * half]
        parts.append(yg * lax.rsqrt(jnp.mean(yg * yg, axis=-1, keepdims=True) + EPS))
    return jnp.concatenate(parts, axis=-1) * w


def _head_layernorm(h):
    parts = []
    for k in range(ML_HEADS):
        hk = h[:, k * ML_HEAD_DIM:(k + 1) * ML_HEAD_DIM]
        mu = jnp.mean(hk, axis=-1, keepdims=True)
        d = hk - mu
        var = jnp.mean(d * d, axis=-1, keepdims=True)
        parts.append(d * lax.rsqrt(var + EPS))
    return jnp.concatenate(parts, axis=-1)


def _mlstm_qkv_gates(xm, xc, wqk_ref, wv_ref, wg_ref, bg_ref):
    q, k = _blockdiag_tiles(xc, wqk_ref)
    v, = _blockdiag_tiles(xm, wv_ref)
    gates = _bdot(jnp.concatenate([q, k, v], axis=-1), _wload(wg_ref[...])) + bg_ref[...]
    ig = gates[:, :LANE]
    logf = -_softplus(-gates[:, LANE:])
    return q, k * (ML_HEAD_DIM ** -0.5), v, ig, logf


N_L0_W = 18
N_L0_S = 6


def _l0_prompt_kernel(x_ref, xnext_ref, *refs, front_pad, rows, stream):
    w_refs = refs[:N_L0_W]
    init_refs = refs[N_L0_W:N_L0_W + N_L0_S]
    pos = N_L0_W + N_L0_S
    stream_in = refs[pos:pos + N_STREAM_IN] if stream else ()
    pos += len(stream_in)
    out_refs = refs[pos:pos + N_L0_S + 1]
    pos += N_L0_S + 1
    stream_out = refs[pos:pos + N_STREAM_OUT] if stream else ()
    pos += len(stream_out)
    scratch = refs[pos:]
    per_row = len(scratch) // rows
    c = pl.program_id(1)
    win_ref, wout_ref = w_refs[1], w_refs[2]
    q_len = x_ref.shape[1]
    pieces = {}

    def each_row(phase):
        return [_l0_prompt_row(x_ref.at[r], xnext_ref.at[r], *w_refs, *init_refs, *(o.at[r] for o in out_refs),
                               *scratch[r * per_row:(r + 1) * per_row], front_pad=front_pad, phase=phase,
                               emit=lambda k0, y, r=r: pieces.setdefault(k0, {}).__setitem__(r, y))
                for r in range(rows)]

    @pl.when(c == 0)
    def _():
        each_row("init")

    @pl.when((c == 0) & (pl.program_id(0) == 0))
    def _():
        each_row("first")

    bodies = each_row("body")
    proj_refs = [scratch[r * per_row + per_row - 1] for r in range(rows)]
    _, to_time = _perm_matrices(q_len)
    partials = []

    def in_proj():
        lhs = jnp.concatenate([hn_next for _, hn_next, _ in bodies], axis=0)
        for lo, hi in L0_PROJ_PIECES:
            res = _bdot(lhs, _wload(win_ref[:, lo:hi]))
            for r in range(rows):
                proj_refs[r][:, lo:hi] = res[r * q_len:(r + 1) * q_len]
            yield

    def out_proj():
        half = SSD_WIDTH // SSD_GROUPS
        pending = ([(g * half, half) for g in range(SSD_GROUPS)]
                   + [(SSD_WIDTH + hd * ML_HEAD_DIM, ML_HEAD_DIM) for hd in range(ML_HEADS)])
        while pending:
            for k0, width in list(pending):
                if len(pieces.get(k0, ())) == rows:
                    y_t = jnp.concatenate([_move_rows(to_time, pieces[k0][r].astype(BF16)) for r in range(rows)],
                                          axis=0)
                    partials.append(_bdot(y_t, _wload(wout_ref[k0 // 2:(k0 + width) // 2, :])))
                    pending.remove((k0, width))
            yield

    chains = [gen for gens, _, _ in bodies for gen in gens] + [in_proj(), out_proj()]
    if stream:
        step = pl.program_id(0) * pl.num_programs(1) + c
        chains.append(_ssd_state_update(*stream_in, *stream_out, base=step * stream_in[1].shape[0]))
    _run_round_robin(chains)
    total = partials[0]
    for part in partials[1:]:
        total = total + part
    for r, (_, _, x) in enumerate(bodies):
        h1 = x + total[r * q_len:(r + 1) * q_len]
        if front_pad:
            h1 = jnp.where(lax.broadcasted_iota(jnp.int32, (q_len, 1), 0) >= front_pad, h1, 0.0)
        out_refs[0][r] = h1

    @pl.when(c == pl.num_programs(1) - 1)
    def _():
        each_row("final")


L0_PROJ_PIECES = ((OFF_XBC, OFF_ZM), (OFF_XM, IN_MIX_PAD), (OFF_ZM, OFF_XM), (OFF_ZS, OFF_XBC))


def _run_round_robin(gens):
    live = list(gens)
    waiting = []
    while live or waiting:
        for _ in range(min(SUBCHAINS_PER_ROUND, len(waiting))):
            live.append(waiting.pop(0))
        for gen in list(live):
            step = next(gen, "done")
            if step == "done":
                live.remove(gen)
            elif step is not None:
                waiting.extend(step)


def _l0_prompt_row(x_ref, xnext_ref, nw_ref, win_ref, wout_ref,
                   scw_ref, scb_ref, dtb_ref, alog_ref, dsk_ref, snw_ref,
                   mcw_ref, mcb_ref, wqk_ref, wv_ref, wg_ref, bg_ref, msk_ref, mnw_ref, expand_ref,
                   isc_ref, iss_ref, imc_ref, ict_ref, inn_ref, imm_ref,
                   h1_ref, osc_ref, oss_ref, omc_ref, oct_ref, onn_ref, omm_ref,
                   sbuf, mbuf, s_st, ct_st, n_st, m_st, proj_s, *, front_pad, phase, emit):
    q_len = x_ref.shape[0]

    if phase == "init":
        sbuf[...] = jnp.zeros(sbuf.shape, F32)
        mbuf[...] = jnp.zeros(mbuf.shape, F32)
        sbuf[SUBLANE - TAIL:SUBLANE, :] = isc_ref[0]
        mbuf[SUBLANE - TAIL:SUBLANE, :] = imc_ref[0]
        for g in range(SSD_GROUPS):
            heads = iss_ref[0, g * SSD_HPG:(g + 1) * SSD_HPG]
            s_st[g] = heads.reshape(SSD_HPG * SSD_HEAD_DIM, SSD_STATE).T
        for hd in range(ML_HEADS):
            ct_st[hd] = ict_ref[0, hd].T
        n_st[...] = inn_ref[0]
        m_st[...] = imm_ref[0]
        return None
    if phase == "first":
        hn0 = _move_rows(_perm_matrices(q_len)[0], _rms(x_ref[...], nw_ref[...]).astype(BF16))
        proj_s[...] = _bdot(hn0, _wload(win_ref[...]))
        return None
    if phase == "final":
        osc_ref[...] = sbuf[SUBLANE - TAIL:SUBLANE, :]
        omc_ref[...] = mbuf[SUBLANE - TAIL:SUBLANE, :]
        for g in range(SSD_GROUPS):
            oss_ref[g * SSD_HPG:(g + 1) * SSD_HPG] = s_st[g].T.reshape(SSD_HPG, SSD_HEAD_DIM, SSD_STATE)
        for hd in range(ML_HEADS):
            oct_ref[hd] = ct_st[hd].T
        onn_ref[...] = n_st[...]
        omm_ref[...] = m_st[...]
        return None

    x = x_ref[...]
    to_perm, to_time = _perm_matrices(q_len)
    t_col = _perm_time(q_len)
    t_row = _perm_time(q_len, row=True)
    causal = t_col >= t_row
    tril = jnp.where(causal, 1.0, 0.0).astype(BF16)
    valid = (t_col >= front_pad) if front_pad else None
    hn_next = _move_rows(to_perm, _rms(xnext_ref[...], nw_ref[...]).astype(BF16))
    xbc_raw = proj_s[:, OFF_XBC:OFF_XBC + SSD_CONV_CH]
    dt_raw = proj_s[:, OFF_DT:OFF_DT + LANE]
    xm = proj_s[:, OFF_XM:OFF_XM + ML_WIDTH]
    z_s = proj_s[:, OFF_ZS:OFF_ZS + SSD_WIDTH]
    z_m = proj_s[:, OFF_ZM:OFF_ZM + ML_WIDTH]

    def ssd():
        xbc = _silu(_conv_perm(sbuf, xbc_raw, scw_ref, scb_ref))
        yield
        xs = xbc[:, :SSD_WIDTH]
        bm = xbc[:, SSD_WIDTH:SSD_WIDTH + SSD_GROUPS * SSD_STATE]
        cm = xbc[:, SSD_WIDTH + SSD_GROUPS * SSD_STATE:]
        dt = _softplus(dt_raw + dtb_ref[...])
        if front_pad:
            dt = jnp.where(valid, dt, 0.0)
        log_a = -dt * jnp.exp(alog_ref[...])
        a_cs = _cumsum_rows(log_a, tril)
        yield
        a_last = a_cs[q_len - 1:q_len, :]
        expand = _wload(expand_ref[...])
        w_state = _expand_heads(dt * jnp.exp(a_last - a_cs), expand)
        e_acs = _expand_heads(jnp.exp(a_cs), expand)
        a_cs_t = a_cs.T
        dt_t = dt.T
        yield
        pair_lo = lax.broadcasted_iota(jnp.int32, (q_len, LANE), 1) < SSD_HEAD_DIM
        half = SSD_WIDTH // SSD_GROUPS

        def group(g):
            cols = slice(g * half, (g + 1) * half)
            bg = bm[:, g * SSD_STATE:(g + 1) * SSD_STATE]
            cg = cm[:, g * SSD_STATE:(g + 1) * SSD_STATE]
            bg_t = bg.T
            xg = xs[:, cols]
            eg = e_acs[:, cols]
            s_old = s_st[g]
            cb = _bdot(cg, bg_t)
            y_off = _bdot(cg, s_old) * eg
            s_st[g] = eg[q_len - 1:q_len, :] * s_old + _bdot(bg_t, xg * w_state[:, cols])
            yield
            y_pairs = []
            for pr in range(SSD_HPG // 2):
                ms = []
                for e in (2 * pr, 2 * pr + 1):
                    hd = g * SSD_HPG + e
                    seg = jnp.exp(jnp.where(causal, a_cs[:, hd:hd + 1] - a_cs_t[hd:hd + 1, :], -jnp.inf))
                    ms.append(cb * seg * dt_t[hd:hd + 1, :])
                xp = xg[:, pr * LANE:(pr + 1) * LANE]
                rhs = jnp.concatenate([jnp.where(pair_lo, xp, 0.0), jnp.where(pair_lo, 0.0, xp)], axis=0)
                y_pairs.append(_bdot(jnp.concatenate(ms, axis=-1), rhs))
                yield
            y_g = (jnp.concatenate(y_pairs, axis=-1) + y_off + dsk_ref[:, cols] * xg) * _silu(z_s[:, cols])
            y_g = y_g * lax.rsqrt(jnp.mean(y_g * y_g, axis=-1, keepdims=True) + EPS)
            emit(g * half, y_g * snw_ref[:, cols])

        yield [group(g) for g in range(SSD_GROUPS)]

    def mlstm():
        xc = _silu(_conv_perm(mbuf, xm, mcw_ref, mcb_ref))
        yield
        q, k = _blockdiag_tiles(xc, wqk_ref)
        v, = _blockdiag_tiles(xm, wv_ref)
        yield
        gates = _bdot(jnp.concatenate([q, k, v], axis=-1), _wload(wg_ref[...])) + bg_ref[...]
        k = k * (ML_HEAD_DIM ** -0.5)
        yield
        ig = gates[:, :LANE]
        logf = -_softplus(-gates[:, LANE:])
        if front_pad:
            ig = jnp.where(valid, ig, NEG)
            logf = jnp.where(valid, logf, 0.0)
        bcum = _cumsum_rows(logf, tril)
        yield
        ftot = bcum[q_len - 1:q_len, :]
        m_prev = m_st[...]
        w_end = ftot - bcum + ig
        m_new = jnp.maximum(ftot + m_prev, jnp.max(w_end, axis=0, keepdims=True))
        sc = jnp.exp(ftot + m_prev - m_new)
        wexp = jnp.exp(w_end - m_new)
        inter = bcum + m_prev
        bcum_t = bcum.T
        ig_t = ig.T
        m_st[...] = m_new

        def head(hd):
            sl = slice(hd * ML_HEAD_DIM, (hd + 1) * ML_HEAD_DIM)
            q_h, k_h, v_h = q[:, sl], k[:, sl], v[:, sl]
            k_t = k_h.T
            dmat = jnp.where(causal, bcum[:, hd:hd + 1] - bcum_t[hd:hd + 1, :] + ig_t[hd:hd + 1, :], -jnp.inf)
            inter_h = inter[:, hd:hd + 1]
            m_t = jnp.maximum(inter_h, jnp.max(dmat, axis=-1, keepdims=True))
            dexp = jnp.exp(dmat - m_t)
            inter_sc = jnp.exp(inter_h - m_t)
            s = _bdot(q_h, k_t) * dexp
            yield
            ct_old = ct_st[hd]
            n_old = n_st[hd:hd + 1, :]
            num = _bdot(s, v_h) + inter_sc * _bdot(q_h, ct_old)
            den = jnp.sum(s, axis=-1, keepdims=True) + inter_sc * jnp.sum(q_h * n_old, axis=-1, keepdims=True)
            h_h = num / jnp.maximum(jnp.abs(den), jnp.exp(-m_t))
            w_col = wexp[:, hd:hd + 1]
            sc_h = sc[:, hd:hd + 1]
            ct_st[hd] = sc_h * ct_old + _bdot(k_t, v_h * w_col)
            n_st[hd:hd + 1, :] = sc_h * n_old + jnp.sum(k_h * w_col, axis=0, keepdims=True)
            yield
            mu = jnp.mean(h_h, axis=-1, keepdims=True)
            dev = h_h - mu
            var = jnp.mean(dev * dev, axis=-1, keepdims=True)
            h_h = dev * lax.rsqrt(var + EPS) * mnw_ref[:, sl]
            emit(SSD_WIDTH + hd * ML_HEAD_DIM, (h_h + msk_ref[:, sl] * xc[:, sl]) * _silu(z_m[:, sl]))

        yield [head(hd) for hd in range(ML_HEADS)]

    return [ssd(), mlstm()], hn_next, x


def _const_spec(shape):
    nd = len(shape)
    return pl.BlockSpec(shape, lambda b, c: (0,) * nd)


def _state_spec(shape, rows):
    nd = len(shape)
    if rows:
        return pl.BlockSpec((rows,) + shape, lambda b, c: (b,) + (0,) * nd)
    return pl.BlockSpec((1,) + shape, lambda b, c: (0,) * (nd + 1))


def _rows_per_step(bsz, want):
    return want if bsz % want == 0 else 1


L0_STATE_SHAPES = ((TAIL, SSD_CONV_CH), (SSD_HEADS, SSD_HEAD_DIM, SSD_STATE), (TAIL, ML_WIDTH),
                   (ML_HEADS, ML_HEAD_DIM, ML_HEAD_DIM), (ML_HEADS, ML_HEAD_DIM), (1, LANE))
L0_CARRY_SHAPES = ((SUBLANE, SSD_CONV_CH), (SUBLANE, ML_WIDTH), (SSD_GROUPS, SSD_STATE, SSD_WIDTH // SSD_GROUPS),
                   (ML_HEADS, ML_HEAD_DIM, ML_HEAD_DIM), (ML_HEADS, ML_HEAD_DIM), (1, LANE))


def _stream_specs(arrays, n_steps, nc):
    specs = []
    for a in arrays:
        if a.ndim == 2:
            specs.append(pl.BlockSpec(memory_space=pltpu.SMEM))
            continue
        assert a.shape[0] % n_steps == 0
        block = (a.shape[0] // n_steps,) + a.shape[1:]
        specs.append(pl.BlockSpec(block, lambda b, c, nd=a.ndim: (b * nc + c,) + (0,) * (nd - 1)))
    return specs


def _l0_prompt(x, weights, init, front_pad, stream_in=(), stream_out_shapes=()):
    bsz, length, _ = x.shape
    q_len = min(CHUNK, length)
    assert length % q_len == 0
    rows = _rows_per_step(bsz, L0_ROWS)
    assert len(weights) == N_L0_W and len(init) == N_L0_S
    nc = length // q_len
    grid = (bsz // rows, nc)
    n_steps = grid[0] * nc
    last = nc - 1
    x_spec = pl.BlockSpec((rows, q_len, D_MODEL), lambda b, c: (b, c, 0))
    next_spec = pl.BlockSpec((rows, q_len, D_MODEL),
                             lambda b, c: (jnp.minimum(b + (c + 1) // nc, grid[0] - 1), (c + 1) % nc, 0))
    stream_outs = [jax.ShapeDtypeStruct(s, F32) for s in stream_out_shapes]
    in_specs = ([x_spec, next_spec] + [_const_spec(w.shape) for w in weights]
                + [_state_spec(s, 0) for s in L0_STATE_SHAPES] + _stream_specs(stream_in, n_steps, nc))
    out_shape = ([jax.ShapeDtypeStruct((bsz, length, D_MODEL), F32)]
                 + [jax.ShapeDtypeStruct((bsz,) + s, F32) for s in L0_STATE_SHAPES] + stream_outs)
    out_specs = ([x_spec] + [_state_spec(s, rows) for s in L0_STATE_SHAPES]
                 + _stream_specs(stream_outs, n_steps, nc))
    row_scratch = L0_CARRY_SHAPES + ((q_len, IN_MIX_PAD),)
    scratch = [pltpu.VMEM(s, F32) for _ in range(rows) for s in row_scratch]
    return pl.pallas_call(
        functools.partial(_l0_prompt_kernel, front_pad=front_pad, rows=rows, stream=bool(stream_in)),
        grid=grid, in_specs=in_specs, out_specs=out_specs, out_shape=out_shape, scratch_shapes=scratch,
        compiler_params=pltpu.CompilerParams(dimension_semantics=("arbitrary", "arbitrary"),
                                             vmem_limit_bytes=L0_VMEM_LIMIT),
        name="l0_prompt",
    )(x, x, *weights, *init, *stream_in)


def _rglru_gates(xc, ra, ix, ba_ref, bx_ref, lam_ref):
    r = _sigmoid(ra + ba_ref[...])
    i = _sigmoid(ix + bx_ref[...])
    log_a = r * (-LRU_C * _softplus(-lam_ref[...]))
    a = jnp.exp(log_a)
    var = 1.0 - a * a
    root = jnp.where(var > 0.0, var * lax.rsqrt(var), 0.0)
    u = root * (i * xc)
    return a, u


def _perm_time(n, row=False):
    p = lax.broadcasted_iota(jnp.int32, (1, n) if row else (n, 1), 1 if row else 0)
    return (n // SUBLANE) * (p & (SUBLANE - 1)) + lax.shift_right_logical(p, 3)


def _perm_matrices(n):
    nb = n // SUBLANE
    r = lax.broadcasted_iota(jnp.int32, (n, n), 0)
    c = lax.broadcasted_iota(jnp.int32, (n, n), 1)
    to_perm = jnp.where(c == nb * (r & (SUBLANE - 1)) + lax.shift_right_logical(r, 3), 1.0, 0.0)
    to_time = jnp.where(r == nb * (c & (SUBLANE - 1)) + lax.shift_right_logical(c, 3), 1.0, 0.0)
    return to_perm.astype(BF16), to_time.astype(BF16)


def _move_rows(sel, x_bf16):
    return jnp.dot(sel, x_bf16, preferred_element_type=F32).astype(BF16)


def _conv_perm(tail_ref, x, w_ref, b_ref):
    n, ch = x.shape
    nb = n // SUBLANE
    x3 = x.reshape(nb, SUBLANE, ch)
    tail8 = tail_ref[...]
    sub = lax.broadcasted_iota(jnp.int32, (SUBLANE, ch), 0)
    y = b_ref[...].reshape(1, 1, ch) + w_ref[TAIL:TAIL + 1, :].reshape(1, 1, ch) * x3
    wrapped = [jnp.where(sub >= 1, pltpu.roll(x3[nb - d], 1, 0), tail8[SUBLANE - d:SUBLANE - d + 1, :])
               for d in range(1, CONV_W)]
    for back in range(1, CONV_W):
        head = jnp.stack([wrapped[back - j - 1] for j in range(back)], axis=0)
        shifted = jnp.concatenate([head, x3[:nb - back]], axis=0)
        y = y + w_ref[TAIL - back:TAIL - back + 1, :].reshape(1, 1, ch) * shifted
    for d in range(1, CONV_W):
        tail_ref[SUBLANE - d:SUBLANE - d + 1, :] = x3[nb - d][SUBLANE - 1:SUBLANE, :]
    return y.reshape(n, ch)


def _scan_perm(a, u, h_prev):
    n, ch = a.shape
    nb = n // SUBLANE
    a3 = a.reshape(nb, SUBLANE, ch)
    u3 = u.reshape(nb, SUBLANE, ch)
    local = [u3[0]]
    decay = [a3[0]]
    for j in range(1, nb):
        local.append(a3[j] * local[-1] + u3[j])
        decay.append(a3[j] * decay[-1])
    seg_u, seg_a = local[-1], decay[-1]
    sub = lax.broadcasted_iota(jnp.int32, (SUBLANE, ch), 0)
    shift = 1
    while shift < SUBLANE:
        keep = sub >= shift
        seg_u = seg_u + seg_a * jnp.where(keep, pltpu.roll(seg_u, shift, 0), 0.0)
        seg_a = seg_a * jnp.where(keep, pltpu.roll(seg_a, shift, 0), 1.0)
        shift *= 2
    seg_end = seg_a * h_prev + seg_u
    carry = jnp.where(sub >= 1, pltpu.roll(seg_end, 1, 0), h_prev)
    h3 = jnp.stack([local[j] + decay[j] * carry for j in range(nb)], axis=0)
    return h3.reshape(n, ch), seg_end[SUBLANE - 1:SUBLANE, :]


N_L1_W = 10
N_L1_S = 2


N_STREAM_IN = 5
N_STREAM_OUT = 2


def _l1_prompt_kernel(h_ref, hnext_ref, *refs, front_pad, rows, stream):
    w_refs = refs[:N_L1_W]
    init_refs = refs[N_L1_W:N_L1_W + N_L1_S]
    pos = N_L1_W + N_L1_S
    stream_in = refs[pos:pos + N_STREAM_IN] if stream else ()
    pos += len(stream_in)
    out_refs = refs[pos:pos + N_L1_S + 1]
    pos += N_L1_S + 1
    stream_out = refs[pos:pos + N_STREAM_OUT] if stream else ()
    pos += len(stream_out)
    scratch = refs[pos:]
    per_row = len(scratch) // rows
    c = pl.program_id(1)
    fnw_ref, win_ref, wout_ref = w_refs[1], w_refs[2], w_refs[3]
    q_len = h_ref.shape[1]
    gw = LRU_WIDTH // L1_GROUPS
    pieces = {}

    def each_row(phase):
        return [_l1_prompt_row(h_ref.at[r], hnext_ref.at[r], *w_refs, *init_refs, *(o.at[r] for o in out_refs),
                               *scratch[r * per_row:(r + 1) * per_row], front_pad=front_pad, phase=phase,
                               emit=lambda g, y, r=r: pieces.setdefault(g, {}).__setitem__(r, y))
                for r in range(rows)]

    @pl.when(c == 0)
    def _():
        each_row("init")

    @pl.when((c == 0) & (pl.program_id(0) == 0))
    def _():
        each_row("first")

    bodies = each_row("body")
    proj_refs = [scratch[r * per_row + per_row - 1] for r in range(rows)]
    _, to_time = _perm_matrices(q_len)
    partials = []

    def in_proj():
        lhs = jnp.concatenate([hn_next for _, hn_next, _ in bodies], axis=0)
        for g in range(L1_GROUPS):
            for lo in (g * gw, LRU_WIDTH + g * gw):
                res = _bdot(lhs, _wload(win_ref[:, lo:lo + gw]))
                for r in range(rows):
                    proj_refs[r][:, lo:lo + gw] = res[r * q_len:(r + 1) * q_len]
            yield

    def out_proj():
        pending = list(range(L1_GROUPS))
        while pending:
            for g in list(pending):
                if len(pieces.get(g, ())) == rows:
                    y_t = jnp.concatenate([_move_rows(to_time, pieces[g][r].astype(BF16)) for r in range(rows)],
                                          axis=0)
                    partials.append(_bdot(y_t, _wload(wout_ref[g * gw // 2:(g + 1) * gw // 2, :])))
                    pending.remove(g)
            yield

    chains = [bodies[r][0][g] for g in range(L1_GROUPS) for r in range(rows)]
    if stream:
        step = pl.program_id(0) * pl.num_programs(1) + c
        chains.append(_mlstm_state_update(*stream_in, *stream_out, base=step * stream_in[1].shape[0]))
    _run_staggered([in_proj()] + chains + [out_proj()])
    total = partials[0]
    for part in partials[1:]:
        total = total + part
    for r, (_, _, h_in) in enumerate(bodies):
        out_refs[0][r] = _rms(h_in + total[r * q_len:(r + 1) * q_len], fnw_ref[...])

    @pl.when(c == pl.num_programs(1) - 1)
    def _():
        each_row("final")


def _l1_in_proj(h_val, nw_ref, win_ref, to_perm):
    hn = _move_rows(to_perm, _rms(h_val, nw_ref[...]).astype(BF16))
    return _bdot(hn, _wload(win_ref[...]))


def _l1_prompt_row(h_ref, hnext_ref, nw_ref, fnw_ref, win_ref, wout_ref, cw_ref, cb_ref,
                   wax_ref, ba_ref, bx_ref, lam_ref, ilc_ref, ilh_ref,
                   y_ref, olc_ref, olh_ref, lbuf, h_st, proj_s, *, front_pad, phase, emit):
    if phase == "init":
        lbuf[...] = jnp.zeros(lbuf.shape, F32)
        lbuf[SUBLANE - TAIL:SUBLANE, :] = ilc_ref[0]
        h_st[...] = ilh_ref[0]
        return None
    if phase == "first":
        proj_s[...] = _l1_in_proj(h_ref[...], nw_ref, win_ref, _perm_matrices(h_ref.shape[0])[0])
        return None
    if phase == "final":
        olc_ref[...] = lbuf[SUBLANE - TAIL:SUBLANE, :]
        olh_ref[...] = h_st[...]
        return None

    q_len = h_ref.shape[0]
    h_in = h_ref[...]
    to_perm, _ = _perm_matrices(q_len)
    hn_next = _move_rows(to_perm, _rms(hnext_ref[...], nw_ref[...]).astype(BF16))
    if front_pad:
        valid = _perm_time(q_len) >= front_pad
    gw = LRU_WIDTH // L1_GROUPS
    tiles = gw // LANE
    gates = [proj_s[:, g * gw:(g + 1) * gw] for g in range(L1_GROUPS)]
    xrs = [proj_s[:, LRU_WIDTH + g * gw:LRU_WIDTH + (g + 1) * gw] for g in range(L1_GROUPS)]

    def group(g):
        cg = slice(g * gw, (g + 1) * gw)
        xc = _conv_perm(lbuf.at[:, cg], xrs[g], cw_ref.at[:, cg], cb_ref.at[:, cg])
        ra, ix = _blockdiag_tiles(xc, wax_ref.at[g * tiles:(g + 1) * tiles])
        yield
        a, u = _rglru_gates(xc, ra, ix, ba_ref.at[:, cg], bx_ref.at[:, cg], lam_ref.at[:, cg])
        if front_pad:
            a = jnp.where(valid, a, 1.0)
            u = jnp.where(valid, u, 0.0)
        yield
        h, h_last = _scan_perm(a, u, h_st[:, cg])
        h_st[:, cg] = h_last
        yield
        emit(g, h * _silu(gates[g]))

    return [group(g) for g in range(L1_GROUPS)], hn_next, h_in


def _run_staggered(gens):
    live = []
    pending = list(gens)
    while pending or live:
        if pending:
            live.append(pending.pop(0))
        for gen in list(live):
            if next(gen, "done") == "done":
                live.remove(gen)


L1_STATE_SHAPES = ((TAIL, LRU_WIDTH), (1, LRU_WIDTH))


def _l1_prompt(h1, weights, init, front_pad, stream_in=(), stream_out_shapes=()):
    bsz, length, _ = h1.shape
    q_len = min(CHUNK, length)
    assert length % q_len == 0
    rows = _rows_per_step(bsz, L1_ROWS)
    assert len(weights) == N_L1_W and len(init) == N_L1_S
    nc = length // q_len
    grid = (bsz // rows, nc)
    n_steps = grid[0] * nc
    last = nc - 1
    x_spec = pl.BlockSpec((rows, q_len, D_MODEL), lambda b, c: (b, c, 0))
    next_spec = pl.BlockSpec((rows, q_len, D_MODEL),
                             lambda b, c: (jnp.minimum(b + (c + 1) // nc, grid[0] - 1), (c + 1) % nc, 0))
    stream_outs = [jax.ShapeDtypeStruct(s, F32) for s in stream_out_shapes]
    in_specs = ([x_spec, next_spec] + [_const_spec(w.shape) for w in weights]
                + [_state_spec(s, 0) for s in L1_STATE_SHAPES] + _stream_specs(stream_in, n_steps, nc))
    out_shape = ([jax.ShapeDtypeStruct((bsz, length, D_MODEL), F32)]
                 + [jax.ShapeDtypeStruct((bsz,) + s, F32) for s in L1_STATE_SHAPES] + stream_outs)
    out_specs = ([x_spec] + [_state_spec(s, rows) for s in L1_STATE_SHAPES]
                 + _stream_specs(stream_outs, n_steps, nc))
    row_scratch = ((SUBLANE, LRU_WIDTH), (1, LRU_WIDTH), (q_len, 2 * LRU_WIDTH))
    scratch = [pltpu.VMEM(s, F32) for _ in range(rows) for s in row_scratch]
    return pl.pallas_call(
        functools.partial(_l1_prompt_kernel, front_pad=front_pad, rows=rows, stream=bool(stream_in)),
        grid=grid, in_specs=in_specs, out_specs=out_specs, out_shape=out_shape, scratch_shapes=scratch,
        compiler_params=pltpu.CompilerParams(dimension_semantics=("arbitrary", "arbitrary"),
                                             vmem_limit_bytes=VMEM_LIMIT),
        name="l1_prompt",
    )(h1, h1, *weights, *init, *stream_in)


def _conv_step(buf_ref, x, w_ref, b_ref, newbuf_ref):
    y = b_ref[...] + w_ref[3:4, :] * x
    for tap in range(TAIL):
        y = y + w_ref[tap:tap + 1, :] * buf_ref[tap]
    for tap in range(TAIL - 1):
        newbuf_ref[tap] = buf_ref[tap + 1]
    newbuf_ref[TAIL - 1] = x
    return y


def _l0_sample_pre_kernel(x_ref, nw_ref, win_ref, scw_ref, scb_ref, dtb_ref, alog_ref,
                          mcw_ref, mcb_ref, wqk_ref, wv_ref, wg_ref, bg_ref,
                          sbuf_ref, mbuf_ref, m0_ref, n0_ref,
                          nsb_ref, nmb_ref, zs_ref, xs_ref, bm_ref, cm_ref, xdt_t_ref, dec_t_ref,
                          zm_ref, xc_ref, q_ref, isv_t_ref, fs_t_ref, k_ref, mnew_ref, nnew_ref, den_ref):
    x = x_ref[...]
    hn = _rms(x, nw_ref[...])
    proj = _bdot(hn, _wload(win_ref[...]))
    zs_ref[...] = proj[:, OFF_ZS:OFF_ZS + SSD_WIDTH]
    zm_ref[...] = proj[:, OFF_ZM:OFF_ZM + ML_WIDTH]
    xbc = proj[:, OFF_XBC:OFF_XBC + SSD_CONV_CH]
    dt_raw = proj[:, OFF_DT:OFF_DT + LANE]
    xm = proj[:, OFF_XM:OFF_XM + ML_WIDTH]
    expand = _expand_matrix()

    xbc = _silu(_conv_step(sbuf_ref, xbc, scw_ref, scb_ref, nsb_ref))
    xs = xbc[:, :SSD_WIDTH]
    xs_ref[...] = xs
    bm_ref[...] = xbc[:, SSD_WIDTH:SSD_WIDTH + SSD_GROUPS * SSD_STATE]
    cm_ref[...] = xbc[:, SSD_WIDTH + SSD_GROUPS * SSD_STATE:]
    dt = _softplus(dt_raw + dtb_ref[...])
    log_a = -dt * jnp.exp(alog_ref[...])
    xdt_t_ref[...] = xs * _expand_heads(dt, expand)
    dec_t_ref[...] = jnp.exp(log_a)

    xc = _silu(_conv_step(mbuf_ref, xm, mcw_ref, mcb_ref, nmb_ref))
    xc_ref[...] = xc
    q, k, v, ig, logf = _mlstm_qkv_gates(xm, xc, wqk_ref, wv_ref, wg_ref, bg_ref)
    m0 = m0_ref[...]
    m_new = jnp.maximum(logf + m0, ig)
    fs = jnp.exp(logf + m0 - m_new)
    is_ = jnp.exp(ig - m_new)
    mnew_ref[...] = m_new
    r = lax.broadcasted_iota(jnp.int32, (LANE, ML_WIDTH), 0)
    cidx = lax.broadcasted_iota(jnp.int32, (LANE, ML_WIDTH), 1)
    expand_m = jnp.where(lax.shift_right_logical(cidx, 8) == r, 1.0, 0.0).astype(BF16)
    fs_e = _expand_heads(fs, expand_m)
    is_e = _expand_heads(is_, expand_m)
    n_new = fs_e * n0_ref[...] + is_e * k
    nnew_ref[...] = n_new
    q_ref[...] = q
    k_ref[...] = k
    isv_t_ref[...] = is_e * v
    fs_t_ref[...] = fs
    nq = n_new * q
    floor = jnp.exp(-m_new)
    for hd in range(ML_HEADS):
        den = jnp.sum(nq[:, hd * ML_HEAD_DIM:(hd + 1) * ML_HEAD_DIM], axis=-1, keepdims=True)
        den_ref[:, hd:hd + 1] = jnp.maximum(jnp.abs(den), floor[:, hd:hd + 1])


def _rows_to_tile(rows8):
    return jnp.concatenate([rows8] + [jnp.zeros_like(rows8)] * (LANE // SUBLANE - 1), axis=0)


def _ssd_state_update(dec_ref, s_ref, xdt_ref, bm_ref, cm_ref, snew_ref, y_ref, *, base):
    n = s_ref.shape[0]
    half = SSD_WIDTH // SSD_GROUPS
    x_cols = _rows_to_tile(xdt_ref[0]).T
    lane = lax.broadcasted_iota(jnp.int32, (half, LANE), 1)
    accs = [jnp.zeros((half, LANE), F32) for _ in range(SSD_GROUPS)]
    for i in range(n):
        x_col = x_cols[:, i:i + 1].reshape(SSD_HEADS, SSD_HEAD_DIM, 1)
        for g in range(SSD_GROUPS):
            hs = slice(g * SSD_HPG, (g + 1) * SSD_HPG)
            gs = slice(g * SSD_STATE, (g + 1) * SSD_STATE)
            b_row = bm_ref[0, i:i + 1, gs].reshape(1, 1, SSD_STATE)
            decay = jnp.stack([jnp.full((1, 1), dec_ref[base + i, hd], F32)
                               for hd in range(hs.start, hs.stop)], axis=0)
            s_new = decay * s_ref[i, hs] + x_col[hs] * b_row
            snew_ref[i, hs] = s_new
            prod = _bdot_nt(s_new.reshape(half, SSD_STATE), _rows_to_tile(cm_ref[0, :, gs]))
            accs[g] = jnp.where(lane == i, prod, accs[g])
            yield
    y_ref[0] = jnp.concatenate(accs, axis=0).T[:SUBLANE]


def _mlstm_state_update(fs_ref, c_ref, isv_ref, k_ref, q_ref, cnew_ref, num_ref, *, base):
    n = c_ref.shape[0]
    v_cols = _rows_to_tile(isv_ref[0]).T
    lane = lax.broadcasted_iota(jnp.int32, (ML_HEAD_DIM, LANE), 1)
    for hd in range(ML_HEADS):
        sl = slice(hd * ML_HEAD_DIM, (hd + 1) * ML_HEAD_DIM)
        q_rows = _rows_to_tile(q_ref[0, :, sl])
        acc = jnp.zeros((ML_HEAD_DIM, LANE), F32)
        for i in range(n):
            c_new = fs_ref[base + i, hd] * c_ref[i, hd] + v_cols[sl, i:i + 1] * k_ref[0, i:i + 1, sl]
            cnew_ref[i, hd] = c_new
            acc = jnp.where(lane == i, _bdot_nt(c_new, q_rows), acc)
            yield
        num_ref[0, :, sl] = acc.T[:SUBLANE]


def _sample_post_kernel(x_ref, ys_t_ref, num_t_ref, den_ref, zs_ref, xs_ref, zm_ref, xc_ref,
                        dsk_ref, snw_ref, msk_ref, mnw_ref, wout_ref,
                        nw1_ref, fnw_ref, win1_ref, wout1_ref, cw_ref, cb_ref,
                        wax_ref, ba_ref, bx_ref, lam_ref, lbuf_ref, h0_ref,
                        y_ref, nlb_ref, hnew_ref):
    xs = xs_ref[...]
    y_s = ys_t_ref[...] + dsk_ref[...] * xs
    y_s = _group_rmsnorm(y_s * _silu(zs_ref[...]), snw_ref[...])
    num = num_t_ref[...]
    den = den_ref[...]
    h_m = jnp.concatenate(
        [num[:, hd * ML_HEAD_DIM:(hd + 1) * ML_HEAD_DIM] / den[:, hd:hd + 1] for hd in range(ML_HEADS)], axis=-1)
    h_m = _head_layernorm(h_m) * mnw_ref[...]
    y_m = (h_m + msk_ref[...] * xc_ref[...]) * _silu(zm_ref[...])
    h1 = x_ref[...] + _bdot(jnp.concatenate([y_s, y_m], axis=-1), _wload(wout_ref[...]))

    hn = _rms(h1, nw1_ref[...])
    proj = _bdot(hn, _wload(win1_ref[...]))
    gate = proj[:, :LRU_WIDTH]
    xr = proj[:, LRU_WIDTH:]
    xc = _conv_step(lbuf_ref, xr, cw_ref, cb_ref, nlb_ref)
    ra, ix = _blockdiag_tiles(xc, wax_ref)
    a, u = _rglru_gates(xc, ra, ix, ba_ref, bx_ref, lam_ref)
    h = a * h0_ref[...] + u
    hnew_ref[...] = h
    h2 = h1 + _bdot(h * _silu(gate), _wload(wout1_ref[...]))
    y_ref[...] = _rms(h2, fnw_ref[...])


def _full_call(kernel_fn, out_shapes, args, name):
    return pl.pallas_call(
        kernel_fn,
        out_shape=[jax.ShapeDtypeStruct(s, F32) for s in out_shapes],
        compiler_params=pltpu.CompilerParams(vmem_limit_bytes=VMEM_LIMIT),
        name=name,
    )(*args)


def _row(v, width=None):
    v = v.reshape(1, -1).astype(F32)
    if width is not None and v.shape[1] < width:
        v = jnp.pad(v, ((0, 0), (0, width - v.shape[1])))
    return v


PACK_STEPS = 8


def _pack_all(w_in_t, weights):
    flats = [w.reshape(-1, w.shape[-1]) for w in weights]
    k_in = w_in_t.shape[1]
    assert k_in == PACK_STEPS * LANE
    for f in flats:
        assert f.shape[0] % (2 * SUBLANE * PACK_STEPS) == 0
    packed = pl.pallas_call(
        _pack_kernel,
        grid=(PACK_STEPS,),
        in_specs=([pl.BlockSpec((w_in_t.shape[0], LANE), lambda i: (0, i))]
                  + [pl.BlockSpec((f.shape[0] // PACK_STEPS, f.shape[1]), lambda i: (i, 0)) for f in flats]),
        out_specs=([pl.BlockSpec((LANE // 2, IN_MIX_PAD), lambda i: (i, 0))]
                   + [pl.BlockSpec((f.shape[0] // PACK_STEPS // 2, f.shape[1]), lambda i: (i, 0)) for f in flats]),
        out_shape=([jax.ShapeDtypeStruct((k_in // 2, IN_MIX_PAD), jnp.uint32)]
                   + [jax.ShapeDtypeStruct((f.shape[0] // 2, f.shape[1]), jnp.uint32) for f in flats]),
        compiler_params=pltpu.CompilerParams(vmem_limit_bytes=VMEM_LIMIT),
        name="pack_weights",
    )(w_in_t, *flats)
    return [packed[0]] + [p.reshape(w.shape[:-2] + (w.shape[-2] // 2, w.shape[-1]))
                          for p, w in zip(packed[1:], weights)]


def _pack_rows(x):
    return pltpu.bitcast(x.astype(BF16), jnp.uint32)


def _pack_kernel(*refs):
    n = len(refs) // 2
    win_ref, wino_ref = refs[0], refs[n]
    wino_ref[:, :OFF_DT] = _pack_rows(win_ref[:OFF_DT, :].T)
    dt_tile = win_ref[OFF_DT:OFF_DT + LANE, :].T
    lane = lax.broadcasted_iota(jnp.int32, dt_tile.shape, 1)
    wino_ref[:, OFF_DT:OFF_ZM] = _pack_rows(jnp.where(lane < SSD_HEADS, dt_tile, 0.0))
    wino_ref[:, OFF_ZM:] = _pack_rows(win_ref[OFF_DT + SSD_HEADS:, :].T)
    for w_ref, o_ref in zip(refs[1:n], refs[n + 1:]):
        o_ref[...] = _pack_rows(w_ref[...])


def _dense_block_tiles(w, tile=LANE):
    nb, bi, bo = w.shape
    per = tile // bi
    rows = w.reshape(nb // per, per * bi, bo)
    col = jnp.arange(per * bo)
    spread = (col[None, :] % bo == jnp.arange(bo)[:, None]).astype(w.dtype)
    rep = jnp.einsum('tro,oc->trc', rows, spread)
    same_block = (jnp.arange(per * bi)[:, None] // bi) == (col[None, :] // bo)
    return jnp.where(same_block, rep, 0.0)


def kernel(x_prompt, x_sample, state_ssd_conv, state_ssd, state_mlstm_conv, state_mlstm_C, state_mlstm_n,
           state_mlstm_m, state_lru_conv, state_lru_h, meta_tokens, norm_w, final_norm_w, w_in_mix, w_out_mix,
           ssd_conv_w, ssd_conv_b, ssd_dt_bias, ssd_a_log, ssd_d, ssd_norm_w, ml_conv_w, ml_conv_b, ml_wq, ml_wk,
           ml_wv, ml_w_gate, ml_b_gate, ml_skip, ml_norm_w, lru_w_in, lru_w_out, lru_conv_w, lru_conv_b, lru_wa,
           lru_ba, lru_wx, lru_bx, lru_lambda):
    bsz = x_prompt.shape[0]
    dec = x_sample.shape[0]

    wout = w_out_mix[0]
    nw0 = _row(norm_w[0])
    nw1 = _row(norm_w[1])
    fnw = _row(final_norm_w)
    scw = ssd_conv_w[0]
    scb = _row(ssd_conv_b[0])
    dtb = _row(ssd_dt_bias[0], LANE)
    alog = _row(ssd_a_log[0], LANE)
    dsk = _row(jnp.repeat(ssd_d[0], SSD_HEAD_DIM))
    snw = _row(ssd_norm_w[0])
    mcw = ml_conv_w[0]
    mcb = _row(ml_conv_b[0])
    wqk = jnp.concatenate([_dense_block_tiles(ml_wq[0]), _dense_block_tiles(ml_wk[0])], axis=2)
    wv = _dense_block_tiles(ml_wv[0], tile=2 * LANE)
    wg_raw = ml_w_gate[0]
    wg = jnp.concatenate([jnp.pad(wg_raw[:, :ML_HEADS], ((0, 0), (0, LANE - ML_HEADS))),
                          jnp.pad(wg_raw[:, ML_HEADS:], ((0, 0), (0, LANE - ML_HEADS)))], axis=1)
    bg = jnp.concatenate([_row(ml_b_gate[0, :ML_HEADS], LANE), _row(ml_b_gate[0, ML_HEADS:], LANE)], axis=1)
    msk = _row(ml_skip[0])
    mnw = _row(ml_norm_w[0])
    win1 = lru_w_in[0]
    wout1 = lru_w_out[0]
    lcw = lru_conv_w[0]
    lcb = _row(lru_conv_b[0])
    wax = jnp.concatenate([lru_wa[0], lru_wx[0]], axis=2)
    r_idx = lax.broadcasted_iota(jnp.int32, (LANE, SSD_WIDTH), 0)
    c_idx = lax.broadcasted_iota(jnp.int32, (LANE, SSD_WIDTH), 1)
    expand = (c_idx // SSD_HEAD_DIM == r_idx).astype(F32)
    ba = _row(lru_ba[0])
    bx = _row(lru_bx[0])
    lam = _row(lru_lambda[0])

    win, wout, wqk, wv, wg, expand, win1, wout1, wax = _pack_all(
        jnp.swapaxes(w_in_mix[0], 0, 1), [wout, wqk, wv, wg, expand, win1, wout1, wax])
    l0_w = (nw0, win, wout, scw, scb, dtb, alog, dsk, snw, mcw, mcb, wqk, wv, wg, bg, msk, mnw, expand)
    l1_w = (nw1, fnw, win1, wout1, lcw, lcb, wax, ba, bx, lam)

    xs2 = x_sample[:, 0]
    sbuf = jnp.moveaxis(state_ssd_conv[0], 1, 0)
    mbuf = jnp.moveaxis(state_mlstm_conv[0], 1, 0)
    lbuf = jnp.moveaxis(state_lru_conv[0], 1, 0)
    m0 = jnp.pad(state_mlstm_m[0], ((0, 0), (0, LANE - ML_HEADS)))
    n0 = state_mlstm_n[0].reshape(dec, ML_WIDTH)
    pre_shapes = ((TAIL, dec, SSD_CONV_CH), (TAIL, dec, ML_WIDTH), (dec, SSD_WIDTH), (dec, SSD_WIDTH),
                  (dec, SSD_GROUPS * SSD_STATE), (dec, SSD_GROUPS * SSD_STATE), (dec, SSD_WIDTH), (dec, LANE),
                  (dec, ML_WIDTH), (dec, ML_WIDTH), (dec, ML_WIDTH), (dec, ML_WIDTH), (dec, LANE),
                  (dec, ML_WIDTH), (dec, LANE), (dec, ML_WIDTH), (dec, ML_HEADS))
    (nsb, nmb, zs, xs_c, bm, cm, xdt_t, dec_t, zm, xc_m, q, isv_t, fs_t, k, m_new, n_new, den) = _full_call(
        _l0_sample_pre_kernel, pre_shapes,
        (xs2, nw0, win, scw, scb, dtb, alog, mcw, mcb, wqk, wv, wg, bg, sbuf, mbuf, m0, n0), "l0_sample_pre")

    zero0 = tuple(jnp.zeros((1,) + s, F32) for s in L0_STATE_SHAPES)
    zero1 = tuple(jnp.zeros((1,) + s, F32) for s in L1_STATE_SHAPES)
    meta = jnp.pad(meta_tokens.astype(F32), ((CHUNK - N_META, 0), (0, 0)))[None]
    meta_out = _l0_prompt(meta, l0_w, zero0, CHUNK - N_META)
    meta1_out = _l1_prompt(meta_out[0], l1_w, zero1, CHUNK - N_META)

    n_steps = (bsz // _rows_per_step(bsz, L0_ROWS)) * (x_prompt.shape[1] // CHUNK)
    assert n_steps == (bsz // _rows_per_step(bsz, L1_ROWS)) * (x_prompt.shape[1] // CHUNK)
    per_step = dec // n_steps
    assert per_step * n_steps == dec

    def step_rows(a):
        a = a.reshape(n_steps, per_step, a.shape[-1])
        return jnp.pad(a, ((0, 0), (0, SUBLANE - per_step), (0, 0)))

    def from_step_rows(a):
        return a[:, :per_step].reshape(dec, a.shape[-1])

    l0_out = _l0_prompt(
        x_prompt, l0_w, tuple(meta_out[1:]), 0,
        stream_in=(dec_t[:, :SSD_HEADS], state_ssd[0], step_rows(xdt_t), step_rows(bm), step_rows(cm)),
        stream_out_shapes=((dec, SSD_HEADS, SSD_HEAD_DIM, SSD_STATE), (n_steps, SUBLANE, SSD_WIDTH)))
    h1_p, p_sc, p_s, p_mc, p_c, p_n, p_m, s_new, ys_r = l0_out
    y_prompt, p_lc, p_lh, c_new, num_r = _l1_prompt(
        h1_p, l1_w, tuple(meta1_out[1:]), 0,
        stream_in=(fs_t[:, :ML_HEADS], state_mlstm_C[0], step_rows(isv_t), step_rows(k), step_rows(q)),
        stream_out_shapes=((dec, ML_HEADS, ML_HEAD_DIM, ML_HEAD_DIM), (n_steps, SUBLANE, ML_WIDTH)))

    p_m = p_m[:, 0, :ML_HEADS]
    p_lh = p_lh[:, 0]

    post_shapes = ((dec, D_MODEL), (TAIL, dec, LRU_WIDTH), (dec, LRU_WIDTH))
    y_s2, nlb, h_new = _full_call(
        _sample_post_kernel, post_shapes,
        (xs2, from_step_rows(ys_r), from_step_rows(num_r), den, zs, xs_c, zm, xc_m, dsk, snw, msk, mnw, wout,
         nw1, fnw, win1, wout1, lcw, lcb, wax, ba, bx, lam, lbuf, state_lru_h[0]), "sample_post")

    s_sc = jnp.moveaxis(nsb, 0, 1)[None]
    s_mc = jnp.moveaxis(nmb, 0, 1)[None]
    s_lc = jnp.moveaxis(nlb, 0, 1)[None]
    return (y_prompt, y_s2[:, None, :],
            p_sc[None], p_s[None], p_mc[None], p_c[None], p_n[None], p_m[None], p_lc[None], p_lh[None],
            s_sc, s_new[None], s_mc, c_new[None], n_new.reshape(dec, ML_HEADS, ML_HEAD_DIM)[None],
            m_new[:, :ML_HEADS][None], s_lc, h_new[None])
```
